```python
import math
import jax
import jax.numpy as jnp
from jax import lax
import numpy as np

D_MODEL = 1024
BATCH = 8
SEQ = 2048
DEPTH = 2
DEC_BATCH = 128
DEC_SEQ = 4
PAST_LEN = 16384
PAGE_SIZE = 128

CONV_W = 4
LRU_WIDTH = D_MODEL // 2
LRU_BLOCKS = 8
LRU_BLOCK = LRU_WIDTH // LRU_BLOCKS
LRU_C = 8.0
GDN_HEADS = 4
GDN_DK = 64
GDN_DV = 64
GDN_CONV_DIM = GDN_HEADS * (2 * GDN_DK + GDN_DV)
GDN_CHUNK = 64
GLA_HEADS = 4
GLA_DK = 32
GLA_DV = 64
GLA_RANK = 16
GLA_TAU = 16.0
GLA_CHUNK = 16
D_MIX = LRU_WIDTH + GDN_HEADS * GDN_DV + GLA_HEADS * GLA_DV
N_EXPERTS = 32
TOP_K = 4
D_FF = D_MODEL
SWIGLU_LIMIT = 7.0
SWIGLU_ALPHA = 1.702
EPS = 1e-6

IN_SPLITS = (LRU_WIDTH, LRU_WIDTH,
             GDN_HEADS * GDN_DK, GDN_HEADS * GDN_DK, GDN_HEADS * GDN_DV, GDN_HEADS * GDN_DV, GDN_HEADS, GDN_HEADS,
             GLA_HEADS * GLA_DK, GLA_HEADS * GLA_DK, GLA_HEADS * GLA_DV, GLA_HEADS * GLA_DV, GLA_RANK)
D_IN = sum(IN_SPLITS)

kernel_name = "hymba_rglru_gdn_gla_moe_step"


def _split_points():
    return [int(v) for v in np.cumsum(IN_SPLITS)[:-1]]


def rmsnorm(x, g):
    xf = x.astype(jnp.float32)
    y = xf * lax.rsqrt(jnp.mean(xf * xf, axis=-1, keepdims=True) + EPS)
    return (y * g).astype(x.dtype)


def l2norm(x):
    xf = x.astype(jnp.float32)
    return xf * lax.rsqrt(jnp.sum(xf * xf, axis=-1, keepdims=True) + EPS)


def causal_conv(x, buf, w):
    T = x.shape[1]
    xp = jnp.concatenate([buf.astype(x.dtype), x], axis=1)
    y = xp[:, 0:T] * w[0]
    for j in range(1, CONV_W):
        y = y + xp[:, j:j + T] * w[j]
    return y, xp[:, T:]


def _affine_combine(left, right):
    return (left[0] * right[0], right[0] * left[1] + right[1])


def rg_lru(x, h0, w_a, b_a, w_x, b_x, lam, reset_first):
    B, T, W = x.shape
    xf = x.astype(jnp.float32)
    xb = xf.reshape(B, T, LRU_BLOCKS, LRU_BLOCK)
    r = jax.nn.sigmoid(jnp.einsum('btnd,nde->btne', xb, w_a).reshape(B, T, W) + b_a)
    i = jax.nn.sigmoid(jnp.einsum('btnd,nde->btne', xb, w_x).reshape(B, T, W) + b_x)
    log_a = -LRU_C * r * jax.nn.softplus(-lam.astype(jnp.float32))
    a = jnp.exp(log_a)
    mult = jnp.sqrt(-jnp.expm1(2.0 * log_a))
    if reset_first:
        mult = mult.at[:, 0].set(1.0)
    u = mult * (i * xf)
    u = u.at[:, 0].add(a[:, 0] * h0.astype(jnp.float32))
    _, h = lax.associative_scan(_affine_combine, (a, u), axis=1)
    return h, h[:, -1]


def _to_chunks(t, C):
    B, T, H, D = t.shape
    return t.astype(jnp.float32).reshape(B, T // C, C, H, D).transpose(1, 0, 3, 2, 4)


def _from_chunks(o):
    N, B, H, C, D = o.shape
    return o.transpose(1, 0, 3, 2, 4).reshape(B, N * C, H, D)


def gated_delta_chunked(q, k, v, g, beta, S0):
    C = math.gcd(q.shape[1], GDN_CHUNK)
    qc, kc, vc = (_to_chunks(t, C) for t in (q, k, v))
    gcum = jnp.cumsum(_to_chunks(g[..., None], C)[..., 0], axis=-1)
    bc = _to_chunks(beta[..., None], C)
    incl = jnp.tril(jnp.ones((C, C), dtype=bool))
    strict = jnp.tril(jnp.ones((C, C), dtype=bool), -1)
    decay = jnp.exp(jnp.where(incl, gcum[..., :, None] - gcum[..., None, :], -jnp.inf))
    kb = kc * bc
    a_mat = jnp.where(strict, jnp.einsum('nbhcd,nbhsd->nbhcs', kb, kc) * decay, 0.0)
    eye = jnp.eye(C, dtype=jnp.float32)
    t_inv = lax.linalg.triangular_solve(eye + a_mat, jnp.broadcast_to(eye, a_mat.shape),
                                        left_side=True, lower=True, unit_diagonal=True)
    u = t_inv @ (vc * bc)
    w = t_inv @ (kb * jnp.exp(gcum)[..., None])
    qk = jnp.einsum('nbhcd,nbhsd->nbhcs', qc, kc) * decay

    def step(S, xs):
        q_i, k_i, u_i, w_i, qk_i, g_i = xs
        v_new = u_i - w_i @ S
        o = (q_i * jnp.exp(g_i)[..., None]) @ S + qk_i @ v_new
        g_last = g_i[..., -1:]
        S = S * jnp.exp(g_last)[..., None] + jnp.einsum(
            'bhcd,bhcv->bhdv', k_i * jnp.exp(g_last - g_i)[..., None], v_new)
        return S, o

    S, o = lax.scan(step, S0.astype(jnp.float32), (qc, kc, u, w, qk, gcum))
    return _from_chunks(o), S


def gla_chunked(q, k, v, gk, S0):
    C = math.gcd(q.shape[1], GLA_CHUNK)
    qc, kc, vc, gc = (_to_chunks(t, C) for t in (q, k, v, gk))
    b = jnp.cumsum(gc, axis=-2)
    b_last = b[..., -1, :]
    q_in = qc * jnp.exp(b)
    scores = jnp.einsum('nbhck,nbhsk->nbhcs', q_in, kc * jnp.exp(-b))
    causal = jnp.tril(jnp.ones((C, C), dtype=bool))
    scores = jnp.where(causal, scores, 0.0)
    k_state = kc * jnp.exp(b_last[..., None, :] - b)

    def step(S, xs):
        q_i, a_i, v_i, ks_i, bl_i = xs
        o = jnp.einsum('bhck,bhkv->bhcv', q_i, S) + jnp.einsum('bhcs,bhsv->bhcv', a_i, v_i)
        S = S * jnp.exp(bl_i)[..., None] + jnp.einsum('bhck,bhcv->bhkv', ks_i, v_i)
        return S, o

    S, o = lax.scan(step, S0.astype(jnp.float32), (q_in, scores, vc, k_state, b_last))
    return _from_chunks(o), S


def token_mixers(h, states, mix_p, reset_first):
    (w_in, lru_conv_w, lru_conv_b, lru_wa, lru_ba, lru_wx, lru_bx, lru_lambda,
     gdn_conv_w, gdn_a_log, gdn_dt_bias, gdn_norm_g, gla_wg2, gla_bg, gla_norm_g, w_out) = mix_p
    lru_buf, lru_h, gdn_buf, gdn_S, gla_S = states
    B, T, _ = h.shape
    dt = h.dtype
    f32 = jnp.float32
    proj = h @ w_in
    (a_x, a_gate, b_q, b_k, b_v, b_z, b_beta, b_alpha,
     c_q, c_k, c_v, c_r, c_g) = jnp.split(proj, _split_points(), axis=-1)

    a_x, lru_buf_new = causal_conv(a_x, lru_buf, lru_conv_w)
    a_y, lru_h_new = rg_lru(a_x + lru_conv_b, lru_h, lru_wa, lru_ba, lru_wx, lru_bx, lru_lambda, reset_first)
    out_a = a_y.astype(dt) * jax.nn.gelu(a_gate)

    qkv, gdn_buf_new = causal_conv(jnp.concatenate([b_q, b_k, b_v], axis=-1), gdn_buf, gdn_conv_w)
    qkv = jax.nn.silu(qkv)
    q, k, v = jnp.split(qkv, [GDN_HEADS * GDN_DK, 2 * GDN_HEADS * GDN_DK], axis=-1)
    q = l2norm(q.reshape(B, T, GDN_HEADS, GDN_DK)) * (GDN_DK ** -0.5)
    k = l2norm(k.reshape(B, T, GDN_HEADS, GDN_DK))
    v = v.reshape(B, T, GDN_HEADS, GDN_DV).astype(f32)
    beta = jax.nn.sigmoid(b_beta.astype(f32))
    g = -jnp.exp(gdn_a_log.astype(f32)) * jax.nn.softplus(b_alpha.astype(f32) + gdn_dt_bias)
    o_b, gdn_S_new = gated_delta_chunked(q, k, v, g, beta, gdn_S)
    z = jax.nn.silu(b_z.reshape(B, T, GDN_HEADS, GDN_DV).astype(f32))
    out_b = (rmsnorm(o_b, gdn_norm_g) * z).reshape(B, T, GDN_HEADS * GDN_DV).astype(dt)

    gk = jax.nn.log_sigmoid((c_g @ gla_wg2 + gla_bg).astype(f32)) / GLA_TAU
    qg = c_q.reshape(B, T, GLA_HEADS, GLA_DK).astype(f32) * (GLA_DK ** -0.5)
    kg = c_k.reshape(B, T, GLA_HEADS, GLA_DK)
    vg = c_v.reshape(B, T, GLA_HEADS, GLA_DV)
    o_c, gla_S_new = gla_chunked(qg, kg, vg, gk.reshape(B, T, GLA_HEADS, GLA_DK), gla_S)
    r = jax.nn.silu(c_r.reshape(B, T, GLA_HEADS, GLA_DV).astype(f32))
    out_c = (rmsnorm(o_c, gla_norm_g) * r).reshape(B, T, GLA_HEADS * GLA_DV).astype(dt)

    y = jnp.concatenate([out_a, out_b, out_c], axis=-1) @ w_out
    return y, (lru_buf_new, lru_h_new, gdn_buf_new, gdn_S_new, gla_S_new)


def routed_moe(h, moe_p):
    router_w, router_b, w_gate, b_gate, w_up, b_up, w_down, b_down = moe_p
    B, T, D = h.shape
    xt = h.reshape(B * T, D)
    logits = (xt @ router_w + router_b).astype(jnp.float32)
    top_val, top_idx = lax.top_k(logits, TOP_K)
    probs = jax.nn.softmax(top_val, axis=-1)
    gates = jnp.einsum('tk,tke->te', probs,
                       jax.nn.one_hot(top_idx, N_EXPERTS, dtype=jnp.float32)).astype(xt.dtype)
    y = jnp.zeros_like(xt)
    for e in range(N_EXPERTS):
        gate = jnp.minimum(xt @ w_gate[e] + b_gate[e], SWIGLU_LIMIT)
        up = jnp.clip(xt @ w_up[e] + b_up[e], -SWIGLU_LIMIT, SWIGLU_LIMIT)
        act = (up + 1.0) * gate * jax.nn.sigmoid(SWIGLU_ALPHA * gate)
        y = y + gates[:, e:e + 1] * (act @ w_down[e] + b_down[e])
    return y.reshape(B, T, D)


def decoder_layer(x, c, states, cond_p, mix_p, moe_p, reset_first):
    ada_w, ada_b, norm1_g, norm2_g = cond_p
    mod = jax.nn.silu(c) @ ada_w + ada_b
    shift1, scale1, gate1, shift2, scale2, gate2 = [m[:, None, :] for m in jnp.split(mod, 6, axis=-1)]
    h = rmsnorm(x, norm1_g) * (1.0 + scale1) + shift1
    y, new_states = token_mixers(h, states, mix_p, reset_first)
    x = x + gate1 * y
    h = rmsnorm(x, norm2_g) * (1.0 + scale2) + shift2
    x = x + gate2 * routed_moe(h, moe_p)
    return x, new_states


def setup_inputs(seed: int = 0) -> dict:
    key = jax.random.key(seed)
    ks = iter(jax.random.split(key, 64))
    f32 = jnp.float32
    L = DEPTH

    def nrm(shape, scale):
        return scale * jax.random.normal(next(ks), shape, f32)

    def gain(shape):
        return 1.0 + nrm(shape, 0.02)

    a0 = jax.random.uniform(next(ks), (L, LRU_WIDTH), f32, 0.9, 0.999)
    s = a0 ** (1.0 / LRU_C)
    lru_lambda = jnp.log(s) - jnp.log1p(-s)
    dt0 = jnp.exp(jax.random.uniform(next(ks), (L, GDN_HEADS), f32, math.log(1e-3), math.log(1e-1)))
    gdn_dt_bias = dt0 + jnp.log(-jnp.expm1(-dt0))
    gdn_a_log = jnp.log(jax.random.uniform(next(ks), (L, GDN_HEADS), f32, 1.0, 16.0))
    return {
        "x_prompt": nrm((BATCH, SEQ, D_MODEL), 1.0),
        "x_sample": nrm((DEC_BATCH, DEC_SEQ, D_MODEL), 1.0),
        "state_lru_conv": nrm((L, DEC_BATCH, CONV_W - 1, LRU_WIDTH), 1.0),
        "state_lru_h": nrm((L, DEC_BATCH, LRU_WIDTH), 0.5),
        "state_gdn_conv": nrm((L, DEC_BATCH, CONV_W - 1, GDN_CONV_DIM), 1.0),
        "state_gdn_S": nrm((L, DEC_BATCH, GDN_HEADS, GDN_DK, GDN_DV), 0.1),
        "state_gla_S": nrm((L, DEC_BATCH, GLA_HEADS, GLA_DK, GLA_DV), 0.1),
        "c_prompt": nrm((BATCH, D_MODEL), 1.0),
        "c_sample": nrm((DEC_BATCH, D_MODEL), 1.0),
        "ada_w": nrm((L, D_MODEL, 6 * D_MODEL), 0.5 * D_MODEL ** -0.5),
        "ada_b": nrm((L, 6 * D_MODEL), 0.02),
        "norm1_g": gain((L, D_MODEL)),
        "norm2_g": gain((L, D_MODEL)),
        "w_in": nrm((L, D_MODEL, D_IN), D_MODEL ** -0.5),
        "lru_conv_w": nrm((L, CONV_W, LRU_WIDTH), 0.5),
        "lru_conv_b": nrm((L, LRU_WIDTH), 0.02),
        "lru_wa": nrm((L, LRU_BLOCKS, LRU_BLOCK, LRU_BLOCK), LRU_BLOCK ** -0.5),
        "lru_ba": nrm((L, LRU_WIDTH), 0.02),
        "lru_wx": nrm((L, LRU_BLOCKS, LRU_BLOCK, LRU_BLOCK), LRU_BLOCK ** -0.5),
        "lru_bx": nrm((L, LRU_WIDTH), 0.02),
        "lru_lambda": lru_lambda,
        "gdn_conv_w": nrm((L, CONV_W, GDN_CONV_DIM), 0.5),
        "gdn_a_log": gdn_a_log,
        "gdn_dt_bias": gdn_dt_bias,
        "gdn_norm_g": gain((L, GDN_DV)),
        "gla_wg2": nrm((L, GLA_RANK, GLA_HEADS * GLA_DK), GLA_RANK ** -0.5),
        "gla_bg": nrm((L, GLA_HEADS * GLA_DK), 0.02),
        "gla_norm_g": gain((L, GLA_DV)),
        "w_out": nrm((L, D_MIX, D_MODEL), D_MIX ** -0.5),
        "router_w": nrm((L, D_MODEL, N_EXPERTS), D_MODEL ** -0.5),
        "router_b": nrm((L, N_EXPERTS), 0.01),
        "exp_w_gate": nrm((L, N_EXPERTS, D_MODEL, D_FF), D_MODEL ** -0.5),
        "exp_b_gate": nrm((L, N_EXPERTS, D_FF), 0.02),
        "exp_w_up": nrm((L, N_EXPERTS, D_MODEL, D_FF), D_MODEL ** -0.5),
        "exp_b_up": nrm((L, N_EXPERTS, D_FF), 0.02),
        "exp_w_down": nrm((L, N_EXPERTS, D_FF, D_MODEL), D_FF ** -0.5),
        "exp_b_down": nrm((L, N_EXPERTS, D_MODEL), 0.02),
        "final_norm_g": gain((D_MODEL,)),
    }


def reference(x_prompt, x_sample, state_lru_conv, state_lru_h, state_gdn_conv, state_gdn_S, state_gla_S,
              c_prompt, c_sample, ada_w, ada_b, norm1_g, norm2_g, w_in, lru_conv_w, lru_conv_b,
              lru_wa, lru_ba, lru_wx, lru_bx, lru_lambda, gdn_conv_w, gdn_a_log, gdn_dt_bias, gdn_norm_g,
              gla_wg2, gla_bg, gla_norm_g, w_out, router_w, router_b, exp_w_gate, exp_b_gate,
              exp_w_up, exp_b_up, exp_w_down, exp_b_down, final_norm_g):
    xp, xs = x_prompt, x_sample
    bp = x_prompt.shape[0]
    prompt_states, sample_states = [], []
    for l in range(DEPTH):
        cond_p = (ada_w[l], ada_b[l], norm1_g[l], norm2_g[l])
        mix_p = (w_in[l], lru_conv_w[l], lru_conv_b[l], lru_wa[l], lru_ba[l], lru_wx[l], lru_bx[l],
                 lru_lambda[l], gdn_conv_w[l], gdn_a_log[l], gdn_dt_bias[l], gdn_norm_g[l],
                 gla_wg2[l], gla_bg[l], gla_norm_g[l], w_out[l])
        moe_p = (router_w[l], router_b[l], exp_w_gate[l], exp_b_gate[l], exp_w_up[l], exp_b_up[l],
                 exp_w_down[l], exp_b_down[l])
        fresh = (jnp.zeros((bp, CONV_W - 1, LRU_WIDTH), x_prompt.dtype),
                 jnp.zeros((bp, LRU_WIDTH), jnp.float32),
                 jnp.zeros((bp, CONV_W - 1, GDN_CONV_DIM), x_prompt.dtype),
                 jnp.zeros((bp, GDN_HEADS, GDN_DK, GDN_DV), jnp.float32),
                 jnp.zeros((bp, GLA_HEADS, GLA_DK, GLA_DV), jnp.float32))
        past = (state_lru_conv[l], state_lru_h[l], state_gdn_conv[l], state_gdn_S[l], state_gla_S[l])
        xp, ps = decoder_layer(xp, c_prompt, fresh, cond_p, mix_p, moe_p, True)
        xs, ss = decoder_layer(xs, c_sample, past, cond_p, mix_p, moe_p, False)
        prompt_states.append(ps)
        sample_states.append(ss)
    y_prompt = rmsnorm(xp, final_norm_g)
    y_sample = rmsnorm(xs, final_norm_g)
    p_lru_conv, p_lru_h, p_gdn_conv, p_gdn_S, p_gla_S = [jnp.stack([s[j] for s in prompt_states]) for j in range(5)]
    s_lru_conv, s_lru_h, s_gdn_conv, s_gdn_S, s_gla_S = [jnp.stack([s[j] for s in sample_states]) for j in range(5)]
    return (y_prompt, y_sample, p_lru_conv, p_lru_h, p_gdn_conv, p_gdn_S, p_gla_S,
            s_lru_conv, s_lru_h, s_gdn_conv, s_gdn_S, s_gla_S)
```

```python
import functools

import numpy as np
import jax
import jax.numpy as jnp
from jax import lax
from jax.experimental import pallas as pl
from jax.experimental.pallas import tpu as pltpu

F32 = jnp.float32
BF16 = jnp.bfloat16

D_MODEL = 1024
BATCH = 8
SEQ = 2048
DEPTH = 2
DEC_BATCH = 128
DEC_SEQ = 4
CONV_W = 4
LRU_W = 512
LRU_BLOCKS = 8
LRU_C = 8.0
GDN_H = 4
GDN_DK = 64
GDN_DV = 64
GDN_CONV = GDN_H * (2 * GDN_DK + GDN_DV)
GLA_H = 4
GLA_DK = 32
GLA_DV = 64
GLA_RANK = 16
GLA_TAU = 16.0
GLA_CHUNK = 16
N_EXP = 32
TOP_K = 4
SW_LIMIT = 7.0
SW_ALPHA = 1.702
EPS = 1e-6

NP = BATCH * SEQ
NS = DEC_BATCH * DEC_SEQ
NTOK = NP + NS
TM = 512
NPT = NP // TM
NTILES = NTOK // TM
LANES = 128
MODB = 128

C_AX, C_AG, C_QKV, C_BZ = 0, 512, 1024, 1792
C_CQ, C_CK, C_CV, C_CR, C_SM = 2048, 2176, 2304, 2560, 2816
D_INP = 2944

TT = 64
BB = 32
TE = 256
NA = NTOK * TOP_K
NROWS = NA + N_EXP * TE
NTE = NROWS // TE

VMEM_LIMIT = 50 * 1024 * 1024


def _dot(a, b):
    return jnp.dot(a, b, preferred_element_type=F32)


def _dot_nt(a, b):
    return lax.dot_general(a, b, (((1,), (1,)), ((), ())), preferred_element_type=F32)


def _dot_tn(a, b):
    return lax.dot_general(a, b, (((0,), (0,)), ((), ())), preferred_element_type=F32)


def _mm(a, b):
    return _dot(a.astype(BF16), b.astype(BF16))


def _mm_nt(a, b):
    return _dot_nt(a.astype(BF16), b.astype(BF16))


def _mm_tn(a, b):
    return _dot_tn(a.astype(BF16), b.astype(BF16))


def _split3(x):
    x1 = x.astype(BF16)
    r = x - x1.astype(F32)
    x2 = r.astype(BF16)
    x3 = (r - x2.astype(F32)).astype(BF16)
    return x1, x2, x3


def _dot3(a, b):
    a1 = a.astype(BF16)
    a2 = (a - a1.astype(F32)).astype(BF16)
    b1 = b.astype(BF16)
    b2 = (b - b1.astype(F32)).astype(BF16)
    return _dot(a1, b1) + (_dot(a2, b1) + _dot(a1, b2))


def _dot_mask_l(mask, x):
    x1, x2, x3 = _split3(x)
    return _dot(mask, x1) + (_dot(mask, x2) + _dot(mask, x3))


def _dot_mask_r(x, mask):
    x1 = x.astype(BF16)
    x2 = (x - x1.astype(F32)).astype(BF16)
    return _dot(x1, mask) + _dot(x2, mask)


def _sigmoid(x):
    return jax.nn.sigmoid(x)


def _silu(x):
    return x * jax.nn.sigmoid(x)


def _softplus(x):
    return jnp.maximum(x, 0.0) + jnp.log1p(jnp.exp(-jnp.abs(x)))


def _rms_rows(x, g):
    return x * lax.rsqrt(jnp.mean(x * x, axis=-1, keepdims=True) + EPS) * g


def _modulate(y, scale, shift):
    rows = y.shape[0]
    y3 = y.reshape(rows // MODB, MODB, y.shape[1])
    return (y3 * (1.0 + scale) + shift).reshape(y.shape)


def _gate_res(x, gate, y):
    rows = y.shape[0]
    y3 = y.reshape(rows // MODB, MODB, y.shape[1])
    return x + (gate * y3).reshape(y.shape)


def _mod_kernel(c_ref, w_ref, b_ref, o_ref):
    o_ref[...] = _dot3(_silu(c_ref[...]), w_ref[...]) + b_ref[...]


def _mod_call(c_all, ada_w, ada_b):
    tn = 768
    rows = c_all.shape[0]
    return pl.pallas_call(
        _mod_kernel,
        grid=(DEPTH, 6 * D_MODEL // tn),
        in_specs=[
            pl.BlockSpec((rows, D_MODEL), lambda l, j: (0, 0)),
            pl.BlockSpec((None, D_MODEL, tn), lambda l, j: (l, 0, j)),
            pl.BlockSpec((None, 1, tn), lambda l, j: (l, 0, j)),
        ],
        out_specs=pl.BlockSpec((None, rows, tn), lambda l, j: (l, 0, j)),
        out_shape=jax.ShapeDtypeStruct((DEPTH, rows, 6 * D_MODEL), F32),
        compiler_params=pltpu.CompilerParams(
            dimension_semantics=("arbitrary", "arbitrary"), vmem_limit_bytes=VMEM_LIMIT),
        name="adaln_mod",
    )(c_all, ada_w, ada_b.reshape(DEPTH, 1, 6 * D_MODEL))


def _mod_spec(chunk):
    return pl.BlockSpec((None, MODB, D_MODEL), lambda i: (i // NPT, 0, chunk))


def _row_spec(width):
    return pl.BlockSpec((TM, width), lambda i: (i, 0))


def _full_spec(shape):
    nd = len(shape)
    return pl.BlockSpec(shape, lambda i: (0,) * nd)


def _in_kernel(has_res, *refs):
    if has_res:
        (x1_ref, moe_ref, g2_ref, n1_ref, sc_ref, sh_ref, w_ref, x_ref, proj_ref) = refs
        x = _gate_res(x1_ref[...], g2_ref[...], moe_ref[...])
        x_ref[...] = x
    else:
        (x_ref, n1_ref, sc_ref, sh_ref, w_ref, proj_ref) = refs
        x = x_ref[...]
    h = _modulate(_rms_rows(x, n1_ref[...]), sc_ref[...], sh_ref[...])
    proj_ref[...] = _dot(h.astype(BF16), w_ref[...])


def _in_call(x, moe, mod_l, norm_g, w_in_r, prev_mod):
    has_res = moe is not None
    ins, specs = [x], [_row_spec(D_MODEL)]
    outs = [jax.ShapeDtypeStruct((NTOK, D_INP), F32)]
    out_specs = [_row_spec(D_INP)]
    if has_res:
        ins += [moe, prev_mod]
        specs += [_row_spec(D_MODEL), _mod_spec(5)]
        outs = [jax.ShapeDtypeStruct((NTOK, D_MODEL), F32)] + outs
        out_specs = [_row_spec(D_MODEL)] + out_specs
    ins += [norm_g, mod_l, mod_l, w_in_r]
    specs += [_full_spec((1, D_MODEL)), _mod_spec(1), _mod_spec(0), _full_spec((D_MODEL, D_INP))]
    res = pl.pallas_call(
        functools.partial(_in_kernel, has_res),
        grid=(NTILES,),
        in_specs=specs,
        out_specs=out_specs,
        out_shape=outs,
        compiler_params=pltpu.CompilerParams(
            dimension_semantics=("arbitrary",), vmem_limit_bytes=VMEM_LIMIT),
        name="in_proj",
    )(*ins)
    return res if has_res else (x, res[0])


def _lru_gates(xc, wa_ref, ba_ref, wx_ref, bx_ref, lam_ref):
    xb = xc.astype(BF16)
    r = _sigmoid(_dot(xb, wa_ref[...]) + ba_ref[...])
    i = _sigmoid(_dot(xb, wx_ref[...]) + bx_ref[...])
    log_a = -LRU_C * r * _softplus(-lam_ref[...])
    a = jnp.exp(log_a)
    mult = jnp.sqrt(1.0 - jnp.exp(2.0 * log_a))
    return a, mult, i * xc


def _gelu(x):
    return jax.nn.gelu(x, approximate=True)


def _gdn_prep(qkv, small, gpar_ref, ones_blk):
    qkv = _silu(qkv)
    q = qkv[:, 0:256]
    k = qkv[:, 256:512]
    v = qkv[:, 512:768]
    q = q * lax.rsqrt(_dot_mask_r(q * q, ones_blk) + EPS) * (GDN_DK ** -0.5)
    k = k * lax.rsqrt(_dot_mask_r(k * k, ones_blk) + EPS)
    beta = _sigmoid(small)
    g = -jnp.exp(gpar_ref[0:1, :]) * _softplus(small + gpar_ref[1:2, :])
    lane = lax.broadcasted_iota(jnp.int32, small.shape, 1)
    bg = jnp.where(lane < GDN_H, beta, g)
    return q, k, v, bg


def _gla_prep(proj_q, proj_k, small, wg2_ref, gbg_ref):
    pre = _dot3(small, wg2_ref[...]) + gbg_ref[...]
    gk = -_softplus(-pre) / GLA_TAU
    return proj_q * (GLA_DK ** -0.5), proj_k, gk


def _head_norm_gate(o, norm_g, gate_in, ones_blk):
    ms = _dot_mask_r(o * o, ones_blk) * (1.0 / GDN_DV)
    return o * lax.rsqrt(ms + EPS) * norm_g * _silu(gate_in)


def _mix_prompt_kernel(proj_ref, lcw_ref, lcb_ref, wa_ref, ba_ref, wx_ref, bx_ref, lam_ref,
                       gcw_ref, gpar_ref, gng_ref, wg2_ref, gbg_ref, lng_ref,
                       ones_ref, tri_ref, bt16_ref, blk16_ref,
                       mix_ref, lconv_ref, lh_ref, gconv_ref, gs_ref, lst_ref,
                       exta, extb, a_s, u_s, hs_s, q_s, k_s, v_s, bg_s, og_s,
                       q2_s, k2_s, v2_s, gk_s, ol_s):
    nb = BATCH
    rows = TT * nb
    step = pl.program_id(0)

    @pl.when(step == 0)
    def _():
        exta[0:3 * nb, :] = jnp.zeros((3 * nb, LRU_W), F32)
        extb[0:3 * nb, :] = jnp.zeros((3 * nb, GDN_CONV), F32)
        lh_ref[...] = jnp.zeros_like(lh_ref)
        gs_ref[...] = jnp.zeros_like(gs_ref)
        lst_ref[...] = jnp.zeros_like(lst_ref)

    def conv(ext, x, w_ref):
        ext[pl.ds(3 * nb, rows), :] = x
        y = ext[pl.ds(0, rows), :] * w_ref[0:1, :]
        for j in range(1, CONV_W):
            y = y + ext[pl.ds(j * nb, rows), :] * w_ref[j:j + 1, :]
        tail = ext[pl.ds(rows, 3 * nb), :]
        ext[pl.ds(0, 3 * nb), :] = tail
        return y, tail

    ones_blk = ones_ref[...]

    xa, tail = conv(exta, proj_ref[:, C_AX:C_AX + LRU_W], lcw_ref)
    lconv_ref[...] = tail
    xc = xa + lcb_ref[...]
    a, mult, ix = _lru_gates(xc, wa_ref, ba_ref, wx_ref, bx_ref, lam_ref)
    rid = lax.broadcasted_iota(jnp.int32, (rows, LRU_W), 0)
    mult = jnp.where((rid < nb) & (step == 0), 1.0, mult)
    a_s[...] = a
    u_s[...] = mult * ix

    def scan_body(t, h):
        off = pl.multiple_of(t * nb, nb)
        h = a_s[pl.ds(off, nb), :] * h + u_s[pl.ds(off, nb), :]
        hs_s[pl.ds(off, nb), :] = h
        return h

    lh_ref[...] = lax.fori_loop(0, TT, scan_body, lh_ref[...], unroll=8)
    mix_ref[:, 0:LRU_W] = (hs_s[...] * _gelu(proj_ref[:, C_AG:C_AG + LRU_W])).astype(BF16)

    small = proj_ref[:, C_SM:C_SM + LANES]
    qkv, tail = conv(extb, proj_ref[:, C_QKV:C_QKV + GDN_CONV], gcw_ref)
    gconv_ref[...] = tail
    q, k, v, bg = _gdn_prep(qkv, small, gpar_ref, ones_blk)
    for j in range(2):
        ls = slice(j * LANES, (j + 1) * LANES)
        q_s[j] = q[:, ls]
        k_s[j] = k[:, ls]
        v_s[j] = v[:, ls]
    bg_s[...] = bg

    q2, k2, gk = _gla_prep(proj_ref[:, C_CQ:C_CQ + 128], proj_ref[:, C_CK:C_CK + 128],
                           small, wg2_ref, gbg_ref)
    q2_s[...] = q2
    k2_s[...] = k2
    gk_s[...] = gk
    for j in range(2):
        v2_s[j] = proj_ref[:, C_CV + j * LANES:C_CV + (j + 1) * LANES]

    ri = lax.broadcasted_iota(jnp.int32, (TT, TT), 0)
    ci = lax.broadcasted_iota(jnp.int32, (TT, TT), 1)
    incl = ri >= ci
    strict = ri > ci
    eye = ri == ci
    blk16m = (ri // 16) == (ci // 16)
    blk32m = (ri // 32) == (ci // 32)
    blk_causal = incl & ((ri // GLA_CHUNK) == (ci // GLA_CHUNK))
    tri = tri_ref[...]
    bt16 = bt16_ref[...]
    blk16 = blk16_ref[...]
    ones8 = jnp.ones((8, TT), BF16)

    def seq_body(b, carry):
        rsel = pl.ds(b, TT, stride=nb)
        bgb = bg_s[rsel, :]
        gc_all = _dot_mask_l(tri, bgb)
        qb_ = [q_s[j, rsel, :] for j in range(2)]
        kb_ = [k_s[j, rsel, :] for j in range(2)]
        vb_ = [v_s[j, rsel, :] for j in range(2)]
        o_heads = []
        for h in range(GDN_H):
            hs = slice((h % 2) * GDN_DK, (h % 2 + 1) * GDN_DK)
            qh = qb_[h // 2][:, hs]
            kh = kb_[h // 2][:, hs]
            vh = vb_[h // 2][:, hs]
            beta = bgb[:, h:h + 1]
            gc = gc_all[:, GDN_H + h:GDN_H + h + 1]
            gc_row = _dot_mask_l(ones8, jnp.where(eye, gc, 0.0))[0:1, :]
            decay = jnp.exp(jnp.where(incl, gc - gc_row, -jnp.inf))
            kb = kh * beta
            amat = jnp.where(strict, _mm_nt(kb, kh) * decay, 0.0)
            x = -jnp.where(blk16m, amat, 0.0)
            n = x
            y = x
            for _ in range(3):
                y = _mm(y, y)
                n = n + y + _mm(n, y)
            t = jnp.where(eye, 1.0, 0.0) + n
            t = t - _mm(_mm(t, jnp.where(blk32m & ~blk16m, amat, 0.0)), t)
            t = t - _mm(_mm(t, jnp.where(blk32m, 0.0, amat)), t)
            uw = _mm(t, jnp.concatenate([vh * beta, kb * jnp.exp(gc)], axis=1))
            u = uw[:, 0:GDN_DV]
            w = uw[:, GDN_DV:]
            qk = jnp.where(incl, _mm_nt(qh, kh) * decay, 0.0)
            s = gs_ref[b, h]
            ws = _mm(jnp.concatenate([w, qh * jnp.exp(gc)], axis=0), s)
            vn = u - ws[0:TT]
            o = ws[TT:] + _mm(qk, vn)
            g_last = gc[TT - 1:TT, :]
            gs_ref[b, h] = s * jnp.exp(g_last) + _mm_tn(kh * jnp.exp(g_last - gc), vn)
            o_heads.append(o)
        for j in range(2):
            og_s[j, rsel, :] = jnp.concatenate(o_heads[2 * j:2 * j + 2], axis=1)
        gkb = gk_s[rsel, :]
        bc = _dot_mask_l(bt16, gkb)
        bl = _dot_mask_l(blk16, gkb)
        qb = q2_s[rsel, :]
        kb2 = k2_s[rsel, :]
        vb = jnp.concatenate([v2_s[j, rsel, :] for j in range(2)], axis=1)
        o_heads = []
        qi = qb * jnp.exp(bc)
        ki = kb2 * jnp.exp(-bc)
        kst = kb2 * jnp.exp(bl - bc)
        ebl = jnp.exp(bl)
        for h in range(GLA_H):
            ls = slice(h * GLA_DK, (h + 1) * GLA_DK)
            vs = slice(h * GLA_DV, (h + 1) * GLA_DV)
            sc = jnp.where(blk_causal, _mm_nt(qi[:, ls], ki[:, ls]), 0.0)
            oh = _mm(sc, vb[:, vs])
            st = lst_ref[b, h]
            outs = []
            for c in range(TT // GLA_CHUNK):
                rs = slice(c * GLA_CHUNK, (c + 1) * GLA_CHUNK)
                outs.append(oh[rs] + _mm_nt(qi[rs, ls], st))
                st = st * ebl[c * GLA_CHUNK:c * GLA_CHUNK + 1, ls] + _mm_tn(vb[rs, vs], kst[rs, ls])
            lst_ref[b, h] = st
            o_heads.append(jnp.concatenate(outs, axis=0))
        for j in range(2):
            ol_s[j, rsel, :] = jnp.concatenate(o_heads[2 * j:2 * j + 2], axis=1)
        return carry

    lax.fori_loop(0, nb, seq_body, 0)

    og = jnp.concatenate([og_s[0], og_s[1]], axis=1)
    out_b = _head_norm_gate(og, gng_ref[...], proj_ref[:, C_BZ:C_BZ + 256], ones_blk)
    mix_ref[:, 512:768] = out_b.astype(BF16)
    ol = jnp.concatenate([ol_s[0], ol_s[1]], axis=1)
    out_c = _head_norm_gate(ol, lng_ref[...], proj_ref[:, C_CR:C_CR + 256], ones_blk)
    mix_ref[:, 768:1024] = out_c.astype(BF16)


def _mix_consts():
    r = np.arange(TT)
    tri = (r[:, None] >= r[None, :]).astype(np.float32)
    same = (r[:, None] // GLA_CHUNK) == (r[None, :] // GLA_CHUNK)
    bt16 = (tri.astype(bool) & same).astype(np.float32)
    blk16 = same.astype(np.float32)
    c = np.arange(256)
    ones_blk = ((c[:, None] // GDN_DV) == (c[None, :] // GDN_DV)).astype(np.float32)
    return (jnp.asarray(ones_blk, BF16), jnp.asarray(tri, BF16),
            jnp.asarray(bt16, BF16), jnp.asarray(blk16, BF16))


def _mix_prompt_call(proj, mp):
    rows = TT * BATCH
    consts = _mix_consts()
    params = (mp["lcw"], mp["lcb"], mp["wa"], mp["ba"], mp["wx"], mp["bx"], mp["lam"],
              mp["gcw"], mp["gpar"], mp["gng"], mp["wg2"], mp["gbg"], mp["lng"]) + consts
    out_shape = [
        jax.ShapeDtypeStruct((NP, D_MODEL), BF16),
        jax.ShapeDtypeStruct((3 * BATCH, LRU_W), F32),
        jax.ShapeDtypeStruct((BATCH, LRU_W), F32),
        jax.ShapeDtypeStruct((3 * BATCH, GDN_CONV), F32),
        jax.ShapeDtypeStruct((BATCH, GDN_H, GDN_DK, GDN_DV), F32),
        jax.ShapeDtypeStruct((BATCH, GLA_H, GLA_DV, GLA_DK), F32),
    ]
    out_specs = [pl.BlockSpec((rows, D_MODEL), lambda i: (i, 0))] + [
        _full_spec(s.shape) for s in out_shape[1:]]
    scratch = [
        pltpu.VMEM((rows + 3 * BATCH, LRU_W), F32),
        pltpu.VMEM((rows + 3 * BATCH, GDN_CONV), F32),
        pltpu.VMEM((rows, LRU_W), F32), pltpu.VMEM((rows, LRU_W), F32), pltpu.VMEM((rows, LRU_W), F32),
        pltpu.VMEM((2, rows, LANES), F32), pltpu.VMEM((2, rows, LANES), F32), pltpu.VMEM((2, rows, LANES), F32),
        pltpu.VMEM((rows, LANES), F32), pltpu.VMEM((2, rows, LANES), F32),
        pltpu.VMEM((rows, LANES), F32), pltpu.VMEM((rows, LANES), F32), pltpu.VMEM((2, rows, LANES), F32),
        pltpu.VMEM((rows, LANES), F32), pltpu.VMEM((2, rows, LANES), F32),
    ]
    return pl.pallas_call(
        _mix_prompt_kernel,
        grid=(SEQ // TT,),
        in_specs=[pl.BlockSpec((rows, D_INP), lambda i: (i, 0))] + [_full_spec(p.shape) for p in params],
        out_specs=out_specs,
        out_shape=out_shape,
        scratch_shapes=scratch,
        compiler_params=pltpu.CompilerParams(
            dimension_semantics=("arbitrary",), vmem_limit_bytes=VMEM_LIMIT),
        name="mix_prompt",
    )(proj, *params)


def _pair_bcast(x, p, lo_mask):
    return jnp.where(lo_mask, x[:, 2 * p:2 * p + 1], x[:, 2 * p + 1:2 * p + 2])


def _fold_pairs(acc):
    return acc[:, 0:64] + acc[:, 64:128]


def _mix_sample_kernel(p0_ref, p1_ref, p2_ref, p3_ref, lconv_in, lh_in, gconv_in, gs_in, ls_in,
                       lcw_ref, lcb_ref, wa_ref, ba_ref, wx_ref, bx_ref, lam_ref,
                       gcw_ref, gpar_ref, gng_ref, wg2_ref, gbg_ref, lng_ref, ones_ref,
                       mix_ref, lconv_ref, lh_ref, gconv_ref, gs_ref, ls_ref):
    rows = DEC_SEQ * BB
    prefs = (p0_ref, p1_ref, p2_ref, p3_ref)
    ones_blk = ones_ref[...]

    def cols(c0, width):
        return [p[:, c0:c0 + width] for p in prefs]

    def conv(prev, xs, w_ref):
        ext = [prev[j] for j in range(CONV_W - 1)] + xs
        ys = []
        for t in range(DEC_SEQ):
            y = ext[t] * w_ref[0:1, :]
            for j in range(1, CONV_W):
                y = y + ext[t + j] * w_ref[j:j + 1, :]
            ys.append(y)
        return jnp.concatenate(ys, axis=0), ext[DEC_SEQ:]

    def rows_of(x, t):
        return x[t * BB:(t + 1) * BB]

    xa, tail = conv(lconv_in, cols(C_AX, LRU_W), lcw_ref)
    for j in range(CONV_W - 1):
        lconv_ref[j] = tail[j]
    xc = xa + lcb_ref[...]
    a, mult, ix = _lru_gates(xc, wa_ref, ba_ref, wx_ref, bx_ref, lam_ref)
    u = mult * ix
    h = lh_in[...]
    hs = []
    for t in range(DEC_SEQ):
        h = rows_of(a, t) * h + rows_of(u, t)
        hs.append(h)
    lh_ref[...] = h
    ag = jnp.concatenate(cols(C_AG, LRU_W), axis=0)
    mix_a = jnp.concatenate(hs, axis=0) * _gelu(ag)

    small = jnp.concatenate(cols(C_SM, LANES), axis=0)
    qkv, tail = conv(gconv_in, cols(C_QKV, GDN_CONV), gcw_ref)
    for j in range(CONV_W - 1):
        gconv_ref[j] = tail[j]
    q, k, v, bg = _gdn_prep(qkv, small, gpar_ref, ones_blk)
    lo_mask = lax.broadcasted_iota(jnp.int32, (BB, LANES), 1) < 64
    gs_ref[...] = gs_in[...]
    hd = GDN_DK * GDN_DV
    o_heads = []
    for h in range(GDN_H):
        o_t = []
        for t in range(DEC_SEQ):
            rs = slice(t * BB, (t + 1) * BB)
            cs = slice(h * GDN_DK, (h + 1) * GDN_DK)
            eg = jnp.exp(bg[rs, GDN_H + h:GDN_H + h + 1])
            beta = bg[rs, h:h + 1]
            kt, qt, vt = k[rs, cs], q[rs, cs], v[rs, cs]
            kks = [_pair_bcast(kt, p, lo_mask) for p in range(GDN_DK // 2)]
            acc = jnp.zeros((BB, LANES), F32)
            for p in range(GDN_DK // 2):
                acc = acc + gs_ref[:, h * hd + p * LANES:h * hd + (p + 1) * LANES] * kks[p]
            vn = beta * (vt - eg * _fold_pairs(acc))
            vn2 = jnp.concatenate([vn, vn], axis=1)
            oacc = jnp.zeros((BB, LANES), F32)
            for p in range(GDN_DK // 2):
                sl = slice(h * hd + p * LANES, h * hd + (p + 1) * LANES)
                s = eg * gs_ref[:, sl] + kks[p] * vn2
                gs_ref[:, sl] = s
                oacc = oacc + s * _pair_bcast(qt, p, lo_mask)
            o_t.append(_fold_pairs(oacc))
        o_heads.append(jnp.concatenate(o_t, axis=0))
    o_b = jnp.concatenate(o_heads, axis=1)
    bz = jnp.concatenate(cols(C_BZ, 256), axis=0)
    mix_b = _head_norm_gate(o_b, gng_ref[...], bz, ones_blk)

    q2, k2, gk = _gla_prep(jnp.concatenate(cols(C_CQ, 128), axis=0),
                           jnp.concatenate(cols(C_CK, 128), axis=0), small, wg2_ref, gbg_ref)
    v2 = jnp.concatenate(cols(C_CV, 256), axis=0)
    ls_ref[...] = ls_in[...]
    hd2 = GLA_DK * GLA_DV
    o_heads = []
    for h in range(GLA_H):
        o_t = []
        for t in range(DEC_SEQ):
            rs = slice(t * BB, (t + 1) * BB)
            cs = slice(h * GLA_DK, (h + 1) * GLA_DK)
            al = jnp.exp(gk[rs, cs])
            kt, qt = k2[rs, cs], q2[rs, cs]
            vt = v2[rs, h * GLA_DV:(h + 1) * GLA_DV]
            v2x = jnp.concatenate([vt, vt], axis=1)
            oacc = jnp.zeros((BB, LANES), F32)
            for p in range(GLA_DK // 2):
                sl = slice(h * hd2 + p * LANES, h * hd2 + (p + 1) * LANES)
                s = _pair_bcast(al, p, lo_mask) * ls_ref[:, sl] + _pair_bcast(kt, p, lo_mask) * v2x
                ls_ref[:, sl] = s
                oacc = oacc + s * _pair_bcast(qt, p, lo_mask)
            o_t.append(_fold_pairs(oacc))
        o_heads.append(jnp.concatenate(o_t, axis=0))
    o_c = jnp.concatenate(o_heads, axis=1)
    cr = jnp.concatenate(cols(C_CR, 256), axis=0)
    mix_c = _head_norm_gate(o_c, lng_ref[...], cr, ones_blk)

    mix = jnp.concatenate([mix_a, mix_b, mix_c], axis=1).astype(BF16)
    for t in range(DEC_SEQ):
        mix_ref[t] = mix[t * BB:(t + 1) * BB]


def _mix_sample_call(proj, states, mp):
    lconv, lh, gconv, gs, ls = states
    ones_blk = _mix_consts()[0]
    params = (mp["lcw"], mp["lcb"], mp["wa"], mp["ba"], mp["wx"], mp["bx"], mp["lam"],
              mp["gcw"], mp["gpar"], mp["gng"], mp["wg2"], mp["gbg"], mp["lng"], ones_blk)
    nblk = DEC_BATCH // BB
    base = NP // BB

    def proj_spec(t):
        return pl.BlockSpec((BB, D_INP), lambda j: (base + t * nblk + j, 0))

    def bspec3(n, width):
        return pl.BlockSpec((n, BB, width), lambda j: (0, j, 0))

    def bspec2(width):
        return pl.BlockSpec((BB, width), lambda j: (j, 0))

    gdn_flat = GDN_H * GDN_DK * GDN_DV
    gla_flat = GLA_H * GLA_DK * GLA_DV
    state_specs = [bspec3(3, LRU_W), bspec2(LRU_W), bspec3(3, GDN_CONV), bspec2(gdn_flat), bspec2(gla_flat)]
    out_shape = [
        jax.ShapeDtypeStruct((DEC_SEQ, DEC_BATCH, D_MODEL), BF16),
        jax.ShapeDtypeStruct((3, DEC_BATCH, LRU_W), F32),
        jax.ShapeDtypeStruct((DEC_BATCH, LRU_W), F32),
        jax.ShapeDtypeStruct((3, DEC_BATCH, GDN_CONV), F32),
        jax.ShapeDtypeStruct((DEC_BATCH, gdn_flat), F32),
        jax.ShapeDtypeStruct((DEC_BATCH, gla_flat), F32),
    ]
    return pl.pallas_call(
        _mix_sample_kernel,
        grid=(nblk,),
        in_specs=[proj_spec(t) for t in range(DEC_SEQ)] + state_specs + [_full_spec(p.shape) for p in params],
        out_specs=[bspec3(DEC_SEQ, D_MODEL)] + state_specs,
        out_shape=out_shape,
        compiler_params=pltpu.CompilerParams(
            dimension_semantics=("arbitrary",), vmem_limit_bytes=VMEM_LIMIT),
        name="mix_sample",
    )(proj, proj, proj, proj, lconv, lh, gconv, gs, ls, *params)


def _out_kernel(mp_ref, ms_ref, x_ref, w_ref, g1_ref, n2_ref, sc_ref, sh_ref, rw_ref, rb_ref,
                x1_ref, h2_ref, ridx_ref, rprob_ref):
    step = pl.program_id(0)
    mix = jnp.where(step < NPT, mp_ref[...], ms_ref[...])
    x1 = _gate_res(x_ref[...], g1_ref[...], _dot(mix, w_ref[...]))
    x1_ref[...] = x1
    h2 = _modulate(_rms_rows(x1, n2_ref[...]), sc_ref[...], sh_ref[...])
    h2_ref[...] = h2.astype(BF16)
    logits = _dot3(h2, rw_ref[...]) + rb_ref[...]
    lane = lax.broadcasted_iota(jnp.int32, logits.shape, 1)
    cur = jnp.where(lane < N_EXP, logits, -jnp.inf)
    vals, idxs = [], []
    for _ in range(TOP_K):
        m = jnp.max(cur, axis=-1, keepdims=True)
        idx = jnp.min(jnp.where(cur == m, lane, LANES), axis=-1, keepdims=True)
        vals.append(m)
        idxs.append(idx)
        cur = jnp.where(lane == idx, -jnp.inf, cur)
    es = [jnp.exp(v - vals[0]) for v in vals]
    den = es[0] + es[1] + es[2] + es[3]
    ridx = jnp.zeros(logits.shape, jnp.int32)
    rprob = jnp.zeros(logits.shape, F32)
    for j in range(TOP_K):
        ridx = jnp.where(lane == j, idxs[j], ridx)
        rprob = jnp.where(lane == j, es[j] / den, rprob)
    ridx_ref[...] = ridx
    rprob_ref[...] = rprob


def _out_call(mix_p, mix_s, x, mod_l, w_out, norm_g, rw, rb):
    return pl.pallas_call(
        _out_kernel,
        grid=(NTILES,),
        in_specs=[
            pl.BlockSpec((TM, D_MODEL), lambda i: (jnp.minimum(i, NPT - 1), 0)),
            _full_spec((NS, D_MODEL)),
            _row_spec(D_MODEL),
            _full_spec((D_MODEL, D_MODEL)),
            _mod_spec(2),
            _full_spec((1, D_MODEL)),
            _mod_spec(4),
            _mod_spec(3),
            _full_spec((D_MODEL, LANES)),
            _full_spec((1, LANES)),
        ],
        out_specs=[_row_spec(D_MODEL), _row_spec(D_MODEL), _row_spec(LANES), _row_spec(LANES)],
        out_shape=[
            jax.ShapeDtypeStruct((NTOK, D_MODEL), F32),
            jax.ShapeDtypeStruct((NTOK, D_MODEL), BF16),
            jax.ShapeDtypeStruct((NTOK, LANES), jnp.int32),
            jax.ShapeDtypeStruct((NTOK, LANES), F32),
        ],
        compiler_params=pltpu.CompilerParams(
            dimension_semantics=("arbitrary",), vmem_limit_bytes=VMEM_LIMIT),
        name="out_proj_router",
    )(mix_p, mix_s, x, w_out, mod_l, norm_g, mod_l, mod_l, rw, rb)


def _ffn_kernel(te_ref, nu_ref, x_ref, wg_ref, bg_ref, wu_ref, bu_ref, wd_ref, bd_ref, rw_ref,
                o_ref, wg_s, wu_s, wd_s):
    i = pl.program_id(0)
    e = te_ref[i]
    prev = te_ref[jnp.maximum(i - 1, 0)]

    @pl.when((i == 0) | (prev != e))
    def _():
        wg_s[...] = wg_ref[...].astype(BF16)
        wu_s[...] = wu_ref[...].astype(BF16)
        wd_s[...] = wd_ref[...].astype(BF16)

    @pl.when(i < nu_ref[0])
    def _():
        x = x_ref[...]
        gate = jnp.minimum(_dot(x, wg_s[...]) + bg_ref[...], SW_LIMIT)
        up = jnp.clip(_dot(x, wu_s[...]) + bu_ref[...], -SW_LIMIT, SW_LIMIT)
        act = (up + 1.0) * gate * _sigmoid(SW_ALPHA * gate)
        y = _dot(act.astype(BF16), wd_s[...]) + bd_ref[...]
        o_ref[...] = y * rw_ref[...]

    @pl.when(i >= nu_ref[0])
    def _():
        o_ref[...] = jnp.zeros_like(o_ref)


def _ffn_call(layer, tile_e, n_used, xs, row_w, wg, bg, wu, bu, wd, bd):
    wspec = pl.BlockSpec((None, None, D_MODEL, D_MODEL), lambda i, te, nu: (layer, te[i], 0, 0))
    bspec = pl.BlockSpec((None, None, 1, D_MODEL), lambda i, te, nu: (layer, te[i], 0, 0))
    grid_spec = pltpu.PrefetchScalarGridSpec(
        num_scalar_prefetch=2,
        grid=(NTE,),
        in_specs=[
            pl.BlockSpec((TE, D_MODEL), lambda i, te, nu: (i, 0)),
            wspec, bspec, wspec, bspec, wspec, bspec,
            pl.BlockSpec((TE, 1), lambda i, te, nu: (i, 0)),
        ],
        out_specs=pl.BlockSpec((TE, D_MODEL), lambda i, te, nu: (i, 0)),
        scratch_shapes=[pltpu.VMEM((D_MODEL, D_MODEL), BF16)] * 3,
    )
    b4 = lambda b: b.reshape(DEPTH, N_EXP, 1, D_MODEL)
    return pl.pallas_call(
        _ffn_kernel,
        grid_spec=grid_spec,
        out_shape=jax.ShapeDtypeStruct((NROWS, D_MODEL), F32),
        compiler_params=pltpu.CompilerParams(
            dimension_semantics=("arbitrary",), vmem_limit_bytes=VMEM_LIMIT),
        name="expert_ffn",
    )(tile_e, n_used, xs, wg, b4(bg), wu, b4(bu), wd, b4(bd), row_w)


def _moe(layer, h2, ridx, rprob, wg, bg, wu, bu, wd, bd):
    eid = ridx[:, :TOP_K].reshape(NA)
    prob = rprob[:, :TOP_K].reshape(NA)
    onehot = (eid[:, None] == jnp.arange(N_EXP, dtype=jnp.int32)[None, :]).astype(jnp.int32)
    csum = jnp.cumsum(onehot, axis=0)
    rank = jnp.take_along_axis(csum, eid[:, None], axis=1)[:, 0] - 1
    counts = csum[-1]
    padded = ((counts + TE - 1) // TE) * TE
    pend = jnp.cumsum(padded)
    pstart = pend - padded
    dest = pstart[eid] + rank
    row_token = jnp.zeros((NROWS,), jnp.int32).at[dest].set(jnp.arange(NA, dtype=jnp.int32) // TOP_K)
    row_w = jnp.zeros((NROWS,), F32).at[dest].set(prob)
    tile_e = jnp.minimum(
        jnp.searchsorted(pend, jnp.arange(NTE, dtype=jnp.int32) * TE, side="right"), N_EXP - 1
    ).astype(jnp.int32)
    n_used = (pend[-1:] // TE).astype(jnp.int32)
    xs = jnp.take(h2, row_token, axis=0)
    ys = _ffn_call(layer, tile_e, n_used, xs, row_w.reshape(NROWS, 1), wg, bg, wu, bu, wd, bd)
    return jnp.take(ys, dest, axis=0).reshape(NTOK, TOP_K, D_MODEL).sum(axis=1)


def _final_kernel(x1_ref, moe_ref, g2_ref, ng_ref, y_ref):
    x = _gate_res(x1_ref[...], g2_ref[...], moe_ref[...])
    y_ref[...] = _rms_rows(x, ng_ref[...])


def _final_call(x1, moe, mod_l, norm_g):
    return pl.pallas_call(
        _final_kernel,
        grid=(NTILES,),
        in_specs=[_row_spec(D_MODEL), _row_spec(D_MODEL), _mod_spec(5), _full_spec((1, D_MODEL))],
        out_specs=_row_spec(D_MODEL),
        out_shape=jax.ShapeDtypeStruct((NTOK, D_MODEL), F32),
        compiler_params=pltpu.CompilerParams(
            dimension_semantics=("arbitrary",), vmem_limit_bytes=VMEM_LIMIT),
        name="final_norm",
    )(x1, moe, mod_l, norm_g)


def _block_diag(w):
    n, d, e = w.shape
    eye = jnp.eye(n, dtype=w.dtype)
    return (eye[:, None, :, None] * w[:, :, None, :]).reshape(n * d, n * e)


def _pad_lanes(v, offset):
    out = jnp.zeros((1, LANES), F32)
    return out.at[0, offset:offset + v.shape[0]].set(v)


def _mixer_params(l, w_in, lru_conv_w, lru_conv_b, lru_wa, lru_ba, lru_wx, lru_bx, lru_lambda,
                  gdn_conv_w, gdn_a_log, gdn_dt_bias, gdn_norm_g, gla_wg2, gla_bg, gla_norm_g):
    w = w_in[l]
    w_in_r = jnp.concatenate(
        [w[:, 0:2048], w[:, 2056:2824], w[:, 2048:2056], w[:, 2824:2840],
         jnp.zeros((D_MODEL, D_INP - 2840), F32)], axis=1).astype(BF16)
    row = lambda v: v.reshape(1, -1)
    mp = dict(
        lcw=lru_conv_w[l], lcb=row(lru_conv_b[l]),
        wa=_block_diag(lru_wa[l]).astype(BF16), ba=row(lru_ba[l]),
        wx=_block_diag(lru_wx[l]).astype(BF16), bx=row(lru_bx[l]),
        lam=row(lru_lambda[l]),
        gcw=gdn_conv_w[l],
        gpar=jnp.concatenate([_pad_lanes(gdn_a_log[l], GDN_H), _pad_lanes(gdn_dt_bias[l], GDN_H)], axis=0),
        gng=row(jnp.tile(gdn_norm_g[l], GDN_H)),
        wg2=jnp.zeros((LANES, LANES), F32).at[2 * GDN_H:2 * GDN_H + GLA_RANK].set(gla_wg2[l]),
        gbg=row(gla_bg[l]),
        lng=row(jnp.tile(gla_norm_g[l], GLA_H)),
    )
    return w_in_r, mp


def kernel(x_prompt, x_sample, state_lru_conv, state_lru_h, state_gdn_conv, state_gdn_S, state_gla_S, c_prompt, c_sample, ada_w, ada_b, norm1_g, norm2_g, w_in, lru_conv_w, lru_conv_b, lru_wa, lru_ba, lru_wx, lru_bx, lru_lambda, gdn_conv_w, gdn_a_log, gdn_dt_bias, gdn_norm_g, gla_wg2, gla_bg, gla_norm_g, w_out, router_w, router_b, exp_w_gate, exp_b_gate, exp_w_up, exp_b_up, exp_w_down, exp_b_down, final_norm_g):
    x = jnp.concatenate([
        x_prompt.transpose(1, 0, 2).reshape(NP, D_MODEL),
        x_sample.transpose(1, 0, 2).reshape(NS, D_MODEL)], axis=0)
    mod = _mod_call(jnp.concatenate([c_prompt, c_sample], axis=0), ada_w, ada_b)
    mod = jnp.stack([jnp.tile(mod[:, :BATCH], (1, MODB // BATCH, 1)), mod[:, BATCH:]], axis=1)

    p_states, s_states = [], []
    moe = None
    for l in range(DEPTH):
        w_in_r, mp = _mixer_params(l, w_in, lru_conv_w, lru_conv_b, lru_wa, lru_ba, lru_wx, lru_bx,
                                   lru_lambda, gdn_conv_w, gdn_a_log, gdn_dt_bias, gdn_norm_g,
                                   gla_wg2, gla_bg, gla_norm_g)
        x, proj = _in_call(x, moe, mod[l], norm1_g[l].reshape(1, D_MODEL), w_in_r,
                           mod[l - 1] if l else None)
        mix_p, p_lconv, p_lh, p_gconv, p_gs, p_lst = _mix_prompt_call(proj, mp)
        states = (state_lru_conv[l].transpose(1, 0, 2), state_lru_h[l],
                  state_gdn_conv[l].transpose(1, 0, 2),
                  state_gdn_S[l].reshape(DEC_BATCH, -1), state_gla_S[l].reshape(DEC_BATCH, -1))
        mix_s, s_lconv, s_lh, s_gconv, s_gs, s_ls = _mix_sample_call(proj, states, mp)
        rw = jnp.zeros((D_MODEL, LANES), F32).at[:, :N_EXP].set(router_w[l])
        rb = jnp.zeros((1, LANES), F32).at[0, :N_EXP].set(router_b[l])
        x, h2, ridx, rprob = _out_call(mix_p, mix_s.reshape(NS, D_MODEL), x, mod[l],
                                       w_out[l].astype(BF16), norm2_g[l].reshape(1, D_MODEL), rw, rb)
        moe = _moe(l, h2, ridx, rprob, exp_w_gate, exp_b_gate, exp_w_up, exp_b_up, exp_w_down, exp_b_down)
        p_states.append((p_lconv.reshape(3, BATCH, LRU_W).transpose(1, 0, 2), p_lh,
                         p_gconv.reshape(3, BATCH, GDN_CONV).transpose(1, 0, 2), p_gs,
                         p_lst.transpose(0, 1, 3, 2)))
        s_states.append((s_lconv.transpose(1, 0, 2), s_lh, s_gconv.transpose(1, 0, 2),
                         s_gs.reshape(DEC_BATCH, GDN_H, GDN_DK, GDN_DV),
                         s_ls.reshape(DEC_BATCH, GLA_H, GLA_DK, GLA_DV)))
    y = _final_call(x, moe, mod[DEPTH - 1], final_norm_g.reshape(1, D_MODEL))
    y_prompt = y[:NP].reshape(SEQ, BATCH, D_MODEL).transpose(1, 0, 2)
    y_sample = y[NP:].reshape(DEC_SEQ, DEC_BATCH, D_MODEL).transpose(1, 0, 2)
    ps = [jnp.stack([s[j] for s in p_states]) for j in range(5)]
    ss = [jnp.stack([s[j] for s in s_states]) for j in range(5)]
    return (y_prompt, y_sample, *ps, *ss)
```

```python
import functools

import numpy as np
import jax
import jax.numpy as jnp
from jax import lax
from jax.experimental import pallas as pl
from jax.experimental.pallas import tpu as pltpu

F32 = jnp.float32
BF16 = jnp.bfloat16

D_MODEL = 1024
BATCH = 8
SEQ = 2048
DEPTH = 2
DEC_BATCH = 128
DEC_SEQ = 4
CONV_W = 4
LRU_W = 512
LRU_BLOCKS = 8
LRU_C = 8.0
GDN_H = 4
GDN_DK = 64
GDN_DV = 64
GDN_CONV = GDN_H * (2 * GDN_DK + GDN_DV)
GLA_H = 4
GLA_DK = 32
GLA_DV = 64
GLA_RANK = 16
GLA_TAU = 16.0
GLA_CHUNK = 16
N_EXP = 32
TOP_K = 4
SW_LIMIT = 7.0
SW_ALPHA = 1.702
EPS = 1e-6

NP = BATCH * SEQ
NS = DEC_BATCH * DEC_SEQ
NTOK = NP + NS
TM = 512
NPT = NP // TM
NTILES = NTOK // TM
LANES = 128
MODB = 128

C_AX, C_AG, C_QKV, C_BZ = 0, 512, 1024, 1792
C_CQ, C_CK, C_CV, C_CR, C_SM = 2048, 2176, 2304, 2560, 2816
D_INP = 2944

TT = 64
BB = 32
TE = 256
TD = 256
NA = NTOK * TOP_K
NROWS = NA + N_EXP * TE
NTE = NROWS // TE

VMEM_LIMIT = 50 * 1024 * 1024


def _dot(a, b):
    return jnp.dot(a, b, preferred_element_type=F32)


def _dot_nt(a, b):
    return lax.dot_general(a, b, (((1,), (1,)), ((), ())), preferred_element_type=F32)


def _dot_tn(a, b):
    return lax.dot_general(a, b, (((0,), (0,)), ((), ())), preferred_element_type=F32)


def _mm(a, b):
    return _dot(a.astype(BF16), b.astype(BF16))


def _mm_nt(a, b):
    return _dot_nt(a.astype(BF16), b.astype(BF16))


def _mm_tn(a, b):
    return _dot_tn(a.astype(BF16), b.astype(BF16))


def _split3(x):
    x1 = x.astype(BF16)
    r = x - x1.astype(F32)
    x2 = r.astype(BF16)
    x3 = (r - x2.astype(F32)).astype(BF16)
    return x1, x2, x3


def _dot3(a, b):
    a1 = a.astype(BF16)
    a2 = (a - a1.astype(F32)).astype(BF16)
    b1 = b.astype(BF16)
    b2 = (b - b1.astype(F32)).astype(BF16)
    return _dot(a1, b1) + (_dot(a2, b1) + _dot(a1, b2))


def _dot_mask_l(mask, x):
    x1, x2, x3 = _split3(x)
    return _dot(mask, x1) + (_dot(mask, x2) + _dot(mask, x3))


def _dot_mask_r(x, mask):
    x1 = x.astype(BF16)
    x2 = (x - x1.astype(F32)).astype(BF16)
    return _dot(x1, mask) + _dot(x2, mask)


def _dot_mask_r3(x, mask):
    x1, x2, x3 = _split3(x)
    return _dot(x1, mask) + (_dot(x2, mask) + _dot(x3, mask))


def _sigmoid(x):
    return jax.nn.sigmoid(x)


def _silu(x):
    return x * jax.nn.sigmoid(x)


def _softplus(x):
    return jnp.maximum(x, 0.0) + jnp.log1p(jnp.exp(-jnp.abs(x)))


def _rms_rows(x, g):
    return x * lax.rsqrt(jnp.mean(x * x, axis=-1, keepdims=True) + EPS) * g


def _modulate(y, scale, shift):
    rows = y.shape[0]
    y3 = y.reshape(rows // MODB, MODB, y.shape[1])
    return (y3 * (1.0 + scale) + shift).reshape(y.shape)


def _gate_res(x, gate, y):
    rows = y.shape[0]
    y3 = y.reshape(rows // MODB, MODB, y.shape[1])
    return x + (gate * y3).reshape(y.shape)


def _mod_kernel(c_ref, w_ref, b_ref, o_ref):
    o_ref[...] = _dot3(_silu(c_ref[...]), w_ref[...]) + b_ref[...]


def _mod_call(c_all, ada_w, ada_b):
    tn = 768
    rows = c_all.shape[0]
    return pl.pallas_call(
        _mod_kernel,
        grid=(DEPTH, 6 * D_MODEL // tn),
        in_specs=[
            pl.BlockSpec((rows, D_MODEL), lambda l, j: (0, 0)),
            pl.BlockSpec((None, D_MODEL, tn), lambda l, j: (l, 0, j)),
            pl.BlockSpec((None, 1, tn), lambda l, j: (l, 0, j)),
        ],
        out_specs=pl.BlockSpec((None, rows, tn), lambda l, j: (l, 0, j)),
        out_shape=jax.ShapeDtypeStruct((DEPTH, rows, 6 * D_MODEL), F32),
        compiler_params=pltpu.CompilerParams(
            dimension_semantics=("arbitrary", "arbitrary"), vmem_limit_bytes=VMEM_LIMIT),
        name="adaln_mod",
    )(c_all, ada_w, ada_b.reshape(DEPTH, 1, 6 * D_MODEL))


def _mod_spec(chunk):
    return pl.BlockSpec((None, MODB, D_MODEL), lambda i: (i // NPT, 0, chunk))


def _row_spec(width):
    return pl.BlockSpec((TM, width), lambda i: (i, 0))


def _full_spec(shape):
    nd = len(shape)
    return pl.BlockSpec(shape, lambda i: (0,) * nd)


def _in_kernel(has_res, *refs):
    if has_res:
        (x1_ref, moe_ref, g2_ref, n1_ref, sc_ref, sh_ref, w_ref, x_ref, proj_ref) = refs
        x = _gate_res(x1_ref[...], g2_ref[...], moe_ref[...])
        x_ref[...] = x
    else:
        (x_ref, n1_ref, sc_ref, sh_ref, w_ref, proj_ref) = refs
        x = x_ref[...]
    h = _modulate(_rms_rows(x, n1_ref[...]), sc_ref[...], sh_ref[...])
    proj_ref[...] = _dot(h.astype(BF16), w_ref[...])


def _in_call(x, moe, mod_l, norm_g, w_in_r, prev_mod):
    has_res = moe is not None
    ins, specs = [x], [_row_spec(D_MODEL)]
    outs = [jax.ShapeDtypeStruct((NTOK, D_INP), F32)]
    out_specs = [_row_spec(D_INP)]
    if has_res:
        ins += [moe, prev_mod]
        specs += [_row_spec(D_MODEL), _mod_spec(5)]
        outs = [jax.ShapeDtypeStruct((NTOK, D_MODEL), F32)] + outs
        out_specs = [_row_spec(D_MODEL)] + out_specs
    ins += [norm_g, mod_l, mod_l, w_in_r]
    specs += [_full_spec((1, D_MODEL)), _mod_spec(1), _mod_spec(0), _full_spec((D_MODEL, D_INP))]
    res = pl.pallas_call(
        functools.partial(_in_kernel, has_res),
        grid=(NTILES,),
        in_specs=specs,
        out_specs=out_specs,
        out_shape=outs,
        compiler_params=pltpu.CompilerParams(
            dimension_semantics=("arbitrary",), vmem_limit_bytes=VMEM_LIMIT),
        name="in_proj",
    )(*ins)
    return res if has_res else (x, res[0])


def _lru_gates(xc, wa_ref, ba_ref, wx_ref, bx_ref, lam_ref):
    xb = xc.astype(BF16)
    r = _sigmoid(_dot(xb, wa_ref[...]) + ba_ref[...])
    i = _sigmoid(_dot(xb, wx_ref[...]) + bx_ref[...])
    log_a = -LRU_C * r * _softplus(-lam_ref[...])
    a = jnp.exp(log_a)
    mult = jnp.sqrt(1.0 - jnp.exp(2.0 * log_a))
    return a, mult, i * xc


def _gelu(x):
    return jax.nn.gelu(x, approximate=True)


def _gdn_prep(qkv, small, gpar_ref, ones_blk):
    qkv = _silu(qkv)
    q = qkv[:, 0:256]
    k = qkv[:, 256:512]
    v = qkv[:, 512:768]
    q = q * lax.rsqrt(_dot_mask_r(q * q, ones_blk) + EPS) * (GDN_DK ** -0.5)
    k = k * lax.rsqrt(_dot_mask_r(k * k, ones_blk) + EPS)
    beta = _sigmoid(small)
    g = -jnp.exp(gpar_ref[0:1, :]) * _softplus(small + gpar_ref[1:2, :])
    lane = lax.broadcasted_iota(jnp.int32, small.shape, 1)
    bg = jnp.where(lane < GDN_H, beta, g)
    return q, k, v, bg


def _gla_prep(proj_q, proj_k, small, wg2_ref, gbg_ref):
    pre = _dot3(small, wg2_ref[...]) + gbg_ref[...]
    gk = -_softplus(-pre) / GLA_TAU
    return proj_q * (GLA_DK ** -0.5), proj_k, gk


def _head_norm_gate(o, norm_g, gate_in, ones_blk):
    ms = _dot_mask_r(o * o, ones_blk) * (1.0 / GDN_DV)
    return o * lax.rsqrt(ms + EPS) * norm_g * _silu(gate_in)


def _mix_prompt_kernel(proj_ref, lcw_ref, lcb_ref, wa_ref, ba_ref, wx_ref, bx_ref, lam_ref,
                       gcw_ref, gpar_ref, gng_ref, wg2_ref, gbg_ref, lng_ref,
                       ones_ref, tri_ref, e2_ref, btblk_ref,
                       mix_ref, lconv_ref, lh_ref, gconv_ref, gs_ref, lst_ref,
                       exta, extb, a_s, u_s, hs_s, q_s, k_s, v_s, bg_s, og_s,
                       q2_s, k2_s, v2_s, gk_s, ol_s):
    nb = BATCH
    rows = TT * nb
    step = pl.program_id(0)

    @pl.when(step == 0)
    def _():
        exta[0:3 * nb, :] = jnp.zeros((3 * nb, LRU_W), F32)
        extb[0:3 * nb, :] = jnp.zeros((3 * nb, GDN_CONV), F32)
        lh_ref[...] = jnp.zeros_like(lh_ref)
        gs_ref[...] = jnp.zeros_like(gs_ref)
        lst_ref[...] = jnp.zeros_like(lst_ref)

    def conv(ext, x, w_ref):
        ext[pl.ds(3 * nb, rows), :] = x
        y = ext[pl.ds(0, rows), :] * w_ref[0:1, :]
        for j in range(1, CONV_W):
            y = y + ext[pl.ds(j * nb, rows), :] * w_ref[j:j + 1, :]
        tail = ext[pl.ds(rows, 3 * nb), :]
        ext[pl.ds(0, 3 * nb), :] = tail
        return y, tail

    ones_blk = ones_ref[...]

    xa, tail = conv(exta, proj_ref[:, C_AX:C_AX + LRU_W], lcw_ref)
    lconv_ref[...] = tail
    xc = xa + lcb_ref[...]
    a, mult, ix = _lru_gates(xc, wa_ref, ba_ref, wx_ref, bx_ref, lam_ref)
    rid = lax.broadcasted_iota(jnp.int32, (rows, LRU_W), 0)
    mult = jnp.where((rid < nb) & (step == 0), 1.0, mult)
    a_s[...] = a
    u_s[...] = mult * ix

    def scan_body(t, h):
        off = pl.multiple_of(t * nb, nb)
        h = a_s[pl.ds(off, nb), :] * h + u_s[pl.ds(off, nb), :]
        hs_s[pl.ds(off, nb), :] = h
        return h

    lh_ref[...] = lax.fori_loop(0, TT, scan_body, lh_ref[...], unroll=8)
    mix_ref[:, 0:LRU_W] = (hs_s[...] * _gelu(proj_ref[:, C_AG:C_AG + LRU_W])).astype(BF16)

    small = proj_ref[:, C_SM:C_SM + LANES]
    qkv, tail = conv(extb, proj_ref[:, C_QKV:C_QKV + GDN_CONV], gcw_ref)
    gconv_ref[...] = tail
    q, k, v, bg = _gdn_prep(qkv, small, gpar_ref, ones_blk)
    for j in range(2):
        ls = slice(j * LANES, (j + 1) * LANES)
        q_s[j] = q[:, ls]
        k_s[j] = k[:, ls]
        v_s[j] = v[:, ls]
    bg_s[...] = bg

    q2, k2, gk = _gla_prep(proj_ref[:, C_CQ:C_CQ + 128], proj_ref[:, C_CK:C_CK + 128],
                           small, wg2_ref, gbg_ref)
    q2_s[...] = q2
    k2_s[...] = k2
    gk_s[...] = gk
    for j in range(2):
        v2_s[j] = proj_ref[:, C_CV + j * LANES:C_CV + (j + 1) * LANES]

    ri = lax.broadcasted_iota(jnp.int32, (TT, 256), 0)
    lane = lax.broadcasted_iota(jnp.int32, (TT, 256), 1)
    cj = lane % TT
    incl = ri >= cj
    strict = ri > cj
    eye = ri == cj
    blk16m = (ri // 16) == (cj // 16)
    blk32m = (ri // 32) == (cj // 32)
    blk_causal = incl & blk16m
    hmask = [(lane // GDN_DV == h).astype(BF16) for h in range(GDN_H)]
    lane2 = lax.broadcasted_iota(jnp.int32, (TT, LANES), 1)
    hmask2 = [(lane2 // GLA_DK == h).astype(BF16) for h in range(GLA_H)]
    bdm = (lax.broadcasted_iota(jnp.int32, (256, 256), 0) // GDN_DK
           == lax.broadcasted_iota(jnp.int32, (256, 256), 1) // GDN_DV)
    bdm2 = (lax.broadcasted_iota(jnp.int32, (256, LANES), 0) // GLA_DV
            == lax.broadcasted_iota(jnp.int32, (256, LANES), 1) // GLA_DK)
    tri = tri_ref[...]
    e2 = e2_ref[...]
    btblk = btblk_ref[...]
    ones8 = jnp.ones((8, TT), BF16)
    seqs = range(nb)
    rsel = [pl.ds(b, TT, stride=nb) for b in seqs]

    def bd(x):
        xb = x.astype(BF16)
        return jnp.concatenate([xb * m for m in hmask], axis=0)

    def bd2(x):
        xb = x.astype(BF16)
        return jnp.concatenate([xb * m for m in hmask2], axis=0)

    def cat2(ref, b):
        return jnp.concatenate([ref[0, rsel[b], :], ref[1, rsel[b], :]], axis=1)

    def each(f, *lists):
        return [f(*xs) for xs in zip(*lists)]

    q = [cat2(q_s, b) for b in seqs]
    k = [cat2(k_s, b) for b in seqs]
    v = [cat2(v_s, b) for b in seqs]
    bgb = [bg_s[rsel[b], :] for b in seqs]
    gc_all = each(lambda x: _dot_mask_l(tri, x), bgb)
    both = each(lambda x, c: _dot_mask_r3(jnp.where(lane2 < GDN_H, x, c), e2), bgb, gc_all)
    beta = [x[:, 0:256] for x in both]
    gc = [x[:, 256:512] for x in both]
    gc_row = each(lambda g: _dot_mask_l(ones8, jnp.where(eye, g, 0.0))[0:1, :], gc)
    decay = each(lambda g, gr: jnp.exp(jnp.where(incl, g - gr, -jnp.inf)), gc, gc_row)
    egc = each(jnp.exp, gc)
    kb = each(lambda a, b_: a * b_, k, beta)
    bdk = each(bd, k)
    kq = each(lambda kb_, q_, m: _dot_nt(jnp.concatenate([kb_, q_], axis=0).astype(BF16), m), kb, q, bdk)
    amat = each(lambda x, d: jnp.where(strict, x[0:TT] * d, 0.0), kq, decay)
    qk = each(lambda x, d: jnp.where(incl, x[TT:] * d, 0.0), kq, decay)
    y = each(lambda a: -jnp.where(blk16m, a, 0.0), amat)
    n = y
    bdy = each(bd, y)
    for _ in range(3):
        y = each(lambda y_, m: _dot(y_.astype(BF16), m), y, bdy)
        bdy = each(bd, y)
        n = each(lambda n_, y_, m: n_ + y_ + _dot(n_.astype(BF16), m), n, y, bdy)
    t = each(lambda n_: jnp.where(eye, 1.0, 0.0) + n_, n)
    for lower in (each(lambda a: jnp.where(blk32m & ~blk16m, a, 0.0), amat),
                  each(lambda a: jnp.where(blk32m, 0.0, a), amat)):
        tl = each(lambda t_, l_: _dot(t_.astype(BF16), bd(l_)), t, lower)
        t = each(lambda t_, x: t_ - _dot(x.astype(BF16), bd(t_)), t, tl)
    uw = each(lambda t_, v_, b_, kb_, e: _dot(
        t_.astype(BF16), jnp.concatenate([bd(v_ * b_), bd(kb_ * e)], axis=1)), t, v, beta, kb, egc)
    s = [gs_ref[b] for b in seqs]
    ws = each(lambda x, q_, e, s_: _dot(
        jnp.concatenate([x[:, 256:512], q_ * e], axis=0).astype(BF16), s_.astype(BF16)), uw, q, egc, s)
    vn = each(lambda x, w_: x[:, 0:256] - w_[0:TT], uw, ws)
    o = each(lambda w_, a, v_: w_[TT:] + _dot(a.astype(BF16), bd(v_)), ws, qk, vn)
    for b in seqs:
        g_last = gc[b][TT - 1:TT, :]
        kd = k[b] * jnp.exp(g_last - gc[b])
        gs_ref[b] = s[b] * jnp.exp(g_last) + jnp.where(bdm, _mm_tn(kd, vn[b]), 0.0)
        for j in range(2):
            og_s[j, rsel[b], :] = o[b][:, j * LANES:(j + 1) * LANES]

    cum = [_dot_mask_l(btblk, gk_s[rsel[b], :]) for b in seqs]
    bc = [x[0:TT] for x in cum]
    bl = [x[TT:] for x in cum]
    qb = [q2_s[rsel[b], :] for b in seqs]
    kb2 = [k2_s[rsel[b], :] for b in seqs]
    vb = [cat2(v2_s, b) for b in seqs]
    qi = each(lambda a, c: (a * jnp.exp(c)).astype(BF16), qb, bc)
    ki = each(lambda a, c: a * jnp.exp(-c), kb2, bc)
    kst = each(lambda a, l_, c: (a * jnp.exp(l_ - c)).astype(BF16), kb2, bl, bc)
    ebl = each(jnp.exp, bl)
    sc = each(lambda a, c: jnp.where(blk_causal, _dot_nt(a, bd2(c)), 0.0), qi, ki)
    oh = each(lambda a, v_: _dot(a.astype(BF16), bd(v_)), sc, vb)
    st = [lst_ref[b] for b in seqs]
    vbb = each(lambda v_: v_.astype(BF16), vb)
    outs = [[] for _ in seqs]
    for c in range(TT // GLA_CHUNK):
        rs = slice(c * GLA_CHUNK, (c + 1) * GLA_CHUNK)
        for b in seqs:
            outs[b].append(oh[b][rs] + _dot_nt(qi[b][rs], st[b].astype(BF16)))
        upd = [jnp.where(bdm2, _dot_tn(vbb[b][rs], kst[b][rs]), 0.0) for b in seqs]
        st = [st[b] * ebl[b][c * GLA_CHUNK:c * GLA_CHUNK + 1, :] + upd[b] for b in seqs]
    for b in seqs:
        lst_ref[b] = st[b]
        ol = jnp.concatenate(outs[b], axis=0)
        for j in range(2):
            ol_s[j, rsel[b], :] = ol[:, j * LANES:(j + 1) * LANES]

    og = jnp.concatenate([og_s[0], og_s[1]], axis=1)
    out_b = _head_norm_gate(og, gng_ref[...], proj_ref[:, C_BZ:C_BZ + 256], ones_blk)
    mix_ref[:, 512:768] = out_b.astype(BF16)
    ol = jnp.concatenate([ol_s[0], ol_s[1]], axis=1)
    out_c = _head_norm_gate(ol, lng_ref[...], proj_ref[:, C_CR:C_CR + 256], ones_blk)
    mix_ref[:, 768:1024] = out_c.astype(BF16)


def _mix_consts():
    r = np.arange(TT)
    tri = (r[:, None] >= r[None, :]).astype(np.float32)
    same = (r[:, None] // GLA_CHUNK) == (r[None, :] // GLA_CHUNK)
    bt16 = (tri.astype(bool) & same).astype(np.float32)
    blk16 = same.astype(np.float32)
    c = np.arange(256)
    ones_blk = ((c[:, None] // GDN_DV) == (c[None, :] // GDN_DV)).astype(np.float32)
    e2 = np.zeros((LANES, 512), np.float32)
    for h in range(GDN_H):
        e2[h, h * GDN_DV:(h + 1) * GDN_DV] = 1.0
        e2[GDN_H + h, 256 + h * GDN_DV:256 + (h + 1) * GDN_DV] = 1.0
    btblk = np.concatenate([bt16, blk16], axis=0)
    return (jnp.asarray(ones_blk, BF16), jnp.asarray(tri, BF16),
            jnp.asarray(e2, BF16), jnp.asarray(btblk, BF16))


def _mix_prompt_call(proj, mp):
    rows = TT * BATCH
    consts = _mix_consts()
    params = (mp["lcw"], mp["lcb"], mp["wa"], mp["ba"], mp["wx"], mp["bx"], mp["lam"],
              mp["gcw"], mp["gpar"], mp["gng"], mp["wg2"], mp["gbg"], mp["lng"]) + consts
    out_shape = [
        jax.ShapeDtypeStruct((NP, D_MODEL), BF16),
        jax.ShapeDtypeStruct((3 * BATCH, LRU_W), F32),
        jax.ShapeDtypeStruct((BATCH, LRU_W), F32),
        jax.ShapeDtypeStruct((3 * BATCH, GDN_CONV), F32),
        jax.ShapeDtypeStruct((BATCH, GDN_H * GDN_DK, GDN_H * GDN_DV), F32),
        jax.ShapeDtypeStruct((BATCH, GLA_H * GLA_DV, GLA_H * GLA_DK), F32),
    ]
    out_specs = [pl.BlockSpec((rows, D_MODEL), lambda i: (i, 0))] + [
        _full_spec(s.shape) for s in out_shape[1:]]
    scratch = [
        pltpu.VMEM((rows + 3 * BATCH, LRU_W), F32),
        pltpu.VMEM((rows + 3 * BATCH, GDN_CONV), F32),
        pltpu.VMEM((rows, LRU_W), F32), pltpu.VMEM((rows, LRU_W), F32), pltpu.VMEM((rows, LRU_W), F32),
        pltpu.VMEM((2, rows, LANES), F32), pltpu.VMEM((2, rows, LANES), F32), pltpu.VMEM((2, rows, LANES), F32),
        pltpu.VMEM((rows, LANES), F32), pltpu.VMEM((2, rows, LANES), F32),
        pltpu.VMEM((rows, LANES), F32), pltpu.VMEM((rows, LANES), F32), pltpu.VMEM((2, rows, LANES), F32),
        pltpu.VMEM((rows, LANES), F32), pltpu.VMEM((2, rows, LANES), F32),
    ]
    return pl.pallas_call(
        _mix_prompt_kernel,
        grid=(SEQ // TT,),
        in_specs=[pl.BlockSpec((rows, D_INP), lambda i: (i, 0))] + [_full_spec(p.shape) for p in params],
        out_specs=out_specs,
        out_shape=out_shape,
        scratch_shapes=scratch,
        compiler_params=pltpu.CompilerParams(
            dimension_semantics=("arbitrary",), vmem_limit_bytes=VMEM_LIMIT),
        name="mix_prompt",
    )(proj, *params)


def _pair_bcast(x, p, lo_mask):
    return jnp.where(lo_mask, x[:, 2 * p:2 * p + 1], x[:, 2 * p + 1:2 * p + 2])


def _fold_pairs(acc):
    return acc[:, 0:64] + acc[:, 64:128]


def _mix_sample_kernel(p0_ref, p1_ref, p2_ref, p3_ref, lconv_in, lh_in, gconv_in, gs_in, ls_in,
                       lcw_ref, lcb_ref, wa_ref, ba_ref, wx_ref, bx_ref, lam_ref,
                       gcw_ref, gpar_ref, gng_ref, wg2_ref, gbg_ref, lng_ref, ones_ref,
                       mix_ref, lconv_ref, lh_ref, gconv_ref, gs_ref, ls_ref):
    rows = DEC_SEQ * BB
    prefs = (p0_ref, p1_ref, p2_ref, p3_ref)
    ones_blk = ones_ref[...]

    def cols(c0, width):
        return [p[:, c0:c0 + width] for p in prefs]

    def conv(prev, xs, w_ref):
        ext = [prev[j] for j in range(CONV_W - 1)] + xs
        ys = []
        for t in range(DEC_SEQ):
            y = ext[t] * w_ref[0:1, :]
            for j in range(1, CONV_W):
                y = y + ext[t + j] * w_ref[j:j + 1, :]
            ys.append(y)
        return jnp.concatenate(ys, axis=0), ext[DEC_SEQ:]

    def rows_of(x, t):
        return x[t * BB:(t + 1) * BB]

    xa, tail = conv(lconv_in, cols(C_AX, LRU_W), lcw_ref)
    for j in range(CONV_W - 1):
        lconv_ref[j] = tail[j]
    xc = xa + lcb_ref[...]
    a, mult, ix = _lru_gates(xc, wa_ref, ba_ref, wx_ref, bx_ref, lam_ref)
    u = mult * ix
    h = lh_in[...]
    hs = []
    for t in range(DEC_SEQ):
        h = rows_of(a, t) * h + rows_of(u, t)
        hs.append(h)
    lh_ref[...] = h
    ag = jnp.concatenate(cols(C_AG, LRU_W), axis=0)
    mix_a = jnp.concatenate(hs, axis=0) * _gelu(ag)

    small = jnp.concatenate(cols(C_SM, LANES), axis=0)
    qkv, tail = conv(gconv_in, cols(C_QKV, GDN_CONV), gcw_ref)
    for j in range(CONV_W - 1):
        gconv_ref[j] = tail[j]
    q, k, v, bg = _gdn_prep(qkv, small, gpar_ref, ones_blk)
    lo_mask = lax.broadcasted_iota(jnp.int32, (BB, LANES), 1) < 64
    gs_ref[...] = gs_in[...]
    hd = GDN_DK * GDN_DV
    o_heads = []
    for h in range(GDN_H):
        o_t = []
        for t in range(DEC_SEQ):
            rs = slice(t * BB, (t + 1) * BB)
            cs = slice(h * GDN_DK, (h + 1) * GDN_DK)
            eg = jnp.exp(bg[rs, GDN_H + h:GDN_H + h + 1])
            beta = bg[rs, h:h + 1]
            kt, qt, vt = k[rs, cs], q[rs, cs], v[rs, cs]
            kks = [_pair_bcast(kt, p, lo_mask) for p in range(GDN_DK // 2)]
            acc = jnp.zeros((BB, LANES), F32)
            for p in range(GDN_DK // 2):
                acc = acc + gs_ref[:, h * hd + p * LANES:h * hd + (p + 1) * LANES] * kks[p]
            vn = beta * (vt - eg * _fold_pairs(acc))
            vn2 = jnp.concatenate([vn, vn], axis=1)
            oacc = jnp.zeros((BB, LANES), F32)
            for p in range(GDN_DK // 2):
                sl = slice(h * hd + p * LANES, h * hd + (p + 1) * LANES)
                s = eg * gs_ref[:, sl] + kks[p] * vn2
                gs_ref[:, sl] = s
                oacc = oacc + s * _pair_bcast(qt, p, lo_mask)
            o_t.append(_fold_pairs(oacc))
        o_heads.append(jnp.concatenate(o_t, axis=0))
    o_b = jnp.concatenate(o_heads, axis=1)
    bz = jnp.concatenate(cols(C_BZ, 256), axis=0)
    mix_b = _head_norm_gate(o_b, gng_ref[...], bz, ones_blk)

    q2, k2, gk = _gla_prep(jnp.concatenate(cols(C_CQ, 128), axis=0),
                           jnp.concatenate(cols(C_CK, 128), axis=0), small, wg2_ref, gbg_ref)
    v2 = jnp.concatenate(cols(C_CV, 256), axis=0)
    ls_ref[...] = ls_in[...]
    hd2 = GLA_DK * GLA_DV
    o_heads = []
    for h in range(GLA_H):
        o_t = []
        for t in range(DEC_SEQ):
            rs = slice(t * BB, (t + 1) * BB)
            cs = slice(h * GLA_DK, (h + 1) * GLA_DK)
            al = jnp.exp(gk[rs, cs])
            kt, qt = k2[rs, cs], q2[rs, cs]
            vt = v2[rs, h * GLA_DV:(h + 1) * GLA_DV]
            v2x = jnp.concatenate([vt, vt], axis=1)
            oacc = jnp.zeros((BB, LANES), F32)
            for p in range(GLA_DK // 2):
                sl = slice(h * hd2 + p * LANES, h * hd2 + (p + 1) * LANES)
                s = _pair_bcast(al, p, lo_mask) * ls_ref[:, sl] + _pair_bcast(kt, p, lo_mask) * v2x
                ls_ref[:, sl] = s
                oacc = oacc + s * _pair_bcast(qt, p, lo_mask)
            o_t.append(_fold_pairs(oacc))
        o_heads.append(jnp.concatenate(o_t, axis=0))
    o_c = jnp.concatenate(o_heads, axis=1)
    cr = jnp.concatenate(cols(C_CR, 256), axis=0)
    mix_c = _head_norm_gate(o_c, lng_ref[...], cr, ones_blk)

    mix = jnp.concatenate([mix_a, mix_b, mix_c], axis=1).astype(BF16)
    for t in range(DEC_SEQ):
        mix_ref[t] = mix[t * BB:(t + 1) * BB]


def _mix_sample_call(proj, states, mp):
    lconv, lh, gconv, gs, ls = states
    ones_blk = _mix_consts()[0]
    params = (mp["lcw"], mp["lcb"], mp["wa"], mp["ba"], mp["wx"], mp["bx"], mp["lam"],
              mp["gcw"], mp["gpar"], mp["gng"], mp["wg2"], mp["gbg"], mp["lng"], ones_blk)
    nblk = DEC_BATCH // BB
    base = NP // BB

    def proj_spec(t):
        return pl.BlockSpec((BB, D_INP), lambda j: (base + t * nblk + j, 0))

    def bspec3(n, width):
        return pl.BlockSpec((n, BB, width), lambda j: (0, j, 0))

    def bspec2(width):
        return pl.BlockSpec((BB, width), lambda j: (j, 0))

    gdn_flat = GDN_H * GDN_DK * GDN_DV
    gla_flat = GLA_H * GLA_DK * GLA_DV
    state_specs = [bspec3(3, LRU_W), bspec2(LRU_W), bspec3(3, GDN_CONV), bspec2(gdn_flat), bspec2(gla_flat)]
    out_shape = [
        jax.ShapeDtypeStruct((DEC_SEQ, DEC_BATCH, D_MODEL), BF16),
        jax.ShapeDtypeStruct((3, DEC_BATCH, LRU_W), F32),
        jax.ShapeDtypeStruct((DEC_BATCH, LRU_W), F32),
        jax.ShapeDtypeStruct((3, DEC_BATCH, GDN_CONV), F32),
        jax.ShapeDtypeStruct((DEC_BATCH, gdn_flat), F32),
        jax.ShapeDtypeStruct((DEC_BATCH, gla_flat), F32),
    ]
    return pl.pallas_call(
        _mix_sample_kernel,
        grid=(nblk,),
        in_specs=[proj_spec(t) for t in range(DEC_SEQ)] + state_specs + [_full_spec(p.shape) for p in params],
        out_specs=[bspec3(DEC_SEQ, D_MODEL)] + state_specs,
        out_shape=out_shape,
        compiler_params=pltpu.CompilerParams(
            dimension_semantics=("arbitrary",), vmem_limit_bytes=VMEM_LIMIT),
        name="mix_sample",
    )(proj, proj, proj, proj, lconv, lh, gconv, gs, ls, *params)


def _out_kernel(mp_ref, ms_ref, x_ref, w_ref, g1_ref, n2_ref, sc_ref, sh_ref, rw_ref, rb_ref,
                x1_ref, h2_ref, ridx_ref, rprob_ref):
    step = pl.program_id(0)
    mix = jnp.where(step < NPT, mp_ref[...], ms_ref[...])
    x1 = _gate_res(x_ref[...], g1_ref[...], _dot(mix, w_ref[...]))
    x1_ref[...] = x1
    h2 = _modulate(_rms_rows(x1, n2_ref[...]), sc_ref[...], sh_ref[...])
    h2_ref[...] = h2
    logits = _dot3(h2, rw_ref[...]) + rb_ref[...]
    lane = lax.broadcasted_iota(jnp.int32, logits.shape, 1)
    cur = jnp.where(lane < N_EXP, logits, -jnp.inf)
    vals, idxs = [], []
    for _ in range(TOP_K):
        m = jnp.max(cur, axis=-1, keepdims=True)
        idx = jnp.min(jnp.where(cur == m, lane, LANES), axis=-1, keepdims=True)
        vals.append(m)
        idxs.append(idx)
        cur = jnp.where(lane == idx, -jnp.inf, cur)
    es = [jnp.exp(v - vals[0]) for v in vals]
    den = es[0] + es[1] + es[2] + es[3]
    ridx = jnp.zeros(logits.shape, jnp.int32)
    rprob = jnp.zeros(logits.shape, F32)
    for j in range(TOP_K):
        ridx = jnp.where(lane == j, idxs[j], ridx)
        rprob = jnp.where(lane == j, es[j] / den, rprob)
    ridx_ref[...] = ridx
    rprob_ref[...] = rprob


def _out_call(mix_p, mix_s, x, mod_l, w_out, norm_g, rw, rb):
    return pl.pallas_call(
        _out_kernel,
        grid=(NTILES,),
        in_specs=[
            pl.BlockSpec((TM, D_MODEL), lambda i: (jnp.minimum(i, NPT - 1), 0)),
            _full_spec((NS, D_MODEL)),
            _row_spec(D_MODEL),
            _full_spec((D_MODEL, D_MODEL)),
            _mod_spec(2),
            _full_spec((1, D_MODEL)),
            _mod_spec(4),
            _mod_spec(3),
            _full_spec((D_MODEL, LANES)),
            _full_spec((1, LANES)),
        ],
        out_specs=[_row_spec(D_MODEL), _row_spec(D_MODEL), _row_spec(LANES), _row_spec(LANES)],
        out_shape=[
            jax.ShapeDtypeStruct((NTOK, D_MODEL), F32),
            jax.ShapeDtypeStruct((NTOK, D_MODEL), F32),
            jax.ShapeDtypeStruct((NTOK, LANES), jnp.int32),
            jax.ShapeDtypeStruct((NTOK, LANES), F32),
        ],
        compiler_params=pltpu.CompilerParams(
            dimension_semantics=("arbitrary",), vmem_limit_bytes=VMEM_LIMIT),
        name="out_proj_router",
    )(mix_p, mix_s, x, w_out, mod_l, norm_g, mod_l, mod_l, rw, rb)


def _route_kernel(ridx_ref, dest_ref, meta_ref, cnt_s, pstart_s):
    phase = pl.program_id(0)
    i = pl.program_id(1)
    lane = lax.broadcasted_iota(jnp.int32, (TM, LANES), 1)
    ridx = ridx_ref[...]
    hits = [lane == ridx[:, k:k + 1] for k in range(TOP_K)]
    onehot = jnp.zeros((TM, LANES), F32)
    for hit in hits:
        onehot = onehot + jnp.where(hit, 1.0, 0.0)
    colsum = jnp.sum(onehot, axis=0, keepdims=True)

    @pl.when((phase == 0) & (i == 0))
    def _():
        cnt_s[...] = jnp.zeros_like(cnt_s)

    @pl.when(phase == 0)
    def _():
        cnt_s[...] += colsum

    @pl.when((phase == 1) & (i == 0))
    def _():
        total = cnt_s[...]
        padded = jnp.floor((total + (TE - 1)) * (1.0 / TE)) * TE
        r = lax.broadcasted_iota(jnp.int32, (LANES, LANES), 0)
        c = lax.broadcasted_iota(jnp.int32, (LANES, LANES), 1)
        before = jnp.where(r < c, 1.0, 0.0).astype(BF16)
        pstart = _dot_mask_r3(jnp.broadcast_to(padded, (8, LANES)), before)[0:1]
        pstart_s[...] = pstart
        meta_ref[...] = jnp.concatenate(
            [total, pstart + padded, jnp.zeros((6, LANES), F32)], axis=0)
        cnt_s[...] = jnp.zeros_like(cnt_s)

    @pl.when(phase == 1)
    def _():
        r = lax.broadcasted_iota(jnp.int32, (TM, TM), 0)
        c = lax.broadcasted_iota(jnp.int32, (TM, TM), 1)
        earlier = jnp.where(c < r, 1.0, 0.0).astype(BF16)
        pos = pstart_s[...] + cnt_s[...] + _dot(earlier, onehot.astype(BF16))
        dest = jnp.zeros((TM, LANES), jnp.int32)
        for k, hit in enumerate(hits):
            d = jnp.sum(jnp.where(hit, pos, 0.0), axis=1, keepdims=True)
            dest = jnp.where(lane == k, d.astype(jnp.int32), dest)
        dest_ref[...] = dest
        cnt_s[...] += colsum


def _route_call(ridx):
    return pl.pallas_call(
        _route_kernel,
        grid=(2, NTILES),
        in_specs=[pl.BlockSpec((TM, LANES), lambda p, i: (i, 0))],
        out_specs=[pl.BlockSpec((TM, LANES), lambda p, i: (p * i, 0)),
                   pl.BlockSpec((8, LANES), lambda p, i: (0, 0))],
        out_shape=[jax.ShapeDtypeStruct((NTOK, LANES), jnp.int32),
                   jax.ShapeDtypeStruct((8, LANES), F32)],
        scratch_shapes=[pltpu.VMEM((1, LANES), F32), pltpu.VMEM((1, LANES), F32)],
        compiler_params=pltpu.CompilerParams(
            dimension_semantics=("arbitrary", "arbitrary"), vmem_limit_bytes=VMEM_LIMIT),
        name="route",
    )(ridx)


def _row_copy(src_ref, src_row, dst_ref, dst_row, sem):
    return pltpu.make_async_copy(src_ref.at[pl.ds(src_row, 1)], dst_ref.at[pl.ds(dst_row, 1)], sem)


def _dispatch_kernel(cnt_ref, pend_ref, dest_ref, h_ref, xs_ref, zero_s, sem):
    i = pl.program_id(0)

    def pad_fill(e):
        start = pl.multiple_of(pend_ref[e] - TE, TE)
        return pltpu.make_async_copy(zero_s, xs_ref.at[pl.ds(start, TE)], sem)

    @pl.when(i == 0)
    def _():
        zero_s[...] = jnp.zeros_like(zero_s)
        for e in range(N_EXP):
            @pl.when(cnt_ref[e] > 0)
            def _():
                pad_fill(e).start()
        for e in range(N_EXP):
            @pl.when(cnt_ref[e] > 0)
            def _():
                pad_fill(e).wait()

        def tail_fill(t, carry):
            cp = pltpu.make_async_copy(zero_s, xs_ref.at[pl.ds(pl.multiple_of(t * TE, TE), TE)], sem)
            cp.start()
            cp.wait()
            return carry

        lax.fori_loop(pend_ref[N_EXP - 1] // TE, NTE, tail_fill, 0)

    def start_rows(j, carry):
        for k in range(TOP_K):
            _row_copy(h_ref, j, xs_ref, dest_ref[0, j * TOP_K + k], sem).start()
        return carry

    lax.fori_loop(0, TD, start_rows, 0, unroll=4)

    def wait_rows(j, carry):
        _row_copy(h_ref, 0, xs_ref, 0, sem).wait()
        return carry

    lax.fori_loop(0, TD * TOP_K, wait_rows, 0, unroll=8)


def _dispatch_call(counts, pend, dest_blocks, h2):
    grid_spec = pltpu.PrefetchScalarGridSpec(
        num_scalar_prefetch=2,
        grid=(NTOK // TD,),
        in_specs=[
            pl.BlockSpec((None, 1, TD * TOP_K), lambda i, c, p: (i, 0, 0), memory_space=pltpu.SMEM),
            pl.BlockSpec((TD, D_MODEL), lambda i, c, p: (i, 0)),
        ],
        out_specs=pl.BlockSpec(memory_space=pl.ANY),
        scratch_shapes=[pltpu.VMEM((TE, D_MODEL), F32), pltpu.SemaphoreType.DMA(())],
    )
    return pl.pallas_call(
        _dispatch_kernel,
        grid_spec=grid_spec,
        out_shape=jax.ShapeDtypeStruct((NROWS, D_MODEL), F32),
        compiler_params=pltpu.CompilerParams(
            dimension_semantics=("arbitrary",), vmem_limit_bytes=VMEM_LIMIT),
        name="moe_dispatch",
    )(counts, pend, dest_blocks, h2)


def _combine_kernel(dcur_ref, dnext_ref, prob_ref, ys_ref, o_ref, buf, sems):
    i = pl.program_id(0)
    n = pl.num_programs(0)

    def start_rows(d_ref, slot):
        def body(j, carry):
            for k in range(TOP_K):
                pltpu.make_async_copy(ys_ref.at[pl.ds(d_ref[0, j * TOP_K + k], 1)],
                                      buf.at[slot, k, pl.ds(j, 1)], sems.at[slot]).start()
            return carry
        lax.fori_loop(0, TD, body, 0, unroll=4)

    @pl.when(i == 0)
    def _():
        start_rows(dcur_ref, 0)

    @pl.when(i + 1 < n)
    def _():
        start_rows(dnext_ref, (i + 1) % 2)

    slot = i % 2

    def wait_rows(j, carry):
        pltpu.make_async_copy(ys_ref.at[pl.ds(0, 1)], buf.at[slot, 0, pl.ds(0, 1)], sems.at[slot]).wait()
        return carry

    lax.fori_loop(0, TD * TOP_K, wait_rows, 0, unroll=8)
    prob = prob_ref[...]
    acc = prob[:, 0:1] * buf[slot, 0]
    for k in range(1, TOP_K):
        acc = acc + prob[:, k:k + 1] * buf[slot, k]
    o_ref[...] = acc


def _combine_call(dest_blocks, rprob, ys):
    nblk = NTOK // TD
    return pl.pallas_call(
        _combine_kernel,
        grid=(nblk,),
        in_specs=[
            pl.BlockSpec((None, 1, TD * TOP_K), lambda i: (i, 0, 0), memory_space=pltpu.SMEM),
            pl.BlockSpec((None, 1, TD * TOP_K), lambda i: (jnp.minimum(i + 1, nblk - 1), 0, 0),
                         memory_space=pltpu.SMEM),
            pl.BlockSpec((TD, LANES), lambda i: (i, 0)),
            pl.BlockSpec(memory_space=pl.ANY),
        ],
        out_specs=pl.BlockSpec((TD, D_MODEL), lambda i: (i, 0)),
        out_shape=jax.ShapeDtypeStruct((NTOK, D_MODEL), F32),
        scratch_shapes=[pltpu.VMEM((2, TOP_K, TD, D_MODEL), F32), pltpu.SemaphoreType.DMA((2,))],
        compiler_params=pltpu.CompilerParams(
            dimension_semantics=("arbitrary",), vmem_limit_bytes=VMEM_LIMIT),
        name="moe_combine",
    )(dest_blocks, dest_blocks, rprob, ys)


def _ffn_kernel(te_ref, nu_ref, x_ref, wg_ref, bg_ref, wu_ref, bu_ref, wd_ref, bd_ref,
                o_ref, wg_s, wu_s, wd_s):
    i = pl.program_id(0)
    e = te_ref[i]
    prev = te_ref[jnp.maximum(i - 1, 0)]

    @pl.when((i == 0) | (prev != e))
    def _():
        wg_s[...] = wg_ref[...].astype(BF16)
        wu_s[...] = wu_ref[...].astype(BF16)
        wd_s[...] = wd_ref[...].astype(BF16)

    @pl.when(i < nu_ref[0])
    def _():
        x = x_ref[...].astype(BF16)
        gate = jnp.minimum(_dot(x, wg_s[...]) + bg_ref[...], SW_LIMIT)
        up = jnp.clip(_dot(x, wu_s[...]) + bu_ref[...], -SW_LIMIT, SW_LIMIT)
        act = (up + 1.0) * gate * _sigmoid(SW_ALPHA * gate)
        o_ref[...] = _dot(act.astype(BF16), wd_s[...]) + bd_ref[...]

    @pl.when(i >= nu_ref[0])
    def _():
        o_ref[...] = jnp.zeros_like(o_ref)


def _ffn_call(layer, tile_e, n_used, xs, wg, bg, wu, bu, wd, bd):
    wspec = pl.BlockSpec((None, None, D_MODEL, D_MODEL), lambda i, te, nu: (layer, te[i], 0, 0))
    bspec = pl.BlockSpec((None, None, 1, D_MODEL), lambda i, te, nu: (layer, te[i], 0, 0))
    grid_spec = pltpu.PrefetchScalarGridSpec(
        num_scalar_prefetch=2,
        grid=(NTE,),
        in_specs=[
            pl.BlockSpec((TE, D_MODEL), lambda i, te, nu: (i, 0)),
            wspec, bspec, wspec, bspec, wspec, bspec,
        ],
        out_specs=pl.BlockSpec((TE, D_MODEL), lambda i, te, nu: (i, 0)),
        scratch_shapes=[pltpu.VMEM((D_MODEL, D_MODEL), BF16)] * 3,
    )
    b4 = lambda b: b.reshape(DEPTH, N_EXP, 1, D_MODEL)
    return pl.pallas_call(
        _ffn_kernel,
        grid_spec=grid_spec,
        out_shape=jax.ShapeDtypeStruct((NROWS, D_MODEL), F32),
        compiler_params=pltpu.CompilerParams(
            dimension_semantics=("arbitrary",), vmem_limit_bytes=VMEM_LIMIT),
        name="expert_ffn",
    )(tile_e, n_used, xs, wg, b4(bg), wu, b4(bu), wd, b4(bd))


def _moe(layer, h2, ridx, rprob, wg, bg, wu, bu, wd, bd):
    dest, meta = _route_call(ridx)
    counts = meta[0, :N_EXP].astype(jnp.int32)
    pend = meta[1, :N_EXP].astype(jnp.int32)
    tile_start = jnp.arange(NTE, dtype=jnp.int32) * TE
    tile_e = jnp.minimum(jnp.sum((pend[None, :] <= tile_start[:, None]).astype(jnp.int32), axis=1),
                         N_EXP - 1)
    n_used = pend[N_EXP - 1:] // TE
    dest_blocks = dest[:, :TOP_K].reshape(NTOK // TD, 1, TD * TOP_K)
    xs = _dispatch_call(counts, pend, dest_blocks, h2)
    ys = _ffn_call(layer, tile_e, n_used, xs, wg, bg, wu, bu, wd, bd)
    return _combine_call(dest_blocks, rprob, ys)


def _final_kernel(x1_ref, moe_ref, g2_ref, ng_ref, y_ref):
    x = _gate_res(x1_ref[...], g2_ref[...], moe_ref[...])
    y_ref[...] = _rms_rows(x, ng_ref[...])


def _final_call(x1, moe, mod_l, norm_g):
    return pl.pallas_call(
        _final_kernel,
        grid=(NTILES,),
        in_specs=[_row_spec(D_MODEL), _row_spec(D_MODEL), _mod_spec(5), _full_spec((1, D_MODEL))],
        out_specs=_row_spec(D_MODEL),
        out_shape=jax.ShapeDtypeStruct((NTOK, D_MODEL), F32),
        compiler_params=pltpu.CompilerParams(
            dimension_semantics=("arbitrary",), vmem_limit_bytes=VMEM_LIMIT),
        name="final_norm",
    )(x1, moe, mod_l, norm_g)


def _block_diag(w):
    n, d, e = w.shape
    eye = jnp.eye(n, dtype=w.dtype)
    return (eye[:, None, :, None] * w[:, :, None, :]).reshape(n * d, n * e)


def _pad_lanes(v, offset):
    out = jnp.zeros((1, LANES), F32)
    return out.at[0, offset:offset + v.shape[0]].set(v)


def _mixer_params(l, w_in, lru_conv_w, lru_conv_b, lru_wa, lru_ba, lru_wx, lru_bx, lru_lambda,
                  gdn_conv_w, gdn_a_log, gdn_dt_bias, gdn_norm_g, gla_wg2, gla_bg, gla_norm_g):
    w = w_in[l]
    w_in_r = jnp.concatenate(
        [w[:, 0:2048], w[:, 2056:2824], w[:, 2048:2056], w[:, 2824:2840],
         jnp.zeros((D_MODEL, D_INP - 2840), F32)], axis=1).astype(BF16)
    row = lambda v: v.reshape(1, -1)
    mp = dict(
        lcw=lru_conv_w[l], lcb=row(lru_conv_b[l]),
        wa=_block_diag(lru_wa[l]).astype(BF16), ba=row(lru_ba[l]),
        wx=_block_diag(lru_wx[l]).astype(BF16), bx=row(lru_bx[l]),
        lam=row(lru_lambda[l]),
        gcw=gdn_conv_w[l],
        gpar=jnp.concatenate([_pad_lanes(gdn_a_log[l], GDN_H), _pad_lanes(gdn_dt_bias[l], GDN_H)], axis=0),
        gng=row(jnp.tile(gdn_norm_g[l], GDN_H)),
        wg2=jnp.zeros((LANES, LANES), F32).at[2 * GDN_H:2 * GDN_H + GLA_RANK].set(gla_wg2[l]),
        gbg=row(gla_bg[l]),
        lng=row(jnp.tile(gla_norm_g[l], GLA_H)),
    )
    return w_in_r, mp


def kernel(x_prompt, x_sample, state_lru_conv, state_lru_h, state_gdn_conv, state_gdn_S, state_gla_S, c_prompt, c_sample, ada_w, ada_b, norm1_g, norm2_g, w_in, lru_conv_w, lru_conv_b, lru_wa, lru_ba, lru_wx, lru_bx, lru_lambda, gdn_conv_w, gdn_a_log, gdn_dt_bias, gdn_norm_g, gla_wg2, gla_bg, gla_norm_g, w_out, router_w, router_b, exp_w_gate, exp_b_gate, exp_w_up, exp_b_up, exp_w_down, exp_b_down, final_norm_g):
    x = jnp.concatenate([
        x_prompt.transpose(1, 0, 2).reshape(NP, D_MODEL),
        x_sample.transpose(1, 0, 2).reshape(NS, D_MODEL)], axis=0)
    mod = _mod_call(jnp.concatenate([c_prompt, c_sample], axis=0), ada_w, ada_b)
    mod = jnp.stack([jnp.tile(mod[:, :BATCH], (1, MODB // BATCH, 1)), mod[:, BATCH:]], axis=1)

    p_states, s_states = [], []
    moe = None
    for l in range(DEPTH):
        w_in_r, mp = _mixer_params(l, w_in, lru_conv_w, lru_conv_b, lru_wa, lru_ba, lru_wx, lru_bx,
                                   lru_lambda, gdn_conv_w, gdn_a_log, gdn_dt_bias, gdn_norm_g,
                                   gla_wg2, gla_bg, gla_norm_g)
        x, proj = _in_call(x, moe, mod[l], norm1_g[l].reshape(1, D_MODEL), w_in_r,
                           mod[l - 1] if l else None)
        mix_p, p_lconv, p_lh, p_gconv, p_gs, p_lst = _mix_prompt_call(proj, mp)
        states = (state_lru_conv[l].transpose(1, 0, 2), state_lru_h[l],
                  state_gdn_conv[l].transpose(1, 0, 2),
                  state_gdn_S[l].reshape(DEC_BATCH, -1), state_gla_S[l].reshape(DEC_BATCH, -1))
        mix_s, s_lconv, s_lh, s_gconv, s_gs, s_ls = _mix_sample_call(proj, states, mp)
        rw = jnp.zeros((D_MODEL, LANES), F32).at[:, :N_EXP].set(router_w[l])
        rb = jnp.zeros((1, LANES), F32).at[0, :N_EXP].set(router_b[l])
        x, h2, ridx, rprob = _out_call(mix_p, mix_s.reshape(NS, D_MODEL), x, mod[l],
                                       w_out[l].astype(BF16), norm2_g[l].reshape(1, D_MODEL), rw, rb)
        moe = _moe(l, h2, ridx, rprob, exp_w_gate, exp_b_gate, exp_w_up, exp_b_up, exp_w_down, exp_b_down)
        p_gs = jnp.stack([p_gs[:, h * GDN_DK:(h + 1) * GDN_DK, h * GDN_DV:(h + 1) * GDN_DV]
                          for h in range(GDN_H)], axis=1)
        p_lst = jnp.stack([p_lst[:, h * GLA_DV:(h + 1) * GLA_DV, h * GLA_DK:(h + 1) * GLA_DK]
                           for h in range(GLA_H)], axis=1)
        p_states.append((p_lconv.reshape(3, BATCH, LRU_W).transpose(1, 0, 2), p_lh,
                         p_gconv.reshape(3, BATCH, GDN_CONV).transpose(1, 0, 2), p_gs,
                         p_lst.transpose(0, 1, 3, 2)))
        s_states.append((s_lconv.transpose(1, 0, 2), s_lh, s_gconv.transpose(1, 0, 2),
                         s_gs.reshape(DEC_BATCH, GDN_H, GDN_DK, GDN_DV),
                         s_ls.reshape(DEC_BATCH, GLA_H, GLA_DK, GLA_DV)))
    y = _final_call(x, moe, mod[DEPTH - 1], final_norm_g.reshape(1, D_MODEL))
    y_prompt = y[:NP].reshape(SEQ, BATCH, D_MODEL).transpose(1, 0, 2)
    y_sample = y[NP:].reshape(DEC_SEQ, DEC_BATCH, D_MODEL).transpose(1, 0, 2)
    ps = [jnp.stack([s[j] for s in p_states]) for j in range(5)]
    ss = [jnp.stack([s[j] for s in s_states]) for j in range(5)]
    return (y_prompt, y_sample, *ps, *ss)
```

```python
import functools

import numpy as np
import jax
import jax.numpy as jnp
from jax import lax
from jax.experimental import pallas as pl
from jax.experimental.pallas import tpu as pltpu

F32 = jnp.float32
BF16 = jnp.bfloat16

D_MODEL = 1024
BATCH = 8
SEQ = 2048
DEPTH = 2
DEC_BATCH = 128
DEC_SEQ = 4
CONV_W = 4
LRU_W = 512
LRU_BLOCKS = 8
LRU_C = 8.0
GDN_H = 4
GDN_DK = 64
GDN_DV = 64
GDN_CONV = GDN_H * (2 * GDN_DK + GDN_DV)
GLA_H = 4
GLA_DK = 32
GLA_DV = 64
GLA_RANK = 16
GLA_TAU = 16.0
GLA_CHUNK = 16
N_EXP = 32
TOP_K = 4
SW_LIMIT = 7.0
SW_ALPHA = 1.702
EPS = 1e-6

NP = BATCH * SEQ
NS = DEC_BATCH * DEC_SEQ
NTOK = NP + NS
TM = 512
NPT = NP // TM
NTILES = NTOK // TM
LANES = 128
MODB = 128

C_AX, C_AG, C_QKV, C_BZ = 0, 512, 1024, 1792
C_CQ, C_CK, C_CV, C_CR, C_SM = 2048, 2176, 2304, 2560, 2816
D_INP = 2944

TT = 64
BB = 32
TE = 256
TD = 256
PIECE = 8
LP = 1280
NA = NTOK * TOP_K
NROWS = -(-(NA + (NTOK // TD) * N_EXP * (PIECE - 1) + N_EXP * (TE - 1)) // TE) * TE
NTE = NROWS // TE

VMEM_LIMIT = 50 * 1024 * 1024


def _dot(a, b):
    return jnp.dot(a, b, preferred_element_type=F32)


def _dot_nt(a, b):
    return lax.dot_general(a, b, (((1,), (1,)), ((), ())), preferred_element_type=F32)


def _dot_tn(a, b):
    return lax.dot_general(a, b, (((0,), (0,)), ((), ())), preferred_element_type=F32)


def _mm(a, b):
    return _dot(a.astype(BF16), b.astype(BF16))


def _mm_nt(a, b):
    return _dot_nt(a.astype(BF16), b.astype(BF16))


def _mm_tn(a, b):
    return _dot_tn(a.astype(BF16), b.astype(BF16))


def _split3(x):
    x1 = x.astype(BF16)
    r = x - x1.astype(F32)
    x2 = r.astype(BF16)
    x3 = (r - x2.astype(F32)).astype(BF16)
    return x1, x2, x3


def _dot3(a, b):
    a1 = a.astype(BF16)
    a2 = (a - a1.astype(F32)).astype(BF16)
    b1 = b.astype(BF16)
    b2 = (b - b1.astype(F32)).astype(BF16)
    return _dot(a1, b1) + (_dot(a2, b1) + _dot(a1, b2))


def _dot_mask_l(mask, x):
    x1, x2, x3 = _split3(x)
    return _dot(mask, x1) + (_dot(mask, x2) + _dot(mask, x3))


def _dot_mask_r(x, mask):
    x1 = x.astype(BF16)
    x2 = (x - x1.astype(F32)).astype(BF16)
    return _dot(x1, mask) + _dot(x2, mask)


def _dot_mask_r3(x, mask):
    x1, x2, x3 = _split3(x)
    return _dot(x1, mask) + (_dot(x2, mask) + _dot(x3, mask))


def _sigmoid(x):
    return jax.nn.sigmoid(x)


def _silu(x):
    return x * jax.nn.sigmoid(x)


def _softplus(x):
    return jnp.maximum(x, 0.0) + jnp.log1p(jnp.exp(-jnp.abs(x)))


def _rms_rows(x, g):
    return x * lax.rsqrt(jnp.mean(x * x, axis=-1, keepdims=True) + EPS) * g


def _modulate(y, scale, shift):
    rows = y.shape[0]
    y3 = y.reshape(rows // MODB, MODB, y.shape[1])
    return (y3 * (1.0 + scale) + shift).reshape(y.shape)


def _gate_res(x, gate, y):
    rows = y.shape[0]
    y3 = y.reshape(rows // MODB, MODB, y.shape[1])
    return x + (gate * y3).reshape(y.shape)


def _mod_kernel(c_ref, w_ref, b_ref, o_ref):
    o_ref[...] = _dot3(_silu(c_ref[...]), w_ref[...]) + b_ref[...]


def _mod_call(c_all, ada_w, ada_b):
    tn = 768
    rows = c_all.shape[0]
    return pl.pallas_call(
        _mod_kernel,
        grid=(DEPTH, 6 * D_MODEL // tn),
        in_specs=[
            pl.BlockSpec((rows, D_MODEL), lambda l, j: (0, 0)),
            pl.BlockSpec((None, D_MODEL, tn), lambda l, j: (l, 0, j)),
            pl.BlockSpec((None, 1, tn), lambda l, j: (l, 0, j)),
        ],
        out_specs=pl.BlockSpec((None, rows, tn), lambda l, j: (l, 0, j)),
        out_shape=jax.ShapeDtypeStruct((DEPTH, rows, 6 * D_MODEL), F32),
        compiler_params=pltpu.CompilerParams(
            dimension_semantics=("arbitrary", "arbitrary"), vmem_limit_bytes=VMEM_LIMIT),
        name="adaln_mod",
    )(c_all, ada_w, ada_b.reshape(DEPTH, 1, 6 * D_MODEL))


def _mod_spec(chunk):
    return pl.BlockSpec((None, MODB, D_MODEL), lambda i: (i // NPT, 0, chunk))


def _row_spec(width):
    return pl.BlockSpec((TM, width), lambda i: (i, 0))


def _full_spec(shape):
    nd = len(shape)
    return pl.BlockSpec(shape, lambda i: (0,) * nd)


def _in_kernel(has_res, *refs):
    if has_res:
        (x1_ref, moe_ref, g2_ref, n1_ref, sc_ref, sh_ref, w_ref, x_ref, proj_ref) = refs
        x = _gate_res(x1_ref[...], g2_ref[...], moe_ref[...])
        x_ref[...] = x
    else:
        (x_ref, n1_ref, sc_ref, sh_ref, w_ref, proj_ref) = refs
        x = x_ref[...]
    h = _modulate(_rms_rows(x, n1_ref[...]), sc_ref[...], sh_ref[...])
    proj_ref[...] = _dot(h.astype(BF16), w_ref[...])


def _in_call(x, moe, mod_l, norm_g, w_in_r, prev_mod):
    has_res = moe is not None
    ins, specs = [x], [_row_spec(D_MODEL)]
    outs = [jax.ShapeDtypeStruct((NTOK, D_INP), F32)]
    out_specs = [_row_spec(D_INP)]
    if has_res:
        ins += [moe, prev_mod]
        specs += [_row_spec(D_MODEL), _mod_spec(5)]
        outs = [jax.ShapeDtypeStruct((NTOK, D_MODEL), F32)] + outs
        out_specs = [_row_spec(D_MODEL)] + out_specs
    ins += [norm_g, mod_l, mod_l, w_in_r]
    specs += [_full_spec((1, D_MODEL)), _mod_spec(1), _mod_spec(0), _full_spec((D_MODEL, D_INP))]
    res = pl.pallas_call(
        functools.partial(_in_kernel, has_res),
        grid=(NTILES,),
        in_specs=specs,
        out_specs=out_specs,
        out_shape=outs,
        compiler_params=pltpu.CompilerParams(
            dimension_semantics=("arbitrary",), vmem_limit_bytes=VMEM_LIMIT),
        name="in_proj",
    )(*ins)
    return res if has_res else (x, res[0])


def _lru_gates(xc, wa_ref, ba_ref, wx_ref, bx_ref, lam_ref):
    xb = xc.astype(BF16)
    r = _sigmoid(_dot(xb, wa_ref[...]) + ba_ref[...])
    i = _sigmoid(_dot(xb, wx_ref[...]) + bx_ref[...])
    log_a = -LRU_C * r * _softplus(-lam_ref[...])
    a = jnp.exp(log_a)
    mult = jnp.sqrt(1.0 - jnp.exp(2.0 * log_a))
    return a, mult, i * xc


def _gelu(x):
    return jax.nn.gelu(x, approximate=True)


def _gdn_prep(qkv, small, gpar_ref, ones_blk):
    qkv = _silu(qkv)
    q = qkv[:, 0:256]
    k = qkv[:, 256:512]
    v = qkv[:, 512:768]
    q = q * lax.rsqrt(_dot_mask_r(q * q, ones_blk) + EPS) * (GDN_DK ** -0.5)
    k = k * lax.rsqrt(_dot_mask_r(k * k, ones_blk) + EPS)
    beta = _sigmoid(small)
    g = -jnp.exp(gpar_ref[0:1, :]) * _softplus(small + gpar_ref[1:2, :])
    lane = lax.broadcasted_iota(jnp.int32, small.shape, 1)
    bg = jnp.where(lane < GDN_H, beta, g)
    return q, k, v, bg


def _gla_prep(proj_q, proj_k, small, wg2_ref, gbg_ref):
    pre = _dot3(small, wg2_ref[...]) + gbg_ref[...]
    gk = -_softplus(-pre) / GLA_TAU
    return proj_q * (GLA_DK ** -0.5), proj_k, gk


def _head_norm_gate(o, norm_g, gate_in, ones_blk):
    ms = _dot_mask_r(o * o, ones_blk) * (1.0 / GDN_DV)
    return o * lax.rsqrt(ms + EPS) * norm_g * _silu(gate_in)


def _mix_prompt_kernel(proj_ref, lcw_ref, lcb_ref, wa_ref, ba_ref, wx_ref, bx_ref, lam_ref,
                       gcw_ref, gpar_ref, gng_ref, wg2_ref, gbg_ref, lng_ref,
                       ones_ref, tri_ref, e2_ref, btblk_ref,
                       mix_ref, lconv_ref, lh_ref, gconv_ref, gs_ref, lst_ref,
                       exta, extb, a_s, u_s, hs_s, q_s, k_s, v_s, bg_s, og_s,
                       q2_s, k2_s, v2_s, gk_s, ol_s):
    nb = BATCH
    rows = TT * nb
    step = pl.program_id(0)

    @pl.when(step == 0)
    def _():
        exta[0:3 * nb, :] = jnp.zeros((3 * nb, LRU_W), F32)
        extb[0:3 * nb, :] = jnp.zeros((3 * nb, GDN_CONV), F32)
        lh_ref[...] = jnp.zeros_like(lh_ref)
        gs_ref[...] = jnp.zeros_like(gs_ref)
        lst_ref[...] = jnp.zeros_like(lst_ref)

    def conv(ext, x, w_ref):
        ext[pl.ds(3 * nb, rows), :] = x
        y = ext[pl.ds(0, rows), :] * w_ref[0:1, :]
        for j in range(1, CONV_W):
            y = y + ext[pl.ds(j * nb, rows), :] * w_ref[j:j + 1, :]
        tail = ext[pl.ds(rows, 3 * nb), :]
        ext[pl.ds(0, 3 * nb), :] = tail
        return y, tail

    ones_blk = ones_ref[...]

    xa, tail = conv(exta, proj_ref[:, C_AX:C_AX + LRU_W], lcw_ref)
    lconv_ref[...] = tail
    xc = xa + lcb_ref[...]
    a, mult, ix = _lru_gates(xc, wa_ref, ba_ref, wx_ref, bx_ref, lam_ref)
    rid = lax.broadcasted_iota(jnp.int32, (rows, LRU_W), 0)
    mult = jnp.where((rid < nb) & (step == 0), 1.0, mult)
    a_s[...] = a
    u_s[...] = mult * ix

    def scan_body(t, h):
        off = pl.multiple_of(t * nb, nb)
        h = a_s[pl.ds(off, nb), :] * h + u_s[pl.ds(off, nb), :]
        hs_s[pl.ds(off, nb), :] = h
        return h

    lh_ref[...] = lax.fori_loop(0, TT, scan_body, lh_ref[...], unroll=8)
    mix_ref[:, 0:LRU_W] = (hs_s[...] * _gelu(proj_ref[:, C_AG:C_AG + LRU_W])).astype(BF16)

    small = proj_ref[:, C_SM:C_SM + LANES]
    qkv, tail = conv(extb, proj_ref[:, C_QKV:C_QKV + GDN_CONV], gcw_ref)
    gconv_ref[...] = tail
    q, k, v, bg = _gdn_prep(qkv, small, gpar_ref, ones_blk)
    for j in range(2):
        ls = slice(j * LANES, (j + 1) * LANES)
        q_s[j] = q[:, ls]
        k_s[j] = k[:, ls]
        v_s[j] = v[:, ls]
    bg_s[...] = bg

    q2, k2, gk = _gla_prep(proj_ref[:, C_CQ:C_CQ + 128], proj_ref[:, C_CK:C_CK + 128],
                           small, wg2_ref, gbg_ref)
    q2_s[...] = q2
    k2_s[...] = k2
    gk_s[...] = gk
    for j in range(2):
        v2_s[j] = proj_ref[:, C_CV + j * LANES:C_CV + (j + 1) * LANES]

    ri = lax.broadcasted_iota(jnp.int32, (TT, 256), 0)
    lane = lax.broadcasted_iota(jnp.int32, (TT, 256), 1)
    cj = lane % TT
    incl = ri >= cj
    strict = ri > cj
    eye = ri == cj
    blk16m = (ri // 16) == (cj // 16)
    blk32m = (ri // 32) == (cj // 32)
    blk_causal = incl & blk16m
    hmask = [(lane // GDN_DV == h).astype(BF16) for h in range(GDN_H)]
    lane2 = lax.broadcasted_iota(jnp.int32, (TT, LANES), 1)
    hmask2 = [(lane2 // GLA_DK == h).astype(BF16) for h in range(GLA_H)]
    bdm = (lax.broadcasted_iota(jnp.int32, (256, 256), 0) // GDN_DK
           == lax.broadcasted_iota(jnp.int32, (256, 256), 1) // GDN_DV)
    bdm2 = (lax.broadcasted_iota(jnp.int32, (256, LANES), 0) // GLA_DV
            == lax.broadcasted_iota(jnp.int32, (256, LANES), 1) // GLA_DK)
    tri = tri_ref[...]
    e2 = e2_ref[...]
    btblk = btblk_ref[...]
    ones8 = jnp.ones((8, TT), BF16)
    seqs = range(nb)
    rsel = [pl.ds(b, TT, stride=nb) for b in seqs]

    def bd(x):
        xb = x.astype(BF16)
        return jnp.concatenate([xb * m for m in hmask], axis=0)

    def bd2(x):
        xb = x.astype(BF16)
        return jnp.concatenate([xb * m for m in hmask2], axis=0)

    def cat2(ref, b):
        return jnp.concatenate([ref[0, rsel[b], :], ref[1, rsel[b], :]], axis=1)

    def each(f, *lists):
        return [f(*xs) for xs in zip(*lists)]

    q = [cat2(q_s, b) for b in seqs]
    k = [cat2(k_s, b) for b in seqs]
    v = [cat2(v_s, b) for b in seqs]
    bgb = [bg_s[rsel[b], :] for b in seqs]
    gc_all = each(lambda x: _dot_mask_l(tri, x), bgb)
    both = each(lambda x, c: _dot_mask_r3(jnp.where(lane2 < GDN_H, x, c), e2), bgb, gc_all)
    beta = [x[:, 0:256] for x in both]
    gc = [x[:, 256:512] for x in both]
    gc_row = each(lambda g: _dot_mask_l(ones8, jnp.where(eye, g, 0.0))[0:1, :], gc)
    decay = each(lambda g, gr: jnp.exp(jnp.where(incl, g - gr, -jnp.inf)), gc, gc_row)
    egc = each(jnp.exp, gc)
    kb = each(lambda a, b_: a * b_, k, beta)
    bdk = each(bd, k)
    kq = each(lambda kb_, q_, m: _dot_nt(jnp.concatenate([kb_, q_], axis=0).astype(BF16), m), kb, q, bdk)
    amat = each(lambda x, d: jnp.where(strict, x[0:TT] * d, 0.0), kq, decay)
    qk = each(lambda x, d: jnp.where(incl, x[TT:] * d, 0.0), kq, decay)
    y = each(lambda a: -jnp.where(blk16m, a, 0.0), amat)
    n = y
    bdy = each(bd, y)
    for _ in range(3):
        y = each(lambda y_, m: _dot(y_.astype(BF16), m), y, bdy)
        bdy = each(bd, y)
        n = each(lambda n_, y_, m: n_ + y_ + _dot(n_.astype(BF16), m), n, y, bdy)
    t = each(lambda n_: jnp.where(eye, 1.0, 0.0) + n_, n)
    for lower in (each(lambda a: jnp.where(blk32m & ~blk16m, a, 0.0), amat),
                  each(lambda a: jnp.where(blk32m, 0.0, a), amat)):
        tl = each(lambda t_, l_: _dot(t_.astype(BF16), bd(l_)), t, lower)
        t = each(lambda t_, x: t_ - _dot(x.astype(BF16), bd(t_)), t, tl)
    uw = each(lambda t_, v_, b_, kb_, e: _dot(
        t_.astype(BF16), jnp.concatenate([bd(v_ * b_), bd(kb_ * e)], axis=1)), t, v, beta, kb, egc)
    s = [gs_ref[b] for b in seqs]
    ws = each(lambda x, q_, e, s_: _dot(
        jnp.concatenate([x[:, 256:512], q_ * e], axis=0).astype(BF16), s_.astype(BF16)), uw, q, egc, s)
    vn = each(lambda x, w_: x[:, 0:256] - w_[0:TT], uw, ws)
    o = each(lambda w_, a, v_: w_[TT:] + _dot(a.astype(BF16), bd(v_)), ws, qk, vn)
    for b in seqs:
        g_last = gc[b][TT - 1:TT, :]
        kd = k[b] * jnp.exp(g_last - gc[b])
        gs_ref[b] = s[b] * jnp.exp(g_last) + jnp.where(bdm, _mm_tn(kd, vn[b]), 0.0)
        for j in range(2):
            og_s[j, rsel[b], :] = o[b][:, j * LANES:(j + 1) * LANES]

    cum = [_dot_mask_l(btblk, gk_s[rsel[b], :]) for b in seqs]
    bc = [x[0:TT] for x in cum]
    bl = [x[TT:] for x in cum]
    qb = [q2_s[rsel[b], :] for b in seqs]
    kb2 = [k2_s[rsel[b], :] for b in seqs]
    vb = [cat2(v2_s, b) for b in seqs]
    qi = each(lambda a, c: (a * jnp.exp(c)).astype(BF16), qb, bc)
    ki = each(lambda a, c: a * jnp.exp(-c), kb2, bc)
    kst = each(lambda a, l_, c: (a * jnp.exp(l_ - c)).astype(BF16), kb2, bl, bc)
    ebl = each(jnp.exp, bl)
    sc = each(lambda a, c: jnp.where(blk_causal, _dot_nt(a, bd2(c)), 0.0), qi, ki)
    oh = each(lambda a, v_: _dot(a.astype(BF16), bd(v_)), sc, vb)
    st = [lst_ref[b] for b in seqs]
    vbb = each(lambda v_: v_.astype(BF16), vb)
    outs = [[] for _ in seqs]
    for c in range(TT // GLA_CHUNK):
        rs = slice(c * GLA_CHUNK, (c + 1) * GLA_CHUNK)
        for b in seqs:
            outs[b].append(oh[b][rs] + _dot_nt(qi[b][rs], st[b].astype(BF16)))
        upd = [jnp.where(bdm2, _dot_tn(vbb[b][rs], kst[b][rs]), 0.0) for b in seqs]
        st = [st[b] * ebl[b][c * GLA_CHUNK:c * GLA_CHUNK + 1, :] + upd[b] for b in seqs]
    for b in seqs:
        lst_ref[b] = st[b]
        ol = jnp.concatenate(outs[b], axis=0)
        for j in range(2):
            ol_s[j, rsel[b], :] = ol[:, j * LANES:(j + 1) * LANES]

    og = jnp.concatenate([og_s[0], og_s[1]], axis=1)
    out_b = _head_norm_gate(og, gng_ref[...], proj_ref[:, C_BZ:C_BZ + 256], ones_blk)
    mix_ref[:, 512:768] = out_b.astype(BF16)
    ol = jnp.concatenate([ol_s[0], ol_s[1]], axis=1)
    out_c = _head_norm_gate(ol, lng_ref[...], proj_ref[:, C_CR:C_CR + 256], ones_blk)
    mix_ref[:, 768:1024] = out_c.astype(BF16)


def _mix_consts():
    r = np.arange(TT)
    tri = (r[:, None] >= r[None, :]).astype(np.float32)
    same = (r[:, None] // GLA_CHUNK) == (r[None, :] // GLA_CHUNK)
    bt16 = (tri.astype(bool) & same).astype(np.float32)
    blk16 = same.astype(np.float32)
    c = np.arange(256)
    ones_blk = ((c[:, None] // GDN_DV) == (c[None, :] // GDN_DV)).astype(np.float32)
    e2 = np.zeros((LANES, 512), np.float32)
    for h in range(GDN_H):
        e2[h, h * GDN_DV:(h + 1) * GDN_DV] = 1.0
        e2[GDN_H + h, 256 + h * GDN_DV:256 + (h + 1) * GDN_DV] = 1.0
    btblk = np.concatenate([bt16, blk16], axis=0)
    return (jnp.asarray(ones_blk, BF16), jnp.asarray(tri, BF16),
            jnp.asarray(e2, BF16), jnp.asarray(btblk, BF16))


def _mix_prompt_call(proj, mp):
    rows = TT * BATCH
    consts = _mix_consts()
    params = (mp["lcw"], mp["lcb"], mp["wa"], mp["ba"], mp["wx"], mp["bx"], mp["lam"],
              mp["gcw"], mp["gpar"], mp["gng"], mp["wg2"], mp["gbg"], mp["lng"]) + consts
    out_shape = [
        jax.ShapeDtypeStruct((NP, D_MODEL), BF16),
        jax.ShapeDtypeStruct((3 * BATCH, LRU_W), F32),
        jax.ShapeDtypeStruct((BATCH, LRU_W), F32),
        jax.ShapeDtypeStruct((3 * BATCH, GDN_CONV), F32),
        jax.ShapeDtypeStruct((BATCH, GDN_H * GDN_DK, GDN_H * GDN_DV), F32),
        jax.ShapeDtypeStruct((BATCH, GLA_H * GLA_DV, GLA_H * GLA_DK), F32),
    ]
    out_specs = [pl.BlockSpec((rows, D_MODEL), lambda i: (i, 0))] + [
        _full_spec(s.shape) for s in out_shape[1:]]
    scratch = [
        pltpu.VMEM((rows + 3 * BATCH, LRU_W), F32),
        pltpu.VMEM((rows + 3 * BATCH, GDN_CONV), F32),
        pltpu.VMEM((rows, LRU_W), F32), pltpu.VMEM((rows, LRU_W), F32), pltpu.VMEM((rows, LRU_W), F32),
        pltpu.VMEM((2, rows, LANES), F32), pltpu.VMEM((2, rows, LANES), F32), pltpu.VMEM((2, rows, LANES), F32),
        pltpu.VMEM((rows, LANES), F32), pltpu.VMEM((2, rows, LANES), F32),
        pltpu.VMEM((rows, LANES), F32), pltpu.VMEM((rows, LANES), F32), pltpu.VMEM((2, rows, LANES), F32),
        pltpu.VMEM((rows, LANES), F32), pltpu.VMEM((2, rows, LANES), F32),
    ]
    return pl.pallas_call(
        _mix_prompt_kernel,
        grid=(SEQ // TT,),
        in_specs=[pl.BlockSpec((rows, D_INP), lambda i: (i, 0))] + [_full_spec(p.shape) for p in params],
        out_specs=out_specs,
        out_shape=out_shape,
        scratch_shapes=scratch,
        compiler_params=pltpu.CompilerParams(
            dimension_semantics=("arbitrary",), vmem_limit_bytes=VMEM_LIMIT),
        name="mix_prompt",
    )(proj, *params)


def _pair_bcast(x, p, lo_mask):
    return jnp.where(lo_mask, x[:, 2 * p:2 * p + 1], x[:, 2 * p + 1:2 * p + 2])


def _fold_pairs(acc):
    return acc[:, 0:64] + acc[:, 64:128]


def _mix_sample_kernel(p0_ref, p1_ref, p2_ref, p3_ref, lconv_in, lh_in, gconv_in, gs_in, ls_in,
                       lcw_ref, lcb_ref, wa_ref, ba_ref, wx_ref, bx_ref, lam_ref,
                       gcw_ref, gpar_ref, gng_ref, wg2_ref, gbg_ref, lng_ref, ones_ref,
                       mix_ref, lconv_ref, lh_ref, gconv_ref, gs_ref, ls_ref):
    rows = DEC_SEQ * BB
    prefs = (p0_ref, p1_ref, p2_ref, p3_ref)
    ones_blk = ones_ref[...]

    def cols(c0, width):
        return [p[:, c0:c0 + width] for p in prefs]

    def conv(prev, xs, w_ref):
        ext = [prev[j] for j in range(CONV_W - 1)] + xs
        ys = []
        for t in range(DEC_SEQ):
            y = ext[t] * w_ref[0:1, :]
            for j in range(1, CONV_W):
                y = y + ext[t + j] * w_ref[j:j + 1, :]
            ys.append(y)
        return jnp.concatenate(ys, axis=0), ext[DEC_SEQ:]

    def rows_of(x, t):
        return x[t * BB:(t + 1) * BB]

    xa, tail = conv(lconv_in, cols(C_AX, LRU_W), lcw_ref)
    for j in range(CONV_W - 1):
        lconv_ref[j] = tail[j]
    xc = xa + lcb_ref[...]
    a, mult, ix = _lru_gates(xc, wa_ref, ba_ref, wx_ref, bx_ref, lam_ref)
    u = mult * ix
    h = lh_in[...]
    hs = []
    for t in range(DEC_SEQ):
        h = rows_of(a, t) * h + rows_of(u, t)
        hs.append(h)
    lh_ref[...] = h
    ag = jnp.concatenate(cols(C_AG, LRU_W), axis=0)
    mix_a = jnp.concatenate(hs, axis=0) * _gelu(ag)

    small = jnp.concatenate(cols(C_SM, LANES), axis=0)
    qkv, tail = conv(gconv_in, cols(C_QKV, GDN_CONV), gcw_ref)
    for j in range(CONV_W - 1):
        gconv_ref[j] = tail[j]
    q, k, v, bg = _gdn_prep(qkv, small, gpar_ref, ones_blk)
    lo_mask = lax.broadcasted_iota(jnp.int32, (BB, LANES), 1) < 64
    gs_ref[...] = gs_in[...]
    hd = GDN_DK * GDN_DV
    o_heads = []
    for h in range(GDN_H):
        o_t = []
        for t in range(DEC_SEQ):
            rs = slice(t * BB, (t + 1) * BB)
            cs = slice(h * GDN_DK, (h + 1) * GDN_DK)
            eg = jnp.exp(bg[rs, GDN_H + h:GDN_H + h + 1])
            beta = bg[rs, h:h + 1]
            kt, qt, vt = k[rs, cs], q[rs, cs], v[rs, cs]
            kks = [_pair_bcast(kt, p, lo_mask) for p in range(GDN_DK // 2)]
            acc = jnp.zeros((BB, LANES), F32)
            for p in range(GDN_DK // 2):
                acc = acc + gs_ref[:, h * hd + p * LANES:h * hd + (p + 1) * LANES] * kks[p]
            vn = beta * (vt - eg * _fold_pairs(acc))
            vn2 = jnp.concatenate([vn, vn], axis=1)
            oacc = jnp.zeros((BB, LANES), F32)
            for p in range(GDN_DK // 2):
                sl = slice(h * hd + p * LANES, h * hd + (p + 1) * LANES)
                s = eg * gs_ref[:, sl] + kks[p] * vn2
                gs_ref[:, sl] = s
                oacc = oacc + s * _pair_bcast(qt, p, lo_mask)
            o_t.append(_fold_pairs(oacc))
        o_heads.append(jnp.concatenate(o_t, axis=0))
    o_b = jnp.concatenate(o_heads, axis=1)
    bz = jnp.concatenate(cols(C_BZ, 256), axis=0)
    mix_b = _head_norm_gate(o_b, gng_ref[...], bz, ones_blk)

    q2, k2, gk = _gla_prep(jnp.concatenate(cols(C_CQ, 128), axis=0),
                           jnp.concatenate(cols(C_CK, 128), axis=0), small, wg2_ref, gbg_ref)
    v2 = jnp.concatenate(cols(C_CV, 256), axis=0)
    ls_ref[...] = ls_in[...]
    hd2 = GLA_DK * GLA_DV
    o_heads = []
    for h in range(GLA_H):
        o_t = []
        for t in range(DEC_SEQ):
            rs = slice(t * BB, (t + 1) * BB)
            cs = slice(h * GLA_DK, (h + 1) * GLA_DK)
            al = jnp.exp(gk[rs, cs])
            kt, qt = k2[rs, cs], q2[rs, cs]
            vt = v2[rs, h * GLA_DV:(h + 1) * GLA_DV]
            v2x = jnp.concatenate([vt, vt], axis=1)
            oacc = jnp.zeros((BB, LANES), F32)
            for p in range(GLA_DK // 2):
                sl = slice(h * hd2 + p * LANES, h * hd2 + (p + 1) * LANES)
                s = _pair_bcast(al, p, lo_mask) * ls_ref[:, sl] + _pair_bcast(kt, p, lo_mask) * v2x
                ls_ref[:, sl] = s
                oacc = oacc + s * _pair_bcast(qt, p, lo_mask)
            o_t.append(_fold_pairs(oacc))
        o_heads.append(jnp.concatenate(o_t, axis=0))
    o_c = jnp.concatenate(o_heads, axis=1)
    cr = jnp.concatenate(cols(C_CR, 256), axis=0)
    mix_c = _head_norm_gate(o_c, lng_ref[...], cr, ones_blk)

    mix = jnp.concatenate([mix_a, mix_b, mix_c], axis=1).astype(BF16)
    for t in range(DEC_SEQ):
        mix_ref[t] = mix[t * BB:(t + 1) * BB]


def _mix_sample_call(proj, states, mp):
    lconv, lh, gconv, gs, ls = states
    ones_blk = _mix_consts()[0]
    params = (mp["lcw"], mp["lcb"], mp["wa"], mp["ba"], mp["wx"], mp["bx"], mp["lam"],
              mp["gcw"], mp["gpar"], mp["gng"], mp["wg2"], mp["gbg"], mp["lng"], ones_blk)
    nblk = DEC_BATCH // BB
    base = NP // BB

    def proj_spec(t):
        return pl.BlockSpec((BB, D_INP), lambda j: (base + t * nblk + j, 0))

    def bspec3(n, width):
        return pl.BlockSpec((n, BB, width), lambda j: (0, j, 0))

    def bspec2(width):
        return pl.BlockSpec((BB, width), lambda j: (j, 0))

    gdn_flat = GDN_H * GDN_DK * GDN_DV
    gla_flat = GLA_H * GLA_DK * GLA_DV
    state_specs = [bspec3(3, LRU_W), bspec2(LRU_W), bspec3(3, GDN_CONV), bspec2(gdn_flat), bspec2(gla_flat)]
    out_shape = [
        jax.ShapeDtypeStruct((DEC_SEQ, DEC_BATCH, D_MODEL), BF16),
        jax.ShapeDtypeStruct((3, DEC_BATCH, LRU_W), F32),
        jax.ShapeDtypeStruct((DEC_BATCH, LRU_W), F32),
        jax.ShapeDtypeStruct((3, DEC_BATCH, GDN_CONV), F32),
        jax.ShapeDtypeStruct((DEC_BATCH, gdn_flat), F32),
        jax.ShapeDtypeStruct((DEC_BATCH, gla_flat), F32),
    ]
    return pl.pallas_call(
        _mix_sample_kernel,
        grid=(nblk,),
        in_specs=[proj_spec(t) for t in range(DEC_SEQ)] + state_specs + [_full_spec(p.shape) for p in params],
        out_specs=[bspec3(DEC_SEQ, D_MODEL)] + state_specs,
        out_shape=out_shape,
        compiler_params=pltpu.CompilerParams(
            dimension_semantics=("arbitrary",), vmem_limit_bytes=VMEM_LIMIT),
        name="mix_sample",
    )(proj, proj, proj, proj, lconv, lh, gconv, gs, ls, *params)


def _out_kernel(mp_ref, ms_ref, x_ref, w_ref, g1_ref, n2_ref, sc_ref, sh_ref, rw_ref, rb_ref,
                x1_ref, h2_ref, ridx_ref, rprob_ref):
    step = pl.program_id(0)
    mix = jnp.where(step < NPT, mp_ref[...], ms_ref[...])
    x1 = _gate_res(x_ref[...], g1_ref[...], _dot(mix, w_ref[...]))
    x1_ref[...] = x1
    h2 = _modulate(_rms_rows(x1, n2_ref[...]), sc_ref[...], sh_ref[...])
    h2_ref[...] = h2
    logits = _dot3(h2, rw_ref[...]) + rb_ref[...]
    lane = lax.broadcasted_iota(jnp.int32, logits.shape, 1)
    cur = jnp.where(lane < N_EXP, logits, -jnp.inf)
    vals, idxs = [], []
    for _ in range(TOP_K):
        m = jnp.max(cur, axis=-1, keepdims=True)
        idx = jnp.min(jnp.where(cur == m, lane, LANES), axis=-1, keepdims=True)
        vals.append(m)
        idxs.append(idx)
        cur = jnp.where(lane == idx, -jnp.inf, cur)
    es = [jnp.exp(v - vals[0]) for v in vals]
    den = es[0] + es[1] + es[2] + es[3]
    ridx = jnp.zeros(logits.shape, jnp.int32)
    rprob = jnp.zeros(logits.shape, F32)
    for j in range(TOP_K):
        ridx = jnp.where(lane == j, idxs[j], ridx)
        rprob = jnp.where(lane == j, es[j] / den, rprob)
    ridx_ref[...] = ridx
    rprob_ref[...] = rprob


def _out_call(mix_p, mix_s, x, mod_l, w_out, norm_g, rw, rb):
    return pl.pallas_call(
        _out_kernel,
        grid=(NTILES,),
        in_specs=[
            pl.BlockSpec((TM, D_MODEL), lambda i: (jnp.minimum(i, NPT - 1), 0)),
            _full_spec((NS, D_MODEL)),
            _row_spec(D_MODEL),
            _full_spec((D_MODEL, D_MODEL)),
            _mod_spec(2),
            _full_spec((1, D_MODEL)),
            _mod_spec(4),
            _mod_spec(3),
            _full_spec((D_MODEL, LANES)),
            _full_spec((1, LANES)),
        ],
        out_specs=[_row_spec(D_MODEL), _row_spec(D_MODEL), _row_spec(LANES), _row_spec(LANES)],
        out_shape=[
            jax.ShapeDtypeStruct((NTOK, D_MODEL), F32),
            jax.ShapeDtypeStruct((NTOK, D_MODEL), F32),
            jax.ShapeDtypeStruct((NTOK, LANES), jnp.int32),
            jax.ShapeDtypeStruct((NTOK, LANES), F32),
        ],
        compiler_params=pltpu.CompilerParams(
            dimension_semantics=("arbitrary",), vmem_limit_bytes=VMEM_LIMIT),
        name="out_proj_router",
    )(mix_p, mix_s, x, w_out, mod_l, norm_g, mod_l, mod_l, rw, rb)


def _excl_lane_cumsum(row):
    r = lax.broadcasted_iota(jnp.int32, (LANES, LANES), 0)
    c = lax.broadcasted_iota(jnp.int32, (LANES, LANES), 1)
    before = jnp.where(r < c, 1.0, 0.0).astype(BF16)
    return _dot_mask_r3(jnp.broadcast_to(row, (8, LANES)), before)[0:1]


def _route_kernel(ridx_ref, pos_ref, post_ref, tab_ref, meta_ref, cnt_s):
    phase = pl.program_id(0)
    i = pl.program_id(1)
    lane = lax.broadcasted_iota(jnp.int32, (TD, LANES), 1)
    ridx = ridx_ref[...]
    hits = [lane == ridx[:, k:k + 1] for k in range(TOP_K)]
    onehot = jnp.zeros((TD, LANES), F32)
    for hit in hits:
        onehot = onehot + jnp.where(hit, 1.0, 0.0)
    colsum = jnp.sum(onehot, axis=0, keepdims=True)
    aligned = jnp.floor((colsum + (PIECE - 1)) * (1.0 / PIECE)) * PIECE

    @pl.when((phase == 0) & (i == 0))
    def _():
        cnt_s[...] = jnp.zeros_like(cnt_s)

    @pl.when(phase == 0)
    def _():
        cnt_s[...] += aligned

    @pl.when((phase == 1) & (i == 0))
    def _():
        total = cnt_s[...]
        padded = jnp.floor((total + (TE - 1)) * (1.0 / TE)) * TE
        pstart = _excl_lane_cumsum(padded)
        meta_ref[...] = jnp.concatenate(
            [total, pstart, pstart + padded, jnp.zeros((5, LANES), F32)], axis=0)
        cnt_s[...] = jnp.zeros_like(cnt_s)

    @pl.when(phase == 1)
    def _():
        r = lax.broadcasted_iota(jnp.int32, (TD, TD), 0)
        c = lax.broadcasted_iota(jnp.int32, (TD, TD), 1)
        earlier = jnp.where(c < r, 1.0, 0.0).astype(BF16)
        pos = _excl_lane_cumsum(aligned) + _dot(earlier, onehot.astype(BF16))
        posk = jnp.zeros((TD, LANES), F32)
        for k, hit in enumerate(hits):
            d = jnp.sum(jnp.where(hit, pos, 0.0), axis=1, keepdims=True)
            posk = jnp.where(lane == k, d, posk)
        pos_ref[...] = posk.astype(jnp.int32)
        post_ref[...] = posk.T[0:8].astype(jnp.int32)
        tab_ref[...] = jnp.concatenate(
            [colsum, cnt_s[...], jnp.zeros((6, LANES), F32)], axis=0).astype(jnp.int32)
        cnt_s[...] += aligned


def _route_call(ridx):
    nblk = NTOK // TD
    return pl.pallas_call(
        _route_kernel,
        grid=(2, nblk),
        in_specs=[pl.BlockSpec((TD, LANES), lambda p, i: (i, 0))],
        out_specs=[pl.BlockSpec((TD, LANES), lambda p, i: (p * i, 0)),
                   pl.BlockSpec((8, TD), lambda p, i: (p * i, 0)),
                   pl.BlockSpec((8, LANES), lambda p, i: (p * i, 0)),
                   pl.BlockSpec((8, LANES), lambda p, i: (0, 0))],
        out_shape=[jax.ShapeDtypeStruct((NTOK, LANES), jnp.int32),
                   jax.ShapeDtypeStruct((nblk * 8, TD), jnp.int32),
                   jax.ShapeDtypeStruct((nblk * 8, LANES), jnp.int32),
                   jax.ShapeDtypeStruct((8, LANES), F32)],
        scratch_shapes=[pltpu.VMEM((1, LANES), F32)],
        compiler_params=pltpu.CompilerParams(
            dimension_semantics=("arbitrary", "arbitrary"), vmem_limit_bytes=VMEM_LIMIT),
        name="route",
    )(ridx)


def _for_each_piece(i, qtab_ref, btab_ref, pstart_ref, fn):
    off = jnp.int32(0)
    for e in range(N_EXP):
        npieces = (qtab_ref[i * N_EXP + e] + (PIECE - 1)) // PIECE
        dst0 = pstart_ref[e] + btab_ref[i * N_EXP + e]

        def body(j, carry, off=off, dst0=dst0):
            fn(pl.multiple_of(off + j * PIECE, PIECE), pl.multiple_of(dst0 + j * PIECE, PIECE))
            return carry

        lax.fori_loop(0, npieces, body, 0)
        off = off + npieces * PIECE


def _dispatch_kernel(qtab_ref, btab_ref, pstart_ref, pend_ref, cnt_ref, post_ref, h_ref, xs_ref,
                     sbuf, zero_s, sems, semz):
    i = pl.program_id(0)
    n = pl.num_programs(0)
    slot = i % 2

    def piece(s):
        def copy(src, dst):
            return pltpu.make_async_copy(sbuf.at[s, pl.ds(src, PIECE)], xs_ref.at[pl.ds(dst, PIECE)],
                                         sems.at[s])
        return copy

    def for_pieces(tile, fn):
        _for_each_piece(tile, qtab_ref, btab_ref, pstart_ref, fn)

    @pl.when(i == 0)
    def _():
        zero_s[...] = jnp.zeros_like(zero_s)

        def fill(start):
            return pltpu.make_async_copy(zero_s, xs_ref.at[pl.ds(pl.multiple_of(start, TE), TE)], semz)

        for e in range(N_EXP):
            @pl.when(cnt_ref[e] > 0)
            def _():
                fill(pend_ref[e] - TE).start()
        for e in range(N_EXP):
            @pl.when(cnt_ref[e] > 0)
            def _():
                fill(pend_ref[e] - TE).wait()

        def tail_fill(t, carry):
            fill(t * TE).start()
            fill(t * TE).wait()
            return carry

        lax.fori_loop(pend_ref[N_EXP - 1] // TE, NTE, tail_fill, 0)

    @pl.when(i >= 2)
    def _():
        for_pieces(i - 2, lambda s, d: piece(slot)(s, d).wait())

    post = post_ref[...]
    prow = lax.broadcasted_iota(jnp.int32, (LP, TD), 0)
    sel = prow == post[0:1, :]
    for k in range(1, TOP_K):
        sel = sel | (prow == post[k:k + 1, :])
    onehot = jnp.where(sel, 1.0, 0.0).astype(BF16)
    sbuf[slot] = _dot(onehot, h_ref[...].astype(BF16))
    for_pieces(i, lambda s, d: piece(slot)(s, d).start())

    @pl.when(i == n - 1)
    def _():
        @pl.when(i >= 1)
        def _():
            for_pieces(i - 1, lambda s, d: piece(1 - slot)(s, d).wait())
        for_pieces(i, lambda s, d: piece(slot)(s, d).wait())


def _dispatch_call(qtab, btab, pstart, pend, counts, post, h2):
    grid_spec = pltpu.PrefetchScalarGridSpec(
        num_scalar_prefetch=5,
        grid=(NTOK // TD,),
        in_specs=[
            pl.BlockSpec((8, TD), lambda i, *_: (i, 0)),
            pl.BlockSpec((TD, D_MODEL), lambda i, *_: (i, 0)),
        ],
        out_specs=pl.BlockSpec(memory_space=pl.ANY),
        scratch_shapes=[pltpu.VMEM((2, LP, D_MODEL), F32), pltpu.VMEM((TE, D_MODEL), F32),
                        pltpu.SemaphoreType.DMA((2,)), pltpu.SemaphoreType.DMA(())],
    )
    return pl.pallas_call(
        _dispatch_kernel,
        grid_spec=grid_spec,
        out_shape=jax.ShapeDtypeStruct((NROWS, D_MODEL), F32),
        compiler_params=pltpu.CompilerParams(
            dimension_semantics=("arbitrary",), vmem_limit_bytes=VMEM_LIMIT),
        name="moe_dispatch",
    )(qtab, btab, pstart, pend, counts, post, h2)


def _combine_kernel(qtab_ref, btab_ref, pstart_ref, pos_ref, prob_ref, ys_ref, o_ref, ybuf, sems):
    i = pl.program_id(0)
    n = pl.num_programs(0)
    slot = i % 2

    def piece(s):
        def copy(loc, src):
            return pltpu.make_async_copy(ys_ref.at[pl.ds(src, PIECE)], ybuf.at[s, pl.ds(loc, PIECE)],
                                         sems.at[s])
        return copy

    @pl.when(i == 0)
    def _():
        ybuf[...] = jnp.zeros_like(ybuf)
        _for_each_piece(0, qtab_ref, btab_ref, pstart_ref, lambda s, d: piece(0)(s, d).start())

    @pl.when(i + 1 < n)
    def _():
        _for_each_piece(i + 1, qtab_ref, btab_ref, pstart_ref, lambda s, d: piece(1 - slot)(s, d).start())

    _for_each_piece(i, qtab_ref, btab_ref, pstart_ref, lambda s, d: piece(slot)(s, d).wait())

    pos = pos_ref[...]
    prob = prob_ref[...]
    pcol = lax.broadcasted_iota(jnp.int32, (TD, LP), 1)
    w = jnp.zeros((TD, LP), F32)
    for k in range(TOP_K):
        w = w + jnp.where(pcol == pos[:, k:k + 1], prob[:, k:k + 1], 0.0)
    o_ref[...] = _dot(w.astype(BF16), ybuf[slot].astype(BF16))


def _combine_call(qtab, btab, pstart, pos, rprob, ys):
    grid_spec = pltpu.PrefetchScalarGridSpec(
        num_scalar_prefetch=3,
        grid=(NTOK // TD,),
        in_specs=[
            pl.BlockSpec((TD, LANES), lambda i, *_: (i, 0)),
            pl.BlockSpec((TD, LANES), lambda i, *_: (i, 0)),
            pl.BlockSpec(memory_space=pl.ANY),
        ],
        out_specs=pl.BlockSpec((TD, D_MODEL), lambda i, *_: (i, 0)),
        scratch_shapes=[pltpu.VMEM((2, LP, D_MODEL), F32), pltpu.SemaphoreType.DMA((2,))],
    )
    return pl.pallas_call(
        _combine_kernel,
        grid_spec=grid_spec,
        out_shape=jax.ShapeDtypeStruct((NTOK, D_MODEL), F32),
        compiler_params=pltpu.CompilerParams(
            dimension_semantics=("arbitrary",), vmem_limit_bytes=VMEM_LIMIT),
        name="moe_combine",
    )(qtab, btab, pstart, pos, rprob, ys)


def _ffn_kernel(cnt_ref, pend_ref, xs_ref, wg_ref, bg_ref, wu_ref, bu_ref, wd_ref, bd_ref,
                ys_ref, wg_s, wu_s, wd_s, xbuf, ybuf, sem_in, sem_out):
    e = pl.program_id(0)
    nt = (cnt_ref[e] + (TE - 1)) // TE
    row0 = pend_ref[e] - nt * TE

    def rows(t):
        return pl.ds(pl.multiple_of(row0 + t * TE, TE), TE)

    def x_copy(t, slot):
        return pltpu.make_async_copy(xs_ref.at[rows(t)], xbuf.at[slot], sem_in.at[slot])

    def y_copy(t, slot):
        return pltpu.make_async_copy(ybuf.at[slot], ys_ref.at[rows(t)], sem_out.at[slot])

    @pl.when(nt > 0)
    def _():
        x_copy(0, 0).start()

    wg_s[...] = wg_ref[...].astype(BF16)
    wu_s[...] = wu_ref[...].astype(BF16)
    wd_s[...] = wd_ref[...].astype(BF16)

    def tile(t, carry):
        slot = t % 2

        @pl.when(t + 1 < nt)
        def _():
            x_copy(t + 1, 1 - slot).start()

        x_copy(t, slot).wait()

        @pl.when(t >= 2)
        def _():
            y_copy(t - 2, slot).wait()

        x = xbuf[slot].astype(BF16)
        gate = jnp.minimum(_dot(x, wg_s[...]) + bg_ref[...], SW_LIMIT)
        up = jnp.clip(_dot(x, wu_s[...]) + bu_ref[...], -SW_LIMIT, SW_LIMIT)
        act = (up + 1.0) * gate * _sigmoid(SW_ALPHA * gate)
        ybuf[slot] = _dot(act.astype(BF16), wd_s[...]) + bd_ref[...]
        y_copy(t, slot).start()
        return carry

    lax.fori_loop(0, nt, tile, 0)

    @pl.when(nt >= 2)
    def _():
        y_copy(nt - 2, nt % 2).wait()

    @pl.when(nt >= 1)
    def _():
        y_copy(nt - 1, (nt - 1) % 2).wait()

    @pl.when(e == N_EXP - 1)
    def _():
        ybuf[0] = jnp.zeros((TE, D_MODEL), F32)

        def tail_fill(t, carry):
            cp = pltpu.make_async_copy(
                ybuf.at[0], ys_ref.at[pl.ds(pl.multiple_of(t * TE, TE), TE)], sem_out.at[0])
            cp.start()
            cp.wait()
            return carry

        lax.fori_loop(pend_ref[N_EXP - 1] // TE, NTE, tail_fill, 0)


def _ffn_call(layer, counts, pend, xs, wg, bg, wu, bu, wd, bd):
    wspec = pl.BlockSpec((None, None, D_MODEL, D_MODEL), lambda e, c, p: (layer, e, 0, 0))
    bspec = pl.BlockSpec((None, None, 1, D_MODEL), lambda e, c, p: (layer, e, 0, 0))
    grid_spec = pltpu.PrefetchScalarGridSpec(
        num_scalar_prefetch=2,
        grid=(N_EXP,),
        in_specs=[pl.BlockSpec(memory_space=pl.ANY), wspec, bspec, wspec, bspec, wspec, bspec],
        out_specs=pl.BlockSpec(memory_space=pl.ANY),
        scratch_shapes=[pltpu.VMEM((D_MODEL, D_MODEL), BF16)] * 3 + [
            pltpu.VMEM((2, TE, D_MODEL), F32), pltpu.VMEM((2, TE, D_MODEL), F32),
            pltpu.SemaphoreType.DMA((2,)), pltpu.SemaphoreType.DMA((2,))],
    )
    b4 = lambda b: b.reshape(DEPTH, N_EXP, 1, D_MODEL)
    return pl.pallas_call(
        _ffn_kernel,
        grid_spec=grid_spec,
        out_shape=jax.ShapeDtypeStruct((NROWS, D_MODEL), F32),
        compiler_params=pltpu.CompilerParams(
            dimension_semantics=("arbitrary",), vmem_limit_bytes=VMEM_LIMIT),
        name="expert_ffn",
    )(counts, pend, xs, wg, b4(bg), wu, b4(bu), wd, b4(bd))


def _moe(layer, h2, ridx, rprob, wg, bg, wu, bu, wd, bd):
    pos, post, tab, meta = _route_call(ridx)
    counts = meta[0, :N_EXP].astype(jnp.int32)
    pstart = meta[1, :N_EXP].astype(jnp.int32)
    pend = meta[2, :N_EXP].astype(jnp.int32)
    tab = tab.reshape(NTOK // TD, 8, LANES)
    qtab = tab[:, 0, :N_EXP].reshape(-1)
    btab = tab[:, 1, :N_EXP].reshape(-1)
    xs = _dispatch_call(qtab, btab, pstart, pend, counts, post, h2)
    ys = _ffn_call(layer, counts, pend, xs, wg, bg, wu, bu, wd, bd)
    return _combine_call(qtab, btab, pstart, pos, rprob, ys)


def _final_kernel(x1_ref, moe_ref, g2_ref, ng_ref, y_ref):
    x = _gate_res(x1_ref[...], g2_ref[...], moe_ref[...])
    y_ref[...] = _rms_rows(x, ng_ref[...])


def _final_call(x1, moe, mod_l, norm_g):
    return pl.pallas_call(
        _final_kernel,
        grid=(NTILES,),
        in_specs=[_row_spec(D_MODEL), _row_spec(D_MODEL), _mod_spec(5), _full_spec((1, D_MODEL))],
        out_specs=_row_spec(D_MODEL),
        out_shape=jax.ShapeDtypeStruct((NTOK, D_MODEL), F32),
        compiler_params=pltpu.CompilerParams(
            dimension_semantics=("arbitrary",), vmem_limit_bytes=VMEM_LIMIT),
        name="final_norm",
    )(x1, moe, mod_l, norm_g)


def _block_diag(w):
    n, d, e = w.shape
    eye = jnp.eye(n, dtype=w.dtype)
    return (eye[:, None, :, None] * w[:, :, None, :]).reshape(n * d, n * e)


def _pad_lanes(v, offset):
    out = jnp.zeros((1, LANES), F32)
    return out.at[0, offset:offset + v.shape[0]].set(v)


def _mixer_params(l, w_in, lru_conv_w, lru_conv_b, lru_wa, lru_ba, lru_wx, lru_bx, lru_lambda,
                  gdn_conv_w, gdn_a_log, gdn_dt_bias, gdn_norm_g, gla_wg2, gla_bg, gla_norm_g):
    w = w_in[l]
    w_in_r = jnp.concatenate(
        [w[:, 0:2048], w[:, 2056:2824], w[:, 2048:2056], w[:, 2824:2840],
         jnp.zeros((D_MODEL, D_INP - 2840), F32)], axis=1).astype(BF16)
    row = lambda v: v.reshape(1, -1)
    mp = dict(
        lcw=lru_conv_w[l], lcb=row(lru_conv_b[l]),
        wa=_block_diag(lru_wa[l]).astype(BF16), ba=row(lru_ba[l]),
        wx=_block_diag(lru_wx[l]).astype(BF16), bx=row(lru_bx[l]),
        lam=row(lru_lambda[l]),
        gcw=gdn_conv_w[l],
        gpar=jnp.concatenate([_pad_lanes(gdn_a_log[l], GDN_H), _pad_lanes(gdn_dt_bias[l], GDN_H)], axis=0),
        gng=row(jnp.tile(gdn_norm_g[l], GDN_H)),
        wg2=jnp.zeros((LANES, LANES), F32).at[2 * GDN_H:2 * GDN_H + GLA_RANK].set(gla_wg2[l]),
        gbg=row(gla_bg[l]),
        lng=row(jnp.tile(gla_norm_g[l], GLA_H)),
    )
    return w_in_r, mp


def kernel(x_prompt, x_sample, state_lru_conv, state_lru_h, state_gdn_conv, state_gdn_S, state_gla_S, c_prompt, c_sample, ada_w, ada_b, norm1_g, norm2_g, w_in, lru_conv_w, lru_conv_b, lru_wa, lru_ba, lru_wx, lru_bx, lru_lambda, gdn_conv_w, gdn_a_log, gdn_dt_bias, gdn_norm_g, gla_wg2, gla_bg, gla_norm_g, w_out, router_w, router_b, exp_w_gate, exp_b_gate, exp_w_up, exp_b_up, exp_w_down, exp_b_down, final_norm_g):
    x = jnp.concatenate([
        x_prompt.transpose(1, 0, 2).reshape(NP, D_MODEL),
        x_sample.transpose(1, 0, 2).reshape(NS, D_MODEL)], axis=0)
    mod = _mod_call(jnp.concatenate([c_prompt, c_sample], axis=0), ada_w, ada_b)
    mod = jnp.stack([jnp.tile(mod[:, :BATCH], (1, MODB // BATCH, 1)), mod[:, BATCH:]], axis=1)

    p_states, s_states = [], []
    moe = None
    for l in range(DEPTH):
        w_in_r, mp = _mixer_params(l, w_in, lru_conv_w, lru_conv_b, lru_wa, lru_ba, lru_wx, lru_bx,
                                   lru_lambda, gdn_conv_w, gdn_a_log, gdn_dt_bias, gdn_norm_g,
                                   gla_wg2, gla_bg, gla_norm_g)
        x, proj = _in_call(x, moe, mod[l], norm1_g[l].reshape(1, D_MODEL), w_in_r,
                           mod[l - 1] if l else None)
        mix_p, p_lconv, p_lh, p_gconv, p_gs, p_lst = _mix_prompt_call(proj, mp)
        states = (state_lru_conv[l].transpose(1, 0, 2), state_lru_h[l],
                  state_gdn_conv[l].transpose(1, 0, 2),
                  state_gdn_S[l].reshape(DEC_BATCH, -1), state_gla_S[l].reshape(DEC_BATCH, -1))
        mix_s, s_lconv, s_lh, s_gconv, s_gs, s_ls = _mix_sample_call(proj, states, mp)
        rw = jnp.zeros((D_MODEL, LANES), F32).at[:, :N_EXP].set(router_w[l])
        rb = jnp.zeros((1, LANES), F32).at[0, :N_EXP].set(router_b[l])
        x, h2, ridx, rprob = _out_call(mix_p, mix_s.reshape(NS, D_MODEL), x, mod[l],
                                       w_out[l].astype(BF16), norm2_g[l].reshape(1, D_MODEL), rw, rb)
        moe = _moe(l, h2, ridx, rprob, exp_w_gate, exp_b_gate, exp_w_up, exp_b_up, exp_w_down, exp_b_down)
        p_gs = jnp.stack([p_gs[:, h * GDN_DK:(h + 1) * GDN_DK, h * GDN_DV:(h + 1) * GDN_DV]
                          for h in range(GDN_H)], axis=1)
        p_lst = jnp.stack([p_lst[:, h * GLA_DV:(h + 1) * GLA_DV, h * GLA_DK:(h + 1) * GLA_DK]
                           for h in range(GLA_H)], axis=1)
        p_states.append((p_lconv.reshape(3, BATCH, LRU_W).transpose(1, 0, 2), p_lh,
                         p_gconv.reshape(3, BATCH, GDN_CONV).transpose(1, 0, 2), p_gs,
                         p_lst.transpose(0, 1, 3, 2)))
        s_states.append((s_lconv.transpose(1, 0, 2), s_lh, s_gconv.transpose(1, 0, 2),
                         s_gs.reshape(DEC_BATCH, GDN_H, GDN_DK, GDN_DV),
                         s_ls.reshape(DEC_BATCH, GLA_H, GLA_DK, GLA_DV)))
    y = _final_call(x, moe, mod[DEPTH - 1], final_norm_g.reshape(1, D_MODEL))
    y_prompt = y[:NP].reshape(SEQ, BATCH, D_MODEL).transpose(1, 0, 2)
    y_sample = y[NP:].reshape(DEC_SEQ, DEC_BATCH, D_MODEL).transpose(1, 0, 2)
    ps = [jnp.stack([s[j] for s in p_states]) for j in range(5)]
    ss = [jnp.stack([s[j] for s in s_states]) for j in range(5)]
    return (y_prompt, y_sample, *ps, *ss)
```

```python
import functools

import numpy as np
import jax
import jax.numpy as jnp
from jax import lax
from jax.experimental import pallas as pl
from jax.experimental.pallas import tpu as pltpu

F32 = jnp.float32
BF16 = jnp.bfloat16

D_MODEL = 1024
BATCH = 8
SEQ = 2048
DEPTH = 2
DEC_BATCH = 128
DEC_SEQ = 4
CONV_W = 4
LRU_W = 512
LRU_BLOCKS = 8
LRU_C = 8.0
GDN_H = 4
GDN_DK = 64
GDN_DV = 64
GDN_CONV = GDN_H * (2 * GDN_DK + GDN_DV)
GLA_H = 4
GLA_DK = 32
GLA_DV = 64
GLA_RANK = 16
GLA_TAU = 16.0
GLA_CHUNK = 16
N_EXP = 32
TOP_K = 4
SW_LIMIT = 7.0
SW_ALPHA = 1.702
EPS = 1e-6

NP = BATCH * SEQ
NS = DEC_BATCH * DEC_SEQ
NTOK = NP + NS
TM = 512
NPT = NP // TM
NTILES = NTOK // TM
LANES = 128
MODB = 128

C_AX, C_AG, C_QKV, C_BZ = 0, 512, 1024, 1792
C_CQ, C_CK, C_CV, C_CR, C_SM = 2048, 2176, 2304, 2560, 2816
D_INP = 2944

TT = 64
BB = 32
TE = 256
TB = 768
FFC = 512
TD = 256
PIECE = 8
LP = 1280
NA = NTOK * TOP_K
NROWS = -(-(NA + (NTOK // TD) * N_EXP * (PIECE - 1) + N_EXP * (TE - 1)) // TE) * TE
NTE = NROWS // TE

VMEM_LIMIT = 50 * 1024 * 1024


def _dot(a, b):
    return jnp.dot(a, b, preferred_element_type=F32)


def _dot_nt(a, b):
    return lax.dot_general(a, b, (((1,), (1,)), ((), ())), preferred_element_type=F32)


def _dot_tn(a, b):
    return lax.dot_general(a, b, (((0,), (0,)), ((), ())), preferred_element_type=F32)


def _mm(a, b):
    return _dot(a.astype(BF16), b.astype(BF16))


def _mm_nt(a, b):
    return _dot_nt(a.astype(BF16), b.astype(BF16))


def _mm_tn(a, b):
    return _dot_tn(a.astype(BF16), b.astype(BF16))


def _split3(x):
    x1 = x.astype(BF16)
    r = x - x1.astype(F32)
    x2 = r.astype(BF16)
    x3 = (r - x2.astype(F32)).astype(BF16)
    return x1, x2, x3


def _dot3(a, b):
    a1 = a.astype(BF16)
    a2 = (a - a1.astype(F32)).astype(BF16)
    b1 = b.astype(BF16)
    b2 = (b - b1.astype(F32)).astype(BF16)
    return _dot(a1, b1) + (_dot(a2, b1) + _dot(a1, b2))


def _dot_mask_l(mask, x):
    x1, x2, x3 = _split3(x)
    return _dot(mask, x1) + (_dot(mask, x2) + _dot(mask, x3))


def _dot_mask_r(x, mask):
    x1 = x.astype(BF16)
    x2 = (x - x1.astype(F32)).astype(BF16)
    return _dot(x1, mask) + _dot(x2, mask)


def _dot_mask_r3(x, mask):
    x1, x2, x3 = _split3(x)
    return _dot(x1, mask) + (_dot(x2, mask) + _dot(x3, mask))


def _sigmoid(x):
    return jax.nn.sigmoid(x)


def _silu(x):
    return x * jax.nn.sigmoid(x)


def _softplus(x):
    return jnp.maximum(x, 0.0) + jnp.log1p(jnp.exp(-jnp.abs(x)))


def _rms_rows(x, g):
    return x * lax.rsqrt(jnp.mean(x * x, axis=-1, keepdims=True) + EPS) * g


def _modulate(y, scale, shift):
    rows = y.shape[0]
    y3 = y.reshape(rows // MODB, MODB, y.shape[1])
    return (y3 * (1.0 + scale) + shift).reshape(y.shape)


def _gate_res(x, gate, y):
    rows = y.shape[0]
    y3 = y.reshape(rows // MODB, MODB, y.shape[1])
    return x + (gate * y3).reshape(y.shape)


def _mod_kernel(c_ref, w_ref, b_ref, o_ref):
    o_ref[...] = _dot3(_silu(c_ref[...]), w_ref[...]) + b_ref[...]


def _mod_call(c_all, ada_w, ada_b):
    tn = 768
    rows = c_all.shape[0]
    return pl.pallas_call(
        _mod_kernel,
        grid=(DEPTH, 6 * D_MODEL // tn),
        in_specs=[
            pl.BlockSpec((rows, D_MODEL), lambda l, j: (0, 0)),
            pl.BlockSpec((None, D_MODEL, tn), lambda l, j: (l, 0, j)),
            pl.BlockSpec((None, 1, tn), lambda l, j: (l, 0, j)),
        ],
        out_specs=pl.BlockSpec((None, rows, tn), lambda l, j: (l, 0, j)),
        out_shape=jax.ShapeDtypeStruct((DEPTH, rows, 6 * D_MODEL), F32),
        compiler_params=pltpu.CompilerParams(
            dimension_semantics=("arbitrary", "arbitrary"), vmem_limit_bytes=VMEM_LIMIT),
        name="adaln_mod",
    )(c_all, ada_w, ada_b.reshape(DEPTH, 1, 6 * D_MODEL))


def _mod_spec(chunk):
    return pl.BlockSpec((None, MODB, D_MODEL), lambda i: (i // NPT, 0, chunk))


def _row_spec(width):
    return pl.BlockSpec((TM, width), lambda i: (i, 0))


def _full_spec(shape):
    nd = len(shape)
    return pl.BlockSpec(shape, lambda i: (0,) * nd)


def _in_kernel(has_res, *refs):
    if has_res:
        (x1_ref, moe_ref, g2_ref, n1_ref, sc_ref, sh_ref, w_ref, x_ref, proj_ref) = refs
        x = _gate_res(x1_ref[...], g2_ref[...], moe_ref[...])
        x_ref[...] = x
    else:
        (x_ref, n1_ref, sc_ref, sh_ref, w_ref, proj_ref) = refs
        x = x_ref[...]
    h = _modulate(_rms_rows(x, n1_ref[...]), sc_ref[...], sh_ref[...])
    proj_ref[...] = _dot(h.astype(BF16), w_ref[...])


def _in_call(x, moe, mod_l, norm_g, w_in_r, prev_mod):
    has_res = moe is not None
    ins, specs = [x], [_row_spec(D_MODEL)]
    outs = [jax.ShapeDtypeStruct((NTOK, D_INP), F32)]
    out_specs = [_row_spec(D_INP)]
    if has_res:
        ins += [moe, prev_mod]
        specs += [_row_spec(D_MODEL), _mod_spec(5)]
        outs = [jax.ShapeDtypeStruct((NTOK, D_MODEL), F32)] + outs
        out_specs = [_row_spec(D_MODEL)] + out_specs
    ins += [norm_g, mod_l, mod_l, w_in_r]
    specs += [_full_spec((1, D_MODEL)), _mod_spec(1), _mod_spec(0), _full_spec((D_MODEL, D_INP))]
    res = pl.pallas_call(
        functools.partial(_in_kernel, has_res),
        grid=(NTILES,),
        in_specs=specs,
        out_specs=out_specs,
        out_shape=outs,
        compiler_params=pltpu.CompilerParams(
            dimension_semantics=("arbitrary",), vmem_limit_bytes=VMEM_LIMIT),
        name="in_proj",
    )(*ins)
    return res if has_res else (x, res[0])


def _lru_gates(xc, wa_ref, ba_ref, wx_ref, bx_ref, lam_ref):
    xb = xc.astype(BF16)
    r = _sigmoid(_dot(xb, wa_ref[...]) + ba_ref[...])
    i = _sigmoid(_dot(xb, wx_ref[...]) + bx_ref[...])
    log_a = -LRU_C * r * _softplus(-lam_ref[...])
    a = jnp.exp(log_a)
    mult = jnp.sqrt(1.0 - jnp.exp(2.0 * log_a))
    return a, mult, i * xc


def _gelu(x):
    return jax.nn.gelu(x, approximate=True)


def _gdn_prep(qkv, small, gpar_ref, ones_blk):
    qkv = _silu(qkv)
    q = qkv[:, 0:256]
    k = qkv[:, 256:512]
    v = qkv[:, 512:768]
    q = q * lax.rsqrt(_dot_mask_r(q * q, ones_blk) + EPS) * (GDN_DK ** -0.5)
    k = k * lax.rsqrt(_dot_mask_r(k * k, ones_blk) + EPS)
    beta = _sigmoid(small)
    g = -jnp.exp(gpar_ref[0:1, :]) * _softplus(small + gpar_ref[1:2, :])
    lane = lax.broadcasted_iota(jnp.int32, small.shape, 1)
    bg = jnp.where(lane < GDN_H, beta, g)
    return q, k, v, bg


def _gla_prep(proj_q, proj_k, small, wg2_ref, gbg_ref):
    pre = _dot3(small, wg2_ref[...]) + gbg_ref[...]
    gk = -_softplus(-pre) / GLA_TAU
    return proj_q * (GLA_DK ** -0.5), proj_k, gk


def _head_norm_gate(o, norm_g, gate_in, ones_blk):
    ms = _dot_mask_r(o * o, ones_blk) * (1.0 / GDN_DV)
    return o * lax.rsqrt(ms + EPS) * norm_g * _silu(gate_in)


def _mix_prompt_kernel(proj_ref, lcw_ref, lcb_ref, wa_ref, ba_ref, wx_ref, bx_ref, lam_ref,
                       gcw_ref, gpar_ref, gng_ref, wg2_ref, gbg_ref, lng_ref,
                       ones_ref, tri_ref, e2_ref, btblk_ref,
                       mix_ref, lconv_ref, lh_ref, gconv_ref, gs_ref, lst_ref,
                       exta, extb, a_s, u_s, hs_s, q_s, k_s, v_s, bg_s, og_s,
                       q2_s, k2_s, v2_s, gk_s, ol_s):
    nb = BATCH
    rows = TT * nb
    step = pl.program_id(0)

    @pl.when(step == 0)
    def _():
        exta[0:3 * nb, :] = jnp.zeros((3 * nb, LRU_W), F32)
        extb[0:3 * nb, :] = jnp.zeros((3 * nb, GDN_CONV), F32)
        lh_ref[...] = jnp.zeros_like(lh_ref)
        gs_ref[...] = jnp.zeros_like(gs_ref)
        lst_ref[...] = jnp.zeros_like(lst_ref)

    def conv(ext, x, w_ref):
        ext[pl.ds(3 * nb, rows), :] = x
        y = ext[pl.ds(0, rows), :] * w_ref[0:1, :]
        for j in range(1, CONV_W):
            y = y + ext[pl.ds(j * nb, rows), :] * w_ref[j:j + 1, :]
        tail = ext[pl.ds(rows, 3 * nb), :]
        ext[pl.ds(0, 3 * nb), :] = tail
        return y, tail

    ones_blk = ones_ref[...]

    xa, tail = conv(exta, proj_ref[:, C_AX:C_AX + LRU_W], lcw_ref)
    lconv_ref[...] = tail
    xc = xa + lcb_ref[...]
    a, mult, ix = _lru_gates(xc, wa_ref, ba_ref, wx_ref, bx_ref, lam_ref)
    rid = lax.broadcasted_iota(jnp.int32, (rows, LRU_W), 0)
    mult = jnp.where((rid < nb) & (step == 0), 1.0, mult)
    a_s[...] = a
    u_s[...] = mult * ix

    def scan_body(t, h):
        off = pl.multiple_of(t * nb, nb)
        h = a_s[pl.ds(off, nb), :] * h + u_s[pl.ds(off, nb), :]
        hs_s[pl.ds(off, nb), :] = h
        return h

    lh_ref[...] = lax.fori_loop(0, TT, scan_body, lh_ref[...], unroll=8)
    mix_ref[:, 0:LRU_W] = (hs_s[...] * _gelu(proj_ref[:, C_AG:C_AG + LRU_W])).astype(BF16)

    small = proj_ref[:, C_SM:C_SM + LANES]
    qkv, tail = conv(extb, proj_ref[:, C_QKV:C_QKV + GDN_CONV], gcw_ref)
    gconv_ref[...] = tail
    q, k, v, bg = _gdn_prep(qkv, small, gpar_ref, ones_blk)
    for j in range(2):
        ls = slice(j * LANES, (j + 1) * LANES)
        q_s[j] = q[:, ls]
        k_s[j] = k[:, ls]
        v_s[j] = v[:, ls]
    bg_s[...] = bg

    q2, k2, gk = _gla_prep(proj_ref[:, C_CQ:C_CQ + 128], proj_ref[:, C_CK:C_CK + 128],
                           small, wg2_ref, gbg_ref)
    q2_s[...] = q2
    k2_s[...] = k2
    gk_s[...] = gk
    for j in range(2):
        v2_s[j] = proj_ref[:, C_CV + j * LANES:C_CV + (j + 1) * LANES]

    ri = lax.broadcasted_iota(jnp.int32, (TT, 256), 0)
    lane = lax.broadcasted_iota(jnp.int32, (TT, 256), 1)
    cj = lane % TT
    incl = ri >= cj
    strict = ri > cj
    eye = ri == cj
    blk16m = (ri // 16) == (cj // 16)
    blk32m = (ri // 32) == (cj // 32)
    blk_causal = incl & blk16m
    hmask = [(lane // GDN_DV == h).astype(BF16) for h in range(GDN_H)]
    lane2 = lax.broadcasted_iota(jnp.int32, (TT, LANES), 1)
    hmask2 = [(lane2 // GLA_DK == h).astype(BF16) for h in range(GLA_H)]
    bdm = (lax.broadcasted_iota(jnp.int32, (256, 256), 0) // GDN_DK
           == lax.broadcasted_iota(jnp.int32, (256, 256), 1) // GDN_DV)
    bdm2 = (lax.broadcasted_iota(jnp.int32, (256, LANES), 0) // GLA_DV
            == lax.broadcasted_iota(jnp.int32, (256, LANES), 1) // GLA_DK)
    tri = tri_ref[...]
    e2 = e2_ref[...]
    btblk = btblk_ref[...]
    ones8 = jnp.ones((8, TT), BF16)
    seqs = range(nb)
    rsel = [pl.ds(b, TT, stride=nb) for b in seqs]

    def bd(x):
        xb = x.astype(BF16)
        return jnp.concatenate([xb * m for m in hmask], axis=0)

    def bd2(x):
        xb = x.astype(BF16)
        return jnp.concatenate([xb * m for m in hmask2], axis=0)

    def cat2(ref, b):
        return jnp.concatenate([ref[0, rsel[b], :], ref[1, rsel[b], :]], axis=1)

    def each(f, *lists):
        return [f(*xs) for xs in zip(*lists)]

    q = [cat2(q_s, b) for b in seqs]
    k = [cat2(k_s, b) for b in seqs]
    v = [cat2(v_s, b) for b in seqs]
    bgb = [bg_s[rsel[b], :] for b in seqs]
    gc_all = each(lambda x: _dot_mask_l(tri, x), bgb)
    both = each(lambda x, c: _dot_mask_r3(jnp.where(lane2 < GDN_H, x, c), e2), bgb, gc_all)
    beta = [x[:, 0:256] for x in both]
    gc = [x[:, 256:512] for x in both]
    gc_row = each(lambda g: _dot_mask_l(ones8, jnp.where(eye, g, 0.0))[0:1, :], gc)
    decay = each(lambda g, gr: jnp.exp(jnp.where(incl, g - gr, -jnp.inf)), gc, gc_row)
    egc = each(jnp.exp, gc)
    kb = each(lambda a, b_: a * b_, k, beta)
    bdk = each(bd, k)
    kq = each(lambda kb_, q_, m: _dot_nt(jnp.concatenate([kb_, q_], axis=0).astype(BF16), m), kb, q, bdk)
    amat = each(lambda x, d: jnp.where(strict, x[0:TT] * d, 0.0), kq, decay)
    qk = each(lambda x, d: jnp.where(incl, x[TT:] * d, 0.0), kq, decay)
    y = each(lambda a: -jnp.where(blk16m, a, 0.0), amat)
    n = y
    bdy = each(bd, y)
    for _ in range(3):
        y = each(lambda y_, m: _dot(y_.astype(BF16), m), y, bdy)
        bdy = each(bd, y)
        n = each(lambda n_, y_, m: n_ + y_ + _dot(n_.astype(BF16), m), n, y, bdy)
    t = each(lambda n_: jnp.where(eye, 1.0, 0.0) + n_, n)
    for lower in (each(lambda a: jnp.where(blk32m & ~blk16m, a, 0.0), amat),
                  each(lambda a: jnp.where(blk32m, 0.0, a), amat)):
        tl = each(lambda t_, l_: _dot(t_.astype(BF16), bd(l_)), t, lower)
        t = each(lambda t_, x: t_ - _dot(x.astype(BF16), bd(t_)), t, tl)
    uw = each(lambda t_, v_, b_, kb_, e: _dot(
        t_.astype(BF16), jnp.concatenate([bd(v_ * b_), bd(kb_ * e)], axis=1)), t, v, beta, kb, egc)
    s = [gs_ref[b] for b in seqs]
    ws = each(lambda x, q_, e, s_: _dot(
        jnp.concatenate([x[:, 256:512], q_ * e], axis=0).astype(BF16), s_.astype(BF16)), uw, q, egc, s)
    vn = each(lambda x, w_: x[:, 0:256] - w_[0:TT], uw, ws)
    o = each(lambda w_, a, v_: w_[TT:] + _dot(a.astype(BF16), bd(v_)), ws, qk, vn)
    for b in seqs:
        g_last = gc[b][TT - 1:TT, :]
        kd = k[b] * jnp.exp(g_last - gc[b])
        gs_ref[b] = s[b] * jnp.exp(g_last) + jnp.where(bdm, _mm_tn(kd, vn[b]), 0.0)
        for j in range(2):
            og_s[j, rsel[b], :] = o[b][:, j * LANES:(j + 1) * LANES]

    cum = [_dot_mask_l(btblk, gk_s[rsel[b], :]) for b in seqs]
    bc = [x[0:TT] for x in cum]
    bl = [x[TT:] for x in cum]
    qb = [q2_s[rsel[b], :] for b in seqs]
    kb2 = [k2_s[rsel[b], :] for b in seqs]
    vb = [cat2(v2_s, b) for b in seqs]
    qi = each(lambda a, c: (a * jnp.exp(c)).astype(BF16), qb, bc)
    ki = each(lambda a, c: a * jnp.exp(-c), kb2, bc)
    kst = each(lambda a, l_, c: (a * jnp.exp(l_ - c)).astype(BF16), kb2, bl, bc)
    ebl = each(jnp.exp, bl)
    sc = each(lambda a, c: jnp.where(blk_causal, _dot_nt(a, bd2(c)), 0.0), qi, ki)
    oh = each(lambda a, v_: _dot(a.astype(BF16), bd(v_)), sc, vb)
    st = [lst_ref[b] for b in seqs]
    vbb = each(lambda v_: v_.astype(BF16), vb)
    outs = [[] for _ in seqs]
    for c in range(TT // GLA_CHUNK):
        rs = slice(c * GLA_CHUNK, (c + 1) * GLA_CHUNK)
        for b in seqs:
            outs[b].append(oh[b][rs] + _dot_nt(qi[b][rs], st[b].astype(BF16)))
        upd = [jnp.where(bdm2, _dot_tn(vbb[b][rs], kst[b][rs]), 0.0) for b in seqs]
        st = [st[b] * ebl[b][c * GLA_CHUNK:c * GLA_CHUNK + 1, :] + upd[b] for b in seqs]
    for b in seqs:
        lst_ref[b] = st[b]
        ol = jnp.concatenate(outs[b], axis=0)
        for j in range(2):
            ol_s[j, rsel[b], :] = ol[:, j * LANES:(j + 1) * LANES]

    og = jnp.concatenate([og_s[0], og_s[1]], axis=1)
    out_b = _head_norm_gate(og, gng_ref[...], proj_ref[:, C_BZ:C_BZ + 256], ones_blk)
    mix_ref[:, 512:768] = out_b.astype(BF16)
    ol = jnp.concatenate([ol_s[0], ol_s[1]], axis=1)
    out_c = _head_norm_gate(ol, lng_ref[...], proj_ref[:, C_CR:C_CR + 256], ones_blk)
    mix_ref[:, 768:1024] = out_c.astype(BF16)


def _mix_consts():
    r = np.arange(TT)
    tri = (r[:, None] >= r[None, :]).astype(np.float32)
    same = (r[:, None] // GLA_CHUNK) == (r[None, :] // GLA_CHUNK)
    bt16 = (tri.astype(bool) & same).astype(np.float32)
    blk16 = same.astype(np.float32)
    c = np.arange(256)
    ones_blk = ((c[:, None] // GDN_DV) == (c[None, :] // GDN_DV)).astype(np.float32)
    e2 = np.zeros((LANES, 512), np.float32)
    for h in range(GDN_H):
        e2[h, h * GDN_DV:(h + 1) * GDN_DV] = 1.0
        e2[GDN_H + h, 256 + h * GDN_DV:256 + (h + 1) * GDN_DV] = 1.0
    btblk = np.concatenate([bt16, blk16], axis=0)
    return (jnp.asarray(ones_blk, BF16), jnp.asarray(tri, BF16),
            jnp.asarray(e2, BF16), jnp.asarray(btblk, BF16))


def _mix_prompt_call(proj, mp):
    rows = TT * BATCH
    consts = _mix_consts()
    params = (mp["lcw"], mp["lcb"], mp["wa"], mp["ba"], mp["wx"], mp["bx"], mp["lam"],
              mp["gcw"], mp["gpar"], mp["gng"], mp["wg2"], mp["gbg"], mp["lng"]) + consts
    out_shape = [
        jax.ShapeDtypeStruct((NP, D_MODEL), BF16),
        jax.ShapeDtypeStruct((3 * BATCH, LRU_W), F32),
        jax.ShapeDtypeStruct((BATCH, LRU_W), F32),
        jax.ShapeDtypeStruct((3 * BATCH, GDN_CONV), F32),
        jax.ShapeDtypeStruct((BATCH, GDN_H * GDN_DK, GDN_H * GDN_DV), F32),
        jax.ShapeDtypeStruct((BATCH, GLA_H * GLA_DV, GLA_H * GLA_DK), F32),
    ]
    out_specs = [pl.BlockSpec((rows, D_MODEL), lambda i: (i, 0))] + [
        _full_spec(s.shape) for s in out_shape[1:]]
    scratch = [
        pltpu.VMEM((rows + 3 * BATCH, LRU_W), F32),
        pltpu.VMEM((rows + 3 * BATCH, GDN_CONV), F32),
        pltpu.VMEM((rows, LRU_W), F32), pltpu.VMEM((rows, LRU_W), F32), pltpu.VMEM((rows, LRU_W), F32),
        pltpu.VMEM((2, rows, LANES), F32), pltpu.VMEM((2, rows, LANES), F32), pltpu.VMEM((2, rows, LANES), F32),
        pltpu.VMEM((rows, LANES), F32), pltpu.VMEM((2, rows, LANES), F32),
        pltpu.VMEM((rows, LANES), F32), pltpu.VMEM((rows, LANES), F32), pltpu.VMEM((2, rows, LANES), F32),
        pltpu.VMEM((rows, LANES), F32), pltpu.VMEM((2, rows, LANES), F32),
    ]
    return pl.pallas_call(
        _mix_prompt_kernel,
        grid=(SEQ // TT,),
        in_specs=[pl.BlockSpec((rows, D_INP), lambda i: (i, 0))] + [_full_spec(p.shape) for p in params],
        out_specs=out_specs,
        out_shape=out_shape,
        scratch_shapes=scratch,
        compiler_params=pltpu.CompilerParams(
            dimension_semantics=("arbitrary",), vmem_limit_bytes=VMEM_LIMIT),
        name="mix_prompt",
    )(proj, *params)


def _pair_bcast(x, p, lo_mask):
    return jnp.where(lo_mask, x[:, 2 * p:2 * p + 1], x[:, 2 * p + 1:2 * p + 2])


def _fold_pairs(acc):
    return acc[:, 0:64] + acc[:, 64:128]


def _mix_sample_kernel(p0_ref, p1_ref, p2_ref, p3_ref, lconv_in, lh_in, gconv_in, gs_in, ls_in,
                       lcw_ref, lcb_ref, wa_ref, ba_ref, wx_ref, bx_ref, lam_ref,
                       gcw_ref, gpar_ref, gng_ref, wg2_ref, gbg_ref, lng_ref, ones_ref,
                       mix_ref, lconv_ref, lh_ref, gconv_ref, gs_ref, ls_ref):
    rows = DEC_SEQ * BB
    prefs = (p0_ref, p1_ref, p2_ref, p3_ref)
    ones_blk = ones_ref[...]

    def cols(c0, width):
        return [p[:, c0:c0 + width] for p in prefs]

    def conv(prev, xs, w_ref):
        ext = [prev[j] for j in range(CONV_W - 1)] + xs
        ys = []
        for t in range(DEC_SEQ):
            y = ext[t] * w_ref[0:1, :]
            for j in range(1, CONV_W):
                y = y + ext[t + j] * w_ref[j:j + 1, :]
            ys.append(y)
        return jnp.concatenate(ys, axis=0), ext[DEC_SEQ:]

    def rows_of(x, t):
        return x[t * BB:(t + 1) * BB]

    xa, tail = conv(lconv_in, cols(C_AX, LRU_W), lcw_ref)
    for j in range(CONV_W - 1):
        lconv_ref[j] = tail[j]
    xc = xa + lcb_ref[...]
    a, mult, ix = _lru_gates(xc, wa_ref, ba_ref, wx_ref, bx_ref, lam_ref)
    u = mult * ix
    h = lh_in[...]
    hs = []
    for t in range(DEC_SEQ):
        h = rows_of(a, t) * h + rows_of(u, t)
        hs.append(h)
    lh_ref[...] = h
    ag = jnp.concatenate(cols(C_AG, LRU_W), axis=0)
    mix_a = jnp.concatenate(hs, axis=0) * _gelu(ag)

    small = jnp.concatenate(cols(C_SM, LANES), axis=0)
    qkv, tail = conv(gconv_in, cols(C_QKV, GDN_CONV), gcw_ref)
    for j in range(CONV_W - 1):
        gconv_ref[j] = tail[j]
    q, k, v, bg = _gdn_prep(qkv, small, gpar_ref, ones_blk)
    lo_mask = lax.broadcasted_iota(jnp.int32, (BB, LANES), 1) < 64
    gs_ref[...] = gs_in[...]
    hd = GDN_DK * GDN_DV
    o_heads = []
    for h in range(GDN_H):
        o_t = []
        for t in range(DEC_SEQ):
            rs = slice(t * BB, (t + 1) * BB)
            cs = slice(h * GDN_DK, (h + 1) * GDN_DK)
            eg = jnp.exp(bg[rs, GDN_H + h:GDN_H + h + 1])
            beta = bg[rs, h:h + 1]
            kt, qt, vt = k[rs, cs], q[rs, cs], v[rs, cs]
            kks = [_pair_bcast(kt, p, lo_mask) for p in range(GDN_DK // 2)]
            acc = jnp.zeros((BB, LANES), F32)
            for p in range(GDN_DK // 2):
                acc = acc + gs_ref[:, h * hd + p * LANES:h * hd + (p + 1) * LANES] * kks[p]
            vn = beta * (vt - eg * _fold_pairs(acc))
            vn2 = jnp.concatenate([vn, vn], axis=1)
            oacc = jnp.zeros((BB, LANES), F32)
            for p in range(GDN_DK // 2):
                sl = slice(h * hd + p * LANES, h * hd + (p + 1) * LANES)
                s = eg * gs_ref[:, sl] + kks[p] * vn2
                gs_ref[:, sl] = s
                oacc = oacc + s * _pair_bcast(qt, p, lo_mask)
            o_t.append(_fold_pairs(oacc))
        o_heads.append(jnp.concatenate(o_t, axis=0))
    o_b = jnp.concatenate(o_heads, axis=1)
    bz = jnp.concatenate(cols(C_BZ, 256), axis=0)
    mix_b = _head_norm_gate(o_b, gng_ref[...], bz, ones_blk)

    q2, k2, gk = _gla_prep(jnp.concatenate(cols(C_CQ, 128), axis=0),
                           jnp.concatenate(cols(C_CK, 128), axis=0), small, wg2_ref, gbg_ref)
    v2 = jnp.concatenate(cols(C_CV, 256), axis=0)
    ls_ref[...] = ls_in[...]
    hd2 = GLA_DK * GLA_DV
    o_heads = []
    for h in range(GLA_H):
        o_t = []
        for t in range(DEC_SEQ):
            rs = slice(t * BB, (t + 1) * BB)
            cs = slice(h * GLA_DK, (h + 1) * GLA_DK)
            al = jnp.exp(gk[rs, cs])
            kt, qt = k2[rs, cs], q2[rs, cs]
            vt = v2[rs, h * GLA_DV:(h + 1) * GLA_DV]
            v2x = jnp.concatenate([vt, vt], axis=1)
            oacc = jnp.zeros((BB, LANES), F32)
            for p in range(GLA_DK // 2):
                sl = slice(h * hd2 + p * LANES, h * hd2 + (p + 1) * LANES)
                s = _pair_bcast(al, p, lo_mask) * ls_ref[:, sl] + _pair_bcast(kt, p, lo_mask) * v2x
                ls_ref[:, sl] = s
                oacc = oacc + s * _pair_bcast(qt, p, lo_mask)
            o_t.append(_fold_pairs(oacc))
        o_heads.append(jnp.concatenate(o_t, axis=0))
    o_c = jnp.concatenate(o_heads, axis=1)
    cr = jnp.concatenate(cols(C_CR, 256), axis=0)
    mix_c = _head_norm_gate(o_c, lng_ref[...], cr, ones_blk)

    mix = jnp.concatenate([mix_a, mix_b, mix_c], axis=1).astype(BF16)
    for t in range(DEC_SEQ):
        mix_ref[t] = mix[t * BB:(t + 1) * BB]


def _mix_sample_call(proj, states, mp):
    lconv, lh, gconv, gs, ls = states
    ones_blk = _mix_consts()[0]
    params = (mp["lcw"], mp["lcb"], mp["wa"], mp["ba"], mp["wx"], mp["bx"], mp["lam"],
              mp["gcw"], mp["gpar"], mp["gng"], mp["wg2"], mp["gbg"], mp["lng"], ones_blk)
    nblk = DEC_BATCH // BB
    base = NP // BB

    def proj_spec(t):
        return pl.BlockSpec((BB, D_INP), lambda j: (base + t * nblk + j, 0))

    def bspec3(n, width):
        return pl.BlockSpec((n, BB, width), lambda j: (0, j, 0))

    def bspec2(width):
        return pl.BlockSpec((BB, width), lambda j: (j, 0))

    gdn_flat = GDN_H * GDN_DK * GDN_DV
    gla_flat = GLA_H * GLA_DK * GLA_DV
    state_specs = [bspec3(3, LRU_W), bspec2(LRU_W), bspec3(3, GDN_CONV), bspec2(gdn_flat), bspec2(gla_flat)]
    out_shape = [
        jax.ShapeDtypeStruct((DEC_SEQ, DEC_BATCH, D_MODEL), BF16),
        jax.ShapeDtypeStruct((3, DEC_BATCH, LRU_W), F32),
        jax.ShapeDtypeStruct((DEC_BATCH, LRU_W), F32),
        jax.ShapeDtypeStruct((3, DEC_BATCH, GDN_CONV), F32),
        jax.ShapeDtypeStruct((DEC_BATCH, gdn_flat), F32),
        jax.ShapeDtypeStruct((DEC_BATCH, gla_flat), F32),
    ]
    return pl.pallas_call(
        _mix_sample_kernel,
        grid=(nblk,),
        in_specs=[proj_spec(t) for t in range(DEC_SEQ)] + state_specs + [_full_spec(p.shape) for p in params],
        out_specs=[bspec3(DEC_SEQ, D_MODEL)] + state_specs,
        out_shape=out_shape,
        compiler_params=pltpu.CompilerParams(
            dimension_semantics=("arbitrary",), vmem_limit_bytes=VMEM_LIMIT),
        name="mix_sample",
    )(proj, proj, proj, proj, lconv, lh, gconv, gs, ls, *params)


def _out_kernel(mp_ref, ms_ref, x_ref, w_ref, g1_ref, n2_ref, sc_ref, sh_ref, rw_ref, rb_ref,
                x1_ref, h2_ref, ridx_ref, rprob_ref):
    step = pl.program_id(0)
    mix = jnp.where(step < NPT, mp_ref[...], ms_ref[...])
    x1 = _gate_res(x_ref[...], g1_ref[...], _dot(mix, w_ref[...]))
    x1_ref[...] = x1
    h2 = _modulate(_rms_rows(x1, n2_ref[...]), sc_ref[...], sh_ref[...])
    h2_ref[...] = h2
    logits = _dot3(h2, rw_ref[...]) + rb_ref[...]
    lane = lax.broadcasted_iota(jnp.int32, logits.shape, 1)
    cur = jnp.where(lane < N_EXP, logits, -jnp.inf)
    vals, idxs = [], []
    for _ in range(TOP_K):
        m = jnp.max(cur, axis=-1, keepdims=True)
        idx = jnp.min(jnp.where(cur == m, lane, LANES), axis=-1, keepdims=True)
        vals.append(m)
        idxs.append(idx)
        cur = jnp.where(lane == idx, -jnp.inf, cur)
    es = [jnp.exp(v - vals[0]) for v in vals]
    den = es[0] + es[1] + es[2] + es[3]
    ridx = jnp.zeros(logits.shape, jnp.int32)
    rprob = jnp.zeros(logits.shape, F32)
    for j in range(TOP_K):
        ridx = jnp.where(lane == j, idxs[j], ridx)
        rprob = jnp.where(lane == j, es[j] / den, rprob)
    ridx_ref[...] = ridx
    rprob_ref[...] = rprob


def _out_call(mix_p, mix_s, x, mod_l, w_out, norm_g, rw, rb):
    return pl.pallas_call(
        _out_kernel,
        grid=(NTILES,),
        in_specs=[
            pl.BlockSpec((TM, D_MODEL), lambda i: (jnp.minimum(i, NPT - 1), 0)),
            _full_spec((NS, D_MODEL)),
            _row_spec(D_MODEL),
            _full_spec((D_MODEL, D_MODEL)),
            _mod_spec(2),
            _full_spec((1, D_MODEL)),
            _mod_spec(4),
            _mod_spec(3),
            _full_spec((D_MODEL, LANES)),
            _full_spec((1, LANES)),
        ],
        out_specs=[_row_spec(D_MODEL), _row_spec(D_MODEL), _row_spec(LANES), _row_spec(LANES)],
        out_shape=[
            jax.ShapeDtypeStruct((NTOK, D_MODEL), F32),
            jax.ShapeDtypeStruct((NTOK, D_MODEL), F32),
            jax.ShapeDtypeStruct((NTOK, LANES), jnp.int32),
            jax.ShapeDtypeStruct((NTOK, LANES), F32),
        ],
        compiler_params=pltpu.CompilerParams(
            dimension_semantics=("arbitrary",), vmem_limit_bytes=VMEM_LIMIT),
        name="out_proj_router",
    )(mix_p, mix_s, x, w_out, mod_l, norm_g, mod_l, mod_l, rw, rb)


def _excl_lane_cumsum(row):
    r = lax.broadcasted_iota(jnp.int32, (LANES, LANES), 0)
    c = lax.broadcasted_iota(jnp.int32, (LANES, LANES), 1)
    before = jnp.where(r < c, 1.0, 0.0).astype(BF16)
    return _dot_mask_r3(jnp.broadcast_to(row, (8, LANES)), before)[0:1]


def _route_kernel(ridx_ref, pos_ref, post_ref, tab_ref, meta_ref, cnt_s):
    phase = pl.program_id(0)
    i = pl.program_id(1)
    lane = lax.broadcasted_iota(jnp.int32, (TD, LANES), 1)
    ridx = ridx_ref[...]
    hits = [lane == ridx[:, k:k + 1] for k in range(TOP_K)]
    onehot = jnp.zeros((TD, LANES), F32)
    for hit in hits:
        onehot = onehot + jnp.where(hit, 1.0, 0.0)
    colsum = jnp.sum(onehot, axis=0, keepdims=True)
    aligned = jnp.floor((colsum + (PIECE - 1)) * (1.0 / PIECE)) * PIECE

    @pl.when((phase == 0) & (i == 0))
    def _():
        cnt_s[...] = jnp.zeros_like(cnt_s)

    @pl.when(phase == 0)
    def _():
        cnt_s[...] += aligned

    @pl.when((phase == 1) & (i == 0))
    def _():
        total = cnt_s[...]
        padded = jnp.floor((total + (TE - 1)) * (1.0 / TE)) * TE
        pstart = _excl_lane_cumsum(padded)
        meta_ref[...] = jnp.concatenate(
            [total, pstart, pstart + padded, jnp.zeros((5, LANES), F32)], axis=0)
        cnt_s[...] = jnp.zeros_like(cnt_s)

    @pl.when(phase == 1)
    def _():
        r = lax.broadcasted_iota(jnp.int32, (TD, TD), 0)
        c = lax.broadcasted_iota(jnp.int32, (TD, TD), 1)
        earlier = jnp.where(c < r, 1.0, 0.0).astype(BF16)
        pos = _excl_lane_cumsum(aligned) + _dot(earlier, onehot.astype(BF16))
        posk = jnp.zeros((TD, LANES), F32)
        for k, hit in enumerate(hits):
            d = jnp.sum(jnp.where(hit, pos, 0.0), axis=1, keepdims=True)
            posk = jnp.where(lane == k, d, posk)
        pos_ref[...] = posk.astype(jnp.int32)
        post_ref[...] = posk.T[0:8].astype(jnp.int32)
        pieces = aligned * (1.0 / PIECE)
        total = jnp.broadcast_to(jnp.sum(pieces, axis=1, keepdims=True), (1, LANES))
        tab_ref[...] = jnp.concatenate(
            [pieces, cnt_s[...], total, jnp.zeros((5, LANES), F32)], axis=0).astype(jnp.int32)
        cnt_s[...] += aligned


def _route_call(ridx):
    nblk = NTOK // TD
    return pl.pallas_call(
        _route_kernel,
        grid=(2, nblk),
        in_specs=[pl.BlockSpec((TD, LANES), lambda p, i: (i, 0))],
        out_specs=[pl.BlockSpec((TD, LANES), lambda p, i: (p * i, 0)),
                   pl.BlockSpec((8, TD), lambda p, i: (p * i, 0)),
                   pl.BlockSpec((8, LANES), lambda p, i: (p * i, 0)),
                   pl.BlockSpec((8, LANES), lambda p, i: (0, 0))],
        out_shape=[jax.ShapeDtypeStruct((NTOK, LANES), jnp.int32),
                   jax.ShapeDtypeStruct((nblk * 8, TD), jnp.int32),
                   jax.ShapeDtypeStruct((nblk * 8, LANES), jnp.int32),
                   jax.ShapeDtypeStruct((8, LANES), F32)],
        scratch_shapes=[pltpu.VMEM((1, LANES), F32)],
        compiler_params=pltpu.CompilerParams(
            dimension_semantics=("arbitrary", "arbitrary"), vmem_limit_bytes=VMEM_LIMIT),
        name="route",
    )(ridx)


def _for_each_piece(i, ptab_ref, btab_ref, pstart_ref, fn):
    off = jnp.int32(0)
    for e in range(N_EXP):
        npieces = ptab_ref[i * N_EXP + e]
        dst0 = pstart_ref[e] + btab_ref[i * N_EXP + e]

        def body(j, carry, off=off, dst0=dst0):
            fn(pl.multiple_of(off + j * PIECE, PIECE), pl.multiple_of(dst0 + j * PIECE, PIECE))
            return carry

        lax.fori_loop(0, npieces, body, 0)
        off = off + npieces * PIECE


def _dispatch_kernel(qtab_ref, btab_ref, ttab_ref, pstart_ref, pend_ref, cnt_ref, post_ref, h_ref, xs_ref,
                     sbuf, zero_s, sems, semz):
    i = pl.program_id(0)
    n = pl.num_programs(0)
    slot = i % 2

    def piece(s):
        def copy(src, dst):
            return pltpu.make_async_copy(sbuf.at[s, pl.ds(src, PIECE)], xs_ref.at[pl.ds(dst, PIECE)],
                                         sems.at[s])
        return copy

    def start_pieces(tile, s):
        _for_each_piece(tile, qtab_ref, btab_ref, pstart_ref, lambda src, dst: piece(s)(src, dst).start())

    def wait_pieces(tile, s):
        def body(j, carry):
            piece(s)(0, 0).wait()
            return carry
        lax.fori_loop(0, ttab_ref[tile], body, 0)

    @pl.when(i == 0)
    def _():
        zero_s[...] = jnp.zeros_like(zero_s)

        def fill(start):
            return pltpu.make_async_copy(zero_s, xs_ref.at[pl.ds(pl.multiple_of(start, TE), TE)], semz)

        for e in range(N_EXP):
            @pl.when(cnt_ref[e] > 0)
            def _():
                fill(pend_ref[e] - TE).start()
        for e in range(N_EXP):
            @pl.when(cnt_ref[e] > 0)
            def _():
                fill(pend_ref[e] - TE).wait()

        def tail_fill(t, carry):
            fill(t * TE).start()
            fill(t * TE).wait()
            return carry

        lax.fori_loop(pend_ref[N_EXP - 1] // TE, NTE, tail_fill, 0)

    @pl.when(i >= 2)
    def _():
        wait_pieces(i - 2, slot)

    post = post_ref[...]
    prow = lax.broadcasted_iota(jnp.int32, (LP, TD), 0)
    sel = prow == post[0:1, :]
    for k in range(1, TOP_K):
        sel = sel | (prow == post[k:k + 1, :])
    onehot = jnp.where(sel, 1.0, 0.0).astype(BF16)
    sbuf[slot] = _dot(onehot, h_ref[...].astype(BF16))
    start_pieces(i, slot)

    @pl.when(i == n - 1)
    def _():
        @pl.when(i >= 1)
        def _():
            wait_pieces(i - 1, 1 - slot)
        wait_pieces(i, slot)


def _dispatch_call(qtab, btab, ttab, pstart, pend, counts, post, h2):
    grid_spec = pltpu.PrefetchScalarGridSpec(
        num_scalar_prefetch=6,
        grid=(NTOK // TD,),
        in_specs=[
            pl.BlockSpec((8, TD), lambda i, *_: (i, 0)),
            pl.BlockSpec((TD, D_MODEL), lambda i, *_: (i, 0)),
        ],
        out_specs=pl.BlockSpec(memory_space=pl.ANY),
        scratch_shapes=[pltpu.VMEM((2, LP, D_MODEL), F32), pltpu.VMEM((TE, D_MODEL), F32),
                        pltpu.SemaphoreType.DMA((2,)), pltpu.SemaphoreType.DMA(())],
    )
    return pl.pallas_call(
        _dispatch_kernel,
        grid_spec=grid_spec,
        out_shape=jax.ShapeDtypeStruct((NROWS, D_MODEL), F32),
        compiler_params=pltpu.CompilerParams(
            dimension_semantics=("arbitrary",), vmem_limit_bytes=VMEM_LIMIT),
        name="moe_dispatch",
    )(qtab, btab, ttab, pstart, pend, counts, post, h2)


def _combine_kernel(qtab_ref, btab_ref, ttab_ref, pstart_ref, pos_ref, prob_ref, ys_ref, o_ref, ybuf, sems):
    i = pl.program_id(0)
    n = pl.num_programs(0)
    slot = i % 2

    def piece(s):
        def copy(loc, src):
            return pltpu.make_async_copy(ys_ref.at[pl.ds(src, PIECE)], ybuf.at[s, pl.ds(loc, PIECE)],
                                         sems.at[s])
        return copy

    def start_pieces(tile, s):
        _for_each_piece(tile, qtab_ref, btab_ref, pstart_ref, lambda loc, src: piece(s)(loc, src).start())

    @pl.when(i == 0)
    def _():
        ybuf[...] = jnp.zeros_like(ybuf)
        start_pieces(0, 0)

    @pl.when(i + 1 < n)
    def _():
        start_pieces(i + 1, 1 - slot)

    def wait_piece(j, carry):
        piece(slot)(0, 0).wait()
        return carry

    lax.fori_loop(0, ttab_ref[i], wait_piece, 0)

    pos = pos_ref[...]
    prob = prob_ref[...]
    pcol = lax.broadcasted_iota(jnp.int32, (TD, LP), 1)
    w = jnp.zeros((TD, LP), F32)
    for k in range(TOP_K):
        w = w + jnp.where(pcol == pos[:, k:k + 1], prob[:, k:k + 1], 0.0)
    o_ref[...] = _dot(w.astype(BF16), ybuf[slot].astype(BF16))


def _combine_call(qtab, btab, ttab, pstart, pos, rprob, ys):
    grid_spec = pltpu.PrefetchScalarGridSpec(
        num_scalar_prefetch=4,
        grid=(NTOK // TD,),
        in_specs=[
            pl.BlockSpec((TD, LANES), lambda i, *_: (i, 0)),
            pl.BlockSpec((TD, LANES), lambda i, *_: (i, 0)),
            pl.BlockSpec(memory_space=pl.ANY),
        ],
        out_specs=pl.BlockSpec((TD, D_MODEL), lambda i, *_: (i, 0)),
        scratch_shapes=[pltpu.VMEM((2, LP, D_MODEL), F32), pltpu.SemaphoreType.DMA((2,))],
    )
    return pl.pallas_call(
        _combine_kernel,
        grid_spec=grid_spec,
        out_shape=jax.ShapeDtypeStruct((NTOK, D_MODEL), F32),
        compiler_params=pltpu.CompilerParams(
            dimension_semantics=("arbitrary",), vmem_limit_bytes=VMEM_LIMIT),
        name="moe_combine",
    )(qtab, btab, ttab, pstart, pos, rprob, ys)


def _ffn_kernel(cnt_ref, pend_ref, xs_ref, wg_ref, bg_ref, wu_ref, bu_ref, wd_ref, bd_ref,
                ys_ref, wg_s, wu_s, wd_s, xbuf, ybuf, sem_in, sem_out):
    e = pl.program_id(0)
    nrows = ((cnt_ref[e] + (TE - 1)) // TE) * TE
    row0 = pend_ref[e] - nrows
    nbig = nrows // TB
    nsmall = (nrows - nbig * TB) // TE

    wg_s[...] = wg_ref[...].astype(BF16)
    wu_s[...] = wu_ref[...].astype(BF16)
    wd_s[...] = wd_ref[...].astype(BF16)

    def run_tiles(nt, base, size):
        def rows(t):
            return pl.ds(pl.multiple_of(base + t * size, TE), size)

        def x_copy(t, slot):
            return pltpu.make_async_copy(xs_ref.at[rows(t)], xbuf.at[slot, pl.ds(0, size)], sem_in.at[slot])

        def y_copy(t, slot):
            return pltpu.make_async_copy(ybuf.at[slot, pl.ds(0, size)], ys_ref.at[rows(t)], sem_out.at[slot])

        @pl.when(nt > 0)
        def _():
            x_copy(0, 0).start()

        def tile(t, carry):
            slot = t % 2

            @pl.when(t + 1 < nt)
            def _():
                x_copy(t + 1, 1 - slot).start()

            x_copy(t, slot).wait()

            @pl.when(t >= 2)
            def _():
                y_copy(t - 2, slot).wait()

            x = xbuf[slot, 0:size].astype(BF16)
            y = bd_ref[...]
            for c in range(D_MODEL // FFC):
                cs = slice(c * FFC, (c + 1) * FFC)
                gate = jnp.minimum(_dot(x, wg_s[:, cs]) + bg_ref[:, cs], SW_LIMIT)
                up = jnp.clip(_dot(x, wu_s[:, cs]) + bu_ref[:, cs], -SW_LIMIT, SW_LIMIT)
                act = (up + 1.0) * gate * _sigmoid(SW_ALPHA * gate)
                y = y + _dot(act.astype(BF16), wd_s[cs, :])
            ybuf[slot, 0:size] = y
            y_copy(t, slot).start()
            return carry

        lax.fori_loop(0, nt, tile, 0)

        @pl.when(nt >= 2)
        def _():
            y_copy(nt - 2, nt % 2).wait()

        @pl.when(nt >= 1)
        def _():
            y_copy(nt - 1, (nt - 1) % 2).wait()

    run_tiles(nbig, row0, TB)
    run_tiles(nsmall, row0 + nbig * TB, TE)

    @pl.when(e == N_EXP - 1)
    def _():
        ybuf[0, 0:TE] = jnp.zeros((TE, D_MODEL), F32)

        def tail_fill(t, carry):
            cp = pltpu.make_async_copy(
                ybuf.at[0, pl.ds(0, TE)], ys_ref.at[pl.ds(pl.multiple_of(t * TE, TE), TE)], sem_out.at[0])
            cp.start()
            cp.wait()
            return carry

        lax.fori_loop(pend_ref[N_EXP - 1] // TE, NTE, tail_fill, 0)


def _ffn_call(layer, counts, pend, xs, wg, bg, wu, bu, wd, bd):
    wspec = pl.BlockSpec((None, None, D_MODEL, D_MODEL), lambda e, c, p: (layer, e, 0, 0))
    bspec = pl.BlockSpec((None, None, 1, D_MODEL), lambda e, c, p: (layer, e, 0, 0))
    grid_spec = pltpu.PrefetchScalarGridSpec(
        num_scalar_prefetch=2,
        grid=(N_EXP,),
        in_specs=[pl.BlockSpec(memory_space=pl.ANY), wspec, bspec, wspec, bspec, wspec, bspec],
        out_specs=pl.BlockSpec(memory_space=pl.ANY),
        scratch_shapes=[pltpu.VMEM((D_MODEL, D_MODEL), BF16)] * 3 + [
            pltpu.VMEM((2, TB, D_MODEL), F32), pltpu.VMEM((2, TB, D_MODEL), F32),
            pltpu.SemaphoreType.DMA((2,)), pltpu.SemaphoreType.DMA((2,))],
    )
    b4 = lambda b: b.reshape(DEPTH, N_EXP, 1, D_MODEL)
    return pl.pallas_call(
        _ffn_kernel,
        grid_spec=grid_spec,
        out_shape=jax.ShapeDtypeStruct((NROWS, D_MODEL), F32),
        compiler_params=pltpu.CompilerParams(
            dimension_semantics=("arbitrary",), vmem_limit_bytes=VMEM_LIMIT),
        name="expert_ffn",
    )(counts, pend, xs, wg, b4(bg), wu, b4(bu), wd, b4(bd))


def _moe(layer, h2, ridx, rprob, wg, bg, wu, bu, wd, bd):
    pos, post, tab, meta = _route_call(ridx)
    counts = meta[0, :N_EXP].astype(jnp.int32)
    pstart = meta[1, :N_EXP].astype(jnp.int32)
    pend = meta[2, :N_EXP].astype(jnp.int32)
    tab = tab.reshape(NTOK // TD, 8, LANES)
    qtab = tab[:, 0, :N_EXP].reshape(-1)
    btab = tab[:, 1, :N_EXP].reshape(-1)
    ttab = tab[:, 2, 0]
    xs = _dispatch_call(qtab, btab, ttab, pstart, pend, counts, post, h2)
    ys = _ffn_call(layer, counts, pend, xs, wg, bg, wu, bu, wd, bd)
    return _combine_call(qtab, btab, ttab, pstart, pos, rprob, ys)


def _final_kernel(x1_ref, moe_ref, g2_ref, ng_ref, y_ref):
    x = _gate_res(x1_ref[...], g2_ref[...], moe_ref[...])
    y_ref[...] = _rms_rows(x, ng_ref[...])


def _final_call(x1, moe, mod_l, norm_g):
    return pl.pallas_call(
        _final_kernel,
        grid=(NTILES,),
        in_specs=[_row_spec(D_MODEL), _row_spec(D_MODEL), _mod_spec(5), _full_spec((1, D_MODEL))],
        out_specs=_row_spec(D_MODEL),
        out_shape=jax.ShapeDtypeStruct((NTOK, D_MODEL), F32),
        compiler_params=pltpu.CompilerParams(
            dimension_semantics=("arbitrary",), vmem_limit_bytes=VMEM_LIMIT),
        name="final_norm",
    )(x1, moe, mod_l, norm_g)


def _block_diag(w):
    n, d, e = w.shape
    eye = jnp.eye(n, dtype=w.dtype)
    return (eye[:, None, :, None] * w[:, :, None, :]).reshape(n * d, n * e)


def _pad_lanes(v, offset):
    out = jnp.zeros((1, LANES), F32)
    return out.at[0, offset:offset + v.shape[0]].set(v)


def _mixer_params(l, w_in, lru_conv_w, lru_conv_b, lru_wa, lru_ba, lru_wx, lru_bx, lru_lambda,
                  gdn_conv_w, gdn_a_log, gdn_dt_bias, gdn_norm_g, gla_wg2, gla_bg, gla_norm_g):
    w = w_in[l]
    w_in_r = jnp.concatenate(
        [w[:, 0:2048], w[:, 2056:2824], w[:, 2048:2056], w[:, 2824:2840],
         jnp.zeros((D_MODEL, D_INP - 2840), F32)], axis=1).astype(BF16)
    row = lambda v: v.reshape(1, -1)
    mp = dict(
        lcw=lru_conv_w[l], lcb=row(lru_conv_b[l]),
        wa=_block_diag(lru_wa[l]).astype(BF16), ba=row(lru_ba[l]),
        wx=_block_diag(lru_wx[l]).astype(BF16), bx=row(lru_bx[l]),
        lam=row(lru_lambda[l]),
        gcw=gdn_conv_w[l],
        gpar=jnp.concatenate([_pad_lanes(gdn_a_log[l], GDN_H), _pad_lanes(gdn_dt_bias[l], GDN_H)], axis=0),
        gng=row(jnp.tile(gdn_norm_g[l], GDN_H)),
        wg2=jnp.zeros((LANES, LANES), F32).at[2 * GDN_H:2 * GDN_H + GLA_RANK].set(gla_wg2[l]),
        gbg=row(gla_bg[l]),
        lng=row(jnp.tile(gla_norm_g[l], GLA_H)),
    )
    return w_in_r, mp


def kernel(x_prompt, x_sample, state_lru_conv, state_lru_h, state_gdn_conv, state_gdn_S, state_gla_S, c_prompt, c_sample, ada_w, ada_b, norm1_g, norm2_g, w_in, lru_conv_w, lru_conv_b, lru_wa, lru_ba, lru_wx, lru_bx, lru_lambda, gdn_conv_w, gdn_a_log, gdn_dt_bias, gdn_norm_g, gla_wg2, gla_bg, gla_norm_g, w_out, router_w, router_b, exp_w_gate, exp_b_gate, exp_w_up, exp_b_up, exp_w_down, exp_b_down, final_norm_g):
    x = jnp.concatenate([
        x_prompt.transpose(1, 0, 2).reshape(NP, D_MODEL),
        x_sample.transpose(1, 0, 2).reshape(NS, D_MODEL)], axis=0)
    mod = _mod_call(jnp.concatenate([c_prompt, c_sample], axis=0), ada_w, ada_b)
    mod = jnp.stack([jnp.tile(mod[:, :BATCH], (1, MODB // BATCH, 1)), mod[:, BATCH:]], axis=1)

    p_states, s_states = [], []
    moe = None
    for l in range(DEPTH):
        w_in_r, mp = _mixer_params(l, w_in, lru_conv_w, lru_conv_b, lru_wa, lru_ba, lru_wx, lru_bx,
                                   lru_lambda, gdn_conv_w, gdn_a_log, gdn_dt_bias, gdn_norm_g,
                                   gla_wg2, gla_bg, gla_norm_g)
        x, proj = _in_call(x, moe, mod[l], norm1_g[l].reshape(1, D_MODEL), w_in_r,
                           mod[l - 1] if l else None)
        mix_p, p_lconv, p_lh, p_gconv, p_gs, p_lst = _mix_prompt_call(proj, mp)
        states = (state_lru_conv[l].transpose(1, 0, 2), state_lru_h[l],
                  state_gdn_conv[l].transpose(1, 0, 2),
                  state_gdn_S[l].reshape(DEC_BATCH, -1), state_gla_S[l].reshape(DEC_BATCH, -1))
        mix_s, s_lconv, s_lh, s_gconv, s_gs, s_ls = _mix_sample_call(proj, states, mp)
        rw = jnp.zeros((D_MODEL, LANES), F32).at[:, :N_EXP].set(router_w[l])
        rb = jnp.zeros((1, LANES), F32).at[0, :N_EXP].set(router_b[l])
        x, h2, ridx, rprob = _out_call(mix_p, mix_s.reshape(NS, D_MODEL), x, mod[l],
                                       w_out[l].astype(BF16), norm2_g[l].reshape(1, D_MODEL), rw, rb)
        moe = _moe(l, h2, ridx, rprob, exp_w_gate, exp_b_gate, exp_w_up, exp_b_up, exp_w_down, exp_b_down)
        p_gs = jnp.stack([p_gs[:, h * GDN_DK:(h + 1) * GDN_DK, h * GDN_DV:(h + 1) * GDN_DV]
                          for h in range(GDN_H)], axis=1)
        p_lst = jnp.stack([p_lst[:, h * GLA_DV:(h + 1) * GLA_DV, h * GLA_DK:(h + 1) * GLA_DK]
                           for h in range(GLA_H)], axis=1)
        p_states.append((p_lconv.reshape(3, BATCH, LRU_W).transpose(1, 0, 2), p_lh,
                         p_gconv.reshape(3, BATCH, GDN_CONV).transpose(1, 0, 2), p_gs,
                         p_lst.transpose(0, 1, 3, 2)))
        s_states.append((s_lconv.transpose(1, 0, 2), s_lh, s_gconv.transpose(1, 0, 2),
                         s_gs.reshape(DEC_BATCH, GDN_H, GDN_DK, GDN_DV),
                         s_ls.reshape(DEC_BATCH, GLA_H, GLA_DK, GLA_DV)))
    y = _final_call(x, moe, mod[DEPTH - 1], final_norm_g.reshape(1, D_MODEL))
    y_prompt = y[:NP].reshape(SEQ, BATCH, D_MODEL).transpose(1, 0, 2)
    y_sample = y[NP:].reshape(DEC_SEQ, DEC_BATCH, D_MODEL).transpose(1, 0, 2)
    ps = [jnp.stack([s[j] for s in p_states]) for j in range(5)]
    ss = [jnp.stack([s[j] for s in s_states]) for j in range(5)]
    return (y_prompt, y_sample, *ps, *ss)
```

```python
import functools

import numpy as np
import jax
import jax.numpy as jnp
from jax import lax
from jax.experimental import pallas as pl
from jax.experimental.pallas import tpu as pltpu

F32 = jnp.float32
BF16 = jnp.bfloat16

D_MODEL = 1024
BATCH = 8
SEQ = 2048
DEPTH = 2
DEC_BATCH = 128
DEC_SEQ = 4
CONV_W = 4
LRU_W = 512
LRU_BLOCKS = 8
LRU_C = 8.0
GDN_H = 4
GDN_DK = 64
GDN_DV = 64
GDN_CONV = GDN_H * (2 * GDN_DK + GDN_DV)
GLA_H = 4
GLA_DK = 32
GLA_DV = 64
GLA_RANK = 16
GLA_TAU = 16.0
GLA_CHUNK = 16
N_EXP = 32
TOP_K = 4
SW_LIMIT = 7.0
SW_ALPHA = 1.702
EPS = 1e-6

NP = BATCH * SEQ
NS = DEC_BATCH * DEC_SEQ
NTOK = NP + NS
TM = 512
NPT = NP // TM
NTILES = NTOK // TM
LANES = 128
MODB = 128

C_AX, C_AG, C_QKV, C_BZ = 0, 512, 1024, 1792
C_CQ, C_CK, C_CV, C_CR, C_SM = 2048, 2176, 2304, 2560, 2816
D_INP = 2944

TT = 64
BB = 32
TE = 256
TB = 768
FFC = 512
TD = TM
PIECE = 8
LP = 2304
NA = NTOK * TOP_K
NROWS = -(-(NA + (NTOK // TD) * N_EXP * (PIECE - 1) + N_EXP * (TE - 1)) // TE) * TE
NTE = NROWS // TE

VMEM_LIMIT = 50 * 1024 * 1024


def _dot(a, b):
    return jnp.dot(a, b, preferred_element_type=F32)


def _dot_nt(a, b):
    return lax.dot_general(a, b, (((1,), (1,)), ((), ())), preferred_element_type=F32)


def _dot_tn(a, b):
    return lax.dot_general(a, b, (((0,), (0,)), ((), ())), preferred_element_type=F32)


def _mm(a, b):
    return _dot(a.astype(BF16), b.astype(BF16))


def _mm_nt(a, b):
    return _dot_nt(a.astype(BF16), b.astype(BF16))


def _mm_tn(a, b):
    return _dot_tn(a.astype(BF16), b.astype(BF16))


def _split3(x):
    x1 = x.astype(BF16)
    r = x - x1.astype(F32)
    x2 = r.astype(BF16)
    x3 = (r - x2.astype(F32)).astype(BF16)
    return x1, x2, x3


def _dot3(a, b):
    a1 = a.astype(BF16)
    a2 = (a - a1.astype(F32)).astype(BF16)
    b1 = b.astype(BF16)
    b2 = (b - b1.astype(F32)).astype(BF16)
    return _dot(a1, b1) + (_dot(a2, b1) + _dot(a1, b2))


def _dot_mask_l(mask, x):
    x1, x2, x3 = _split3(x)
    return _dot(mask, x1) + (_dot(mask, x2) + _dot(mask, x3))


def _dot_mask_r(x, mask):
    x1 = x.astype(BF16)
    x2 = (x - x1.astype(F32)).astype(BF16)
    return _dot(x1, mask) + _dot(x2, mask)


def _dot_mask_r3(x, mask):
    x1, x2, x3 = _split3(x)
    return _dot(x1, mask) + (_dot(x2, mask) + _dot(x3, mask))


def _sigmoid(x):
    return jax.nn.sigmoid(x)


def _silu(x):
    return x * jax.nn.sigmoid(x)


def _softplus(x):
    return jnp.maximum(x, 0.0) + jnp.log1p(jnp.exp(-jnp.abs(x)))


def _rms_rows(x, g):
    return x * lax.rsqrt(jnp.mean(x * x, axis=-1, keepdims=True) + EPS) * g


def _modulate(y, scale, shift):
    rows = y.shape[0]
    y3 = y.reshape(rows // MODB, MODB, y.shape[1])
    return (y3 * (1.0 + scale) + shift).reshape(y.shape)


def _gate_res(x, gate, y):
    rows = y.shape[0]
    y3 = y.reshape(rows // MODB, MODB, y.shape[1])
    return x + (gate * y3).reshape(y.shape)


def _mod_kernel(c_ref, w_ref, b_ref, o_ref):
    o_ref[...] = _dot3(_silu(c_ref[...]), w_ref[...]) + b_ref[...]


def _mod_call(c_all, ada_w, ada_b):
    tn = 768
    rows = c_all.shape[0]
    return pl.pallas_call(
        _mod_kernel,
        grid=(DEPTH, 6 * D_MODEL // tn),
        in_specs=[
            pl.BlockSpec((rows, D_MODEL), lambda l, j: (0, 0)),
            pl.BlockSpec((None, D_MODEL, tn), lambda l, j: (l, 0, j)),
            pl.BlockSpec((None, 1, tn), lambda l, j: (l, 0, j)),
        ],
        out_specs=pl.BlockSpec((None, rows, tn), lambda l, j: (l, 0, j)),
        out_shape=jax.ShapeDtypeStruct((DEPTH, rows, 6 * D_MODEL), F32),
        compiler_params=pltpu.CompilerParams(
            dimension_semantics=("arbitrary", "arbitrary"), vmem_limit_bytes=VMEM_LIMIT),
        name="adaln_mod",
    )(c_all, ada_w, ada_b.reshape(DEPTH, 1, 6 * D_MODEL))


def _mod_spec(chunk):
    return pl.BlockSpec((None, MODB, D_MODEL), lambda i: (i // NPT, 0, chunk))


def _row_spec(width):
    return pl.BlockSpec((TM, width), lambda i: (i, 0))


def _full_spec(shape):
    nd = len(shape)
    return pl.BlockSpec(shape, lambda i: (0,) * nd)


def _in_kernel(has_res, *refs):
    if has_res:
        (x1_ref, moe_ref, g2_ref, n1_ref, sc_ref, sh_ref, w_ref, x_ref, proj_ref) = refs
        x = _gate_res(x1_ref[...], g2_ref[...], moe_ref[...])
        x_ref[...] = x
    else:
        (x_ref, n1_ref, sc_ref, sh_ref, w_ref, proj_ref) = refs
        x = x_ref[...]
    h = _modulate(_rms_rows(x, n1_ref[...]), sc_ref[...], sh_ref[...])
    proj_ref[...] = _dot(h.astype(BF16), w_ref[...])


def _in_call(x, moe, mod_l, norm_g, w_in_r, prev_mod):
    has_res = moe is not None
    ins, specs = [x], [_row_spec(D_MODEL)]
    outs = [jax.ShapeDtypeStruct((NTOK, D_INP), F32)]
    out_specs = [_row_spec(D_INP)]
    if has_res:
        ins += [moe, prev_mod]
        specs += [_row_spec(D_MODEL), _mod_spec(5)]
        outs = [jax.ShapeDtypeStruct((NTOK, D_MODEL), F32)] + outs
        out_specs = [_row_spec(D_MODEL)] + out_specs
    ins += [norm_g, mod_l, mod_l, w_in_r]
    specs += [_full_spec((1, D_MODEL)), _mod_spec(1), _mod_spec(0), _full_spec((D_MODEL, D_INP))]
    res = pl.pallas_call(
        functools.partial(_in_kernel, has_res),
        grid=(NTILES,),
        in_specs=specs,
        out_specs=out_specs,
        out_shape=outs,
        compiler_params=pltpu.CompilerParams(
            dimension_semantics=("arbitrary",), vmem_limit_bytes=VMEM_LIMIT),
        name="in_proj",
    )(*ins)
    return res if has_res else (x, res[0])


def _lru_gates(xc, wa_ref, ba_ref, wx_ref, bx_ref, lam_ref):
    xb = xc.astype(BF16)
    r = _sigmoid(_dot(xb, wa_ref[...]) + ba_ref[...])
    i = _sigmoid(_dot(xb, wx_ref[...]) + bx_ref[...])
    log_a = -LRU_C * r * _softplus(-lam_ref[...])
    a = jnp.exp(log_a)
    mult = jnp.sqrt(1.0 - jnp.exp(2.0 * log_a))
    return a, mult, i * xc


def _gelu(x):
    return jax.nn.gelu(x, approximate=True)


def _gdn_prep(qkv, small, gpar_ref, ones_blk):
    qkv = _silu(qkv)
    q = qkv[:, 0:256]
    k = qkv[:, 256:512]
    v = qkv[:, 512:768]
    q = q * lax.rsqrt(_dot_mask_r(q * q, ones_blk) + EPS) * (GDN_DK ** -0.5)
    k = k * lax.rsqrt(_dot_mask_r(k * k, ones_blk) + EPS)
    beta = _sigmoid(small)
    g = -jnp.exp(gpar_ref[0:1, :]) * _softplus(small + gpar_ref[1:2, :])
    lane = lax.broadcasted_iota(jnp.int32, small.shape, 1)
    bg = jnp.where(lane < GDN_H, beta, g)
    return q, k, v, bg


def _gla_prep(proj_q, proj_k, small, wg2_ref, gbg_ref):
    pre = _dot3(small, wg2_ref[...]) + gbg_ref[...]
    gk = -_softplus(-pre) / GLA_TAU
    return proj_q * (GLA_DK ** -0.5), proj_k, gk


def _head_norm_gate(o, norm_g, gate_in, ones_blk):
    ms = _dot_mask_r(o * o, ones_blk) * (1.0 / GDN_DV)
    return o * lax.rsqrt(ms + EPS) * norm_g * _silu(gate_in)


def _mix_prompt_kernel(proj_ref, lcw_ref, lcb_ref, wa_ref, ba_ref, wx_ref, bx_ref, lam_ref,
                       gcw_ref, gpar_ref, gng_ref, wg2_ref, gbg_ref, lng_ref,
                       ones_ref, tri_ref, e2_ref, btblk_ref,
                       mix_ref, lconv_ref, lh_ref, gconv_ref, gs_ref, lst_ref,
                       exta, extb, a_s, u_s, hs_s, q_s, k_s, v_s, bg_s, og_s,
                       q2_s, k2_s, v2_s, gk_s, ol_s):
    nb = BATCH
    rows = TT * nb
    step = pl.program_id(0)

    @pl.when(step == 0)
    def _():
        exta[0:3 * nb, :] = jnp.zeros((3 * nb, LRU_W), F32)
        extb[0:3 * nb, :] = jnp.zeros((3 * nb, GDN_CONV), F32)
        lh_ref[...] = jnp.zeros_like(lh_ref)
        gs_ref[...] = jnp.zeros_like(gs_ref)
        lst_ref[...] = jnp.zeros_like(lst_ref)

    def conv(ext, x, w_ref):
        ext[pl.ds(3 * nb, rows), :] = x
        y = ext[pl.ds(0, rows), :] * w_ref[0:1, :]
        for j in range(1, CONV_W):
            y = y + ext[pl.ds(j * nb, rows), :] * w_ref[j:j + 1, :]
        tail = ext[pl.ds(rows, 3 * nb), :]
        ext[pl.ds(0, 3 * nb), :] = tail
        return y, tail

    ones_blk = ones_ref[...]

    xa, tail = conv(exta, proj_ref[:, C_AX:C_AX + LRU_W], lcw_ref)
    lconv_ref[...] = tail
    xc = xa + lcb_ref[...]
    a, mult, ix = _lru_gates(xc, wa_ref, ba_ref, wx_ref, bx_ref, lam_ref)
    rid = lax.broadcasted_iota(jnp.int32, (rows, LRU_W), 0)
    mult = jnp.where((rid < nb) & (step == 0), 1.0, mult)
    a_s[...] = a
    u_s[...] = mult * ix

    def scan_body(t, h):
        off = pl.multiple_of(t * nb, nb)
        h = a_s[pl.ds(off, nb), :] * h + u_s[pl.ds(off, nb), :]
        hs_s[pl.ds(off, nb), :] = h
        return h

    lh_ref[...] = lax.fori_loop(0, TT, scan_body, lh_ref[...], unroll=8)
    mix_ref[:, 0:LRU_W] = (hs_s[...] * _gelu(proj_ref[:, C_AG:C_AG + LRU_W])).astype(BF16)

    small = proj_ref[:, C_SM:C_SM + LANES]
    qkv, tail = conv(extb, proj_ref[:, C_QKV:C_QKV + GDN_CONV], gcw_ref)
    gconv_ref[...] = tail
    q, k, v, bg = _gdn_prep(qkv, small, gpar_ref, ones_blk)
    for j in range(2):
        ls = slice(j * LANES, (j + 1) * LANES)
        q_s[j] = q[:, ls]
        k_s[j] = k[:, ls]
        v_s[j] = v[:, ls]
    bg_s[...] = bg

    q2, k2, gk = _gla_prep(proj_ref[:, C_CQ:C_CQ + 128], proj_ref[:, C_CK:C_CK + 128],
                           small, wg2_ref, gbg_ref)
    q2_s[...] = q2
    k2_s[...] = k2
    gk_s[...] = gk
    for j in range(2):
        v2_s[j] = proj_ref[:, C_CV + j * LANES:C_CV + (j + 1) * LANES]

    ri = lax.broadcasted_iota(jnp.int32, (TT, 256), 0)
    lane = lax.broadcasted_iota(jnp.int32, (TT, 256), 1)
    cj = lane % TT
    incl = ri >= cj
    strict = ri > cj
    eye = ri == cj
    blk16m = (ri // 16) == (cj // 16)
    blk32m = (ri // 32) == (cj // 32)
    blk_causal = incl & blk16m
    hmask = [(lane // GDN_DV == h).astype(BF16) for h in range(GDN_H)]
    lane2 = lax.broadcasted_iota(jnp.int32, (TT, LANES), 1)
    hmask2 = [(lane2 // GLA_DK == h).astype(BF16) for h in range(GLA_H)]
    bdm = (lax.broadcasted_iota(jnp.int32, (256, 256), 0) // GDN_DK
           == lax.broadcasted_iota(jnp.int32, (256, 256), 1) // GDN_DV)
    bdm2 = (lax.broadcasted_iota(jnp.int32, (256, LANES), 0) // GLA_DV
            == lax.broadcasted_iota(jnp.int32, (256, LANES), 1) // GLA_DK)
    tri = tri_ref[...]
    e2 = e2_ref[...]
    btblk = btblk_ref[...]
    ones8 = jnp.ones((8, TT), BF16)
    seqs = range(nb)
    rsel = [pl.ds(b, TT, stride=nb) for b in seqs]

    def bd(x):
        xb = x.astype(BF16)
        return jnp.concatenate([xb * m for m in hmask], axis=0)

    def bd2(x):
        xb = x.astype(BF16)
        return jnp.concatenate([xb * m for m in hmask2], axis=0)

    def cat2(ref, b):
        return jnp.concatenate([ref[0, rsel[b], :], ref[1, rsel[b], :]], axis=1)

    def each(f, *lists):
        return [f(*xs) for xs in zip(*lists)]

    q = [cat2(q_s, b) for b in seqs]
    k = [cat2(k_s, b) for b in seqs]
    v = [cat2(v_s, b) for b in seqs]
    bgb = [bg_s[rsel[b], :] for b in seqs]
    gc_all = each(lambda x: _dot_mask_l(tri, x), bgb)
    both = each(lambda x, c: _dot_mask_r3(jnp.where(lane2 < GDN_H, x, c), e2), bgb, gc_all)
    beta = [x[:, 0:256] for x in both]
    gc = [x[:, 256:512] for x in both]
    gc_row = each(lambda g: _dot_mask_l(ones8, jnp.where(eye, g, 0.0))[0:1, :], gc)
    decay = each(lambda g, gr: jnp.exp(jnp.where(incl, g - gr, -jnp.inf)), gc, gc_row)
    egc = each(jnp.exp, gc)
    kb = each(lambda a, b_: a * b_, k, beta)
    bdk = each(bd, k)
    kq = each(lambda kb_, q_, m: _dot_nt(jnp.concatenate([kb_, q_], axis=0).astype(BF16), m), kb, q, bdk)
    amat = each(lambda x, d: jnp.where(strict, x[0:TT] * d, 0.0), kq, decay)
    qk = each(lambda x, d: jnp.where(incl, x[TT:] * d, 0.0), kq, decay)
    y = each(lambda a: -jnp.where(blk16m, a, 0.0), amat)
    n = y
    bdy = each(bd, y)
    for _ in range(3):
        y = each(lambda y_, m: _dot(y_.astype(BF16), m), y, bdy)
        bdy = each(bd, y)
        n = each(lambda n_, y_, m: n_ + y_ + _dot(n_.astype(BF16), m), n, y, bdy)
    t = each(lambda n_: jnp.where(eye, 1.0, 0.0) + n_, n)
    for lower in (each(lambda a: jnp.where(blk32m & ~blk16m, a, 0.0), amat),
                  each(lambda a: jnp.where(blk32m, 0.0, a), amat)):
        tl = each(lambda t_, l_: _dot(t_.astype(BF16), bd(l_)), t, lower)
        t = each(lambda t_, x: t_ - _dot(x.astype(BF16), bd(t_)), t, tl)
    uw = each(lambda t_, v_, b_, kb_, e: _dot(
        t_.astype(BF16), jnp.concatenate([bd(v_ * b_), bd(kb_ * e)], axis=1)), t, v, beta, kb, egc)
    s = [gs_ref[b] for b in seqs]
    ws = each(lambda x, q_, e, s_: _dot(
        jnp.concatenate([x[:, 256:512], q_ * e], axis=0).astype(BF16), s_.astype(BF16)), uw, q, egc, s)
    vn = each(lambda x, w_: x[:, 0:256] - w_[0:TT], uw, ws)
    o = each(lambda w_, a, v_: w_[TT:] + _dot(a.astype(BF16), bd(v_)), ws, qk, vn)
    for b in seqs:
        g_last = gc[b][TT - 1:TT, :]
        kd = k[b] * jnp.exp(g_last - gc[b])
        gs_ref[b] = s[b] * jnp.exp(g_last) + jnp.where(bdm, _mm_tn(kd, vn[b]), 0.0)
        for j in range(2):
            og_s[j, rsel[b], :] = o[b][:, j * LANES:(j + 1) * LANES]

    cum = [_dot_mask_l(btblk, gk_s[rsel[b], :]) for b in seqs]
    bc = [x[0:TT] for x in cum]
    bl = [x[TT:] for x in cum]
    qb = [q2_s[rsel[b], :] for b in seqs]
    kb2 = [k2_s[rsel[b], :] for b in seqs]
    vb = [cat2(v2_s, b) for b in seqs]
    qi = each(lambda a, c: (a * jnp.exp(c)).astype(BF16), qb, bc)
    ki = each(lambda a, c: a * jnp.exp(-c), kb2, bc)
    kst = each(lambda a, l_, c: (a * jnp.exp(l_ - c)).astype(BF16), kb2, bl, bc)
    ebl = each(jnp.exp, bl)
    sc = each(lambda a, c: jnp.where(blk_causal, _dot_nt(a, bd2(c)), 0.0), qi, ki)
    oh = each(lambda a, v_: _dot(a.astype(BF16), bd(v_)), sc, vb)
    st = [lst_ref[b] for b in seqs]
    vbb = each(lambda v_: v_.astype(BF16), vb)
    outs = [[] for _ in seqs]
    for c in range(TT // GLA_CHUNK):
        rs = slice(c * GLA_CHUNK, (c + 1) * GLA_CHUNK)
        for b in seqs:
            outs[b].append(oh[b][rs] + _dot_nt(qi[b][rs], st[b].astype(BF16)))
        upd = [jnp.where(bdm2, _dot_tn(vbb[b][rs], kst[b][rs]), 0.0) for b in seqs]
        st = [st[b] * ebl[b][c * GLA_CHUNK:c * GLA_CHUNK + 1, :] + upd[b] for b in seqs]
    for b in seqs:
        lst_ref[b] = st[b]
        ol = jnp.concatenate(outs[b], axis=0)
        for j in range(2):
            ol_s[j, rsel[b], :] = ol[:, j * LANES:(j + 1) * LANES]

    og = jnp.concatenate([og_s[0], og_s[1]], axis=1)
    out_b = _head_norm_gate(og, gng_ref[...], proj_ref[:, C_BZ:C_BZ + 256], ones_blk)
    mix_ref[:, 512:768] = out_b.astype(BF16)
    ol = jnp.concatenate([ol_s[0], ol_s[1]], axis=1)
    out_c = _head_norm_gate(ol, lng_ref[...], proj_ref[:, C_CR:C_CR + 256], ones_blk)
    mix_ref[:, 768:1024] = out_c.astype(BF16)


def _mix_consts():
    r = np.arange(TT)
    tri = (r[:, None] >= r[None, :]).astype(np.float32)
    same = (r[:, None] // GLA_CHUNK) == (r[None, :] // GLA_CHUNK)
    bt16 = (tri.astype(bool) & same).astype(np.float32)
    blk16 = same.astype(np.float32)
    c = np.arange(256)
    ones_blk = ((c[:, None] // GDN_DV) == (c[None, :] // GDN_DV)).astype(np.float32)
    e2 = np.zeros((LANES, 512), np.float32)
    for h in range(GDN_H):
        e2[h, h * GDN_DV:(h + 1) * GDN_DV] = 1.0
        e2[GDN_H + h, 256 + h * GDN_DV:256 + (h + 1) * GDN_DV] = 1.0
    btblk = np.concatenate([bt16, blk16], axis=0)
    return (jnp.asarray(ones_blk, BF16), jnp.asarray(tri, BF16),
            jnp.asarray(e2, BF16), jnp.asarray(btblk, BF16))


def _mix_prompt_call(proj, mp):
    rows = TT * BATCH
    consts = _mix_consts()
    params = (mp["lcw"], mp["lcb"], mp["wa"], mp["ba"], mp["wx"], mp["bx"], mp["lam"],
              mp["gcw"], mp["gpar"], mp["gng"], mp["wg2"], mp["gbg"], mp["lng"]) + consts
    out_shape = [
        jax.ShapeDtypeStruct((NP, D_MODEL), BF16),
        jax.ShapeDtypeStruct((3 * BATCH, LRU_W), F32),
        jax.ShapeDtypeStruct((BATCH, LRU_W), F32),
        jax.ShapeDtypeStruct((3 * BATCH, GDN_CONV), F32),
        jax.ShapeDtypeStruct((BATCH, GDN_H * GDN_DK, GDN_H * GDN_DV), F32),
        jax.ShapeDtypeStruct((BATCH, GLA_H * GLA_DV, GLA_H * GLA_DK), F32),
    ]
    out_specs = [pl.BlockSpec((rows, D_MODEL), lambda i: (i, 0))] + [
        _full_spec(s.shape) for s in out_shape[1:]]
    scratch = [
        pltpu.VMEM((rows + 3 * BATCH, LRU_W), F32),
        pltpu.VMEM((rows + 3 * BATCH, GDN_CONV), F32),
        pltpu.VMEM((rows, LRU_W), F32), pltpu.VMEM((rows, LRU_W), F32), pltpu.VMEM((rows, LRU_W), F32),
        pltpu.VMEM((2, rows, LANES), F32), pltpu.VMEM((2, rows, LANES), F32), pltpu.VMEM((2, rows, LANES), F32),
        pltpu.VMEM((rows, LANES), F32), pltpu.VMEM((2, rows, LANES), F32),
        pltpu.VMEM((rows, LANES), F32), pltpu.VMEM((rows, LANES), F32), pltpu.VMEM((2, rows, LANES), F32),
        pltpu.VMEM((rows, LANES), F32), pltpu.VMEM((2, rows, LANES), F32),
    ]
    return pl.pallas_call(
        _mix_prompt_kernel,
        grid=(SEQ // TT,),
        in_specs=[pl.BlockSpec((rows, D_INP), lambda i: (i, 0))] + [_full_spec(p.shape) for p in params],
        out_specs=out_specs,
        out_shape=out_shape,
        scratch_shapes=scratch,
        compiler_params=pltpu.CompilerParams(
            dimension_semantics=("arbitrary",), vmem_limit_bytes=VMEM_LIMIT),
        name="mix_prompt",
    )(proj, *params)


def _pair_bcast(x, p, lo_mask):
    return jnp.where(lo_mask, x[:, 2 * p:2 * p + 1], x[:, 2 * p + 1:2 * p + 2])


def _fold_pairs(acc):
    return acc[:, 0:64] + acc[:, 64:128]


def _mix_sample_kernel(p0_ref, p1_ref, p2_ref, p3_ref, lconv_in, lh_in, gconv_in, gs_in, ls_in,
                       lcw_ref, lcb_ref, wa_ref, ba_ref, wx_ref, bx_ref, lam_ref,
                       gcw_ref, gpar_ref, gng_ref, wg2_ref, gbg_ref, lng_ref, ones_ref,
                       mix_ref, lconv_ref, lh_ref, gconv_ref, gs_ref, ls_ref):
    rows = DEC_SEQ * BB
    prefs = (p0_ref, p1_ref, p2_ref, p3_ref)
    ones_blk = ones_ref[...]

    def cols(c0, width):
        return [p[:, c0:c0 + width] for p in prefs]

    def conv(prev, xs, w_ref):
        ext = [prev[j] for j in range(CONV_W - 1)] + xs
        ys = []
        for t in range(DEC_SEQ):
            y = ext[t] * w_ref[0:1, :]
            for j in range(1, CONV_W):
                y = y + ext[t + j] * w_ref[j:j + 1, :]
            ys.append(y)
        return jnp.concatenate(ys, axis=0), ext[DEC_SEQ:]

    def rows_of(x, t):
        return x[t * BB:(t + 1) * BB]

    xa, tail = conv(lconv_in, cols(C_AX, LRU_W), lcw_ref)
    for j in range(CONV_W - 1):
        lconv_ref[j] = tail[j]
    xc = xa + lcb_ref[...]
    a, mult, ix = _lru_gates(xc, wa_ref, ba_ref, wx_ref, bx_ref, lam_ref)
    u = mult * ix
    h = lh_in[...]
    hs = []
    for t in range(DEC_SEQ):
        h = rows_of(a, t) * h + rows_of(u, t)
        hs.append(h)
    lh_ref[...] = h
    ag = jnp.concatenate(cols(C_AG, LRU_W), axis=0)
    mix_a = jnp.concatenate(hs, axis=0) * _gelu(ag)

    small = jnp.concatenate(cols(C_SM, LANES), axis=0)
    qkv, tail = conv(gconv_in, cols(C_QKV, GDN_CONV), gcw_ref)
    for j in range(CONV_W - 1):
        gconv_ref[j] = tail[j]
    q, k, v, bg = _gdn_prep(qkv, small, gpar_ref, ones_blk)
    lo_mask = lax.broadcasted_iota(jnp.int32, (BB, LANES), 1) < 64
    gs_ref[...] = gs_in[...]
    hd = GDN_DK * GDN_DV
    o_heads = []
    for h in range(GDN_H):
        o_t = []
        for t in range(DEC_SEQ):
            rs = slice(t * BB, (t + 1) * BB)
            cs = slice(h * GDN_DK, (h + 1) * GDN_DK)
            eg = jnp.exp(bg[rs, GDN_H + h:GDN_H + h + 1])
            beta = bg[rs, h:h + 1]
            kt, qt, vt = k[rs, cs], q[rs, cs], v[rs, cs]
            kks = [_pair_bcast(kt, p, lo_mask) for p in range(GDN_DK // 2)]
            acc = jnp.zeros((BB, LANES), F32)
            for p in range(GDN_DK // 2):
                acc = acc + gs_ref[:, h * hd + p * LANES:h * hd + (p + 1) * LANES] * kks[p]
            vn = beta * (vt - eg * _fold_pairs(acc))
            vn2 = jnp.concatenate([vn, vn], axis=1)
            oacc = jnp.zeros((BB, LANES), F32)
            for p in range(GDN_DK // 2):
                sl = slice(h * hd + p * LANES, h * hd + (p + 1) * LANES)
                s = eg * gs_ref[:, sl] + kks[p] * vn2
                gs_ref[:, sl] = s
                oacc = oacc + s * _pair_bcast(qt, p, lo_mask)
            o_t.append(_fold_pairs(oacc))
        o_heads.append(jnp.concatenate(o_t, axis=0))
    o_b = jnp.concatenate(o_heads, axis=1)
    bz = jnp.concatenate(cols(C_BZ, 256), axis=0)
    mix_b = _head_norm_gate(o_b, gng_ref[...], bz, ones_blk)

    q2, k2, gk = _gla_prep(jnp.concatenate(cols(C_CQ, 128), axis=0),
                           jnp.concatenate(cols(C_CK, 128), axis=0), small, wg2_ref, gbg_ref)
    v2 = jnp.concatenate(cols(C_CV, 256), axis=0)
    ls_ref[...] = ls_in[...]
    hd2 = GLA_DK * GLA_DV
    o_heads = []
    for h in range(GLA_H):
        o_t = []
        for t in range(DEC_SEQ):
            rs = slice(t * BB, (t + 1) * BB)
            cs = slice(h * GLA_DK, (h + 1) * GLA_DK)
            al = jnp.exp(gk[rs, cs])
            kt, qt = k2[rs, cs], q2[rs, cs]
            vt = v2[rs, h * GLA_DV:(h + 1) * GLA_DV]
            v2x = jnp.concatenate([vt, vt], axis=1)
            oacc = jnp.zeros((BB, LANES), F32)
            for p in range(GLA_DK // 2):
                sl = slice(h * hd2 + p * LANES, h * hd2 + (p + 1) * LANES)
                s = _pair_bcast(al, p, lo_mask) * ls_ref[:, sl] + _pair_bcast(kt, p, lo_mask) * v2x
                ls_ref[:, sl] = s
                oacc = oacc + s * _pair_bcast(qt, p, lo_mask)
            o_t.append(_fold_pairs(oacc))
        o_heads.append(jnp.concatenate(o_t, axis=0))
    o_c = jnp.concatenate(o_heads, axis=1)
    cr = jnp.concatenate(cols(C_CR, 256), axis=0)
    mix_c = _head_norm_gate(o_c, lng_ref[...], cr, ones_blk)

    mix = jnp.concatenate([mix_a, mix_b, mix_c], axis=1).astype(BF16)
    for t in range(DEC_SEQ):
        mix_ref[t] = mix[t * BB:(t + 1) * BB]


def _mix_sample_call(proj, states, mp):
    lconv, lh, gconv, gs, ls = states
    ones_blk = _mix_consts()[0]
    params = (mp["lcw"], mp["lcb"], mp["wa"], mp["ba"], mp["wx"], mp["bx"], mp["lam"],
              mp["gcw"], mp["gpar"], mp["gng"], mp["wg2"], mp["gbg"], mp["lng"], ones_blk)
    nblk = DEC_BATCH // BB
    base = NP // BB

    def proj_spec(t):
        return pl.BlockSpec((BB, D_INP), lambda j: (base + t * nblk + j, 0))

    def bspec3(n, width):
        return pl.BlockSpec((n, BB, width), lambda j: (0, j, 0))

    def bspec2(width):
        return pl.BlockSpec((BB, width), lambda j: (j, 0))

    gdn_flat = GDN_H * GDN_DK * GDN_DV
    gla_flat = GLA_H * GLA_DK * GLA_DV
    state_specs = [bspec3(3, LRU_W), bspec2(LRU_W), bspec3(3, GDN_CONV), bspec2(gdn_flat), bspec2(gla_flat)]
    out_shape = [
        jax.ShapeDtypeStruct((DEC_SEQ, DEC_BATCH, D_MODEL), BF16),
        jax.ShapeDtypeStruct((3, DEC_BATCH, LRU_W), F32),
        jax.ShapeDtypeStruct((DEC_BATCH, LRU_W), F32),
        jax.ShapeDtypeStruct((3, DEC_BATCH, GDN_CONV), F32),
        jax.ShapeDtypeStruct((DEC_BATCH, gdn_flat), F32),
        jax.ShapeDtypeStruct((DEC_BATCH, gla_flat), F32),
    ]
    return pl.pallas_call(
        _mix_sample_kernel,
        grid=(nblk,),
        in_specs=[proj_spec(t) for t in range(DEC_SEQ)] + state_specs + [_full_spec(p.shape) for p in params],
        out_specs=[bspec3(DEC_SEQ, D_MODEL)] + state_specs,
        out_shape=out_shape,
        compiler_params=pltpu.CompilerParams(
            dimension_semantics=("arbitrary",), vmem_limit_bytes=VMEM_LIMIT),
        name="mix_sample",
    )(proj, proj, proj, proj, lconv, lh, gconv, gs, ls, *params)


def _out_kernel(mp_ref, ms_ref, x_ref, w_ref, g1_ref, n2_ref, sc_ref, sh_ref, rw_ref, rb_ref,
                x1_ref, h2_ref, ridx_ref, rprob_ref, tcnt_ref):
    step = pl.program_id(0)
    mix = jnp.where(step < NPT, mp_ref[...], ms_ref[...])
    x1 = _gate_res(x_ref[...], g1_ref[...], _dot(mix, w_ref[...]))
    x1_ref[...] = x1
    h2 = _modulate(_rms_rows(x1, n2_ref[...]), sc_ref[...], sh_ref[...])
    h2_ref[...] = h2
    logits = _dot3(h2, rw_ref[...]) + rb_ref[...]
    lane = lax.broadcasted_iota(jnp.int32, logits.shape, 1)
    cur = jnp.where(lane < N_EXP, logits, -jnp.inf)
    vals, idxs = [], []
    for _ in range(TOP_K):
        m = jnp.max(cur, axis=-1, keepdims=True)
        idx = jnp.min(jnp.where(cur == m, lane, LANES), axis=-1, keepdims=True)
        vals.append(m)
        idxs.append(idx)
        cur = jnp.where(lane == idx, -jnp.inf, cur)
    es = [jnp.exp(v - vals[0]) for v in vals]
    den = es[0] + es[1] + es[2] + es[3]
    ridx = jnp.zeros(logits.shape, jnp.int32)
    rprob = jnp.zeros(logits.shape, F32)
    for j in range(TOP_K):
        ridx = jnp.where(lane == j, idxs[j], ridx)
        rprob = jnp.where(lane == j, es[j] / den, rprob)
    ridx_ref[...] = ridx
    rprob_ref[...] = rprob
    onehot = jnp.zeros(logits.shape, F32)
    for idx in idxs:
        onehot = onehot + jnp.where(lane == idx, 1.0, 0.0)
    colsum = jnp.sum(onehot, axis=0, keepdims=True)
    aligned = jnp.floor((colsum + (PIECE - 1)) * (1.0 / PIECE)) * PIECE
    tcnt_ref[...] = jnp.concatenate([aligned, jnp.zeros((7, LANES), F32)], axis=0)


def _out_call(mix_p, mix_s, x, mod_l, w_out, norm_g, rw, rb):
    return pl.pallas_call(
        _out_kernel,
        grid=(NTILES,),
        in_specs=[
            pl.BlockSpec((TM, D_MODEL), lambda i: (jnp.minimum(i, NPT - 1), 0)),
            _full_spec((NS, D_MODEL)),
            _row_spec(D_MODEL),
            _full_spec((D_MODEL, D_MODEL)),
            _mod_spec(2),
            _full_spec((1, D_MODEL)),
            _mod_spec(4),
            _mod_spec(3),
            _full_spec((D_MODEL, LANES)),
            _full_spec((1, LANES)),
        ],
        out_specs=[_row_spec(D_MODEL), _row_spec(D_MODEL), _row_spec(LANES), _row_spec(LANES),
                   pl.BlockSpec((8, LANES), lambda i: (i, 0))],
        out_shape=[
            jax.ShapeDtypeStruct((NTOK, D_MODEL), F32),
            jax.ShapeDtypeStruct((NTOK, D_MODEL), F32),
            jax.ShapeDtypeStruct((NTOK, LANES), jnp.int32),
            jax.ShapeDtypeStruct((NTOK, LANES), F32),
            jax.ShapeDtypeStruct((NTILES * 8, LANES), F32),
        ],
        compiler_params=pltpu.CompilerParams(
            dimension_semantics=("arbitrary",), vmem_limit_bytes=VMEM_LIMIT),
        name="out_proj_router",
    )(mix_p, mix_s, x, w_out, mod_l, norm_g, mod_l, mod_l, rw, rb)


def _excl_lane_cumsum(row):
    r = lax.broadcasted_iota(jnp.int32, (LANES, LANES), 0)
    c = lax.broadcasted_iota(jnp.int32, (LANES, LANES), 1)
    before = jnp.where(r < c, 1.0, 0.0).astype(BF16)
    return _dot_mask_r3(jnp.broadcast_to(row, (8, LANES)), before)[0:1]


def _route_kernel(ridx_ref, tcnt_ref, total_ref, pos_ref, post_ref, tab_ref, meta_ref, cnt_s):
    i = pl.program_id(0)
    lane = lax.broadcasted_iota(jnp.int32, (TD, LANES), 1)
    ridx = ridx_ref[...]
    hits = [lane == ridx[:, k:k + 1] for k in range(TOP_K)]
    onehot = jnp.zeros((TD, LANES), F32)
    for hit in hits:
        onehot = onehot + jnp.where(hit, 1.0, 0.0)
    aligned = tcnt_ref[0:1, :]

    @pl.when(i == 0)
    def _():
        total = total_ref[...]
        padded = jnp.floor((total + (TE - 1)) * (1.0 / TE)) * TE
        pstart = _excl_lane_cumsum(padded)
        meta_ref[...] = jnp.concatenate(
            [total, pstart, pstart + padded, jnp.zeros((5, LANES), F32)], axis=0)
        cnt_s[...] = jnp.zeros_like(cnt_s)

    r = lax.broadcasted_iota(jnp.int32, (TD, TD), 0)
    c = lax.broadcasted_iota(jnp.int32, (TD, TD), 1)
    earlier = jnp.where(c < r, 1.0, 0.0).astype(BF16)
    pos = _excl_lane_cumsum(aligned) + _dot(earlier, onehot.astype(BF16))
    posk = jnp.zeros((TD, LANES), F32)
    for k, hit in enumerate(hits):
        d = jnp.sum(jnp.where(hit, pos, 0.0), axis=1, keepdims=True)
        posk = jnp.where(lane == k, d, posk)
    pos_ref[...] = posk.astype(jnp.int32)
    post_ref[...] = posk.T[0:8].astype(jnp.int32)
    pieces = aligned * (1.0 / PIECE)
    npieces = jnp.broadcast_to(jnp.sum(pieces, axis=1, keepdims=True), (1, LANES))
    tab_ref[...] = jnp.concatenate(
        [pieces, cnt_s[...], npieces, jnp.zeros((5, LANES), F32)], axis=0).astype(jnp.int32)
    cnt_s[...] += aligned


def _route_call(ridx, tcnt):
    nblk = NTOK // TD
    total = jnp.sum(tcnt.reshape(nblk, 8, LANES)[:, 0], axis=0, keepdims=True)
    return pl.pallas_call(
        _route_kernel,
        grid=(nblk,),
        in_specs=[pl.BlockSpec((TD, LANES), lambda i: (i, 0)),
                  pl.BlockSpec((8, LANES), lambda i: (i, 0)),
                  _full_spec((1, LANES))],
        out_specs=[pl.BlockSpec((TD, LANES), lambda i: (i, 0)),
                   pl.BlockSpec((8, TD), lambda i: (i, 0)),
                   pl.BlockSpec((8, LANES), lambda i: (i, 0)),
                   pl.BlockSpec((8, LANES), lambda i: (0, 0))],
        out_shape=[jax.ShapeDtypeStruct((NTOK, LANES), jnp.int32),
                   jax.ShapeDtypeStruct((nblk * 8, TD), jnp.int32),
                   jax.ShapeDtypeStruct((nblk * 8, LANES), jnp.int32),
                   jax.ShapeDtypeStruct((8, LANES), F32)],
        scratch_shapes=[pltpu.VMEM((1, LANES), F32)],
        compiler_params=pltpu.CompilerParams(
            dimension_semantics=("arbitrary",), vmem_limit_bytes=VMEM_LIMIT),
        name="route",
    )(ridx, tcnt, total)


def _for_each_piece(i, ptab_ref, btab_ref, pstart_ref, fn):
    off = jnp.int32(0)
    for e in range(N_EXP):
        npieces = ptab_ref[i * N_EXP + e]
        dst0 = pstart_ref[e] + btab_ref[i * N_EXP + e]

        def body(j, carry, off=off, dst0=dst0):
            fn(pl.multiple_of(off + j * PIECE, PIECE), pl.multiple_of(dst0 + j * PIECE, PIECE))
            return carry

        lax.fori_loop(0, npieces, body, 0)
        off = off + npieces * PIECE


def _dispatch_kernel(qtab_ref, btab_ref, ttab_ref, pstart_ref, pend_ref, cnt_ref, post_ref, h_ref, xs_ref,
                     sbuf, zero_s, sems, semz):
    i = pl.program_id(0)
    n = pl.num_programs(0)
    slot = i % 2

    def piece(s):
        def copy(src, dst):
            return pltpu.make_async_copy(sbuf.at[s, pl.ds(src, PIECE)], xs_ref.at[pl.ds(dst, PIECE)],
                                         sems.at[s])
        return copy

    def start_pieces(tile, s):
        _for_each_piece(tile, qtab_ref, btab_ref, pstart_ref, lambda src, dst: piece(s)(src, dst).start())

    def wait_pieces(tile, s):
        def body(j, carry):
            piece(s)(0, 0).wait()
            return carry
        lax.fori_loop(0, ttab_ref[tile], body, 0)

    @pl.when(i == 0)
    def _():
        zero_s[...] = jnp.zeros_like(zero_s)

        def fill(start):
            return pltpu.make_async_copy(zero_s, xs_ref.at[pl.ds(pl.multiple_of(start, TE), TE)], semz)

        for e in range(N_EXP):
            @pl.when(cnt_ref[e] > 0)
            def _():
                fill(pend_ref[e] - TE).start()
        for e in range(N_EXP):
            @pl.when(cnt_ref[e] > 0)
            def _():
                fill(pend_ref[e] - TE).wait()

        def tail_fill(t, carry):
            fill(t * TE).start()
            fill(t * TE).wait()
            return carry

        lax.fori_loop(pend_ref[N_EXP - 1] // TE, NTE, tail_fill, 0)

    @pl.when(i >= 2)
    def _():
        wait_pieces(i - 2, slot)

    post = post_ref[...]
    prow = lax.broadcasted_iota(jnp.int32, (LP, TD), 0)
    sel = prow == post[0:1, :]
    for k in range(1, TOP_K):
        sel = sel | (prow == post[k:k + 1, :])
    onehot = jnp.where(sel, 1.0, 0.0).astype(BF16)
    sbuf[slot] = _dot(onehot, h_ref[...].astype(BF16))
    start_pieces(i, slot)

    @pl.when(i == n - 1)
    def _():
        @pl.when(i >= 1)
        def _():
            wait_pieces(i - 1, 1 - slot)
        wait_pieces(i, slot)


def _dispatch_call(qtab, btab, ttab, pstart, pend, counts, post, h2):
    grid_spec = pltpu.PrefetchScalarGridSpec(
        num_scalar_prefetch=6,
        grid=(NTOK // TD,),
        in_specs=[
            pl.BlockSpec((8, TD), lambda i, *_: (i, 0)),
            pl.BlockSpec((TD, D_MODEL), lambda i, *_: (i, 0)),
        ],
        out_specs=pl.BlockSpec(memory_space=pl.ANY),
        scratch_shapes=[pltpu.VMEM((2, LP, D_MODEL), F32), pltpu.VMEM((TE, D_MODEL), F32),
                        pltpu.SemaphoreType.DMA((2,)), pltpu.SemaphoreType.DMA(())],
    )
    return pl.pallas_call(
        _dispatch_kernel,
        grid_spec=grid_spec,
        out_shape=jax.ShapeDtypeStruct((NROWS, D_MODEL), F32),
        compiler_params=pltpu.CompilerParams(
            dimension_semantics=("arbitrary",), vmem_limit_bytes=VMEM_LIMIT),
        name="moe_dispatch",
    )(qtab, btab, ttab, pstart, pend, counts, post, h2)


def _combine_kernel(qtab_ref, btab_ref, ttab_ref, pstart_ref, pos_ref, prob_ref, ys_ref, o_ref, ybuf, sems):
    i = pl.program_id(0)
    n = pl.num_programs(0)
    slot = i % 2

    def piece(s):
        def copy(loc, src):
            return pltpu.make_async_copy(ys_ref.at[pl.ds(src, PIECE)], ybuf.at[s, pl.ds(loc, PIECE)],
                                         sems.at[s])
        return copy

    def start_pieces(tile, s):
        _for_each_piece(tile, qtab_ref, btab_ref, pstart_ref, lambda loc, src: piece(s)(loc, src).start())

    @pl.when(i == 0)
    def _():
        ybuf[...] = jnp.zeros_like(ybuf)
        start_pieces(0, 0)

    @pl.when(i + 1 < n)
    def _():
        start_pieces(i + 1, 1 - slot)

    def wait_piece(j, carry):
        piece(slot)(0, 0).wait()
        return carry

    lax.fori_loop(0, ttab_ref[i], wait_piece, 0)

    pos = pos_ref[...]
    prob = prob_ref[...]
    pcol = lax.broadcasted_iota(jnp.int32, (TD, LP), 1)
    w = jnp.zeros((TD, LP), F32)
    for k in range(TOP_K):
        w = w + jnp.where(pcol == pos[:, k:k + 1], prob[:, k:k + 1], 0.0)
    o_ref[...] = _dot(w.astype(BF16), ybuf[slot].astype(BF16))


def _combine_call(qtab, btab, ttab, pstart, pos, rprob, ys):
    grid_spec = pltpu.PrefetchScalarGridSpec(
        num_scalar_prefetch=4,
        grid=(NTOK // TD,),
        in_specs=[
            pl.BlockSpec((TD, LANES), lambda i, *_: (i, 0)),
            pl.BlockSpec((TD, LANES), lambda i, *_: (i, 0)),
            pl.BlockSpec(memory_space=pl.ANY),
        ],
        out_specs=pl.BlockSpec((TD, D_MODEL), lambda i, *_: (i, 0)),
        scratch_shapes=[pltpu.VMEM((2, LP, D_MODEL), F32), pltpu.SemaphoreType.DMA((2,))],
    )
    return pl.pallas_call(
        _combine_kernel,
        grid_spec=grid_spec,
        out_shape=jax.ShapeDtypeStruct((NTOK, D_MODEL), F32),
        compiler_params=pltpu.CompilerParams(
            dimension_semantics=("arbitrary",), vmem_limit_bytes=VMEM_LIMIT),
        name="moe_combine",
    )(qtab, btab, ttab, pstart, pos, rprob, ys)


def _ffn_kernel(cnt_ref, pend_ref, xs_ref, wg_ref, bg_ref, wu_ref, bu_ref, wd_ref, bd_ref,
                ys_ref, wg_s, wu_s, wd_s, xbuf, ybuf, sem_in, sem_out):
    e = pl.program_id(0)
    nrows = ((cnt_ref[e] + (TE - 1)) // TE) * TE
    row0 = pend_ref[e] - nrows
    nbig = nrows // TB
    nsmall = (nrows - nbig * TB) // TE

    wg_s[:, 0:D_MODEL] = wg_ref[...].astype(BF16)
    wu_s[:, 0:D_MODEL] = wu_ref[...].astype(BF16)
    wd_s[:, 0:D_MODEL] = wd_ref[...].astype(BF16)

    def run_tiles(nt, base, size):
        def rows(t):
            return pl.ds(pl.multiple_of(base + t * size, TE), size)

        def x_copy(t, slot):
            return pltpu.make_async_copy(xs_ref.at[rows(t)], xbuf.at[slot, pl.ds(0, size)], sem_in.at[slot])

        def y_copy(t, slot):
            return pltpu.make_async_copy(ybuf.at[slot, pl.ds(0, size)], ys_ref.at[rows(t)], sem_out.at[slot])

        @pl.when(nt > 0)
        def _():
            x_copy(0, 0).start()

        def tile(t, carry):
            slot = t % 2

            @pl.when(t + 1 < nt)
            def _():
                x_copy(t + 1, 1 - slot).start()

            x_copy(t, slot).wait()

            @pl.when(t >= 2)
            def _():
                y_copy(t - 2, slot).wait()

            x = xbuf[slot, 0:size].astype(BF16)
            y = bd_ref[...]
            for c in range(D_MODEL // FFC):
                cs = slice(c * FFC, (c + 1) * FFC)
                gate = jnp.minimum(_dot(x, wg_s[:, cs]) + bg_ref[:, cs], SW_LIMIT)
                up = jnp.clip(_dot(x, wu_s[:, cs]) + bu_ref[:, cs], -SW_LIMIT, SW_LIMIT)
                act = (up + 1.0) * gate * _sigmoid(SW_ALPHA * gate)
                y = y + _dot(act.astype(BF16), wd_s[cs, 0:D_MODEL])
            ybuf[slot, 0:size] = y
            y_copy(t, slot).start()
            return carry

        lax.fori_loop(0, nt, tile, 0)

        @pl.when(nt >= 2)
        def _():
            y_copy(nt - 2, nt % 2).wait()

        @pl.when(nt >= 1)
        def _():
            y_copy(nt - 1, (nt - 1) % 2).wait()

    run_tiles(nbig, row0, TB)
    run_tiles(nsmall, row0 + nbig * TB, TE)

    @pl.when(e == N_EXP - 1)
    def _():
        ybuf[0, 0:TE] = jnp.zeros((TE, D_MODEL), F32)

        def tail_fill(t, carry):
            cp = pltpu.make_async_copy(
                ybuf.at[0, pl.ds(0, TE)], ys_ref.at[pl.ds(pl.multiple_of(t * TE, TE), TE)], sem_out.at[0])
            cp.start()
            cp.wait()
            return carry

        lax.fori_loop(pend_ref[N_EXP - 1] // TE, NTE, tail_fill, 0)


def _ffn_call(layer, counts, pend, xs, wg, bg, wu, bu, wd, bd):
    wspec = pl.BlockSpec((None, None, D_MODEL, D_MODEL), lambda e, c, p: (layer, e, 0, 0))
    bspec = pl.BlockSpec((None, None, 1, D_MODEL), lambda e, c, p: (layer, e, 0, 0))
    grid_spec = pltpu.PrefetchScalarGridSpec(
        num_scalar_prefetch=2,
        grid=(N_EXP,),
        in_specs=[pl.BlockSpec(memory_space=pl.ANY), wspec, bspec, wspec, bspec, wspec, bspec],
        out_specs=pl.BlockSpec(memory_space=pl.ANY),
        scratch_shapes=[pltpu.VMEM((D_MODEL, D_MODEL + LANES), BF16)] * 3 + [
            pltpu.VMEM((2, TB, D_MODEL), F32), pltpu.VMEM((2, TB, D_MODEL), F32),
            pltpu.SemaphoreType.DMA((2,)), pltpu.SemaphoreType.DMA((2,))],
    )
    b4 = lambda b: b.reshape(DEPTH, N_EXP, 1, D_MODEL)
    return pl.pallas_call(
        _ffn_kernel,
        grid_spec=grid_spec,
        out_shape=jax.ShapeDtypeStruct((NROWS, D_MODEL), F32),
        compiler_params=pltpu.CompilerParams(
            dimension_semantics=("arbitrary",), vmem_limit_bytes=VMEM_LIMIT),
        name="expert_ffn",
    )(counts, pend, xs, wg, b4(bg), wu, b4(bu), wd, b4(bd))


def _moe(layer, h2, ridx, rprob, tcnt, wg, bg, wu, bu, wd, bd):
    pos, post, tab, meta = _route_call(ridx, tcnt)
    counts = meta[0, :N_EXP].astype(jnp.int32)
    pstart = meta[1, :N_EXP].astype(jnp.int32)
    pend = meta[2, :N_EXP].astype(jnp.int32)
    tab = tab.reshape(NTOK // TD, 8, LANES)
    qtab = tab[:, 0, :N_EXP].reshape(-1)
    btab = tab[:, 1, :N_EXP].reshape(-1)
    ttab = tab[:, 2, 0]
    xs = _dispatch_call(qtab, btab, ttab, pstart, pend, counts, post, h2)
    ys = _ffn_call(layer, counts, pend, xs, wg, bg, wu, bu, wd, bd)
    return _combine_call(qtab, btab, ttab, pstart, pos, rprob, ys)


def _final_kernel(x1_ref, moe_ref, g2_ref, ng_ref, y_ref):
    x = _gate_res(x1_ref[...], g2_ref[...], moe_ref[...])
    y_ref[...] = _rms_rows(x, ng_ref[...])


def _final_call(x1, moe, mod_l, norm_g):
    return pl.pallas_call(
        _final_kernel,
        grid=(NTILES,),
        in_specs=[_row_spec(D_MODEL), _row_spec(D_MODEL), _mod_spec(5), _full_spec((1, D_MODEL))],
        out_specs=_row_spec(D_MODEL),
        out_shape=jax.ShapeDtypeStruct((NTOK, D_MODEL), F32),
        compiler_params=pltpu.CompilerParams(
            dimension_semantics=("arbitrary",), vmem_limit_bytes=VMEM_LIMIT),
        name="final_norm",
    )(x1, moe, mod_l, norm_g)


def _block_diag(w):
    n, d, e = w.shape
    eye = jnp.eye(n, dtype=w.dtype)
    return (eye[:, None, :, None] * w[:, :, None, :]).reshape(n * d, n * e)


def _pad_lanes(v, offset):
    out = jnp.zeros((1, LANES), F32)
    return out.at[0, offset:offset + v.shape[0]].set(v)


def _mixer_params(l, w_in, lru_conv_w, lru_conv_b, lru_wa, lru_ba, lru_wx, lru_bx, lru_lambda,
                  gdn_conv_w, gdn_a_log, gdn_dt_bias, gdn_norm_g, gla_wg2, gla_bg, gla_norm_g):
    w = w_in[l]
    w_in_r = jnp.concatenate(
        [w[:, 0:2048], w[:, 2056:2824], w[:, 2048:2056], w[:, 2824:2840],
         jnp.zeros((D_MODEL, D_INP - 2840), F32)], axis=1).astype(BF16)
    row = lambda v: v.reshape(1, -1)
    mp = dict(
        lcw=lru_conv_w[l], lcb=row(lru_conv_b[l]),
        wa=_block_diag(lru_wa[l]).astype(BF16), ba=row(lru_ba[l]),
        wx=_block_diag(lru_wx[l]).astype(BF16), bx=row(lru_bx[l]),
        lam=row(lru_lambda[l]),
        gcw=gdn_conv_w[l],
        gpar=jnp.concatenate([_pad_lanes(gdn_a_log[l], GDN_H), _pad_lanes(gdn_dt_bias[l], GDN_H)], axis=0),
        gng=row(jnp.tile(gdn_norm_g[l], GDN_H)),
        wg2=jnp.zeros((LANES, LANES), F32).at[2 * GDN_H:2 * GDN_H + GLA_RANK].set(gla_wg2[l]),
        gbg=row(gla_bg[l]),
        lng=row(jnp.tile(gla_norm_g[l], GLA_H)),
    )
    return w_in_r, mp


def kernel(x_prompt, x_sample, state_lru_conv, state_lru_h, state_gdn_conv, state_gdn_S, state_gla_S, c_prompt, c_sample, ada_w, ada_b, norm1_g, norm2_g, w_in, lru_conv_w, lru_conv_b, lru_wa, lru_ba, lru_wx, lru_bx, lru_lambda, gdn_conv_w, gdn_a_log, gdn_dt_bias, gdn_norm_g, gla_wg2, gla_bg, gla_norm_g, w_out, router_w, router_b, exp_w_gate, exp_b_gate, exp_w_up, exp_b_up, exp_w_down, exp_b_down, final_norm_g):
    x = jnp.concatenate([
        x_prompt.transpose(1, 0, 2).reshape(NP, D_MODEL),
        x_sample.transpose(1, 0, 2).reshape(NS, D_MODEL)], axis=0)
    mod = _mod_call(jnp.concatenate([c_prompt, c_sample], axis=0), ada_w, ada_b)
    mod = jnp.stack([jnp.tile(mod[:, :BATCH], (1, MODB // BATCH, 1)), mod[:, BATCH:]], axis=1)

    p_states, s_states = [], []
    moe = None
    for l in range(DEPTH):
        w_in_r, mp = _mixer_params(l, w_in, lru_conv_w, lru_conv_b, lru_wa, lru_ba, lru_wx, lru_bx,
                                   lru_lambda, gdn_conv_w, gdn_a_log, gdn_dt_bias, gdn_norm_g,
                                   gla_wg2, gla_bg, gla_norm_g)
        x, proj = _in_call(x, moe, mod[l], norm1_g[l].reshape(1, D_MODEL), w_in_r,
                           mod[l - 1] if l else None)
        mix_p, p_lconv, p_lh, p_gconv, p_gs, p_lst = _mix_prompt_call(proj, mp)
        states = (state_lru_conv[l].transpose(1, 0, 2), state_lru_h[l],
                  state_gdn_conv[l].transpose(1, 0, 2),
                  state_gdn_S[l].reshape(DEC_BATCH, -1), state_gla_S[l].reshape(DEC_BATCH, -1))
        mix_s, s_lconv, s_lh, s_gconv, s_gs, s_ls = _mix_sample_call(proj, states, mp)
        rw = jnp.zeros((D_MODEL, LANES), F32).at[:, :N_EXP].set(router_w[l])
        rb = jnp.zeros((1, LANES), F32).at[0, :N_EXP].set(router_b[l])
        x, h2, ridx, rprob, tcnt = _out_call(mix_p, mix_s.reshape(NS, D_MODEL), x, mod[l],
                                             w_out[l].astype(BF16), norm2_g[l].reshape(1, D_MODEL), rw, rb)
        moe = _moe(l, h2, ridx, rprob, tcnt, exp_w_gate, exp_b_gate, exp_w_up, exp_b_up,
                   exp_w_down, exp_b_down)
        p_gs = jnp.stack([p_gs[:, h * GDN_DK:(h + 1) * GDN_DK, h * GDN_DV:(h + 1) * GDN_DV]
                          for h in range(GDN_H)], axis=1)
        p_lst = jnp.stack([p_lst[:, h * GLA_DV:(h + 1) * GLA_DV, h * GLA_DK:(h + 1) * GLA_DK]
                           for h in range(GLA_H)], axis=1)
        p_states.append((p_lconv.reshape(3, BATCH, LRU_W).transpose(1, 0, 2), p_lh,
                         p_gconv.reshape(3, BATCH, GDN_CONV).transpose(1, 0, 2), p_gs,
                         p_lst.transpose(0, 1, 3, 2)))
        s_states.append((s_lconv.transpose(1, 0, 2), s_lh, s_gconv.transpose(1, 0, 2),
                         s_gs.reshape(DEC_BATCH, GDN_H, GDN_DK, GDN_DV),
                         s_ls.reshape(DEC_BATCH, GLA_H, GLA_DK, GLA_DV)))
    y = _final_call(x, moe, mod[DEPTH - 1], final_norm_g.reshape(1, D_MODEL))
    y_prompt = y[:NP].reshape(SEQ, BATCH, D_MODEL).transpose(1, 0, 2)
    y_sample = y[NP:].reshape(DEC_SEQ, DEC_BATCH, D_MODEL).transpose(1, 0, 2)
    ps = [jnp.stack([s[j] for s in p_states]) for j in range(5)]
    ss = [jnp.stack([s[j] for s in s_states]) for j in range(5)]
    return (y_prompt, y_sample, *ps, *ss)
```

```python
import functools

import numpy as np
import jax
import jax.numpy as jnp
from jax import lax
from jax.experimental import pallas as pl
from jax.experimental.pallas import tpu as pltpu

F32 = jnp.float32
BF16 = jnp.bfloat16

D_MODEL = 1024
BATCH = 8
SEQ = 2048
DEPTH = 2
DEC_BATCH = 128
DEC_SEQ = 4
CONV_W = 4
LRU_W = 512
LRU_BLOCKS = 8
LRU_C = 8.0
GDN_H = 4
GDN_DK = 64
GDN_DV = 64
GDN_CONV = GDN_H * (2 * GDN_DK + GDN_DV)
GLA_H = 4
GLA_DK = 32
GLA_DV = 64
GLA_RANK = 16
GLA_TAU = 16.0
GLA_CHUNK = 16
N_EXP = 32
TOP_K = 4
SW_LIMIT = 7.0
SW_ALPHA = 1.702
EPS = 1e-6

NP = BATCH * SEQ
NS = DEC_BATCH * DEC_SEQ
NTOK = NP + NS
TM = 512
NPT = NP // TM
NTILES = NTOK // TM
LANES = 128
MODB = 128

C_AX, C_AG, C_QKV, C_BZ = 0, 512, 1024, 1792
C_CQ, C_CK, C_CV, C_CR, C_SM = 2048, 2176, 2304, 2560, 2816
D_INP = 2944

TT = 64
BB = 32
TE = 256
TB = 768
FFC = 512
TD = TM
PIECE = 8
LP = 2304
NA = NTOK * TOP_K
NROWS = -(-(NA + (NTOK // TD) * N_EXP * (PIECE - 1) + N_EXP * (TE - 1)) // TE) * TE
NTE = NROWS // TE

VMEM_LIMIT = 50 * 1024 * 1024


def _dot(a, b):
    return jnp.dot(a, b, preferred_element_type=F32)


def _dot_nt(a, b):
    return lax.dot_general(a, b, (((1,), (1,)), ((), ())), preferred_element_type=F32)


def _dot_tn(a, b):
    return lax.dot_general(a, b, (((0,), (0,)), ((), ())), preferred_element_type=F32)


def _mm(a, b):
    return _dot(a.astype(BF16), b.astype(BF16))


def _mm_nt(a, b):
    return _dot_nt(a.astype(BF16), b.astype(BF16))


def _mm_tn(a, b):
    return _dot_tn(a.astype(BF16), b.astype(BF16))


def _split3(x):
    x1 = x.astype(BF16)
    r = x - x1.astype(F32)
    x2 = r.astype(BF16)
    x3 = (r - x2.astype(F32)).astype(BF16)
    return x1, x2, x3


def _dot3(a, b):
    a1 = a.astype(BF16)
    a2 = (a - a1.astype(F32)).astype(BF16)
    b1 = b.astype(BF16)
    b2 = (b - b1.astype(F32)).astype(BF16)
    return _dot(a1, b1) + (_dot(a2, b1) + _dot(a1, b2))


def _dot_mask_l(mask, x):
    x1, x2, x3 = _split3(x)
    return _dot(mask, x1) + (_dot(mask, x2) + _dot(mask, x3))


def _dot_mask_r(x, mask):
    x1 = x.astype(BF16)
    x2 = (x - x1.astype(F32)).astype(BF16)
    return _dot(x1, mask) + _dot(x2, mask)


def _dot_mask_r3(x, mask):
    x1, x2, x3 = _split3(x)
    return _dot(x1, mask) + (_dot(x2, mask) + _dot(x3, mask))


HALF = D_MODEL // 2
U32 = jnp.uint32


def _pack_bf16_pairs(x):
    r = x.astype(BF16).astype(F32)
    hi = lax.bitcast_convert_type(r[:, 0:HALF], U32)
    lo = lax.bitcast_convert_type(r[:, HALF:], U32)
    return (hi & jnp.uint32(0xFFFF0000)) | (lo >> 16)


def _unpack_bf16_pairs(w):
    hi = lax.bitcast_convert_type(w & jnp.uint32(0xFFFF0000), F32)
    lo = lax.bitcast_convert_type(w << 16, F32)
    return jnp.concatenate([hi, lo], axis=1).astype(BF16)


def _sigmoid(x):
    return jax.nn.sigmoid(x)


def _silu(x):
    return x * jax.nn.sigmoid(x)


def _softplus(x):
    return jnp.maximum(x, 0.0) + jnp.log1p(jnp.exp(-jnp.abs(x)))


def _rms_rows(x, g):
    return x * lax.rsqrt(jnp.mean(x * x, axis=-1, keepdims=True) + EPS) * g


def _modulate(y, scale, shift):
    rows = y.shape[0]
    y3 = y.reshape(rows // MODB, MODB, y.shape[1])
    return (y3 * (1.0 + scale) + shift).reshape(y.shape)


def _gate_res(x, gate, y):
    rows = y.shape[0]
    y3 = y.reshape(rows // MODB, MODB, y.shape[1])
    return x + (gate * y3).reshape(y.shape)


def _mod_kernel(c_ref, w_ref, b_ref, o_ref):
    o_ref[...] = _dot3(_silu(c_ref[...]), w_ref[...]) + b_ref[...]


def _mod_call(c_all, ada_w, ada_b):
    tn = 768
    rows = c_all.shape[0]
    return pl.pallas_call(
        _mod_kernel,
        grid=(DEPTH, 6 * D_MODEL // tn),
        in_specs=[
            pl.BlockSpec((rows, D_MODEL), lambda l, j: (0, 0)),
            pl.BlockSpec((None, D_MODEL, tn), lambda l, j: (l, 0, j)),
            pl.BlockSpec((None, 1, tn), lambda l, j: (l, 0, j)),
        ],
        out_specs=pl.BlockSpec((None, rows, tn), lambda l, j: (l, 0, j)),
        out_shape=jax.ShapeDtypeStruct((DEPTH, rows, 6 * D_MODEL), F32),
        compiler_params=pltpu.CompilerParams(
            dimension_semantics=("arbitrary", "arbitrary"), vmem_limit_bytes=VMEM_LIMIT),
        name="adaln_mod",
    )(c_all, ada_w, ada_b.reshape(DEPTH, 1, 6 * D_MODEL))


def _mod_spec(chunk):
    return pl.BlockSpec((None, MODB, D_MODEL), lambda i: (i // NPT, 0, chunk))


def _row_spec(width):
    return pl.BlockSpec((TM, width), lambda i: (i, 0))


def _full_spec(shape):
    nd = len(shape)
    return pl.BlockSpec(shape, lambda i: (0,) * nd)


def _in_kernel(has_res, *refs):
    if has_res:
        (x1_ref, moe_ref, g2_ref, n1_ref, sc_ref, sh_ref, w_ref, x_ref, proj_ref) = refs
        x = _gate_res(x1_ref[...], g2_ref[...], moe_ref[...])
        x_ref[...] = x
    else:
        (x_ref, n1_ref, sc_ref, sh_ref, w_ref, proj_ref) = refs
        x = x_ref[...]
    h = _modulate(_rms_rows(x, n1_ref[...]), sc_ref[...], sh_ref[...])
    proj_ref[...] = _dot(h.astype(BF16), w_ref[...])


def _in_call(x, moe, mod_l, norm_g, w_in_r, prev_mod):
    has_res = moe is not None
    ins, specs = [x], [_row_spec(D_MODEL)]
    outs = [jax.ShapeDtypeStruct((NTOK, D_INP), F32)]
    out_specs = [_row_spec(D_INP)]
    if has_res:
        ins += [moe, prev_mod]
        specs += [_row_spec(D_MODEL), _mod_spec(5)]
        outs = [jax.ShapeDtypeStruct((NTOK, D_MODEL), F32)] + outs
        out_specs = [_row_spec(D_MODEL)] + out_specs
    ins += [norm_g, mod_l, mod_l, w_in_r]
    specs += [_full_spec((1, D_MODEL)), _mod_spec(1), _mod_spec(0), _full_spec((D_MODEL, D_INP))]
    res = pl.pallas_call(
        functools.partial(_in_kernel, has_res),
        grid=(NTILES,),
        in_specs=specs,
        out_specs=out_specs,
        out_shape=outs,
        compiler_params=pltpu.CompilerParams(
            dimension_semantics=("arbitrary",), vmem_limit_bytes=VMEM_LIMIT),
        name="in_proj",
    )(*ins)
    return res if has_res else (x, res[0])


def _lru_gates(xc, wa_ref, ba_ref, wx_ref, bx_ref, lam_ref):
    xb = xc.astype(BF16)
    r = _sigmoid(_dot(xb, wa_ref[...]) + ba_ref[...])
    i = _sigmoid(_dot(xb, wx_ref[...]) + bx_ref[...])
    log_a = -LRU_C * r * _softplus(-lam_ref[...])
    a = jnp.exp(log_a)
    mult = jnp.sqrt(1.0 - jnp.exp(2.0 * log_a))
    return a, mult, i * xc


def _gelu(x):
    return jax.nn.gelu(x, approximate=True)


def _gdn_prep(qkv, small, gpar_ref, ones_blk):
    qkv = _silu(qkv)
    q = qkv[:, 0:256]
    k = qkv[:, 256:512]
    v = qkv[:, 512:768]
    q = q * lax.rsqrt(_dot_mask_r(q * q, ones_blk) + EPS) * (GDN_DK ** -0.5)
    k = k * lax.rsqrt(_dot_mask_r(k * k, ones_blk) + EPS)
    beta = _sigmoid(small)
    g = -jnp.exp(gpar_ref[0:1, :]) * _softplus(small + gpar_ref[1:2, :])
    lane = lax.broadcasted_iota(jnp.int32, small.shape, 1)
    bg = jnp.where(lane < GDN_H, beta, g)
    return q, k, v, bg


def _gla_prep(proj_q, proj_k, small, wg2_ref, gbg_ref):
    pre = _dot3(small, wg2_ref[...]) + gbg_ref[...]
    gk = -_softplus(-pre) / GLA_TAU
    return proj_q * (GLA_DK ** -0.5), proj_k, gk


def _head_norm_gate(o, norm_g, gate_in, ones_blk):
    ms = _dot_mask_r(o * o, ones_blk) * (1.0 / GDN_DV)
    return o * lax.rsqrt(ms + EPS) * norm_g * _silu(gate_in)


def _mix_prompt_kernel(proj_ref, lcw_ref, lcb_ref, wa_ref, ba_ref, wx_ref, bx_ref, lam_ref,
                       gcw_ref, gpar_ref, gng_ref, wg2_ref, gbg_ref, lng_ref,
                       ones_ref, tri_ref, e2_ref, btblk_ref,
                       mix_ref, lconv_ref, lh_ref, gconv_ref, gs_ref, lst_ref,
                       exta, extb, a_s, u_s, hs_s, q_s, k_s, v_s, bg_s, og_s,
                       q2_s, k2_s, v2_s, gk_s, ol_s):
    nb = BATCH
    rows = TT * nb
    step = pl.program_id(0)

    @pl.when(step == 0)
    def _():
        exta[0:3 * nb, :] = jnp.zeros((3 * nb, LRU_W), F32)
        extb[0:3 * nb, :] = jnp.zeros((3 * nb, GDN_CONV), F32)
        lh_ref[...] = jnp.zeros_like(lh_ref)
        gs_ref[...] = jnp.zeros_like(gs_ref)
        lst_ref[...] = jnp.zeros_like(lst_ref)

    def conv(ext, x, w_ref):
        ext[pl.ds(3 * nb, rows), :] = x
        y = ext[pl.ds(0, rows), :] * w_ref[0:1, :]
        for j in range(1, CONV_W):
            y = y + ext[pl.ds(j * nb, rows), :] * w_ref[j:j + 1, :]
        tail = ext[pl.ds(rows, 3 * nb), :]
        ext[pl.ds(0, 3 * nb), :] = tail
        return y, tail

    ones_blk = ones_ref[...]

    xa, tail = conv(exta, proj_ref[:, C_AX:C_AX + LRU_W], lcw_ref)
    lconv_ref[...] = tail
    xc = xa + lcb_ref[...]
    a, mult, ix = _lru_gates(xc, wa_ref, ba_ref, wx_ref, bx_ref, lam_ref)
    rid = lax.broadcasted_iota(jnp.int32, (rows, LRU_W), 0)
    mult = jnp.where((rid < nb) & (step == 0), 1.0, mult)
    a_s[...] = a
    u_s[...] = mult * ix

    def scan_body(t, h):
        off = pl.multiple_of(t * nb, nb)
        h = a_s[pl.ds(off, nb), :] * h + u_s[pl.ds(off, nb), :]
        hs_s[pl.ds(off, nb), :] = h
        return h

    lh_ref[...] = lax.fori_loop(0, TT, scan_body, lh_ref[...], unroll=8)
    mix_ref[:, 0:LRU_W] = (hs_s[...] * _gelu(proj_ref[:, C_AG:C_AG + LRU_W])).astype(BF16)

    small = proj_ref[:, C_SM:C_SM + LANES]
    qkv, tail = conv(extb, proj_ref[:, C_QKV:C_QKV + GDN_CONV], gcw_ref)
    gconv_ref[...] = tail
    q, k, v, bg = _gdn_prep(qkv, small, gpar_ref, ones_blk)
    for j in range(2):
        ls = slice(j * LANES, (j + 1) * LANES)
        q_s[j] = q[:, ls]
        k_s[j] = k[:, ls]
        v_s[j] = v[:, ls]
    bg_s[...] = bg

    q2, k2, gk = _gla_prep(proj_ref[:, C_CQ:C_CQ + 128], proj_ref[:, C_CK:C_CK + 128],
                           small, wg2_ref, gbg_ref)
    q2_s[...] = q2
    k2_s[...] = k2
    gk_s[...] = gk
    for j in range(2):
        v2_s[j] = proj_ref[:, C_CV + j * LANES:C_CV + (j + 1) * LANES]

    ri = lax.broadcasted_iota(jnp.int32, (TT, 256), 0)
    lane = lax.broadcasted_iota(jnp.int32, (TT, 256), 1)
    cj = lane % TT
    incl = ri >= cj
    strict = ri > cj
    eye = ri == cj
    blk16m = (ri // 16) == (cj // 16)
    blk32m = (ri // 32) == (cj // 32)
    blk_causal = incl & blk16m
    hmask = [(lane // GDN_DV == h).astype(BF16) for h in range(GDN_H)]
    lane2 = lax.broadcasted_iota(jnp.int32, (TT, LANES), 1)
    hmask2 = [(lane2 // GLA_DK == h).astype(BF16) for h in range(GLA_H)]
    bdm = (lax.broadcasted_iota(jnp.int32, (256, 256), 0) // GDN_DK
           == lax.broadcasted_iota(jnp.int32, (256, 256), 1) // GDN_DV)
    bdm2 = (lax.broadcasted_iota(jnp.int32, (256, LANES), 0) // GLA_DV
            == lax.broadcasted_iota(jnp.int32, (256, LANES), 1) // GLA_DK)
    tri = tri_ref[...]
    e2 = e2_ref[...]
    btblk = btblk_ref[...]
    ones8 = jnp.ones((8, TT), BF16)
    seqs = range(nb)
    rsel = [pl.ds(b, TT, stride=nb) for b in seqs]

    def bd(x):
        xb = x.astype(BF16)
        return jnp.concatenate([xb * m for m in hmask], axis=0)

    def bd2(x):
        xb = x.astype(BF16)
        return jnp.concatenate([xb * m for m in hmask2], axis=0)

    def cat2(ref, b):
        return jnp.concatenate([ref[0, rsel[b], :], ref[1, rsel[b], :]], axis=1)

    def each(f, *lists):
        return [f(*xs) for xs in zip(*lists)]

    q = [cat2(q_s, b) for b in seqs]
    k = [cat2(k_s, b) for b in seqs]
    v = [cat2(v_s, b) for b in seqs]
    bgb = [bg_s[rsel[b], :] for b in seqs]
    gc_all = each(lambda x: _dot_mask_l(tri, x), bgb)
    both = each(lambda x, c: _dot_mask_r3(jnp.where(lane2 < GDN_H, x, c), e2), bgb, gc_all)
    beta = [x[:, 0:256] for x in both]
    gc = [x[:, 256:512] for x in both]
    gc_row = each(lambda g: _dot_mask_l(ones8, jnp.where(eye, g, 0.0))[0:1, :], gc)
    decay = each(lambda g, gr: jnp.exp(jnp.where(incl, g - gr, -jnp.inf)), gc, gc_row)
    egc = each(jnp.exp, gc)
    kb = each(lambda a, b_: a * b_, k, beta)
    bdk = each(bd, k)
    kq = each(lambda kb_, q_, m: _dot_nt(jnp.concatenate([kb_, q_], axis=0).astype(BF16), m), kb, q, bdk)
    amat = each(lambda x, d: jnp.where(strict, x[0:TT] * d, 0.0), kq, decay)
    qk = each(lambda x, d: jnp.where(incl, x[TT:] * d, 0.0), kq, decay)
    y = each(lambda a: -jnp.where(blk16m, a, 0.0), amat)
    n = y
    bdy = each(bd, y)
    for _ in range(3):
        y = each(lambda y_, m: _dot(y_.astype(BF16), m), y, bdy)
        bdy = each(bd, y)
        n = each(lambda n_, y_, m: n_ + y_ + _dot(n_.astype(BF16), m), n, y, bdy)
    t = each(lambda n_: jnp.where(eye, 1.0, 0.0) + n_, n)
    for lower in (each(lambda a: jnp.where(blk32m & ~blk16m, a, 0.0), amat),
                  each(lambda a: jnp.where(blk32m, 0.0, a), amat)):
        tl = each(lambda t_, l_: _dot(t_.astype(BF16), bd(l_)), t, lower)
        t = each(lambda t_, x: t_ - _dot(x.astype(BF16), bd(t_)), t, tl)
    uw = each(lambda t_, v_, b_, kb_, e: _dot(
        t_.astype(BF16), jnp.concatenate([bd(v_ * b_), bd(kb_ * e)], axis=1)), t, v, beta, kb, egc)
    s = [gs_ref[b] for b in seqs]
    ws = each(lambda x, q_, e, s_: _dot(
        jnp.concatenate([x[:, 256:512], q_ * e], axis=0).astype(BF16), s_.astype(BF16)), uw, q, egc, s)
    vn = each(lambda x, w_: x[:, 0:256] - w_[0:TT], uw, ws)
    o = each(lambda w_, a, v_: w_[TT:] + _dot(a.astype(BF16), bd(v_)), ws, qk, vn)
    for b in seqs:
        g_last = gc[b][TT - 1:TT, :]
        kd = k[b] * jnp.exp(g_last - gc[b])
        gs_ref[b] = s[b] * jnp.exp(g_last) + jnp.where(bdm, _mm_tn(kd, vn[b]), 0.0)
        for j in range(2):
            og_s[j, rsel[b], :] = o[b][:, j * LANES:(j + 1) * LANES]

    cum = [_dot_mask_l(btblk, gk_s[rsel[b], :]) for b in seqs]
    bc = [x[0:TT] for x in cum]
    bl = [x[TT:] for x in cum]
    qb = [q2_s[rsel[b], :] for b in seqs]
    kb2 = [k2_s[rsel[b], :] for b in seqs]
    vb = [cat2(v2_s, b) for b in seqs]
    qi = each(lambda a, c: (a * jnp.exp(c)).astype(BF16), qb, bc)
    ki = each(lambda a, c: a * jnp.exp(-c), kb2, bc)
    kst = each(lambda a, l_, c: (a * jnp.exp(l_ - c)).astype(BF16), kb2, bl, bc)
    ebl = each(jnp.exp, bl)
    sc = each(lambda a, c: jnp.where(blk_causal, _dot_nt(a, bd2(c)), 0.0), qi, ki)
    oh = each(lambda a, v_: _dot(a.astype(BF16), bd(v_)), sc, vb)
    st = [lst_ref[b] for b in seqs]
    vbb = each(lambda v_: v_.astype(BF16), vb)
    outs = [[] for _ in seqs]
    for c in range(TT // GLA_CHUNK):
        rs = slice(c * GLA_CHUNK, (c + 1) * GLA_CHUNK)
        for b in seqs:
            outs[b].append(oh[b][rs] + _dot_nt(qi[b][rs], st[b].astype(BF16)))
        upd = [jnp.where(bdm2, _dot_tn(vbb[b][rs], kst[b][rs]), 0.0) for b in seqs]
        st = [st[b] * ebl[b][c * GLA_CHUNK:c * GLA_CHUNK + 1, :] + upd[b] for b in seqs]
    for b in seqs:
        lst_ref[b] = st[b]
        ol = jnp.concatenate(outs[b], axis=0)
        for j in range(2):
            ol_s[j, rsel[b], :] = ol[:, j * LANES:(j + 1) * LANES]

    og = jnp.concatenate([og_s[0], og_s[1]], axis=1)
    out_b = _head_norm_gate(og, gng_ref[...], proj_ref[:, C_BZ:C_BZ + 256], ones_blk)
    mix_ref[:, 512:768] = out_b.astype(BF16)
    ol = jnp.concatenate([ol_s[0], ol_s[1]], axis=1)
    out_c = _head_norm_gate(ol, lng_ref[...], proj_ref[:, C_CR:C_CR + 256], ones_blk)
    mix_ref[:, 768:1024] = out_c.astype(BF16)


def _mix_consts():
    r = np.arange(TT)
    tri = (r[:, None] >= r[None, :]).astype(np.float32)
    same = (r[:, None] // GLA_CHUNK) == (r[None, :] // GLA_CHUNK)
    bt16 = (tri.astype(bool) & same).astype(np.float32)
    blk16 = same.astype(np.float32)
    c = np.arange(256)
    ones_blk = ((c[:, None] // GDN_DV) == (c[None, :] // GDN_DV)).astype(np.float32)
    e2 = np.zeros((LANES, 512), np.float32)
    for h in range(GDN_H):
        e2[h, h * GDN_DV:(h + 1) * GDN_DV] = 1.0
        e2[GDN_H + h, 256 + h * GDN_DV:256 + (h + 1) * GDN_DV] = 1.0
    btblk = np.concatenate([bt16, blk16], axis=0)
    return (jnp.asarray(ones_blk, BF16), jnp.asarray(tri, BF16),
            jnp.asarray(e2, BF16), jnp.asarray(btblk, BF16))


def _mix_prompt_call(proj, mp):
    rows = TT * BATCH
    consts = _mix_consts()
    params = (mp["lcw"], mp["lcb"], mp["wa"], mp["ba"], mp["wx"], mp["bx"], mp["lam"],
              mp["gcw"], mp["gpar"], mp["gng"], mp["wg2"], mp["gbg"], mp["lng"]) + consts
    out_shape = [
        jax.ShapeDtypeStruct((NP, D_MODEL), BF16),
        jax.ShapeDtypeStruct((3 * BATCH, LRU_W), F32),
        jax.ShapeDtypeStruct((BATCH, LRU_W), F32),
        jax.ShapeDtypeStruct((3 * BATCH, GDN_CONV), F32),
        jax.ShapeDtypeStruct((BATCH, GDN_H * GDN_DK, GDN_H * GDN_DV), F32),
        jax.ShapeDtypeStruct((BATCH, GLA_H * GLA_DV, GLA_H * GLA_DK), F32),
    ]
    out_specs = [pl.BlockSpec((rows, D_MODEL), lambda i: (i, 0))] + [
        _full_spec(s.shape) for s in out_shape[1:]]
    scratch = [
        pltpu.VMEM((rows + 3 * BATCH, LRU_W), F32),
        pltpu.VMEM((rows + 3 * BATCH, GDN_CONV), F32),
        pltpu.VMEM((rows, LRU_W), F32), pltpu.VMEM((rows, LRU_W), F32), pltpu.VMEM((rows, LRU_W), F32),
        pltpu.VMEM((2, rows, LANES), F32), pltpu.VMEM((2, rows, LANES), F32), pltpu.VMEM((2, rows, LANES), F32),
        pltpu.VMEM((rows, LANES), F32), pltpu.VMEM((2, rows, LANES), F32),
        pltpu.VMEM((rows, LANES), F32), pltpu.VMEM((rows, LANES), F32), pltpu.VMEM((2, rows, LANES), F32),
        pltpu.VMEM((rows, LANES), F32), pltpu.VMEM((2, rows, LANES), F32),
    ]
    return pl.pallas_call(
        _mix_prompt_kernel,
        grid=(SEQ // TT,),
        in_specs=[pl.BlockSpec((rows, D_INP), lambda i: (i, 0))] + [_full_spec(p.shape) for p in params],
        out_specs=out_specs,
        out_shape=out_shape,
        scratch_shapes=scratch,
        compiler_params=pltpu.CompilerParams(
            dimension_semantics=("arbitrary",), vmem_limit_bytes=VMEM_LIMIT),
        name="mix_prompt",
    )(proj, *params)


def _pair_bcast(x, p, lo_mask):
    return jnp.where(lo_mask, x[:, 2 * p:2 * p + 1], x[:, 2 * p + 1:2 * p + 2])


def _fold_pairs(acc):
    return acc[:, 0:64] + acc[:, 64:128]


def _mix_sample_kernel(p0_ref, p1_ref, p2_ref, p3_ref, lconv_in, lh_in, gconv_in, gs_in, ls_in,
                       lcw_ref, lcb_ref, wa_ref, ba_ref, wx_ref, bx_ref, lam_ref,
                       gcw_ref, gpar_ref, gng_ref, wg2_ref, gbg_ref, lng_ref, ones_ref,
                       mix_ref, lconv_ref, lh_ref, gconv_ref, gs_ref, ls_ref):
    rows = DEC_SEQ * BB
    prefs = (p0_ref, p1_ref, p2_ref, p3_ref)
    ones_blk = ones_ref[...]

    def cols(c0, width):
        return [p[:, c0:c0 + width] for p in prefs]

    def conv(prev, xs, w_ref):
        ext = [prev[j] for j in range(CONV_W - 1)] + xs
        ys = []
        for t in range(DEC_SEQ):
            y = ext[t] * w_ref[0:1, :]
            for j in range(1, CONV_W):
                y = y + ext[t + j] * w_ref[j:j + 1, :]
            ys.append(y)
        return jnp.concatenate(ys, axis=0), ext[DEC_SEQ:]

    def rows_of(x, t):
        return x[t * BB:(t + 1) * BB]

    xa, tail = conv(lconv_in, cols(C_AX, LRU_W), lcw_ref)
    for j in range(CONV_W - 1):
        lconv_ref[j] = tail[j]
    xc = xa + lcb_ref[...]
    a, mult, ix = _lru_gates(xc, wa_ref, ba_ref, wx_ref, bx_ref, lam_ref)
    u = mult * ix
    h = lh_in[...]
    hs = []
    for t in range(DEC_SEQ):
        h = rows_of(a, t) * h + rows_of(u, t)
        hs.append(h)
    lh_ref[...] = h
    ag = jnp.concatenate(cols(C_AG, LRU_W), axis=0)
    mix_a = jnp.concatenate(hs, axis=0) * _gelu(ag)

    small = jnp.concatenate(cols(C_SM, LANES), axis=0)
    qkv, tail = conv(gconv_in, cols(C_QKV, GDN_CONV), gcw_ref)
    for j in range(CONV_W - 1):
        gconv_ref[j] = tail[j]
    q, k, v, bg = _gdn_prep(qkv, small, gpar_ref, ones_blk)
    lo_mask = lax.broadcasted_iota(jnp.int32, (BB, LANES), 1) < 64
    gs_ref[...] = gs_in[...]
    hd = GDN_DK * GDN_DV
    o_heads = []
    for h in range(GDN_H):
        o_t = []
        for t in range(DEC_SEQ):
            rs = slice(t * BB, (t + 1) * BB)
            cs = slice(h * GDN_DK, (h + 1) * GDN_DK)
            eg = jnp.exp(bg[rs, GDN_H + h:GDN_H + h + 1])
            beta = bg[rs, h:h + 1]
            kt, qt, vt = k[rs, cs], q[rs, cs], v[rs, cs]
            kks = [_pair_bcast(kt, p, lo_mask) for p in range(GDN_DK // 2)]
            acc = jnp.zeros((BB, LANES), F32)
            for p in range(GDN_DK // 2):
                acc = acc + gs_ref[:, h * hd + p * LANES:h * hd + (p + 1) * LANES] * kks[p]
            vn = beta * (vt - eg * _fold_pairs(acc))
            vn2 = jnp.concatenate([vn, vn], axis=1)
            oacc = jnp.zeros((BB, LANES), F32)
            for p in range(GDN_DK // 2):
                sl = slice(h * hd + p * LANES, h * hd + (p + 1) * LANES)
                s = eg * gs_ref[:, sl] + kks[p] * vn2
                gs_ref[:, sl] = s
                oacc = oacc + s * _pair_bcast(qt, p, lo_mask)
            o_t.append(_fold_pairs(oacc))
        o_heads.append(jnp.concatenate(o_t, axis=0))
    o_b = jnp.concatenate(o_heads, axis=1)
    bz = jnp.concatenate(cols(C_BZ, 256), axis=0)
    mix_b = _head_norm_gate(o_b, gng_ref[...], bz, ones_blk)

    q2, k2, gk = _gla_prep(jnp.concatenate(cols(C_CQ, 128), axis=0),
                           jnp.concatenate(cols(C_CK, 128), axis=0), small, wg2_ref, gbg_ref)
    v2 = jnp.concatenate(cols(C_CV, 256), axis=0)
    ls_ref[...] = ls_in[...]
    hd2 = GLA_DK * GLA_DV
    o_heads = []
    for h in range(GLA_H):
        o_t = []
        for t in range(DEC_SEQ):
            rs = slice(t * BB, (t + 1) * BB)
            cs = slice(h * GLA_DK, (h + 1) * GLA_DK)
            al = jnp.exp(gk[rs, cs])
            kt, qt = k2[rs, cs], q2[rs, cs]
            vt = v2[rs, h * GLA_DV:(h + 1) * GLA_DV]
            v2x = jnp.concatenate([vt, vt], axis=1)
            oacc = jnp.zeros((BB, LANES), F32)
            for p in range(GLA_DK // 2):
                sl = slice(h * hd2 + p * LANES, h * hd2 + (p + 1) * LANES)
                s = _pair_bcast(al, p, lo_mask) * ls_ref[:, sl] + _pair_bcast(kt, p, lo_mask) * v2x
                ls_ref[:, sl] = s
                oacc = oacc + s * _pair_bcast(qt, p, lo_mask)
            o_t.append(_fold_pairs(oacc))
        o_heads.append(jnp.concatenate(o_t, axis=0))
    o_c = jnp.concatenate(o_heads, axis=1)
    cr = jnp.concatenate(cols(C_CR, 256), axis=0)
    mix_c = _head_norm_gate(o_c, lng_ref[...], cr, ones_blk)

    mix = jnp.concatenate([mix_a, mix_b, mix_c], axis=1).astype(BF16)
    for t in range(DEC_SEQ):
        mix_ref[t] = mix[t * BB:(t + 1) * BB]


def _mix_sample_call(proj, states, mp):
    lconv, lh, gconv, gs, ls = states
    ones_blk = _mix_consts()[0]
    params = (mp["lcw"], mp["lcb"], mp["wa"], mp["ba"], mp["wx"], mp["bx"], mp["lam"],
              mp["gcw"], mp["gpar"], mp["gng"], mp["wg2"], mp["gbg"], mp["lng"], ones_blk)
    nblk = DEC_BATCH // BB
    base = NP // BB

    def proj_spec(t):
        return pl.BlockSpec((BB, D_INP), lambda j: (base + t * nblk + j, 0))

    def bspec3(n, width):
        return pl.BlockSpec((n, BB, width), lambda j: (0, j, 0))

    def bspec2(width):
        return pl.BlockSpec((BB, width), lambda j: (j, 0))

    gdn_flat = GDN_H * GDN_DK * GDN_DV
    gla_flat = GLA_H * GLA_DK * GLA_DV
    state_specs = [bspec3(3, LRU_W), bspec2(LRU_W), bspec3(3, GDN_CONV), bspec2(gdn_flat), bspec2(gla_flat)]
    out_shape = [
        jax.ShapeDtypeStruct((DEC_SEQ, DEC_BATCH, D_MODEL), BF16),
        jax.ShapeDtypeStruct((3, DEC_BATCH, LRU_W), F32),
        jax.ShapeDtypeStruct((DEC_BATCH, LRU_W), F32),
        jax.ShapeDtypeStruct((3, DEC_BATCH, GDN_CONV), F32),
        jax.ShapeDtypeStruct((DEC_BATCH, gdn_flat), F32),
        jax.ShapeDtypeStruct((DEC_BATCH, gla_flat), F32),
    ]
    return pl.pallas_call(
        _mix_sample_kernel,
        grid=(nblk,),
        in_specs=[proj_spec(t) for t in range(DEC_SEQ)] + state_specs + [_full_spec(p.shape) for p in params],
        out_specs=[bspec3(DEC_SEQ, D_MODEL)] + state_specs,
        out_shape=out_shape,
        compiler_params=pltpu.CompilerParams(
            dimension_semantics=("arbitrary",), vmem_limit_bytes=VMEM_LIMIT),
        name="mix_sample",
    )(proj, proj, proj, proj, lconv, lh, gconv, gs, ls, *params)


def _out_kernel(mp_ref, ms_ref, x_ref, w_ref, g1_ref, n2_ref, sc_ref, sh_ref, rw_ref, rb_ref,
                x1_ref, h2_ref, ridx_ref, rprob_ref, tcnt_ref):
    step = pl.program_id(0)
    mix = jnp.where(step < NPT, mp_ref[...], ms_ref[...])
    x1 = _gate_res(x_ref[...], g1_ref[...], _dot(mix, w_ref[...]))
    x1_ref[...] = x1
    h2 = _modulate(_rms_rows(x1, n2_ref[...]), sc_ref[...], sh_ref[...])
    h2_ref[...] = h2
    logits = _dot3(h2, rw_ref[...]) + rb_ref[...]
    lane = lax.broadcasted_iota(jnp.int32, logits.shape, 1)
    cur = jnp.where(lane < N_EXP, logits, -jnp.inf)
    vals, idxs = [], []
    for _ in range(TOP_K):
        m = jnp.max(cur, axis=-1, keepdims=True)
        idx = jnp.min(jnp.where(cur == m, lane, LANES), axis=-1, keepdims=True)
        vals.append(m)
        idxs.append(idx)
        cur = jnp.where(lane == idx, -jnp.inf, cur)
    es = [jnp.exp(v - vals[0]) for v in vals]
    den = es[0] + es[1] + es[2] + es[3]
    ridx = jnp.zeros(logits.shape, jnp.int32)
    rprob = jnp.zeros(logits.shape, F32)
    for j in range(TOP_K):
        ridx = jnp.where(lane == j, idxs[j], ridx)
        rprob = jnp.where(lane == j, es[j] / den, rprob)
    ridx_ref[...] = ridx
    rprob_ref[...] = rprob
    onehot = jnp.zeros(logits.shape, F32)
    for idx in idxs:
        onehot = onehot + jnp.where(lane == idx, 1.0, 0.0)
    colsum = jnp.sum(onehot, axis=0, keepdims=True)
    aligned = jnp.floor((colsum + (PIECE - 1)) * (1.0 / PIECE)) * PIECE
    tcnt_ref[...] = jnp.concatenate([aligned, jnp.zeros((7, LANES), F32)], axis=0)


def _out_call(mix_p, mix_s, x, mod_l, w_out, norm_g, rw, rb):
    return pl.pallas_call(
        _out_kernel,
        grid=(NTILES,),
        in_specs=[
            pl.BlockSpec((TM, D_MODEL), lambda i: (jnp.minimum(i, NPT - 1), 0)),
            _full_spec((NS, D_MODEL)),
            _row_spec(D_MODEL),
            _full_spec((D_MODEL, D_MODEL)),
            _mod_spec(2),
            _full_spec((1, D_MODEL)),
            _mod_spec(4),
            _mod_spec(3),
            _full_spec((D_MODEL, LANES)),
            _full_spec((1, LANES)),
        ],
        out_specs=[_row_spec(D_MODEL), _row_spec(D_MODEL), _row_spec(LANES), _row_spec(LANES),
                   pl.BlockSpec((8, LANES), lambda i: (i, 0))],
        out_shape=[
            jax.ShapeDtypeStruct((NTOK, D_MODEL), F32),
            jax.ShapeDtypeStruct((NTOK, D_MODEL), F32),
            jax.ShapeDtypeStruct((NTOK, LANES), jnp.int32),
            jax.ShapeDtypeStruct((NTOK, LANES), F32),
            jax.ShapeDtypeStruct((NTILES * 8, LANES), F32),
        ],
        compiler_params=pltpu.CompilerParams(
            dimension_semantics=("arbitrary",), vmem_limit_bytes=VMEM_LIMIT),
        name="out_proj_router",
    )(mix_p, mix_s, x, w_out, mod_l, norm_g, mod_l, mod_l, rw, rb)


def _excl_lane_cumsum(row):
    r = lax.broadcasted_iota(jnp.int32, (LANES, LANES), 0)
    c = lax.broadcasted_iota(jnp.int32, (LANES, LANES), 1)
    before = jnp.where(r < c, 1.0, 0.0).astype(BF16)
    return _dot_mask_r3(jnp.broadcast_to(row, (8, LANES)), before)[0:1]


def _route_kernel(ridx_ref, tcnt_ref, total_ref, pos_ref, post_ref, tab_ref, meta_ref, cnt_s):
    i = pl.program_id(0)
    lane = lax.broadcasted_iota(jnp.int32, (TD, LANES), 1)
    ridx = ridx_ref[...]
    hits = [lane == ridx[:, k:k + 1] for k in range(TOP_K)]
    onehot = jnp.zeros((TD, LANES), F32)
    for hit in hits:
        onehot = onehot + jnp.where(hit, 1.0, 0.0)
    aligned = tcnt_ref[0:1, :]

    @pl.when(i == 0)
    def _():
        total = total_ref[...]
        padded = jnp.floor((total + (TE - 1)) * (1.0 / TE)) * TE
        pstart = _excl_lane_cumsum(padded)
        meta_ref[...] = jnp.concatenate(
            [total, pstart, pstart + padded, jnp.zeros((5, LANES), F32)], axis=0)
        cnt_s[...] = jnp.zeros_like(cnt_s)

    r = lax.broadcasted_iota(jnp.int32, (TD, TD), 0)
    c = lax.broadcasted_iota(jnp.int32, (TD, TD), 1)
    earlier = jnp.where(c < r, 1.0, 0.0).astype(BF16)
    pos = _excl_lane_cumsum(aligned) + _dot(earlier, onehot.astype(BF16))
    posk = jnp.zeros((TD, LANES), F32)
    for k, hit in enumerate(hits):
        d = jnp.sum(jnp.where(hit, pos, 0.0), axis=1, keepdims=True)
        posk = jnp.where(lane == k, d, posk)
    pos_ref[...] = posk.astype(jnp.int32)
    post_ref[...] = posk.T[0:8].astype(jnp.int32)
    pieces = aligned * (1.0 / PIECE)
    npieces = jnp.broadcast_to(jnp.sum(pieces, axis=1, keepdims=True), (1, LANES))
    tab_ref[...] = jnp.concatenate(
        [pieces, cnt_s[...], npieces, jnp.zeros((5, LANES), F32)], axis=0).astype(jnp.int32)
    cnt_s[...] += aligned


def _route_call(ridx, tcnt):
    nblk = NTOK // TD
    total = jnp.sum(tcnt.reshape(nblk, 8, LANES)[:, 0], axis=0, keepdims=True)
    return pl.pallas_call(
        _route_kernel,
        grid=(nblk,),
        in_specs=[pl.BlockSpec((TD, LANES), lambda i: (i, 0)),
                  pl.BlockSpec((8, LANES), lambda i: (i, 0)),
                  _full_spec((1, LANES))],
        out_specs=[pl.BlockSpec((TD, LANES), lambda i: (i, 0)),
                   pl.BlockSpec((8, TD), lambda i: (i, 0)),
                   pl.BlockSpec((8, LANES), lambda i: (i, 0)),
                   pl.BlockSpec((8, LANES), lambda i: (0, 0))],
        out_shape=[jax.ShapeDtypeStruct((NTOK, LANES), jnp.int32),
                   jax.ShapeDtypeStruct((nblk * 8, TD), jnp.int32),
                   jax.ShapeDtypeStruct((nblk * 8, LANES), jnp.int32),
                   jax.ShapeDtypeStruct((8, LANES), F32)],
        scratch_shapes=[pltpu.VMEM((1, LANES), F32)],
        compiler_params=pltpu.CompilerParams(
            dimension_semantics=("arbitrary",), vmem_limit_bytes=VMEM_LIMIT),
        name="route",
    )(ridx, tcnt, total)


def _for_each_piece(i, ptab_ref, btab_ref, pstart_ref, fn):
    off = jnp.int32(0)
    for e in range(N_EXP):
        npieces = ptab_ref[i * N_EXP + e]
        dst0 = pstart_ref[e] + btab_ref[i * N_EXP + e]

        def body(j, carry, off=off, dst0=dst0):
            fn(pl.multiple_of(off + j * PIECE, PIECE), pl.multiple_of(dst0 + j * PIECE, PIECE))
            return carry

        lax.fori_loop(0, npieces, body, 0)
        off = off + npieces * PIECE


def _dispatch_kernel(qtab_ref, btab_ref, ttab_ref, pstart_ref, pend_ref, cnt_ref, post_ref, h_ref, xs_ref,
                     sbuf, zero_s, sems, semz):
    i = pl.program_id(0)
    n = pl.num_programs(0)
    slot = i % 2

    def piece(s):
        def copy(src, dst):
            return pltpu.make_async_copy(sbuf.at[s, pl.ds(src, PIECE)], xs_ref.at[pl.ds(dst, PIECE)],
                                         sems.at[s])
        return copy

    def start_pieces(tile, s):
        _for_each_piece(tile, qtab_ref, btab_ref, pstart_ref, lambda src, dst: piece(s)(src, dst).start())

    def wait_pieces(tile, s):
        def body(j, carry):
            piece(s)(0, 0).wait()
            return carry
        lax.fori_loop(0, ttab_ref[tile], body, 0)

    @pl.when(i == 0)
    def _():
        zero_s[...] = jnp.zeros_like(zero_s)

        def fill(start):
            return pltpu.make_async_copy(zero_s, xs_ref.at[pl.ds(pl.multiple_of(start, TE), TE)], semz)

        for e in range(N_EXP):
            @pl.when(cnt_ref[e] > 0)
            def _():
                fill(pend_ref[e] - TE).start()
        for e in range(N_EXP):
            @pl.when(cnt_ref[e] > 0)
            def _():
                fill(pend_ref[e] - TE).wait()

        def tail_fill(t, carry):
            fill(t * TE).start()
            fill(t * TE).wait()
            return carry

        lax.fori_loop(pend_ref[N_EXP - 1] // TE, NTE, tail_fill, 0)

    @pl.when(i >= 2)
    def _():
        wait_pieces(i - 2, slot)

    post = post_ref[...]
    prow = lax.broadcasted_iota(jnp.int32, (LP, TD), 0)
    sel = prow == post[0:1, :]
    for k in range(1, TOP_K):
        sel = sel | (prow == post[k:k + 1, :])
    onehot = jnp.where(sel, 1.0, 0.0).astype(BF16)
    sbuf[slot] = _pack_bf16_pairs(_dot(onehot, h_ref[...].astype(BF16)))
    start_pieces(i, slot)

    @pl.when(i == n - 1)
    def _():
        @pl.when(i >= 1)
        def _():
            wait_pieces(i - 1, 1 - slot)
        wait_pieces(i, slot)


def _dispatch_call(qtab, btab, ttab, pstart, pend, counts, post, h2):
    grid_spec = pltpu.PrefetchScalarGridSpec(
        num_scalar_prefetch=6,
        grid=(NTOK // TD,),
        in_specs=[
            pl.BlockSpec((8, TD), lambda i, *_: (i, 0)),
            pl.BlockSpec((TD, D_MODEL), lambda i, *_: (i, 0)),
        ],
        out_specs=pl.BlockSpec(memory_space=pl.ANY),
        scratch_shapes=[pltpu.VMEM((2, LP, HALF), U32), pltpu.VMEM((TE, HALF), U32),
                        pltpu.SemaphoreType.DMA((2,)), pltpu.SemaphoreType.DMA(())],
    )
    return pl.pallas_call(
        _dispatch_kernel,
        grid_spec=grid_spec,
        out_shape=jax.ShapeDtypeStruct((NROWS, HALF), U32),
        compiler_params=pltpu.CompilerParams(
            dimension_semantics=("arbitrary",), vmem_limit_bytes=VMEM_LIMIT),
        name="moe_dispatch",
    )(qtab, btab, ttab, pstart, pend, counts, post, h2)


def _combine_kernel(qtab_ref, btab_ref, ttab_ref, pstart_ref, pos_ref, prob_ref, ys_ref, o_ref, ybuf, sems):
    i = pl.program_id(0)
    n = pl.num_programs(0)
    slot = i % 2

    def piece(s):
        def copy(loc, src):
            return pltpu.make_async_copy(ys_ref.at[pl.ds(src, PIECE)], ybuf.at[s, pl.ds(loc, PIECE)],
                                         sems.at[s])
        return copy

    def start_pieces(tile, s):
        _for_each_piece(tile, qtab_ref, btab_ref, pstart_ref, lambda loc, src: piece(s)(loc, src).start())

    @pl.when(i == 0)
    def _():
        ybuf[...] = jnp.zeros_like(ybuf)
        start_pieces(0, 0)

    @pl.when(i + 1 < n)
    def _():
        start_pieces(i + 1, 1 - slot)

    def wait_piece(j, carry):
        piece(slot)(0, 0).wait()
        return carry

    lax.fori_loop(0, ttab_ref[i], wait_piece, 0)

    pos = pos_ref[...]
    prob = prob_ref[...]
    pcol = lax.broadcasted_iota(jnp.int32, (TD, LP), 1)
    w = jnp.zeros((TD, LP), F32)
    for k in range(TOP_K):
        w = w + jnp.where(pcol == pos[:, k:k + 1], prob[:, k:k + 1], 0.0)
    o_ref[...] = _dot(w.astype(BF16), _unpack_bf16_pairs(ybuf[slot]))


def _combine_call(qtab, btab, ttab, pstart, pos, rprob, ys):
    grid_spec = pltpu.PrefetchScalarGridSpec(
        num_scalar_prefetch=4,
        grid=(NTOK // TD,),
        in_specs=[
            pl.BlockSpec((TD, LANES), lambda i, *_: (i, 0)),
            pl.BlockSpec((TD, LANES), lambda i, *_: (i, 0)),
            pl.BlockSpec(memory_space=pl.ANY),
        ],
        out_specs=pl.BlockSpec((TD, D_MODEL), lambda i, *_: (i, 0)),
        scratch_shapes=[pltpu.VMEM((2, LP, HALF), U32), pltpu.SemaphoreType.DMA((2,))],
    )
    return pl.pallas_call(
        _combine_kernel,
        grid_spec=grid_spec,
        out_shape=jax.ShapeDtypeStruct((NTOK, D_MODEL), F32),
        compiler_params=pltpu.CompilerParams(
            dimension_semantics=("arbitrary",), vmem_limit_bytes=VMEM_LIMIT),
        name="moe_combine",
    )(qtab, btab, ttab, pstart, pos, rprob, ys)


def _ffn_kernel(cnt_ref, pend_ref, xs_ref, wg_ref, bg_ref, wu_ref, bu_ref, wd_ref, bd_ref,
                ys_ref, wg_s, wu_s, wd_s, xbuf, ybuf, sem_in, sem_out):
    e = pl.program_id(0)
    nrows = ((cnt_ref[e] + (TE - 1)) // TE) * TE
    row0 = pend_ref[e] - nrows
    nbig = nrows // TB
    nsmall = (nrows - nbig * TB) // TE

    wg_s[:, 0:D_MODEL] = wg_ref[...].astype(BF16)
    wu_s[:, 0:D_MODEL] = wu_ref[...].astype(BF16)
    wd_s[:, 0:D_MODEL] = wd_ref[...].astype(BF16)

    def run_tiles(nt, base, size):
        def rows(t):
            return pl.ds(pl.multiple_of(base + t * size, TE), size)

        def x_copy(t, slot):
            return pltpu.make_async_copy(xs_ref.at[rows(t)], xbuf.at[slot, pl.ds(0, size)], sem_in.at[slot])

        def y_copy(t, slot):
            return pltpu.make_async_copy(ybuf.at[slot, pl.ds(0, size)], ys_ref.at[rows(t)], sem_out.at[slot])

        @pl.when(nt > 0)
        def _():
            x_copy(0, 0).start()

        def tile(t, carry):
            slot = t % 2

            @pl.when(t + 1 < nt)
            def _():
                x_copy(t + 1, 1 - slot).start()

            x_copy(t, slot).wait()

            @pl.when(t >= 2)
            def _():
                y_copy(t - 2, slot).wait()

            x = _unpack_bf16_pairs(xbuf[slot, 0:size])
            y = bd_ref[...]
            for c in range(D_MODEL // FFC):
                cs = slice(c * FFC, (c + 1) * FFC)
                gate = jnp.minimum(_dot(x, wg_s[:, cs]) + bg_ref[:, cs], SW_LIMIT)
                up = jnp.clip(_dot(x, wu_s[:, cs]) + bu_ref[:, cs], -SW_LIMIT, SW_LIMIT)
                act = (up + 1.0) * gate * _sigmoid(SW_ALPHA * gate)
                y = y + _dot(act.astype(BF16), wd_s[cs, 0:D_MODEL])
            ybuf[slot, 0:size] = _pack_bf16_pairs(y)
            y_copy(t, slot).start()
            return carry

        lax.fori_loop(0, nt, tile, 0)

        @pl.when(nt >= 2)
        def _():
            y_copy(nt - 2, nt % 2).wait()

        @pl.when(nt >= 1)
        def _():
            y_copy(nt - 1, (nt - 1) % 2).wait()

    run_tiles(nbig, row0, TB)
    run_tiles(nsmall, row0 + nbig * TB, TE)

    @pl.when(e == N_EXP - 1)
    def _():
        ybuf[0, 0:TE] = jnp.zeros((TE, HALF), U32)

        def tail_fill(t, carry):
            cp = pltpu.make_async_copy(
                ybuf.at[0, pl.ds(0, TE)], ys_ref.at[pl.ds(pl.multiple_of(t * TE, TE), TE)], sem_out.at[0])
            cp.start()
            cp.wait()
            return carry

        lax.fori_loop(pend_ref[N_EXP - 1] // TE, NTE, tail_fill, 0)


def _ffn_call(layer, counts, pend, xs, wg, bg, wu, bu, wd, bd):
    wspec = pl.BlockSpec((None, None, D_MODEL, D_MODEL), lambda e, c, p: (layer, e, 0, 0))
    bspec = pl.BlockSpec((None, None, 1, D_MODEL), lambda e, c, p: (layer, e, 0, 0))
    grid_spec = pltpu.PrefetchScalarGridSpec(
        num_scalar_prefetch=2,
        grid=(N_EXP,),
        in_specs=[pl.BlockSpec(memory_space=pl.ANY), wspec, bspec, wspec, bspec, wspec, bspec],
        out_specs=pl.BlockSpec(memory_space=pl.ANY),
        scratch_shapes=[pltpu.VMEM((D_MODEL, D_MODEL + LANES), BF16)] * 3 + [
            pltpu.VMEM((2, TB, HALF), U32), pltpu.VMEM((2, TB, HALF), U32),
            pltpu.SemaphoreType.DMA((2,)), pltpu.SemaphoreType.DMA((2,))],
    )
    b4 = lambda b: b.reshape(DEPTH, N_EXP, 1, D_MODEL)
    return pl.pallas_call(
        _ffn_kernel,
        grid_spec=grid_spec,
        out_shape=jax.ShapeDtypeStruct((NROWS, HALF), U32),
        compiler_params=pltpu.CompilerParams(
            dimension_semantics=("arbitrary",), vmem_limit_bytes=VMEM_LIMIT),
        name="expert_ffn",
    )(counts, pend, xs, wg, b4(bg), wu, b4(bu), wd, b4(bd))


def _moe(layer, h2, ridx, rprob, tcnt, wg, bg, wu, bu, wd, bd):
    pos, post, tab, meta = _route_call(ridx, tcnt)
    counts = meta[0, :N_EXP].astype(jnp.int32)
    pstart = meta[1, :N_EXP].astype(jnp.int32)
    pend = meta[2, :N_EXP].astype(jnp.int32)
    tab = tab.reshape(NTOK // TD, 8, LANES)
    qtab = tab[:, 0, :N_EXP].reshape(-1)
    btab = tab[:, 1, :N_EXP].reshape(-1)
    ttab = tab[:, 2, 0]
    xs = _dispatch_call(qtab, btab, ttab, pstart, pend, counts, post, h2)
    ys = _ffn_call(layer, counts, pend, xs, wg, bg, wu, bu, wd, bd)
    return _combine_call(qtab, btab, ttab, pstart, pos, rprob, ys)


def _final_kernel(x1_ref, moe_ref, g2_ref, ng_ref, y_ref):
    x = _gate_res(x1_ref[...], g2_ref[...], moe_ref[...])
    y_ref[...] = _rms_rows(x, ng_ref[...])


def _final_call(x1, moe, mod_l, norm_g):
    return pl.pallas_call(
        _final_kernel,
        grid=(NTILES,),
        in_specs=[_row_spec(D_MODEL), _row_spec(D_MODEL), _mod_spec(5), _full_spec((1, D_MODEL))],
        out_specs=_row_spec(D_MODEL),
        out_shape=jax.ShapeDtypeStruct((NTOK, D_MODEL), F32),
        compiler_params=pltpu.CompilerParams(
            dimension_semantics=("arbitrary",), vmem_limit_bytes=VMEM_LIMIT),
        name="final_norm",
    )(x1, moe, mod_l, norm_g)


def _block_diag(w):
    n, d, e = w.shape
    eye = jnp.eye(n, dtype=w.dtype)
    return (eye[:, None, :, None] * w[:, :, None, :]).reshape(n * d, n * e)


def _pad_lanes(v, offset):
    out = jnp.zeros((1, LANES), F32)
    return out.at[0, offset:offset + v.shape[0]].set(v)


def _mixer_params(l, w_in, lru_conv_w, lru_conv_b, lru_wa, lru_ba, lru_wx, lru_bx, lru_lambda,
                  gdn_conv_w, gdn_a_log, gdn_dt_bias, gdn_norm_g, gla_wg2, gla_bg, gla_norm_g):
    w = w_in[l]
    w_in_r = jnp.concatenate(
        [w[:, 0:2048], w[:, 2056:2824], w[:, 2048:2056], w[:, 2824:2840],
         jnp.zeros((D_MODEL, D_INP - 2840), F32)], axis=1).astype(BF16)
    row = lambda v: v.reshape(1, -1)
    mp = dict(
        lcw=lru_conv_w[l], lcb=row(lru_conv_b[l]),
        wa=_block_diag(lru_wa[l]).astype(BF16), ba=row(lru_ba[l]),
        wx=_block_diag(lru_wx[l]).astype(BF16), bx=row(lru_bx[l]),
        lam=row(lru_lambda[l]),
        gcw=gdn_conv_w[l],
        gpar=jnp.concatenate([_pad_lanes(gdn_a_log[l], GDN_H), _pad_lanes(gdn_dt_bias[l], GDN_H)], axis=0),
        gng=row(jnp.tile(gdn_norm_g[l], GDN_H)),
        wg2=jnp.zeros((LANES, LANES), F32).at[2 * GDN_H:2 * GDN_H + GLA_RANK].set(gla_wg2[l]),
        gbg=row(gla_bg[l]),
        lng=row(jnp.tile(gla_norm_g[l], GLA_H)),
    )
    return w_in_r, mp


def kernel(x_prompt, x_sample, state_lru_conv, state_lru_h, state_gdn_conv, state_gdn_S, state_gla_S, c_prompt, c_sample, ada_w, ada_b, norm1_g, norm2_g, w_in, lru_conv_w, lru_conv_b, lru_wa, lru_ba, lru_wx, lru_bx, lru_lambda, gdn_conv_w, gdn_a_log, gdn_dt_bias, gdn_norm_g, gla_wg2, gla_bg, gla_norm_g, w_out, router_w, router_b, exp_w_gate, exp_b_gate, exp_w_up, exp_b_up, exp_w_down, exp_b_down, final_norm_g):
    x = jnp.concatenate([
        x_prompt.transpose(1, 0, 2).reshape(NP, D_MODEL),
        x_sample.transpose(1, 0, 2).reshape(NS, D_MODEL)], axis=0)
    mod = _mod_call(jnp.concatenate([c_prompt, c_sample], axis=0), ada_w, ada_b)
    mod = jnp.stack([jnp.tile(mod[:, :BATCH], (1, MODB // BATCH, 1)), mod[:, BATCH:]], axis=1)

    p_states, s_states = [], []
    moe = None
    for l in range(DEPTH):
        w_in_r, mp = _mixer_params(l, w_in, lru_conv_w, lru_conv_b, lru_wa, lru_ba, lru_wx, lru_bx,
                                   lru_lambda, gdn_conv_w, gdn_a_log, gdn_dt_bias, gdn_norm_g,
                                   gla_wg2, gla_bg, gla_norm_g)
        x, proj = _in_call(x, moe, mod[l], norm1_g[l].reshape(1, D_MODEL), w_in_r,
                           mod[l - 1] if l else None)
        mix_p, p_lconv, p_lh, p_gconv, p_gs, p_lst = _mix_prompt_call(proj, mp)
        states = (state_lru_conv[l].transpose(1, 0, 2), state_lru_h[l],
                  state_gdn_conv[l].transpose(1, 0, 2),
                  state_gdn_S[l].reshape(DEC_BATCH, -1), state_gla_S[l].reshape(DEC_BATCH, -1))
        mix_s, s_lconv, s_lh, s_gconv, s_gs, s_ls = _mix_sample_call(proj, states, mp)
        rw = jnp.zeros((D_MODEL, LANES), F32).at[:, :N_EXP].set(router_w[l])
        rb = jnp.zeros((1, LANES), F32).at[0, :N_EXP].set(router_b[l])
        x, h2, ridx, rprob, tcnt = _out_call(mix_p, mix_s.reshape(NS, D_MODEL), x, mod[l],
                                             w_out[l].astype(BF16), norm2_g[l].reshape(1, D_MODEL), rw, rb)
        moe = _moe(l, h2, ridx, rprob, tcnt, exp_w_gate, exp_b_gate, exp_w_up, exp_b_up,
                   exp_w_down, exp_b_down)
        p_gs = jnp.stack([p_gs[:, h * GDN_DK:(h + 1) * GDN_DK, h * GDN_DV:(h + 1) * GDN_DV]
                          for h in range(GDN_H)], axis=1)
        p_lst = jnp.stack([p_lst[:, h * GLA_DV:(h + 1) * GLA_DV, h * GLA_DK:(h + 1) * GLA_DK]
                           for h in range(GLA_H)], axis=1)
        p_states.append((p_lconv.reshape(3, BATCH, LRU_W).transpose(1, 0, 2), p_lh,
                         p_gconv.reshape(3, BATCH, GDN_CONV).transpose(1, 0, 2), p_gs,
                         p_lst.transpose(0, 1, 3, 2)))
        s_states.append((s_lconv.transpose(1, 0, 2), s_lh, s_gconv.transpose(1, 0, 2),
                         s_gs.reshape(DEC_BATCH, GDN_H, GDN_DK, GDN_DV),
                         s_ls.reshape(DEC_BATCH, GLA_H, GLA_DK, GLA_DV)))
    y = _final_call(x, moe, mod[DEPTH - 1], final_norm_g.reshape(1, D_MODEL))
    y_prompt = y[:NP].reshape(SEQ, BATCH, D_MODEL).transpose(1, 0, 2)
    y_sample = y[NP:].reshape(DEC_SEQ, DEC_BATCH, D_MODEL).transpose(1, 0, 2)
    ps = [jnp.stack([s[j] for s in p_states]) for j in range(5)]
    ss = [jnp.stack([s[j] for s in s_states]) for j in range(5)]
    return (y_prompt, y_sample, *ps, *ss)
```

```python
import functools

import numpy as np
import jax
import jax.numpy as jnp
from jax import lax
from jax.experimental import pallas as pl
from jax.experimental.pallas import tpu as pltpu

F32 = jnp.float32
BF16 = jnp.bfloat16

D_MODEL = 1024
BATCH = 8
SEQ = 2048
DEPTH = 2
DEC_BATCH = 128
DEC_SEQ = 4
CONV_W = 4
LRU_W = 512
LRU_BLOCKS = 8
LRU_C = 8.0
GDN_H = 4
GDN_DK = 64
GDN_DV = 64
GDN_CONV = GDN_H * (2 * GDN_DK + GDN_DV)
GLA_H = 4
GLA_DK = 32
GLA_DV = 64
GLA_RANK = 16
GLA_TAU = 16.0
GLA_CHUNK = 16
N_EXP = 32
TOP_K = 4
SW_LIMIT = 7.0
SW_ALPHA = 1.702
EPS = 1e-6

NP = BATCH * SEQ
NS = DEC_BATCH * DEC_SEQ
NTOK = NP + NS
TM = 512
NPT = NP // TM
NTILES = NTOK // TM
LANES = 128
MODB = 128

C_AX, C_AG, C_QKV, C_BZ = 0, 512, 1024, 1792
C_CQ, C_CK, C_CV, C_CR, C_SM = 2048, 2176, 2304, 2560, 2816
D_INP = 2944

TT = 64
BB = 32
TE = 256
TB = 768
FFC = 512
TD = TM
PIECE = 8
LP = 2304
NA = NTOK * TOP_K
NROWS = -(-(NA + (NTOK // TD) * N_EXP * (PIECE - 1) + N_EXP * (TE - 1)) // TE) * TE
NTE = NROWS // TE

VMEM_LIMIT = 50 * 1024 * 1024


def _dot(a, b):
    return jnp.dot(a, b, preferred_element_type=F32)


def _dot_nt(a, b):
    return lax.dot_general(a, b, (((1,), (1,)), ((), ())), preferred_element_type=F32)


def _dot_tn(a, b):
    return lax.dot_general(a, b, (((0,), (0,)), ((), ())), preferred_element_type=F32)


def _mm_tn(a, b):
    return _dot_tn(a.astype(BF16), b.astype(BF16))


def _split3(x):
    x1 = x.astype(BF16)
    r = x - x1.astype(F32)
    x2 = r.astype(BF16)
    x3 = (r - x2.astype(F32)).astype(BF16)
    return x1, x2, x3


def _dot3(a, b):
    a1 = a.astype(BF16)
    a2 = (a - a1.astype(F32)).astype(BF16)
    b1 = b.astype(BF16)
    b2 = (b - b1.astype(F32)).astype(BF16)
    return _dot(a1, b1) + (_dot(a2, b1) + _dot(a1, b2))


def _dot_mask_l(mask, x):
    x1, x2, x3 = _split3(x)
    return _dot(mask, x1) + (_dot(mask, x2) + _dot(mask, x3))


def _dot_mask_r(x, mask):
    x1 = x.astype(BF16)
    x2 = (x - x1.astype(F32)).astype(BF16)
    return _dot(x1, mask) + _dot(x2, mask)


def _dot_mask_r3(x, mask):
    x1, x2, x3 = _split3(x)
    return _dot(x1, mask) + (_dot(x2, mask) + _dot(x3, mask))


def _sigmoid(x):
    return jax.nn.sigmoid(x)


def _silu(x):
    return x * jax.nn.sigmoid(x)


def _softplus(x):
    return jnp.maximum(x, 0.0) + jnp.log1p(jnp.exp(-jnp.abs(x)))


def _rms_rows(x, g):
    return x * lax.rsqrt(jnp.mean(x * x, axis=-1, keepdims=True) + EPS) * g


def _modulate(y, scale, shift):
    rows = y.shape[0]
    y3 = y.reshape(rows // MODB, MODB, y.shape[1])
    return (y3 * (1.0 + scale) + shift).reshape(y.shape)


def _gate_res(x, gate, y):
    rows = y.shape[0]
    y3 = y.reshape(rows // MODB, MODB, y.shape[1])
    return x + (gate * y3).reshape(y.shape)


def _mod_kernel(c_ref, w_ref, b_ref, o_ref):
    o_ref[...] = _dot3(_silu(c_ref[...]), w_ref[...]) + b_ref[...]


def _mod_call(c_all, ada_w, ada_b):
    tn = 768
    rows = c_all.shape[0]
    return pl.pallas_call(
        _mod_kernel,
        grid=(DEPTH, 6 * D_MODEL // tn),
        in_specs=[
            pl.BlockSpec((rows, D_MODEL), lambda l, j: (0, 0)),
            pl.BlockSpec((None, D_MODEL, tn), lambda l, j: (l, 0, j)),
            pl.BlockSpec((None, 1, tn), lambda l, j: (l, 0, j)),
        ],
        out_specs=pl.BlockSpec((None, rows, tn), lambda l, j: (l, 0, j)),
        out_shape=jax.ShapeDtypeStruct((DEPTH, rows, 6 * D_MODEL), F32),
        compiler_params=pltpu.CompilerParams(
            dimension_semantics=("arbitrary", "arbitrary"), vmem_limit_bytes=VMEM_LIMIT),
        name="adaln_mod",
    )(c_all, ada_w, ada_b.reshape(DEPTH, 1, 6 * D_MODEL))


def _mod_spec(chunk):
    return pl.BlockSpec((None, MODB, D_MODEL), lambda i: (i // NPT, 0, chunk))


def _row_spec(width):
    return pl.BlockSpec((TM, width), lambda i: (i, 0))


def _full_spec(shape):
    nd = len(shape)
    return pl.BlockSpec(shape, lambda i: (0,) * nd)


def _in_kernel(has_res, *refs):
    if has_res:
        (x1_ref, moe_ref, g2_ref, n1_ref, sc_ref, sh_ref, w_ref, x_ref, proj_ref) = refs
        x = _gate_res(x1_ref[...], g2_ref[...], moe_ref[...])
        x_ref[...] = x
    else:
        (x_ref, n1_ref, sc_ref, sh_ref, w_ref, proj_ref) = refs
        x = x_ref[...]
    h = _modulate(_rms_rows(x, n1_ref[...]), sc_ref[...], sh_ref[...])
    proj_ref[...] = _dot(h.astype(BF16), w_ref[...])


def _in_call(x, moe, mod_l, norm_g, w_in_r, prev_mod):
    has_res = moe is not None
    ins, specs = [x], [_row_spec(D_MODEL)]
    outs = [jax.ShapeDtypeStruct((NTOK, D_INP), F32)]
    out_specs = [_row_spec(D_INP)]
    if has_res:
        ins += [moe, prev_mod]
        specs += [_row_spec(D_MODEL), _mod_spec(5)]
        outs = [jax.ShapeDtypeStruct((NTOK, D_MODEL), F32)] + outs
        out_specs = [_row_spec(D_MODEL)] + out_specs
    ins += [norm_g, mod_l, mod_l, w_in_r]
    specs += [_full_spec((1, D_MODEL)), _mod_spec(1), _mod_spec(0), _full_spec((D_MODEL, D_INP))]
    res = pl.pallas_call(
        functools.partial(_in_kernel, has_res),
        grid=(NTILES,),
        in_specs=specs,
        out_specs=out_specs,
        out_shape=outs,
        compiler_params=pltpu.CompilerParams(
            dimension_semantics=("arbitrary",), vmem_limit_bytes=VMEM_LIMIT),
        name="in_proj",
    )(*ins)
    return res if has_res else (x, res[0])


def _lru_gates(xc, wa_ref, ba_ref, wx_ref, bx_ref, lam_ref):
    xb = xc.astype(BF16)
    r = _sigmoid(_dot(xb, wa_ref[...]) + ba_ref[...])
    i = _sigmoid(_dot(xb, wx_ref[...]) + bx_ref[...])
    log_a = -LRU_C * r * _softplus(-lam_ref[...])
    a = jnp.exp(log_a)
    mult = jnp.sqrt(1.0 - jnp.exp(2.0 * log_a))
    return a, mult, i * xc


def _gelu(x):
    return jax.nn.gelu(x, approximate=True)


def _gdn_prep(qkv, small, gpar_ref, ones_blk):
    qkv = _silu(qkv)
    q = qkv[:, 0:256]
    k = qkv[:, 256:512]
    v = qkv[:, 512:768]
    q = q * lax.rsqrt(_dot_mask_r(q * q, ones_blk) + EPS) * (GDN_DK ** -0.5)
    k = k * lax.rsqrt(_dot_mask_r(k * k, ones_blk) + EPS)
    beta = _sigmoid(small)
    g = -jnp.exp(gpar_ref[0:1, :]) * _softplus(small + gpar_ref[1:2, :])
    lane = lax.broadcasted_iota(jnp.int32, small.shape, 1)
    bg = jnp.where(lane < GDN_H, beta, g)
    return q, k, v, bg


def _gla_prep(proj_q, proj_k, small, wg2_ref, gbg_ref):
    pre = _dot3(small, wg2_ref[...]) + gbg_ref[...]
    gk = -_softplus(-pre) / GLA_TAU
    return proj_q * (GLA_DK ** -0.5), proj_k, gk


def _head_norm_gate(o, norm_g, gate_in, ones_blk):
    ms = _dot_mask_r(o * o, ones_blk) * (1.0 / GDN_DV)
    return o * lax.rsqrt(ms + EPS) * norm_g * _silu(gate_in)


def _mix_prompt_kernel(proj_ref, lcw_ref, lcb_ref, wa_ref, ba_ref, wx_ref, bx_ref, lam_ref,
                       gcw_ref, gpar_ref, gng_ref, wg2_ref, gbg_ref, lng_ref,
                       ones_ref, tri_ref, e2_ref, btblk_ref,
                       mix_ref, lconv_ref, lh_ref, gconv_ref, gs_ref, lst_ref,
                       exta, extb, a_s, u_s, hs_s, q_s, k_s, v_s, bg_s, og_s,
                       q2_s, k2_s, v2_s, gk_s, ol_s):
    nb = BATCH
    rows = TT * nb
    step = pl.program_id(0)

    @pl.when(step == 0)
    def _():
        exta[0:3 * nb, :] = jnp.zeros((3 * nb, LRU_W), F32)
        extb[0:3 * nb, :] = jnp.zeros((3 * nb, GDN_CONV), F32)
        lh_ref[...] = jnp.zeros_like(lh_ref)
        gs_ref[...] = jnp.zeros_like(gs_ref)
        lst_ref[...] = jnp.zeros_like(lst_ref)

    def conv(ext, x, w_ref):
        ext[pl.ds(3 * nb, rows), :] = x
        y = ext[pl.ds(0, rows), :] * w_ref[0:1, :]
        for j in range(1, CONV_W):
            y = y + ext[pl.ds(j * nb, rows), :] * w_ref[j:j + 1, :]
        tail = ext[pl.ds(rows, 3 * nb), :]
        ext[pl.ds(0, 3 * nb), :] = tail
        return y, tail

    ones_blk = ones_ref[...]

    xa, tail = conv(exta, proj_ref[:, C_AX:C_AX + LRU_W], lcw_ref)
    lconv_ref[...] = tail
    xc = xa + lcb_ref[...]
    a, mult, ix = _lru_gates(xc, wa_ref, ba_ref, wx_ref, bx_ref, lam_ref)
    rid = lax.broadcasted_iota(jnp.int32, (rows, LRU_W), 0)
    mult = jnp.where((rid < nb) & (step == 0), 1.0, mult)
    a_s[...] = a
    u_s[...] = mult * ix

    def scan_body(t, h):
        off = pl.multiple_of(t * nb, nb)
        h = a_s[pl.ds(off, nb), :] * h + u_s[pl.ds(off, nb), :]
        hs_s[pl.ds(off, nb), :] = h
        return h

    lh_ref[...] = lax.fori_loop(0, TT, scan_body, lh_ref[...], unroll=8)
    mix_ref[:, 0:LRU_W] = (hs_s[...] * _gelu(proj_ref[:, C_AG:C_AG + LRU_W])).astype(BF16)

    small = proj_ref[:, C_SM:C_SM + LANES]
    qkv, tail = conv(extb, proj_ref[:, C_QKV:C_QKV + GDN_CONV], gcw_ref)
    gconv_ref[...] = tail
    q, k, v, bg = _gdn_prep(qkv, small, gpar_ref, ones_blk)
    for j in range(2):
        ls = slice(j * LANES, (j + 1) * LANES)
        q_s[j] = q[:, ls]
        k_s[j] = k[:, ls]
        v_s[j] = v[:, ls]
    bg_s[...] = bg

    q2, k2, gk = _gla_prep(proj_ref[:, C_CQ:C_CQ + 128], proj_ref[:, C_CK:C_CK + 128],
                           small, wg2_ref, gbg_ref)
    q2_s[...] = q2
    k2_s[...] = k2
    gk_s[...] = gk
    for j in range(2):
        v2_s[j] = proj_ref[:, C_CV + j * LANES:C_CV + (j + 1) * LANES]

    ri = lax.broadcasted_iota(jnp.int32, (TT, 256), 0)
    lane = lax.broadcasted_iota(jnp.int32, (TT, 256), 1)
    cj = lane % TT
    incl = ri >= cj
    strict = ri > cj
    eye = ri == cj
    blk16m = (ri // 16) == (cj // 16)
    blk32m = (ri // 32) == (cj // 32)
    blk_causal = incl & blk16m
    hmask = [(lane // GDN_DV == h).astype(BF16) for h in range(GDN_H)]
    lane2 = lax.broadcasted_iota(jnp.int32, (TT, LANES), 1)
    hmask2 = [(lane2 // GLA_DK == h).astype(BF16) for h in range(GLA_H)]
    bdm = (lax.broadcasted_iota(jnp.int32, (256, 256), 0) // GDN_DK
           == lax.broadcasted_iota(jnp.int32, (256, 256), 1) // GDN_DV)
    bdm2 = (lax.broadcasted_iota(jnp.int32, (256, LANES), 0) // GLA_DV
            == lax.broadcasted_iota(jnp.int32, (256, LANES), 1) // GLA_DK)
    tri = tri_ref[...]
    e2 = e2_ref[...]
    btblk = btblk_ref[...]
    ones8 = jnp.ones((8, TT), BF16)
    seqs = range(nb)
    rsel = [pl.ds(b, TT, stride=nb) for b in seqs]

    def bd(x):
        xb = x.astype(BF16)
        return jnp.concatenate([xb * m for m in hmask], axis=0)

    def bd2(x):
        xb = x.astype(BF16)
        return jnp.concatenate([xb * m for m in hmask2], axis=0)

    def cat2(ref, b):
        return jnp.concatenate([ref[0, rsel[b], :], ref[1, rsel[b], :]], axis=1)

    def each(f, *lists):
        return [f(*xs) for xs in zip(*lists)]

    q = [cat2(q_s, b) for b in seqs]
    k = [cat2(k_s, b) for b in seqs]
    v = [cat2(v_s, b) for b in seqs]
    bgb = [bg_s[rsel[b], :] for b in seqs]
    gc_all = each(lambda x: _dot_mask_l(tri, x), bgb)
    both = each(lambda x, c: _dot_mask_r3(jnp.where(lane2 < GDN_H, x, c), e2), bgb, gc_all)
    beta = [x[:, 0:256] for x in both]
    gc = [x[:, 256:512] for x in both]
    gc_row = each(lambda g: _dot_mask_l(ones8, jnp.where(eye, g, 0.0))[0:1, :], gc)
    decay = each(lambda g, gr: jnp.exp(jnp.where(incl, g - gr, -jnp.inf)), gc, gc_row)
    egc = each(jnp.exp, gc)
    kb = each(lambda a, b_: a * b_, k, beta)
    bdk = each(bd, k)
    kq = each(lambda kb_, q_, m: _dot_nt(jnp.concatenate([kb_, q_], axis=0).astype(BF16), m), kb, q, bdk)
    amat = each(lambda x, d: jnp.where(strict, x[0:TT] * d, 0.0), kq, decay)
    qk = each(lambda x, d: jnp.where(incl, x[TT:] * d, 0.0), kq, decay)
    y = each(lambda a: -jnp.where(blk16m, a, 0.0), amat)
    n = y
    bdy = each(bd, y)
    for _ in range(3):
        y = each(lambda y_, m: _dot(y_.astype(BF16), m), y, bdy)
        bdy = each(bd, y)
        n = each(lambda n_, y_, m: n_ + y_ + _dot(n_.astype(BF16), m), n, y, bdy)
    t = each(lambda n_: jnp.where(eye, 1.0, 0.0) + n_, n)
    for lower in (each(lambda a: jnp.where(blk32m & ~blk16m, a, 0.0), amat),
                  each(lambda a: jnp.where(blk32m, 0.0, a), amat)):
        tl = each(lambda t_, l_: _dot(t_.astype(BF16), bd(l_)), t, lower)
        t = each(lambda t_, x: t_ - _dot(x.astype(BF16), bd(t_)), t, tl)
    uw = each(lambda t_, v_, b_, kb_, e: _dot(
        t_.astype(BF16), jnp.concatenate([bd(v_ * b_), bd(kb_ * e)], axis=1)), t, v, beta, kb, egc)
    s = [gs_ref[b] for b in seqs]
    ws = each(lambda x, q_, e, s_: _dot(
        jnp.concatenate([x[:, 256:512], q_ * e], axis=0).astype(BF16), s_.astype(BF16)), uw, q, egc, s)
    vn = each(lambda x, w_: x[:, 0:256] - w_[0:TT], uw, ws)
    o = each(lambda w_, a, v_: w_[TT:] + _dot(a.astype(BF16), bd(v_)), ws, qk, vn)
    for b in seqs:
        g_last = gc[b][TT - 1:TT, :]
        kd = k[b] * jnp.exp(g_last - gc[b])
        gs_ref[b] = s[b] * jnp.exp(g_last) + jnp.where(bdm, _mm_tn(kd, vn[b]), 0.0)
        for j in range(2):
            og_s[j, rsel[b], :] = o[b][:, j * LANES:(j + 1) * LANES]

    cum = [_dot_mask_l(btblk, gk_s[rsel[b], :]) for b in seqs]
    bc = [x[0:TT] for x in cum]
    bl = [x[TT:] for x in cum]
    qb = [q2_s[rsel[b], :] for b in seqs]
    kb2 = [k2_s[rsel[b], :] for b in seqs]
    vb = [cat2(v2_s, b) for b in seqs]
    qi = each(lambda a, c: (a * jnp.exp(c)).astype(BF16), qb, bc)
    ki = each(lambda a, c: a * jnp.exp(-c), kb2, bc)
    kst = each(lambda a, l_, c: (a * jnp.exp(l_ - c)).astype(BF16), kb2, bl, bc)
    ebl = each(jnp.exp, bl)
    sc = each(lambda a, c: jnp.where(blk_causal, _dot_nt(a, bd2(c)), 0.0), qi, ki)
    oh = each(lambda a, v_: _dot(a.astype(BF16), bd(v_)), sc, vb)
    st = [lst_ref[b] for b in seqs]
    vbb = each(lambda v_: v_.astype(BF16), vb)
    outs = [[] for _ in seqs]
    for c in range(TT // GLA_CHUNK):
        rs = slice(c * GLA_CHUNK, (c + 1) * GLA_CHUNK)
        for b in seqs:
            outs[b].append(oh[b][rs] + _dot_nt(qi[b][rs], st[b].astype(BF16)))
        upd = [jnp.where(bdm2, _dot_tn(vbb[b][rs], kst[b][rs]), 0.0) for b in seqs]
        st = [st[b] * ebl[b][c * GLA_CHUNK:c * GLA_CHUNK + 1, :] + upd[b] for b in seqs]
    for b in seqs:
        lst_ref[b] = st[b]
        ol = jnp.concatenate(outs[b], axis=0)
        for j in range(2):
            ol_s[j, rsel[b], :] = ol[:, j * LANES:(j + 1) * LANES]

    og = jnp.concatenate([og_s[0], og_s[1]], axis=1)
    out_b = _head_norm_gate(og, gng_ref[...], proj_ref[:, C_BZ:C_BZ + 256], ones_blk)
    mix_ref[:, 512:768] = out_b.astype(BF16)
    ol = jnp.concatenate([ol_s[0], ol_s[1]], axis=1)
    out_c = _head_norm_gate(ol, lng_ref[...], proj_ref[:, C_CR:C_CR + 256], ones_blk)
    mix_ref[:, 768:1024] = out_c.astype(BF16)


def _mix_consts():
    r = np.arange(TT)
    tri = (r[:, None] >= r[None, :]).astype(np.float32)
    same = (r[:, None] // GLA_CHUNK) == (r[None, :] // GLA_CHUNK)
    bt16 = (tri.astype(bool) & same).astype(np.float32)
    blk16 = same.astype(np.float32)
    c = np.arange(256)
    ones_blk = ((c[:, None] // GDN_DV) == (c[None, :] // GDN_DV)).astype(np.float32)
    e2 = np.zeros((LANES, 512), np.float32)
    for h in range(GDN_H):
        e2[h, h * GDN_DV:(h + 1) * GDN_DV] = 1.0
        e2[GDN_H + h, 256 + h * GDN_DV:256 + (h + 1) * GDN_DV] = 1.0
    btblk = np.concatenate([bt16, blk16], axis=0)
    return (jnp.asarray(ones_blk, BF16), jnp.asarray(tri, BF16),
            jnp.asarray(e2, BF16), jnp.asarray(btblk, BF16))


def _mix_prompt_call(proj, mp):
    rows = TT * BATCH
    consts = _mix_consts()
    params = (mp["lcw"], mp["lcb"], mp["wa"], mp["ba"], mp["wx"], mp["bx"], mp["lam"],
              mp["gcw"], mp["gpar"], mp["gng"], mp["wg2"], mp["gbg"], mp["lng"]) + consts
    out_shape = [
        jax.ShapeDtypeStruct((NP, D_MODEL), BF16),
        jax.ShapeDtypeStruct((3 * BATCH, LRU_W), F32),
        jax.ShapeDtypeStruct((BATCH, LRU_W), F32),
        jax.ShapeDtypeStruct((3 * BATCH, GDN_CONV), F32),
        jax.ShapeDtypeStruct((BATCH, GDN_H * GDN_DK, GDN_H * GDN_DV), F32),
        jax.ShapeDtypeStruct((BATCH, GLA_H * GLA_DV, GLA_H * GLA_DK), F32),
    ]
    out_specs = [pl.BlockSpec((rows, D_MODEL), lambda i: (i, 0))] + [
        _full_spec(s.shape) for s in out_shape[1:]]
    scratch = [
        pltpu.VMEM((rows + 3 * BATCH, LRU_W), F32),
        pltpu.VMEM((rows + 3 * BATCH, GDN_CONV), F32),
        pltpu.VMEM((rows, LRU_W), F32), pltpu.VMEM((rows, LRU_W), F32), pltpu.VMEM((rows, LRU_W), F32),
        pltpu.VMEM((2, rows, LANES), F32), pltpu.VMEM((2, rows, LANES), F32), pltpu.VMEM((2, rows, LANES), F32),
        pltpu.VMEM((rows, LANES), F32), pltpu.VMEM((2, rows, LANES), F32),
        pltpu.VMEM((rows, LANES), F32), pltpu.VMEM((rows, LANES), F32), pltpu.VMEM((2, rows, LANES), F32),
        pltpu.VMEM((rows, LANES), F32), pltpu.VMEM((2, rows, LANES), F32),
    ]
    return pl.pallas_call(
        _mix_prompt_kernel,
        grid=(SEQ // TT,),
        in_specs=[pl.BlockSpec((rows, D_INP), lambda i: (i, 0))] + [_full_spec(p.shape) for p in params],
        out_specs=out_specs,
        out_shape=out_shape,
        scratch_shapes=scratch,
        compiler_params=pltpu.CompilerParams(
            dimension_semantics=("arbitrary",), vmem_limit_bytes=VMEM_LIMIT),
        name="mix_prompt",
    )(proj, *params)


def _pair_bcast(x, p, lo_mask):
    return jnp.where(lo_mask, x[:, 2 * p:2 * p + 1], x[:, 2 * p + 1:2 * p + 2])


def _fold_pairs(acc):
    return acc[:, 0:64] + acc[:, 64:128]


def _mix_sample_kernel(p0_ref, p1_ref, p2_ref, p3_ref, lconv_in, lh_in, gconv_in, gs_in, ls_in,
                       lcw_ref, lcb_ref, wa_ref, ba_ref, wx_ref, bx_ref, lam_ref,
                       gcw_ref, gpar_ref, gng_ref, wg2_ref, gbg_ref, lng_ref, ones_ref,
                       mix_ref, lconv_ref, lh_ref, gconv_ref, gs_ref, ls_ref):
    rows = DEC_SEQ * BB
    prefs = (p0_ref, p1_ref, p2_ref, p3_ref)
    ones_blk = ones_ref[...]

    def cols(c0, width):
        return [p[:, c0:c0 + width] for p in prefs]

    def conv(prev, xs, w_ref):
        ext = [prev[j] for j in range(CONV_W - 1)] + xs
        ys = []
        for t in range(DEC_SEQ):
            y = ext[t] * w_ref[0:1, :]
            for j in range(1, CONV_W):
                y = y + ext[t + j] * w_ref[j:j + 1, :]
            ys.append(y)
        return jnp.concatenate(ys, axis=0), ext[DEC_SEQ:]

    def rows_of(x, t):
        return x[t * BB:(t + 1) * BB]

    xa, tail = conv(lconv_in, cols(C_AX, LRU_W), lcw_ref)
    for j in range(CONV_W - 1):
        lconv_ref[j] = tail[j]
    xc = xa + lcb_ref[...]
    a, mult, ix = _lru_gates(xc, wa_ref, ba_ref, wx_ref, bx_ref, lam_ref)
    u = mult * ix
    h = lh_in[...]
    hs = []
    for t in range(DEC_SEQ):
        h = rows_of(a, t) * h + rows_of(u, t)
        hs.append(h)
    lh_ref[...] = h
    ag = jnp.concatenate(cols(C_AG, LRU_W), axis=0)
    mix_a = jnp.concatenate(hs, axis=0) * _gelu(ag)

    small = jnp.concatenate(cols(C_SM, LANES), axis=0)
    qkv, tail = conv(gconv_in, cols(C_QKV, GDN_CONV), gcw_ref)
    for j in range(CONV_W - 1):
        gconv_ref[j] = tail[j]
    q, k, v, bg = _gdn_prep(qkv, small, gpar_ref, ones_blk)
    lo_mask = lax.broadcasted_iota(jnp.int32, (BB, LANES), 1) < 64
    gs_ref[...] = gs_in[...]
    hd = GDN_DK * GDN_DV
    o_heads = []
    for h in range(GDN_H):
        o_t = []
        for t in range(DEC_SEQ):
            rs = slice(t * BB, (t + 1) * BB)
            cs = slice(h * GDN_DK, (h + 1) * GDN_DK)
            eg = jnp.exp(bg[rs, GDN_H + h:GDN_H + h + 1])
            beta = bg[rs, h:h + 1]
            kt, qt, vt = k[rs, cs], q[rs, cs], v[rs, cs]
            kks = [_pair_bcast(kt, p, lo_mask) for p in range(GDN_DK // 2)]
            acc = jnp.zeros((BB, LANES), F32)
            for p in range(GDN_DK // 2):
                acc = acc + gs_ref[:, h * hd + p * LANES:h * hd + (p + 1) * LANES] * kks[p]
            vn = beta * (vt - eg * _fold_pairs(acc))
            vn2 = jnp.concatenate([vn, vn], axis=1)
            oacc = jnp.zeros((BB, LANES), F32)
            for p in range(GDN_DK // 2):
                sl = slice(h * hd + p * LANES, h * hd + (p + 1) * LANES)
                s = eg * gs_ref[:, sl] + kks[p] * vn2
                gs_ref[:, sl] = s
                oacc = oacc + s * _pair_bcast(qt, p, lo_mask)
            o_t.append(_fold_pairs(oacc))
        o_heads.append(jnp.concatenate(o_t, axis=0))
    o_b = jnp.concatenate(o_heads, axis=1)
    bz = jnp.concatenate(cols(C_BZ, 256), axis=0)
    mix_b = _head_norm_gate(o_b, gng_ref[...], bz, ones_blk)

    q2, k2, gk = _gla_prep(jnp.concatenate(cols(C_CQ, 128), axis=0),
                           jnp.concatenate(cols(C_CK, 128), axis=0), small, wg2_ref, gbg_ref)
    v2 = jnp.concatenate(cols(C_CV, 256), axis=0)
    ls_ref[...] = ls_in[...]
    hd2 = GLA_DK * GLA_DV
    o_heads = []
    for h in range(GLA_H):
        o_t = []
        for t in range(DEC_SEQ):
            rs = slice(t * BB, (t + 1) * BB)
            cs = slice(h * GLA_DK, (h + 1) * GLA_DK)
            al = jnp.exp(gk[rs, cs])
            kt, qt = k2[rs, cs], q2[rs, cs]
            vt = v2[rs, h * GLA_DV:(h + 1) * GLA_DV]
            v2x = jnp.concatenate([vt, vt], axis=1)
            oacc = jnp.zeros((BB, LANES), F32)
            for p in range(GLA_DK // 2):
                sl = slice(h * hd2 + p * LANES, h * hd2 + (p + 1) * LANES)
                s = _pair_bcast(al, p, lo_mask) * ls_ref[:, sl] + _pair_bcast(kt, p, lo_mask) * v2x
                ls_ref[:, sl] = s
                oacc = oacc + s * _pair_bcast(qt, p, lo_mask)
            o_t.append(_fold_pairs(oacc))
        o_heads.append(jnp.concatenate(o_t, axis=0))
    o_c = jnp.concatenate(o_heads, axis=1)
    cr = jnp.concatenate(cols(C_CR, 256), axis=0)
    mix_c = _head_norm_gate(o_c, lng_ref[...], cr, ones_blk)

    mix = jnp.concatenate([mix_a, mix_b, mix_c], axis=1).astype(BF16)
    for t in range(DEC_SEQ):
        mix_ref[t] = mix[t * BB:(t + 1) * BB]


def _mix_sample_call(proj, states, mp):
    lconv, lh, gconv, gs, ls = states
    ones_blk = _mix_consts()[0]
    params = (mp["lcw"], mp["lcb"], mp["wa"], mp["ba"], mp["wx"], mp["bx"], mp["lam"],
              mp["gcw"], mp["gpar"], mp["gng"], mp["wg2"], mp["gbg"], mp["lng"], ones_blk)
    nblk = DEC_BATCH // BB
    base = NP // BB

    def proj_spec(t):
        return pl.BlockSpec((BB, D_INP), lambda j: (base + t * nblk + j, 0))

    def bspec3(n, width):
        return pl.BlockSpec((n, BB, width), lambda j: (0, j, 0))

    def bspec2(width):
        return pl.BlockSpec((BB, width), lambda j: (j, 0))

    gdn_flat = GDN_H * GDN_DK * GDN_DV
    gla_flat = GLA_H * GLA_DK * GLA_DV
    state_specs = [bspec3(3, LRU_W), bspec2(LRU_W), bspec3(3, GDN_CONV), bspec2(gdn_flat), bspec2(gla_flat)]
    out_shape = [
        jax.ShapeDtypeStruct((DEC_SEQ, DEC_BATCH, D_MODEL), BF16),
        jax.ShapeDtypeStruct((3, DEC_BATCH, LRU_W), F32),
        jax.ShapeDtypeStruct((DEC_BATCH, LRU_W), F32),
        jax.ShapeDtypeStruct((3, DEC_BATCH, GDN_CONV), F32),
        jax.ShapeDtypeStruct((DEC_BATCH, gdn_flat), F32),
        jax.ShapeDtypeStruct((DEC_BATCH, gla_flat), F32),
    ]
    return pl.pallas_call(
        _mix_sample_kernel,
        grid=(nblk,),
        in_specs=[proj_spec(t) for t in range(DEC_SEQ)] + state_specs + [_full_spec(p.shape) for p in params],
        out_specs=[bspec3(DEC_SEQ, D_MODEL)] + state_specs,
        out_shape=out_shape,
        compiler_params=pltpu.CompilerParams(
            dimension_semantics=("arbitrary",), vmem_limit_bytes=VMEM_LIMIT),
        name="mix_sample",
    )(proj, proj, proj, proj, lconv, lh, gconv, gs, ls, *params)


def _out_kernel(mp_ref, ms_ref, x_ref, w_ref, g1_ref, n2_ref, sc_ref, sh_ref, rw_ref, rb_ref,
                x1_ref, h2_ref, ridx_ref, rprob_ref, tcnt_ref):
    step = pl.program_id(0)
    mix = jnp.where(step < NPT, mp_ref[...], ms_ref[...])
    x1 = _gate_res(x_ref[...], g1_ref[...], _dot(mix, w_ref[...]))
    x1_ref[...] = x1
    h2 = _modulate(_rms_rows(x1, n2_ref[...]), sc_ref[...], sh_ref[...])
    h2_ref[...] = h2
    logits = _dot3(h2, rw_ref[...]) + rb_ref[...]
    lane = lax.broadcasted_iota(jnp.int32, logits.shape, 1)
    cur = jnp.where(lane < N_EXP, logits, -jnp.inf)
    vals, idxs = [], []
    for _ in range(TOP_K):
        m = jnp.max(cur, axis=-1, keepdims=True)
        idx = jnp.min(jnp.where(cur == m, lane, LANES), axis=-1, keepdims=True)
        vals.append(m)
        idxs.append(idx)
        cur = jnp.where(lane == idx, -jnp.inf, cur)
    es = [jnp.exp(v - vals[0]) for v in vals]
    den = es[0] + es[1] + es[2] + es[3]
    ridx = jnp.zeros(logits.shape, jnp.int32)
    rprob = jnp.zeros(logits.shape, F32)
    for j in range(TOP_K):
        ridx = jnp.where(lane == j, idxs[j], ridx)
        rprob = jnp.where(lane == j, es[j] / den, rprob)
    ridx_ref[...] = ridx
    rprob_ref[...] = rprob
    onehot = jnp.zeros(logits.shape, F32)
    for idx in idxs:
        onehot = onehot + jnp.where(lane == idx, 1.0, 0.0)
    colsum = jnp.sum(onehot, axis=0, keepdims=True)
    aligned = jnp.floor((colsum + (PIECE - 1)) * (1.0 / PIECE)) * PIECE
    tcnt_ref[...] = jnp.concatenate([aligned, jnp.zeros((7, LANES), F32)], axis=0)


def _out_call(mix_p, mix_s, x, mod_l, w_out, norm_g, rw, rb):
    return pl.pallas_call(
        _out_kernel,
        grid=(NTILES,),
        in_specs=[
            pl.BlockSpec((TM, D_MODEL), lambda i: (jnp.minimum(i, NPT - 1), 0)),
            _full_spec((NS, D_MODEL)),
            _row_spec(D_MODEL),
            _full_spec((D_MODEL, D_MODEL)),
            _mod_spec(2),
            _full_spec((1, D_MODEL)),
            _mod_spec(4),
            _mod_spec(3),
            _full_spec((D_MODEL, LANES)),
            _full_spec((1, LANES)),
        ],
        out_specs=[_row_spec(D_MODEL), _row_spec(D_MODEL), _row_spec(LANES), _row_spec(LANES),
                   pl.BlockSpec((8, LANES), lambda i: (i, 0))],
        out_shape=[
            jax.ShapeDtypeStruct((NTOK, D_MODEL), F32),
            jax.ShapeDtypeStruct((NTOK, D_MODEL), F32),
            jax.ShapeDtypeStruct((NTOK, LANES), jnp.int32),
            jax.ShapeDtypeStruct((NTOK, LANES), F32),
            jax.ShapeDtypeStruct((NTILES * 8, LANES), F32),
        ],
        compiler_params=pltpu.CompilerParams(
            dimension_semantics=("arbitrary",), vmem_limit_bytes=VMEM_LIMIT),
        name="out_proj_router",
    )(mix_p, mix_s, x, w_out, mod_l, norm_g, mod_l, mod_l, rw, rb)


def _excl_lane_cumsum(row):
    r = lax.broadcasted_iota(jnp.int32, (LANES, LANES), 0)
    c = lax.broadcasted_iota(jnp.int32, (LANES, LANES), 1)
    before = jnp.where(r < c, 1.0, 0.0).astype(BF16)
    return _dot_mask_r3(jnp.broadcast_to(row, (8, LANES)), before)[0:1]


def _route_kernel(ridx_ref, tcnt_ref, total_ref, pos_ref, post_ref, tab_ref, meta_ref, cnt_s):
    i = pl.program_id(0)
    lane = lax.broadcasted_iota(jnp.int32, (TD, LANES), 1)
    ridx = ridx_ref[...]
    hits = [lane == ridx[:, k:k + 1] for k in range(TOP_K)]
    onehot = jnp.zeros((TD, LANES), F32)
    for hit in hits:
        onehot = onehot + jnp.where(hit, 1.0, 0.0)
    aligned = tcnt_ref[0:1, :]

    @pl.when(i == 0)
    def _():
        total = total_ref[...]
        padded = jnp.floor((total + (TE - 1)) * (1.0 / TE)) * TE
        pstart = _excl_lane_cumsum(padded)
        meta_ref[...] = jnp.concatenate(
            [total, pstart, pstart + padded, jnp.zeros((5, LANES), F32)], axis=0)
        cnt_s[...] = jnp.zeros_like(cnt_s)

    r = lax.broadcasted_iota(jnp.int32, (TD, TD), 0)
    c = lax.broadcasted_iota(jnp.int32, (TD, TD), 1)
    earlier = jnp.where(c < r, 1.0, 0.0).astype(BF16)
    pos = _excl_lane_cumsum(aligned) + _dot(earlier, onehot.astype(BF16))
    posk = jnp.zeros((TD, LANES), F32)
    for k, hit in enumerate(hits):
        d = jnp.sum(jnp.where(hit, pos, 0.0), axis=1, keepdims=True)
        posk = jnp.where(lane == k, d, posk)
    pos_ref[...] = posk.astype(jnp.int32)
    post_ref[...] = posk.T[0:8].astype(jnp.int32)
    pieces = aligned * (1.0 / PIECE)
    npieces = jnp.broadcast_to(jnp.sum(pieces, axis=1, keepdims=True), (1, LANES))
    tab_ref[...] = jnp.concatenate(
        [pieces, cnt_s[...], npieces, jnp.zeros((5, LANES), F32)], axis=0).astype(jnp.int32)
    cnt_s[...] += aligned


def _route_call(ridx, tcnt):
    nblk = NTOK // TD
    total = jnp.sum(tcnt.reshape(nblk, 8, LANES)[:, 0], axis=0, keepdims=True)
    return pl.pallas_call(
        _route_kernel,
        grid=(nblk,),
        in_specs=[pl.BlockSpec((TD, LANES), lambda i: (i, 0)),
                  pl.BlockSpec((8, LANES), lambda i: (i, 0)),
                  _full_spec((1, LANES))],
        out_specs=[pl.BlockSpec((TD, LANES), lambda i: (i, 0)),
                   pl.BlockSpec((8, TD), lambda i: (i, 0)),
                   pl.BlockSpec((8, LANES), lambda i: (i, 0)),
                   pl.BlockSpec((8, LANES), lambda i: (0, 0))],
        out_shape=[jax.ShapeDtypeStruct((NTOK, LANES), jnp.int32),
                   jax.ShapeDtypeStruct((nblk * 8, TD), jnp.int32),
                   jax.ShapeDtypeStruct((nblk * 8, LANES), jnp.int32),
                   jax.ShapeDtypeStruct((8, LANES), F32)],
        scratch_shapes=[pltpu.VMEM((1, LANES), F32)],
        compiler_params=pltpu.CompilerParams(
            dimension_semantics=("arbitrary",), vmem_limit_bytes=VMEM_LIMIT),
        name="route",
    )(ridx, tcnt, total)


def _for_each_piece(i, ptab_ref, btab_ref, pstart_ref, fn):
    off = jnp.int32(0)
    for e in range(N_EXP):
        npieces = ptab_ref[i * N_EXP + e]
        dst0 = pstart_ref[e] + btab_ref[i * N_EXP + e]

        def body(j, carry, off=off, dst0=dst0):
            fn(pl.multiple_of(off + j * PIECE, PIECE), pl.multiple_of(dst0 + j * PIECE, PIECE))
            return carry

        lax.fori_loop(0, npieces, body, 0)
        off = off + npieces * PIECE


def _dispatch_kernel(qtab_ref, btab_ref, ttab_ref, pstart_ref, pend_ref, cnt_ref, post_ref, h_ref, xs_ref,
                     sbuf, zero_s, sems, semz):
    i = pl.program_id(0)
    n = pl.num_programs(0)
    slot = i % 2

    def piece(s):
        def copy(src, dst):
            return pltpu.make_async_copy(sbuf.at[s, pl.ds(src, PIECE)], xs_ref.at[pl.ds(dst, PIECE)],
                                         sems.at[s])
        return copy

    def start_pieces(tile, s):
        _for_each_piece(tile, qtab_ref, btab_ref, pstart_ref, lambda src, dst: piece(s)(src, dst).start())

    def wait_pieces(tile, s):
        def body(j, carry):
            piece(s)(0, 0).wait()
            return carry
        lax.fori_loop(0, ttab_ref[tile], body, 0)

    @pl.when(i == 0)
    def _():
        zero_s[...] = jnp.zeros_like(zero_s)

        def fill(start):
            return pltpu.make_async_copy(zero_s, xs_ref.at[pl.ds(pl.multiple_of(start, TE), TE)], semz)

        for e in range(N_EXP):
            @pl.when(cnt_ref[e] > 0)
            def _():
                fill(pend_ref[e] - TE).start()
        for e in range(N_EXP):
            @pl.when(cnt_ref[e] > 0)
            def _():
                fill(pend_ref[e] - TE).wait()

        def tail_fill(t, carry):
            fill(t * TE).start()
            fill(t * TE).wait()
            return carry

        lax.fori_loop(pend_ref[N_EXP - 1] // TE, NTE, tail_fill, 0)

    @pl.when(i >= 2)
    def _():
        wait_pieces(i - 2, slot)

    post = post_ref[...]
    prow = lax.broadcasted_iota(jnp.int32, (LP, TD), 0)
    sel = prow == post[0:1, :]
    for k in range(1, TOP_K):
        sel = sel | (prow == post[k:k + 1, :])
    onehot = jnp.where(sel, 1.0, 0.0).astype(BF16)
    sbuf[slot] = _dot(onehot, h_ref[...].astype(BF16))
    start_pieces(i, slot)

    @pl.when(i == n - 1)
    def _():
        @pl.when(i >= 1)
        def _():
            wait_pieces(i - 1, 1 - slot)
        wait_pieces(i, slot)


def _dispatch_call(qtab, btab, ttab, pstart, pend, counts, post, h2):
    grid_spec = pltpu.PrefetchScalarGridSpec(
        num_scalar_prefetch=6,
        grid=(NTOK // TD,),
        in_specs=[
            pl.BlockSpec((8, TD), lambda i, *_: (i, 0)),
            pl.BlockSpec((TD, D_MODEL), lambda i, *_: (i, 0)),
        ],
        out_specs=pl.BlockSpec(memory_space=pl.ANY),
        scratch_shapes=[pltpu.VMEM((2, LP, D_MODEL), F32), pltpu.VMEM((TE, D_MODEL), F32),
                        pltpu.SemaphoreType.DMA((2,)), pltpu.SemaphoreType.DMA(())],
    )
    return pl.pallas_call(
        _dispatch_kernel,
        grid_spec=grid_spec,
        out_shape=jax.ShapeDtypeStruct((NROWS, D_MODEL), F32),
        compiler_params=pltpu.CompilerParams(
            dimension_semantics=("arbitrary",), vmem_limit_bytes=VMEM_LIMIT),
        name="moe_dispatch",
    )(qtab, btab, ttab, pstart, pend, counts, post, h2)


def _combine_kernel(qtab_ref, btab_ref, ttab_ref, pstart_ref, pos_ref, prob_ref, ys_ref, o_ref, ybuf, sems):
    i = pl.program_id(0)
    n = pl.num_programs(0)
    slot = i % 2

    def piece(s):
        def copy(loc, src):
            return pltpu.make_async_copy(ys_ref.at[pl.ds(src, PIECE)], ybuf.at[s, pl.ds(loc, PIECE)],
                                         sems.at[s])
        return copy

    def start_pieces(tile, s):
        _for_each_piece(tile, qtab_ref, btab_ref, pstart_ref, lambda loc, src: piece(s)(loc, src).start())

    @pl.when(i == 0)
    def _():
        ybuf[...] = jnp.zeros_like(ybuf)
        start_pieces(0, 0)

    @pl.when(i + 1 < n)
    def _():
        start_pieces(i + 1, 1 - slot)

    def wait_piece(j, carry):
        piece(slot)(0, 0).wait()
        return carry

    lax.fori_loop(0, ttab_ref[i], wait_piece, 0)

    pos = pos_ref[...]
    prob = prob_ref[...]
    pcol = lax.broadcasted_iota(jnp.int32, (TD, LP), 1)
    w = jnp.zeros((TD, LP), F32)
    for k in range(TOP_K):
        w = w + jnp.where(pcol == pos[:, k:k + 1], prob[:, k:k + 1], 0.0)
    o_ref[...] = _dot(w.astype(BF16), ybuf[slot].astype(BF16))


def _combine_call(qtab, btab, ttab, pstart, pos, rprob, ys):
    grid_spec = pltpu.PrefetchScalarGridSpec(
        num_scalar_prefetch=4,
        grid=(NTOK // TD,),
        in_specs=[
            pl.BlockSpec((TD, LANES), lambda i, *_: (i, 0)),
            pl.BlockSpec((TD, LANES), lambda i, *_: (i, 0)),
            pl.BlockSpec(memory_space=pl.ANY),
        ],
        out_specs=pl.BlockSpec((TD, D_MODEL), lambda i, *_: (i, 0)),
        scratch_shapes=[pltpu.VMEM((2, LP, D_MODEL), F32), pltpu.SemaphoreType.DMA((2,))],
    )
    return pl.pallas_call(
        _combine_kernel,
        grid_spec=grid_spec,
        out_shape=jax.ShapeDtypeStruct((NTOK, D_MODEL), F32),
        compiler_params=pltpu.CompilerParams(
            dimension_semantics=("arbitrary",), vmem_limit_bytes=VMEM_LIMIT),
        name="moe_combine",
    )(qtab, btab, ttab, pstart, pos, rprob, ys)


def _ffn_kernel(cnt_ref, pend_ref, xs_ref, wg_ref, bg_ref, wu_ref, bu_ref, wd_ref, bd_ref,
                ys_ref, wg_s, wu_s, wd_s, xbuf, ybuf, sem_in, sem_out):
    e = pl.program_id(0)
    nrows = ((cnt_ref[e] + (TE - 1)) // TE) * TE
    row0 = pend_ref[e] - nrows
    nbig = nrows // TB
    nsmall = (nrows - nbig * TB) // TE

    wg_s[...] = wg_ref[...].astype(BF16)
    wu_s[...] = wu_ref[...].astype(BF16)
    wd_s[...] = wd_ref[...].astype(BF16)

    def run_tiles(nt, base, size):
        def rows(t):
            return pl.ds(pl.multiple_of(base + t * size, TE), size)

        def x_copy(t, slot):
            return pltpu.make_async_copy(xs_ref.at[rows(t)], xbuf.at[slot, pl.ds(0, size)], sem_in.at[slot])

        def y_copy(t, slot):
            return pltpu.make_async_copy(ybuf.at[slot, pl.ds(0, size)], ys_ref.at[rows(t)], sem_out.at[slot])

        @pl.when(nt > 0)
        def _():
            x_copy(0, 0).start()

        def tile(t, carry):
            slot = t % 2

            @pl.when(t + 1 < nt)
            def _():
                x_copy(t + 1, 1 - slot).start()

            x_copy(t, slot).wait()

            @pl.when(t >= 2)
            def _():
                y_copy(t - 2, slot).wait()

            x = xbuf[slot, 0:size].astype(BF16)
            y = bd_ref[...]
            for c in range(D_MODEL // FFC):
                cs = slice(c * FFC, (c + 1) * FFC)
                gate = jnp.minimum(_dot(x, wg_s[:, cs]) + bg_ref[:, cs], SW_LIMIT)
                up = jnp.clip(_dot(x, wu_s[:, cs]) + bu_ref[:, cs], -SW_LIMIT, SW_LIMIT)
                act = (up + 1.0) * gate * _sigmoid(SW_ALPHA * gate)
                y = y + _dot(act.astype(BF16), wd_s[cs, :])
            ybuf[slot, 0:size] = y
            y_copy(t, slot).start()
            return carry

        lax.fori_loop(0, nt, tile, 0)

        @pl.when(nt >= 2)
        def _():
            y_copy(nt - 2, nt % 2).wait()

        @pl.when(nt >= 1)
        def _():
            y_copy(nt - 1, (nt - 1) % 2).wait()

    run_tiles(nbig, row0, TB)
    run_tiles(nsmall, row0 + nbig * TB, TE)

    @pl.when(e == N_EXP - 1)
    def _():
        ybuf[0, 0:TE] = jnp.zeros((TE, D_MODEL), F32)

        def tail_fill(t, carry):
            cp = pltpu.make_async_copy(
                ybuf.at[0, pl.ds(0, TE)], ys_ref.at[pl.ds(pl.multiple_of(t * TE, TE), TE)], sem_out.at[0])
            cp.start()
            cp.wait()
            return carry

        lax.fori_loop(pend_ref[N_EXP - 1] // TE, NTE, tail_fill, 0)


def _ffn_call(layer, counts, pend, xs, wg, bg, wu, bu, wd, bd):
    wspec = pl.BlockSpec((None, None, D_MODEL, D_MODEL), lambda e, c, p: (layer, e, 0, 0))
    bspec = pl.BlockSpec((None, None, 1, D_MODEL), lambda e, c, p: (layer, e, 0, 0))
    grid_spec = pltpu.PrefetchScalarGridSpec(
        num_scalar_prefetch=2,
        grid=(N_EXP,),
        in_specs=[pl.BlockSpec(memory_space=pl.ANY), wspec, bspec, wspec, bspec, wspec, bspec],
        out_specs=pl.BlockSpec(memory_space=pl.ANY),
        scratch_shapes=[pltpu.VMEM((D_MODEL, D_MODEL), BF16)] * 3 + [
            pltpu.VMEM((2, TB, D_MODEL), F32), pltpu.VMEM((2, TB, D_MODEL), F32),
            pltpu.SemaphoreType.DMA((2,)), pltpu.SemaphoreType.DMA((2,))],
    )
    b4 = lambda b: b.reshape(DEPTH, N_EXP, 1, D_MODEL)
    return pl.pallas_call(
        _ffn_kernel,
        grid_spec=grid_spec,
        out_shape=jax.ShapeDtypeStruct((NROWS, D_MODEL), F32),
        compiler_params=pltpu.CompilerParams(
            dimension_semantics=("arbitrary",), vmem_limit_bytes=VMEM_LIMIT),
        name="expert_ffn",
    )(counts, pend, xs, wg, b4(bg), wu, b4(bu), wd, b4(bd))


def _moe(layer, h2, ridx, rprob, tcnt, wg, bg, wu, bu, wd, bd):
    pos, post, tab, meta = _route_call(ridx, tcnt)
    counts = meta[0, :N_EXP].astype(jnp.int32)
    pstart = meta[1, :N_EXP].astype(jnp.int32)
    pend = meta[2, :N_EXP].astype(jnp.int32)
    tab = tab.reshape(NTOK // TD, 8, LANES)
    qtab = tab[:, 0, :N_EXP].reshape(-1)
    btab = tab[:, 1, :N_EXP].reshape(-1)
    ttab = tab[:, 2, 0]
    xs = _dispatch_call(qtab, btab, ttab, pstart, pend, counts, post, h2)
    ys = _ffn_call(layer, counts, pend, xs, wg, bg, wu, bu, wd, bd)
    return _combine_call(qtab, btab, ttab, pstart, pos, rprob, ys)


def _final_kernel(x1_ref, moe_ref, g2_ref, ng_ref, yp_ref, ys_ref, y_s):
    step = pl.program_id(0)
    x = _gate_res(x1_ref[...], g2_ref[...], moe_ref[...])
    y = _rms_rows(x, ng_ref[...])

    @pl.when(step < NPT)
    def _():
        nlb = D_MODEL // LANES
        for c in range(nlb):
            y_s[c] = y[:, c * LANES:(c + 1) * LANES]
        for b in range(BATCH):
            yp_ref[b] = jnp.concatenate(
                [y_s[c, pl.ds(b, TM // BATCH, stride=BATCH), :] for c in range(nlb)], axis=1)

    @pl.when(step == NPT)
    def _():
        ys_ref[...] = y


def _final_call(x1, moe, mod_l, norm_g):
    tt = TM // BATCH
    return pl.pallas_call(
        _final_kernel,
        grid=(NTILES,),
        in_specs=[_row_spec(D_MODEL), _row_spec(D_MODEL), _mod_spec(5), _full_spec((1, D_MODEL))],
        out_specs=[pl.BlockSpec((BATCH, tt, D_MODEL), lambda i: (0, jnp.minimum(i, NPT - 1), 0)),
                   _full_spec((NS, D_MODEL))],
        out_shape=[jax.ShapeDtypeStruct((BATCH, SEQ, D_MODEL), F32),
                   jax.ShapeDtypeStruct((NS, D_MODEL), F32)],
        scratch_shapes=[pltpu.VMEM((D_MODEL // LANES, TM, LANES), F32)],
        compiler_params=pltpu.CompilerParams(
            dimension_semantics=("arbitrary",), vmem_limit_bytes=VMEM_LIMIT),
        name="final_norm",
    )(x1, moe, mod_l, norm_g)


def _block_diag(w):
    n, d, e = w.shape
    eye = jnp.eye(n, dtype=w.dtype)
    return (eye[:, None, :, None] * w[:, :, None, :]).reshape(n * d, n * e)


def _pad_lanes(v, offset):
    out = jnp.zeros((1, LANES), F32)
    return out.at[0, offset:offset + v.shape[0]].set(v)


def _mixer_params(l, w_in, lru_conv_w, lru_conv_b, lru_wa, lru_ba, lru_wx, lru_bx, lru_lambda,
                  gdn_conv_w, gdn_a_log, gdn_dt_bias, gdn_norm_g, gla_wg2, gla_bg, gla_norm_g):
    w = w_in[l]
    w_in_r = jnp.concatenate(
        [w[:, 0:2048], w[:, 2056:2824], w[:, 2048:2056], w[:, 2824:2840],
         jnp.zeros((D_MODEL, D_INP - 2840), F32)], axis=1).astype(BF16)
    row = lambda v: v.reshape(1, -1)
    mp = dict(
        lcw=lru_conv_w[l], lcb=row(lru_conv_b[l]),
        wa=_block_diag(lru_wa[l]).astype(BF16), ba=row(lru_ba[l]),
        wx=_block_diag(lru_wx[l]).astype(BF16), bx=row(lru_bx[l]),
        lam=row(lru_lambda[l]),
        gcw=gdn_conv_w[l],
        gpar=jnp.concatenate([_pad_lanes(gdn_a_log[l], GDN_H), _pad_lanes(gdn_dt_bias[l], GDN_H)], axis=0),
        gng=row(jnp.tile(gdn_norm_g[l], GDN_H)),
        wg2=jnp.zeros((LANES, LANES), F32).at[2 * GDN_H:2 * GDN_H + GLA_RANK].set(gla_wg2[l]),
        gbg=row(gla_bg[l]),
        lng=row(jnp.tile(gla_norm_g[l], GLA_H)),
    )
    return w_in_r, mp


def kernel(x_prompt, x_sample, state_lru_conv, state_lru_h, state_gdn_conv, state_gdn_S, state_gla_S, c_prompt, c_sample, ada_w, ada_b, norm1_g, norm2_g, w_in, lru_conv_w, lru_conv_b, lru_wa, lru_ba, lru_wx, lru_bx, lru_lambda, gdn_conv_w, gdn_a_log, gdn_dt_bias, gdn_norm_g, gla_wg2, gla_bg, gla_norm_g, w_out, router_w, router_b, exp_w_gate, exp_b_gate, exp_w_up, exp_b_up, exp_w_down, exp_b_down, final_norm_g):
    x = jnp.concatenate([
        x_prompt.transpose(1, 0, 2).reshape(NP, D_MODEL),
        x_sample.transpose(1, 0, 2).reshape(NS, D_MODEL)], axis=0)
    mod = _mod_call(jnp.concatenate([c_prompt, c_sample], axis=0), ada_w, ada_b)
    mod = jnp.stack([jnp.tile(mod[:, :BATCH], (1, MODB // BATCH, 1)), mod[:, BATCH:]], axis=1)

    p_states, s_states = [], []
    moe = None
    for l in range(DEPTH):
        w_in_r, mp = _mixer_params(l, w_in, lru_conv_w, lru_conv_b, lru_wa, lru_ba, lru_wx, lru_bx,
                                   lru_lambda, gdn_conv_w, gdn_a_log, gdn_dt_bias, gdn_norm_g,
                                   gla_wg2, gla_bg, gla_norm_g)
        x, proj = _in_call(x, moe, mod[l], norm1_g[l].reshape(1, D_MODEL), w_in_r,
                           mod[l - 1] if l else None)
        mix_p, p_lconv, p_lh, p_gconv, p_gs, p_lst = _mix_prompt_call(proj, mp)
        states = (state_lru_conv[l].transpose(1, 0, 2), state_lru_h[l],
                  state_gdn_conv[l].transpose(1, 0, 2),
                  state_gdn_S[l].reshape(DEC_BATCH, -1), state_gla_S[l].reshape(DEC_BATCH, -1))
        mix_s, s_lconv, s_lh, s_gconv, s_gs, s_ls = _mix_sample_call(proj, states, mp)
        rw = jnp.zeros((D_MODEL, LANES), F32).at[:, :N_EXP].set(router_w[l])
        rb = jnp.zeros((1, LANES), F32).at[0, :N_EXP].set(router_b[l])
        x, h2, ridx, rprob, tcnt = _out_call(mix_p, mix_s.reshape(NS, D_MODEL), x, mod[l],
                                             w_out[l].astype(BF16), norm2_g[l].reshape(1, D_MODEL), rw, rb)
        moe = _moe(l, h2, ridx, rprob, tcnt, exp_w_gate, exp_b_gate, exp_w_up, exp_b_up,
                   exp_w_down, exp_b_down)
        p_gs = jnp.stack([p_gs[:, h * GDN_DK:(h + 1) * GDN_DK, h * GDN_DV:(h + 1) * GDN_DV]
                          for h in range(GDN_H)], axis=1)
        p_lst = jnp.stack([p_lst[:, h * GLA_DV:(h + 1) * GLA_DV, h * GLA_DK:(h + 1) * GLA_DK]
                           for h in range(GLA_H)], axis=1)
        p_states.append((p_lconv.reshape(3, BATCH, LRU_W).transpose(1, 0, 2), p_lh,
                         p_gconv.reshape(3, BATCH, GDN_CONV).transpose(1, 0, 2), p_gs,
                         p_lst.transpose(0, 1, 3, 2)))
        s_states.append((s_lconv.transpose(1, 0, 2), s_lh, s_gconv.transpose(1, 0, 2),
                         s_gs.reshape(DEC_BATCH, GDN_H, GDN_DK, GDN_DV),
                         s_ls.reshape(DEC_BATCH, GLA_H, GLA_DK, GLA_DV)))
    y_prompt, y_s = _final_call(x, moe, mod[DEPTH - 1], final_norm_g.reshape(1, D_MODEL))
    y_sample = y_s.reshape(DEC_SEQ, DEC_BATCH, D_MODEL).transpose(1, 0, 2)
    ps = [jnp.stack([s[j] for s in p_states]) for j in range(5)]
    ss = [jnp.stack([s[j] for s in s_states]) for j in range(5)]
    return (y_prompt, y_sample, *ps, *ss)
```

```python
import functools

import numpy as np
import jax
import jax.numpy as jnp
from jax import lax
from jax.experimental import pallas as pl
from jax.experimental.pallas import tpu as pltpu

F32 = jnp.float32
BF16 = jnp.bfloat16

D_MODEL = 1024
BATCH = 8
SEQ = 2048
DEPTH = 2
DEC_BATCH = 128
DEC_SEQ = 4
CONV_W = 4
LRU_W = 512
LRU_BLOCKS = 8
LRU_C = 8.0
GDN_H = 4
GDN_DK = 64
GDN_DV = 64
GDN_CONV = GDN_H * (2 * GDN_DK + GDN_DV)
GLA_H = 4
GLA_DK = 32
GLA_DV = 64
GLA_RANK = 16
GLA_TAU = 16.0
GLA_CHUNK = 16
N_EXP = 32
TOP_K = 4
SW_LIMIT = 7.0
SW_ALPHA = 1.702
EPS = 1e-6

NP = BATCH * SEQ
NS = DEC_BATCH * DEC_SEQ
NTOK = NP + NS
TM = 512
NPT = NP // TM
NTILES = NTOK // TM
LANES = 128
MODB = 128

C_AX, C_AG, C_QKV, C_BZ = 0, 512, 1024, 1792
C_CQ, C_CK, C_CV, C_CR, C_SM = 2048, 2176, 2304, 2560, 2816
D_INP = 2944

TT = 64
BB = 32
TE = 256
TB = 768
FFC = 512
TD = TM
PIECE = 8
LP = 2304
NA = NTOK * TOP_K
NROWS = -(-(NA + (NTOK // TD) * N_EXP * (PIECE - 1) + N_EXP * (TE - 1)) // TE) * TE
NTE = NROWS // TE

VMEM_LIMIT = 50 * 1024 * 1024


def _dot(a, b):
    return jnp.dot(a, b, preferred_element_type=F32)


def _dot_nt(a, b):
    return lax.dot_general(a, b, (((1,), (1,)), ((), ())), preferred_element_type=F32)


def _dot_tn(a, b):
    return lax.dot_general(a, b, (((0,), (0,)), ((), ())), preferred_element_type=F32)


def _mm_tn(a, b):
    return _dot_tn(a.astype(BF16), b.astype(BF16))


def _split3(x):
    x1 = x.astype(BF16)
    r = x - x1.astype(F32)
    x2 = r.astype(BF16)
    x3 = (r - x2.astype(F32)).astype(BF16)
    return x1, x2, x3


def _dot3(a, b):
    a1 = a.astype(BF16)
    a2 = (a - a1.astype(F32)).astype(BF16)
    b1 = b.astype(BF16)
    b2 = (b - b1.astype(F32)).astype(BF16)
    return _dot(a1, b1) + (_dot(a2, b1) + _dot(a1, b2))


def _dot_mask_l(mask, x):
    x1, x2, x3 = _split3(x)
    return _dot(mask, x1) + (_dot(mask, x2) + _dot(mask, x3))


def _dot_mask_r(x, mask):
    x1 = x.astype(BF16)
    x2 = (x - x1.astype(F32)).astype(BF16)
    return _dot(x1, mask) + _dot(x2, mask)


def _dot_mask_r3(x, mask):
    x1, x2, x3 = _split3(x)
    return _dot(x1, mask) + (_dot(x2, mask) + _dot(x3, mask))


def _sigmoid(x):
    return jax.nn.sigmoid(x)


def _silu(x):
    return x * jax.nn.sigmoid(x)


def _softplus(x):
    return jnp.maximum(x, 0.0) + jnp.log1p(jnp.exp(-jnp.abs(x)))


def _rms_rows(x, g):
    return x * lax.rsqrt(jnp.mean(x * x, axis=-1, keepdims=True) + EPS) * g


def _modulate(y, scale, shift):
    rows = y.shape[0]
    y3 = y.reshape(rows // MODB, MODB, y.shape[1])
    return (y3 * (1.0 + scale) + shift).reshape(y.shape)


def _gate_res(x, gate, y):
    rows = y.shape[0]
    y3 = y.reshape(rows // MODB, MODB, y.shape[1])
    return x + (gate * y3).reshape(y.shape)


def _mod_kernel(c_ref, w_ref, b_ref, o_ref):
    o_ref[...] = _dot3(_silu(c_ref[...]), w_ref[...]) + b_ref[...]


def _mod_call(c_all, ada_w, ada_b):
    tn = 768
    rows = c_all.shape[0]
    return pl.pallas_call(
        _mod_kernel,
        grid=(DEPTH, 6 * D_MODEL // tn),
        in_specs=[
            pl.BlockSpec((rows, D_MODEL), lambda l, j: (0, 0)),
            pl.BlockSpec((None, D_MODEL, tn), lambda l, j: (l, 0, j)),
            pl.BlockSpec((None, 1, tn), lambda l, j: (l, 0, j)),
        ],
        out_specs=pl.BlockSpec((None, rows, tn), lambda l, j: (l, 0, j)),
        out_shape=jax.ShapeDtypeStruct((DEPTH, rows, 6 * D_MODEL), F32),
        compiler_params=pltpu.CompilerParams(
            dimension_semantics=("arbitrary", "arbitrary"), vmem_limit_bytes=VMEM_LIMIT),
        name="adaln_mod",
    )(c_all, ada_w, ada_b.reshape(DEPTH, 1, 6 * D_MODEL))


def _mod_spec(chunk):
    return pl.BlockSpec((None, MODB, D_MODEL), lambda i: (i // NPT, 0, chunk))


def _row_spec(width):
    return pl.BlockSpec((TM, width), lambda i: (i, 0))


def _full_spec(shape):
    nd = len(shape)
    return pl.BlockSpec(shape, lambda i: (0,) * nd)


def _in_kernel(first, *refs):
    if first:
        (xp_ref, xs_ref, n1_ref, sc_ref, sh_ref, w_ref, x_ref, proj_ref, x_s) = refs
        step = pl.program_id(0)
        nlb = D_MODEL // LANES

        @pl.when(step < NPT)
        def _():
            for b in range(BATCH):
                for c in range(nlb):
                    x_s[c, pl.ds(b, TM // BATCH, stride=BATCH), :] = xp_ref[b, :, c * LANES:(c + 1) * LANES]

        @pl.when(step == NPT)
        def _():
            for c in range(nlb):
                x_s[c] = xs_ref[:, c * LANES:(c + 1) * LANES]

        x = jnp.concatenate([x_s[c] for c in range(nlb)], axis=1)
    else:
        (x1_ref, moe_ref, g2_ref, n1_ref, sc_ref, sh_ref, w_ref, x_ref, proj_ref) = refs
        x = _gate_res(x1_ref[...], g2_ref[...], moe_ref[...])
    x_ref[...] = x
    h = _modulate(_rms_rows(x, n1_ref[...]), sc_ref[...], sh_ref[...])
    proj_ref[...] = _dot(h.astype(BF16), w_ref[...])


def _in_call(first, stream_ins, mod_l, norm_g, w_in_r):
    if first:
        specs = [pl.BlockSpec((BATCH, TM // BATCH, D_MODEL), lambda i: (0, jnp.minimum(i, NPT - 1), 0)),
                 _full_spec((NS, D_MODEL))]
        scratch = [pltpu.VMEM((D_MODEL // LANES, TM, LANES), F32)]
    else:
        specs = [_row_spec(D_MODEL), _row_spec(D_MODEL), _mod_spec(5)]
        scratch = []
    specs += [_full_spec((1, D_MODEL)), _mod_spec(1), _mod_spec(0), _full_spec((D_MODEL, D_INP))]
    return pl.pallas_call(
        functools.partial(_in_kernel, first),
        grid=(NTILES,),
        in_specs=specs,
        out_specs=[_row_spec(D_MODEL), _row_spec(D_INP)],
        out_shape=[jax.ShapeDtypeStruct((NTOK, D_MODEL), F32), jax.ShapeDtypeStruct((NTOK, D_INP), F32)],
        scratch_shapes=scratch,
        compiler_params=pltpu.CompilerParams(
            dimension_semantics=("arbitrary",), vmem_limit_bytes=VMEM_LIMIT),
        name="in_proj",
    )(*stream_ins, norm_g, mod_l, mod_l, w_in_r)


def _lru_gates(xc, wa_ref, ba_ref, wx_ref, bx_ref, lam_ref):
    xb = xc.astype(BF16)
    r = _sigmoid(_dot(xb, wa_ref[...]) + ba_ref[...])
    i = _sigmoid(_dot(xb, wx_ref[...]) + bx_ref[...])
    log_a = -LRU_C * r * _softplus(-lam_ref[...])
    a = jnp.exp(log_a)
    mult = jnp.sqrt(1.0 - jnp.exp(2.0 * log_a))
    return a, mult, i * xc


def _gelu(x):
    return jax.nn.gelu(x, approximate=True)


def _gdn_prep(qkv, small, gpar_ref, ones_blk):
    qkv = _silu(qkv)
    q = qkv[:, 0:256]
    k = qkv[:, 256:512]
    v = qkv[:, 512:768]
    q = q * lax.rsqrt(_dot_mask_r(q * q, ones_blk) + EPS) * (GDN_DK ** -0.5)
    k = k * lax.rsqrt(_dot_mask_r(k * k, ones_blk) + EPS)
    beta = _sigmoid(small)
    g = -jnp.exp(gpar_ref[0:1, :]) * _softplus(small + gpar_ref[1:2, :])
    lane = lax.broadcasted_iota(jnp.int32, small.shape, 1)
    bg = jnp.where(lane < GDN_H, beta, g)
    return q, k, v, bg


def _gla_prep(proj_q, proj_k, small, wg2_ref, gbg_ref):
    pre = _dot3(small, wg2_ref[...]) + gbg_ref[...]
    gk = -_softplus(-pre) / GLA_TAU
    return proj_q * (GLA_DK ** -0.5), proj_k, gk


def _head_norm_gate(o, norm_g, gate_in, ones_blk):
    ms = _dot_mask_r(o * o, ones_blk) * (1.0 / GDN_DV)
    return o * lax.rsqrt(ms + EPS) * norm_g * _silu(gate_in)


def _mix_prompt_kernel(proj_ref, lcw_ref, lcb_ref, wa_ref, ba_ref, wx_ref, bx_ref, lam_ref,
                       gcw_ref, gpar_ref, gng_ref, wg2_ref, gbg_ref, lng_ref,
                       ones_ref, tri_ref, e2_ref, btblk_ref,
                       mix_ref, lconv_ref, lh_ref, gconv_ref, gs_ref, lst_ref,
                       exta, extb, a_s, u_s, hs_s, q_s, k_s, v_s, bg_s, og_s,
                       q2_s, k2_s, v2_s, gk_s, ol_s):
    nb = BATCH
    rows = TT * nb
    step = pl.program_id(0)

    @pl.when(step == 0)
    def _():
        exta[0:3 * nb, :] = jnp.zeros((3 * nb, LRU_W), F32)
        extb[0:3 * nb, :] = jnp.zeros((3 * nb, GDN_CONV), F32)
        lh_ref[...] = jnp.zeros_like(lh_ref)
        gs_ref[...] = jnp.zeros_like(gs_ref)
        lst_ref[...] = jnp.zeros_like(lst_ref)

    def conv(ext, x, w_ref):
        ext[pl.ds(3 * nb, rows), :] = x
        y = ext[pl.ds(0, rows), :] * w_ref[0:1, :]
        for j in range(1, CONV_W):
            y = y + ext[pl.ds(j * nb, rows), :] * w_ref[j:j + 1, :]
        tail = ext[pl.ds(rows, 3 * nb), :]
        ext[pl.ds(0, 3 * nb), :] = tail
        return y, tail

    ones_blk = ones_ref[...]

    xa, tail = conv(exta, proj_ref[:, C_AX:C_AX + LRU_W], lcw_ref)
    lconv_ref[...] = tail
    xc = xa + lcb_ref[...]
    a, mult, ix = _lru_gates(xc, wa_ref, ba_ref, wx_ref, bx_ref, lam_ref)
    rid = lax.broadcasted_iota(jnp.int32, (rows, LRU_W), 0)
    mult = jnp.where((rid < nb) & (step == 0), 1.0, mult)
    a_s[...] = a
    u_s[...] = mult * ix

    def scan_body(t, h):
        off = pl.multiple_of(t * nb, nb)
        h = a_s[pl.ds(off, nb), :] * h + u_s[pl.ds(off, nb), :]
        hs_s[pl.ds(off, nb), :] = h
        return h

    lh_ref[...] = lax.fori_loop(0, TT, scan_body, lh_ref[...], unroll=8)
    mix_ref[:, 0:LRU_W] = (hs_s[...] * _gelu(proj_ref[:, C_AG:C_AG + LRU_W])).astype(BF16)

    small = proj_ref[:, C_SM:C_SM + LANES]
    qkv, tail = conv(extb, proj_ref[:, C_QKV:C_QKV + GDN_CONV], gcw_ref)
    gconv_ref[...] = tail
    q, k, v, bg = _gdn_prep(qkv, small, gpar_ref, ones_blk)
    for j in range(2):
        ls = slice(j * LANES, (j + 1) * LANES)
        q_s[j] = q[:, ls]
        k_s[j] = k[:, ls]
        v_s[j] = v[:, ls]
    bg_s[...] = bg

    q2, k2, gk = _gla_prep(proj_ref[:, C_CQ:C_CQ + 128], proj_ref[:, C_CK:C_CK + 128],
                           small, wg2_ref, gbg_ref)
    q2_s[...] = q2
    k2_s[...] = k2
    gk_s[...] = gk
    for j in range(2):
        v2_s[j] = proj_ref[:, C_CV + j * LANES:C_CV + (j + 1) * LANES]

    ri = lax.broadcasted_iota(jnp.int32, (TT, 256), 0)
    lane = lax.broadcasted_iota(jnp.int32, (TT, 256), 1)
    cj = lane % TT
    incl = ri >= cj
    strict = ri > cj
    eye = ri == cj
    blk16m = (ri // 16) == (cj // 16)
    blk32m = (ri // 32) == (cj // 32)
    blk_causal = incl & blk16m
    hmask = [(lane // GDN_DV == h).astype(BF16) for h in range(GDN_H)]
    lane2 = lax.broadcasted_iota(jnp.int32, (TT, LANES), 1)
    hmask2 = [(lane2 // GLA_DK == h).astype(BF16) for h in range(GLA_H)]
    bdm = (lax.broadcasted_iota(jnp.int32, (256, 256), 0) // GDN_DK
           == lax.broadcasted_iota(jnp.int32, (256, 256), 1) // GDN_DV)
    bdm2 = (lax.broadcasted_iota(jnp.int32, (256, LANES), 0) // GLA_DV
            == lax.broadcasted_iota(jnp.int32, (256, LANES), 1) // GLA_DK)
    tri = tri_ref[...]
    e2 = e2_ref[...]
    btblk = btblk_ref[...]
    ones8 = jnp.ones((8, TT), BF16)
    seqs = range(nb)
    rsel = [pl.ds(b, TT, stride=nb) for b in seqs]

    def bd(x):
        xb = x.astype(BF16)
        return jnp.concatenate([xb * m for m in hmask], axis=0)

    def bd2(x):
        xb = x.astype(BF16)
        return jnp.concatenate([xb * m for m in hmask2], axis=0)

    def cat2(ref, b):
        return jnp.concatenate([ref[0, rsel[b], :], ref[1, rsel[b], :]], axis=1)

    def each(f, *lists):
        return [f(*xs) for xs in zip(*lists)]

    q = [cat2(q_s, b) for b in seqs]
    k = [cat2(k_s, b) for b in seqs]
    v = [cat2(v_s, b) for b in seqs]
    bgb = [bg_s[rsel[b], :] for b in seqs]
    gc_all = each(lambda x: _dot_mask_l(tri, x), bgb)
    both = each(lambda x, c: _dot_mask_r3(jnp.where(lane2 < GDN_H, x, c), e2), bgb, gc_all)
    beta = [x[:, 0:256] for x in both]
    gc = [x[:, 256:512] for x in both]
    gc_row = each(lambda g: _dot_mask_l(ones8, jnp.where(eye, g, 0.0))[0:1, :], gc)
    decay = each(lambda g, gr: jnp.exp(jnp.where(incl, g - gr, -jnp.inf)), gc, gc_row)
    egc = each(jnp.exp, gc)
    kb = each(lambda a, b_: a * b_, k, beta)
    bdk = each(bd, k)
    kq = each(lambda kb_, q_, m: _dot_nt(jnp.concatenate([kb_, q_], axis=0).astype(BF16), m), kb, q, bdk)
    amat = each(lambda x, d: jnp.where(strict, x[0:TT] * d, 0.0), kq, decay)
    qk = each(lambda x, d: jnp.where(incl, x[TT:] * d, 0.0), kq, decay)
    y = each(lambda a: -jnp.where(blk16m, a, 0.0), amat)
    n = y
    bdy = each(bd, y)
    for _ in range(3):
        y = each(lambda y_, m: _dot(y_.astype(BF16), m), y, bdy)
        bdy = each(bd, y)
        n = each(lambda n_, y_, m: n_ + y_ + _dot(n_.astype(BF16), m), n, y, bdy)
    t = each(lambda n_: jnp.where(eye, 1.0, 0.0) + n_, n)
    for lower in (each(lambda a: jnp.where(blk32m & ~blk16m, a, 0.0), amat),
                  each(lambda a: jnp.where(blk32m, 0.0, a), amat)):
        tl = each(lambda t_, l_: _dot(t_.astype(BF16), bd(l_)), t, lower)
        t = each(lambda t_, x: t_ - _dot(x.astype(BF16), bd(t_)), t, tl)
    uw = each(lambda t_, v_, b_, kb_, e: _dot(
        t_.astype(BF16), jnp.concatenate([bd(v_ * b_), bd(kb_ * e)], axis=1)), t, v, beta, kb, egc)
    s = [gs_ref[b] for b in seqs]
    ws = each(lambda x, q_, e, s_: _dot(
        jnp.concatenate([x[:, 256:512], q_ * e], axis=0).astype(BF16), s_.astype(BF16)), uw, q, egc, s)
    vn = each(lambda x, w_: x[:, 0:256] - w_[0:TT], uw, ws)
    o = each(lambda w_, a, v_: w_[TT:] + _dot(a.astype(BF16), bd(v_)), ws, qk, vn)
    for b in seqs:
        g_last = gc[b][TT - 1:TT, :]
        kd = k[b] * jnp.exp(g_last - gc[b])
        gs_ref[b] = s[b] * jnp.exp(g_last) + jnp.where(bdm, _mm_tn(kd, vn[b]), 0.0)
        for j in range(2):
            og_s[j, rsel[b], :] = o[b][:, j * LANES:(j + 1) * LANES]

    cum = [_dot_mask_l(btblk, gk_s[rsel[b], :]) for b in seqs]
    bc = [x[0:TT] for x in cum]
    bl = [x[TT:] for x in cum]
    qb = [q2_s[rsel[b], :] for b in seqs]
    kb2 = [k2_s[rsel[b], :] for b in seqs]
    vb = [cat2(v2_s, b) for b in seqs]
    qi = each(lambda a, c: (a * jnp.exp(c)).astype(BF16), qb, bc)
    ki = each(lambda a, c: a * jnp.exp(-c), kb2, bc)
    kst = each(lambda a, l_, c: (a * jnp.exp(l_ - c)).astype(BF16), kb2, bl, bc)
    ebl = each(jnp.exp, bl)
    sc = each(lambda a, c: jnp.where(blk_causal, _dot_nt(a, bd2(c)), 0.0), qi, ki)
    oh = each(lambda a, v_: _dot(a.astype(BF16), bd(v_)), sc, vb)
    st = [lst_ref[b] for b in seqs]
    vbb = each(lambda v_: v_.astype(BF16), vb)
    outs = [[] for _ in seqs]
    for c in range(TT // GLA_CHUNK):
        rs = slice(c * GLA_CHUNK, (c + 1) * GLA_CHUNK)
        for b in seqs:
            outs[b].append(oh[b][rs] + _dot_nt(qi[b][rs], st[b].astype(BF16)))
        upd = [jnp.where(bdm2, _dot_tn(vbb[b][rs], kst[b][rs]), 0.0) for b in seqs]
        st = [st[b] * ebl[b][c * GLA_CHUNK:c * GLA_CHUNK + 1, :] + upd[b] for b in seqs]
    for b in seqs:
        lst_ref[b] = st[b]
        ol = jnp.concatenate(outs[b], axis=0)
        for j in range(2):
            ol_s[j, rsel[b], :] = ol[:, j * LANES:(j + 1) * LANES]

    og = jnp.concatenate([og_s[0], og_s[1]], axis=1)
    out_b = _head_norm_gate(og, gng_ref[...], proj_ref[:, C_BZ:C_BZ + 256], ones_blk)
    mix_ref[:, 512:768] = out_b.astype(BF16)
    ol = jnp.concatenate([ol_s[0], ol_s[1]], axis=1)
    out_c = _head_norm_gate(ol, lng_ref[...], proj_ref[:, C_CR:C_CR + 256], ones_blk)
    mix_ref[:, 768:1024] = out_c.astype(BF16)


def _mix_consts():
    r = np.arange(TT)
    tri = (r[:, None] >= r[None, :]).astype(np.float32)
    same = (r[:, None] // GLA_CHUNK) == (r[None, :] // GLA_CHUNK)
    bt16 = (tri.astype(bool) & same).astype(np.float32)
    blk16 = same.astype(np.float32)
    c = np.arange(256)
    ones_blk = ((c[:, None] // GDN_DV) == (c[None, :] // GDN_DV)).astype(np.float32)
    e2 = np.zeros((LANES, 512), np.float32)
    for h in range(GDN_H):
        e2[h, h * GDN_DV:(h + 1) * GDN_DV] = 1.0
        e2[GDN_H + h, 256 + h * GDN_DV:256 + (h + 1) * GDN_DV] = 1.0
    btblk = np.concatenate([bt16, blk16], axis=0)
    return (jnp.asarray(ones_blk, BF16), jnp.asarray(tri, BF16),
            jnp.asarray(e2, BF16), jnp.asarray(btblk, BF16))


def _mix_prompt_call(proj, mp):
    rows = TT * BATCH
    consts = _mix_consts()
    params = (mp["lcw"], mp["lcb"], mp["wa"], mp["ba"], mp["wx"], mp["bx"], mp["lam"],
              mp["gcw"], mp["gpar"], mp["gng"], mp["wg2"], mp["gbg"], mp["lng"]) + consts
    out_shape = [
        jax.ShapeDtypeStruct((NP, D_MODEL), BF16),
        jax.ShapeDtypeStruct((3 * BATCH, LRU_W), F32),
        jax.ShapeDtypeStruct((BATCH, LRU_W), F32),
        jax.ShapeDtypeStruct((3 * BATCH, GDN_CONV), F32),
        jax.ShapeDtypeStruct((BATCH, GDN_H * GDN_DK, GDN_H * GDN_DV), F32),
        jax.ShapeDtypeStruct((BATCH, GLA_H * GLA_DV, GLA_H * GLA_DK), F32),
    ]
    out_specs = [pl.BlockSpec((rows, D_MODEL), lambda i: (i, 0))] + [
        _full_spec(s.shape) for s in out_shape[1:]]
    scratch = [
        pltpu.VMEM((rows + 3 * BATCH, LRU_W), F32),
        pltpu.VMEM((rows + 3 * BATCH, GDN_CONV), F32),
        pltpu.VMEM((rows, LRU_W), F32), pltpu.VMEM((rows, LRU_W), F32), pltpu.VMEM((rows, LRU_W), F32),
        pltpu.VMEM((2, rows, LANES), F32), pltpu.VMEM((2, rows, LANES), F32), pltpu.VMEM((2, rows, LANES), F32),
        pltpu.VMEM((rows, LANES), F32), pltpu.VMEM((2, rows, LANES), F32),
        pltpu.VMEM((rows, LANES), F32), pltpu.VMEM((rows, LANES), F32), pltpu.VMEM((2, rows, LANES), F32),
        pltpu.VMEM((rows, LANES), F32), pltpu.VMEM((2, rows, LANES), F32),
    ]
    return pl.pallas_call(
        _mix_prompt_kernel,
        grid=(SEQ // TT,),
        in_specs=[pl.BlockSpec((rows, D_INP), lambda i: (i, 0))] + [_full_spec(p.shape) for p in params],
        out_specs=out_specs,
        out_shape=out_shape,
        scratch_shapes=scratch,
        compiler_params=pltpu.CompilerParams(
            dimension_semantics=("arbitrary",), vmem_limit_bytes=VMEM_LIMIT),
        name="mix_prompt",
    )(proj, *params)


def _pair_bcast(x, p, lo_mask):
    return jnp.where(lo_mask, x[:, 2 * p:2 * p + 1], x[:, 2 * p + 1:2 * p + 2])


def _fold_pairs(acc):
    return acc[:, 0:64] + acc[:, 64:128]


def _mix_sample_kernel(p0_ref, p1_ref, p2_ref, p3_ref, lconv_in, lh_in, gconv_in, gs_in, ls_in,
                       lcw_ref, lcb_ref, wa_ref, ba_ref, wx_ref, bx_ref, lam_ref,
                       gcw_ref, gpar_ref, gng_ref, wg2_ref, gbg_ref, lng_ref, ones_ref,
                       mix_ref, lconv_ref, lh_ref, gconv_ref, gs_ref, ls_ref):
    rows = DEC_SEQ * BB
    prefs = (p0_ref, p1_ref, p2_ref, p3_ref)
    ones_blk = ones_ref[...]

    def cols(c0, width):
        return [p[:, c0:c0 + width] for p in prefs]

    def conv(prev, xs, w_ref):
        ext = [prev[j] for j in range(CONV_W - 1)] + xs
        ys = []
        for t in range(DEC_SEQ):
            y = ext[t] * w_ref[0:1, :]
            for j in range(1, CONV_W):
                y = y + ext[t + j] * w_ref[j:j + 1, :]
            ys.append(y)
        return jnp.concatenate(ys, axis=0), ext[DEC_SEQ:]

    def rows_of(x, t):
        return x[t * BB:(t + 1) * BB]

    xa, tail = conv(lconv_in, cols(C_AX, LRU_W), lcw_ref)
    for j in range(CONV_W - 1):
        lconv_ref[j] = tail[j]
    xc = xa + lcb_ref[...]
    a, mult, ix = _lru_gates(xc, wa_ref, ba_ref, wx_ref, bx_ref, lam_ref)
    u = mult * ix
    h = lh_in[...]
    hs = []
    for t in range(DEC_SEQ):
        h = rows_of(a, t) * h + rows_of(u, t)
        hs.append(h)
    lh_ref[...] = h
    ag = jnp.concatenate(cols(C_AG, LRU_W), axis=0)
    mix_a = jnp.concatenate(hs, axis=0) * _gelu(ag)

    small = jnp.concatenate(cols(C_SM, LANES), axis=0)
    qkv, tail = conv(gconv_in, cols(C_QKV, GDN_CONV), gcw_ref)
    for j in range(CONV_W - 1):
        gconv_ref[j] = tail[j]
    q, k, v, bg = _gdn_prep(qkv, small, gpar_ref, ones_blk)
    lo_mask = lax.broadcasted_iota(jnp.int32, (BB, LANES), 1) < 64
    gs_ref[...] = gs_in[...]
    hd = GDN_DK * GDN_DV
    o_heads = []
    for h in range(GDN_H):
        o_t = []
        for t in range(DEC_SEQ):
            rs = slice(t * BB, (t + 1) * BB)
            cs = slice(h * GDN_DK, (h + 1) * GDN_DK)
            eg = jnp.exp(bg[rs, GDN_H + h:GDN_H + h + 1])
            beta = bg[rs, h:h + 1]
            kt, qt, vt = k[rs, cs], q[rs, cs], v[rs, cs]
            kks = [_pair_bcast(kt, p, lo_mask) for p in range(GDN_DK // 2)]
            acc = jnp.zeros((BB, LANES), F32)
            for p in range(GDN_DK // 2):
                acc = acc + gs_ref[:, h * hd + p * LANES:h * hd + (p + 1) * LANES] * kks[p]
            vn = beta * (vt - eg * _fold_pairs(acc))
            vn2 = jnp.concatenate([vn, vn], axis=1)
            oacc = jnp.zeros((BB, LANES), F32)
            for p in range(GDN_DK // 2):
                sl = slice(h * hd + p * LANES, h * hd + (p + 1) * LANES)
                s = eg * gs_ref[:, sl] + kks[p] * vn2
                gs_ref[:, sl] = s
                oacc = oacc + s * _pair_bcast(qt, p, lo_mask)
            o_t.append(_fold_pairs(oacc))
        o_heads.append(jnp.concatenate(o_t, axis=0))
    o_b = jnp.concatenate(o_heads, axis=1)
    bz = jnp.concatenate(cols(C_BZ, 256), axis=0)
    mix_b = _head_norm_gate(o_b, gng_ref[...], bz, ones_blk)

    q2, k2, gk = _gla_prep(jnp.concatenate(cols(C_CQ, 128), axis=0),
                           jnp.concatenate(cols(C_CK, 128), axis=0), small, wg2_ref, gbg_ref)
    v2 = jnp.concatenate(cols(C_CV, 256), axis=0)
    ls_ref[...] = ls_in[...]
    hd2 = GLA_DK * GLA_DV
    o_heads = []
    for h in range(GLA_H):
        o_t = []
        for t in range(DEC_SEQ):
            rs = slice(t * BB, (t + 1) * BB)
            cs = slice(h * GLA_DK, (h + 1) * GLA_DK)
            al = jnp.exp(gk[rs, cs])
            kt, qt = k2[rs, cs], q2[rs, cs]
            vt = v2[rs, h * GLA_DV:(h + 1) * GLA_DV]
            v2x = jnp.concatenate([vt, vt], axis=1)
            oacc = jnp.zeros((BB, LANES), F32)
            for p in range(GLA_DK // 2):
                sl = slice(h * hd2 + p * LANES, h * hd2 + (p + 1) * LANES)
                s = _pair_bcast(al, p, lo_mask) * ls_ref[:, sl] + _pair_bcast(kt, p, lo_mask) * v2x
                ls_ref[:, sl] = s
                oacc = oacc + s * _pair_bcast(qt, p, lo_mask)
            o_t.append(_fold_pairs(oacc))
        o_heads.append(jnp.concatenate(o_t, axis=0))
    o_c = jnp.concatenate(o_heads, axis=1)
    cr = jnp.concatenate(cols(C_CR, 256), axis=0)
    mix_c = _head_norm_gate(o_c, lng_ref[...], cr, ones_blk)

    mix = jnp.concatenate([mix_a, mix_b, mix_c], axis=1).astype(BF16)
    for t in range(DEC_SEQ):
        mix_ref[t] = mix[t * BB:(t + 1) * BB]


def _mix_sample_call(proj, states, mp):
    lconv, lh, gconv, gs, ls = states
    ones_blk = _mix_consts()[0]
    params = (mp["lcw"], mp["lcb"], mp["wa"], mp["ba"], mp["wx"], mp["bx"], mp["lam"],
              mp["gcw"], mp["gpar"], mp["gng"], mp["wg2"], mp["gbg"], mp["lng"], ones_blk)
    nblk = DEC_BATCH // BB
    base = NP // BB

    def proj_spec(t):
        return pl.BlockSpec((BB, D_INP), lambda j: (base + t * nblk + j, 0))

    def bspec3(n, width):
        return pl.BlockSpec((n, BB, width), lambda j: (0, j, 0))

    def bspec2(width):
        return pl.BlockSpec((BB, width), lambda j: (j, 0))

    gdn_flat = GDN_H * GDN_DK * GDN_DV
    gla_flat = GLA_H * GLA_DK * GLA_DV
    state_specs = [bspec3(3, LRU_W), bspec2(LRU_W), bspec3(3, GDN_CONV), bspec2(gdn_flat), bspec2(gla_flat)]
    out_shape = [
        jax.ShapeDtypeStruct((DEC_SEQ, DEC_BATCH, D_MODEL), BF16),
        jax.ShapeDtypeStruct((3, DEC_BATCH, LRU_W), F32),
        jax.ShapeDtypeStruct((DEC_BATCH, LRU_W), F32),
        jax.ShapeDtypeStruct((3, DEC_BATCH, GDN_CONV), F32),
        jax.ShapeDtypeStruct((DEC_BATCH, gdn_flat), F32),
        jax.ShapeDtypeStruct((DEC_BATCH, gla_flat), F32),
    ]
    return pl.pallas_call(
        _mix_sample_kernel,
        grid=(nblk,),
        in_specs=[proj_spec(t) for t in range(DEC_SEQ)] + state_specs + [_full_spec(p.shape) for p in params],
        out_specs=[bspec3(DEC_SEQ, D_MODEL)] + state_specs,
        out_shape=out_shape,
        compiler_params=pltpu.CompilerParams(
            dimension_semantics=("arbitrary",), vmem_limit_bytes=VMEM_LIMIT),
        name="mix_sample",
    )(proj, proj, proj, proj, lconv, lh, gconv, gs, ls, *params)


def _out_kernel(mp_ref, ms_ref, x_ref, w_ref, g1_ref, n2_ref, sc_ref, sh_ref, rw_ref, rb_ref,
                x1_ref, h2_ref, ridx_ref, rprob_ref, tcnt_ref):
    step = pl.program_id(0)
    mix = jnp.where(step < NPT, mp_ref[...], ms_ref[...])
    x1 = _gate_res(x_ref[...], g1_ref[...], _dot(mix, w_ref[...]))
    x1_ref[...] = x1
    h2 = _modulate(_rms_rows(x1, n2_ref[...]), sc_ref[...], sh_ref[...])
    h2_ref[...] = h2
    logits = _dot3(h2, rw_ref[...]) + rb_ref[...]
    lane = lax.broadcasted_iota(jnp.int32, logits.shape, 1)
    cur = jnp.where(lane < N_EXP, logits, -jnp.inf)
    vals, idxs = [], []
    for _ in range(TOP_K):
        m = jnp.max(cur, axis=-1, keepdims=True)
        idx = jnp.min(jnp.where(cur == m, lane, LANES), axis=-1, keepdims=True)
        vals.append(m)
        idxs.append(idx)
        cur = jnp.where(lane == idx, -jnp.inf, cur)
    es = [jnp.exp(v - vals[0]) for v in vals]
    den = es[0] + es[1] + es[2] + es[3]
    ridx = jnp.zeros(logits.shape, jnp.int32)
    rprob = jnp.zeros(logits.shape, F32)
    for j in range(TOP_K):
        ridx = jnp.where(lane == j, idxs[j], ridx)
        rprob = jnp.where(lane == j, es[j] / den, rprob)
    ridx_ref[...] = ridx
    rprob_ref[...] = rprob
    onehot = jnp.zeros(logits.shape, F32)
    for idx in idxs:
        onehot = onehot + jnp.where(lane == idx, 1.0, 0.0)
    colsum = jnp.sum(onehot, axis=0, keepdims=True)
    aligned = jnp.floor((colsum + (PIECE - 1)) * (1.0 / PIECE)) * PIECE
    tcnt_ref[...] = jnp.concatenate([aligned, jnp.zeros((7, LANES), F32)], axis=0)


def _out_call(mix_p, mix_s, x, mod_l, w_out, norm_g, rw, rb):
    return pl.pallas_call(
        _out_kernel,
        grid=(NTILES,),
        in_specs=[
            pl.BlockSpec((TM, D_MODEL), lambda i: (jnp.minimum(i, NPT - 1), 0)),
            _full_spec((NS, D_MODEL)),
            _row_spec(D_MODEL),
            _full_spec((D_MODEL, D_MODEL)),
            _mod_spec(2),
            _full_spec((1, D_MODEL)),
            _mod_spec(4),
            _mod_spec(3),
            _full_spec((D_MODEL, LANES)),
            _full_spec((1, LANES)),
        ],
        out_specs=[_row_spec(D_MODEL), _row_spec(D_MODEL), _row_spec(LANES), _row_spec(LANES),
                   pl.BlockSpec((8, LANES), lambda i: (i, 0))],
        out_shape=[
            jax.ShapeDtypeStruct((NTOK, D_MODEL), F32),
            jax.ShapeDtypeStruct((NTOK, D_MODEL), F32),
            jax.ShapeDtypeStruct((NTOK, LANES), jnp.int32),
            jax.ShapeDtypeStruct((NTOK, LANES), F32),
            jax.ShapeDtypeStruct((NTILES * 8, LANES), F32),
        ],
        compiler_params=pltpu.CompilerParams(
            dimension_semantics=("arbitrary",), vmem_limit_bytes=VMEM_LIMIT),
        name="out_proj_router",
    )(mix_p, mix_s, x, w_out, mod_l, norm_g, mod_l, mod_l, rw, rb)


def _excl_lane_cumsum(row):
    r = lax.broadcasted_iota(jnp.int32, (LANES, LANES), 0)
    c = lax.broadcasted_iota(jnp.int32, (LANES, LANES), 1)
    before = jnp.where(r < c, 1.0, 0.0).astype(BF16)
    return _dot_mask_r3(jnp.broadcast_to(row, (8, LANES)), before)[0:1]


def _route_kernel(ridx_ref, tcnt_ref, total_ref, pos_ref, post_ref, tab_ref, meta_ref, cnt_s):
    i = pl.program_id(0)
    lane = lax.broadcasted_iota(jnp.int32, (TD, LANES), 1)
    ridx = ridx_ref[...]
    hits = [lane == ridx[:, k:k + 1] for k in range(TOP_K)]
    onehot = jnp.zeros((TD, LANES), F32)
    for hit in hits:
        onehot = onehot + jnp.where(hit, 1.0, 0.0)
    aligned = tcnt_ref[0:1, :]

    @pl.when(i == 0)
    def _():
        total = total_ref[...]
        padded = jnp.floor((total + (TE - 1)) * (1.0 / TE)) * TE
        pstart = _excl_lane_cumsum(padded)
        meta_ref[...] = jnp.concatenate(
            [total, pstart, pstart + padded, jnp.zeros((5, LANES), F32)], axis=0)
        cnt_s[...] = jnp.zeros_like(cnt_s)

    r = lax.broadcasted_iota(jnp.int32, (TD, TD), 0)
    c = lax.broadcasted_iota(jnp.int32, (TD, TD), 1)
    earlier = jnp.where(c < r, 1.0, 0.0).astype(BF16)
    pos = _excl_lane_cumsum(aligned) + _dot(earlier, onehot.astype(BF16))
    posk = jnp.zeros((TD, LANES), F32)
    for k, hit in enumerate(hits):
        d = jnp.sum(jnp.where(hit, pos, 0.0), axis=1, keepdims=True)
        posk = jnp.where(lane == k, d, posk)
    pos_ref[...] = posk.astype(jnp.int32)
    post_ref[...] = posk.T[0:8].astype(jnp.int32)
    pieces = aligned * (1.0 / PIECE)
    npieces = jnp.broadcast_to(jnp.sum(pieces, axis=1, keepdims=True), (1, LANES))
    tab_ref[...] = jnp.concatenate(
        [pieces, cnt_s[...], npieces, jnp.zeros((5, LANES), F32)], axis=0).astype(jnp.int32)
    cnt_s[...] += aligned


def _route_call(ridx, tcnt):
    nblk = NTOK // TD
    total = jnp.sum(tcnt.reshape(nblk, 8, LANES)[:, 0], axis=0, keepdims=True)
    return pl.pallas_call(
        _route_kernel,
        grid=(nblk,),
        in_specs=[pl.BlockSpec((TD, LANES), lambda i: (i, 0)),
                  pl.BlockSpec((8, LANES), lambda i: (i, 0)),
                  _full_spec((1, LANES))],
        out_specs=[pl.BlockSpec((TD, LANES), lambda i: (i, 0)),
                   pl.BlockSpec((8, TD), lambda i: (i, 0)),
                   pl.BlockSpec((8, LANES), lambda i: (i, 0)),
                   pl.BlockSpec((8, LANES), lambda i: (0, 0))],
        out_shape=[jax.ShapeDtypeStruct((NTOK, LANES), jnp.int32),
                   jax.ShapeDtypeStruct((nblk * 8, TD), jnp.int32),
                   jax.ShapeDtypeStruct((nblk * 8, LANES), jnp.int32),
                   jax.ShapeDtypeStruct((8, LANES), F32)],
        scratch_shapes=[pltpu.VMEM((1, LANES), F32)],
        compiler_params=pltpu.CompilerParams(
            dimension_semantics=("arbitrary",), vmem_limit_bytes=VMEM_LIMIT),
        name="route",
    )(ridx, tcnt, total)


def _for_each_piece(i, ptab_ref, btab_ref, pstart_ref, fn):
    off = jnp.int32(0)
    for e in range(N_EXP):
        npieces = ptab_ref[i * N_EXP + e]
        dst0 = pstart_ref[e] + btab_ref[i * N_EXP + e]

        def body(j, carry, off=off, dst0=dst0):
            fn(pl.multiple_of(off + j * PIECE, PIECE), pl.multiple_of(dst0 + j * PIECE, PIECE))
            return carry

        lax.fori_loop(0, npieces, body, 0)
        off = off + npieces * PIECE


def _dispatch_kernel(qtab_ref, btab_ref, ttab_ref, pstart_ref, pend_ref, cnt_ref, post_ref, h_ref, xs_ref,
                     sbuf, zero_s, sems, semz):
    i = pl.program_id(0)
    n = pl.num_programs(0)
    slot = i % 2

    def piece(s):
        def copy(src, dst):
            return pltpu.make_async_copy(sbuf.at[s, pl.ds(src, PIECE)], xs_ref.at[pl.ds(dst, PIECE)],
                                         sems.at[s])
        return copy

    def start_pieces(tile, s):
        _for_each_piece(tile, qtab_ref, btab_ref, pstart_ref, lambda src, dst: piece(s)(src, dst).start())

    def wait_pieces(tile, s):
        def body(j, carry):
            piece(s)(0, 0).wait()
            return carry
        lax.fori_loop(0, ttab_ref[tile], body, 0)

    @pl.when(i == 0)
    def _():
        zero_s[...] = jnp.zeros_like(zero_s)

        def fill(start):
            return pltpu.make_async_copy(zero_s, xs_ref.at[pl.ds(pl.multiple_of(start, TE), TE)], semz)

        for e in range(N_EXP):
            @pl.when(cnt_ref[e] > 0)
            def _():
                fill(pend_ref[e] - TE).start()
        for e in range(N_EXP):
            @pl.when(cnt_ref[e] > 0)
            def _():
                fill(pend_ref[e] - TE).wait()

        def tail_fill(t, carry):
            fill(t * TE).start()
            fill(t * TE).wait()
            return carry

        lax.fori_loop(pend_ref[N_EXP - 1] // TE, NTE, tail_fill, 0)

    @pl.when(i >= 2)
    def _():
        wait_pieces(i - 2, slot)

    post = post_ref[...]
    prow = lax.broadcasted_iota(jnp.int32, (LP, TD), 0)
    sel = prow == post[0:1, :]
    for k in range(1, TOP_K):
        sel = sel | (prow == post[k:k + 1, :])
    onehot = jnp.where(sel, 1.0, 0.0).astype(BF16)
    sbuf[slot] = _dot(onehot, h_ref[...].astype(BF16))
    start_pieces(i, slot)

    @pl.when(i == n - 1)
    def _():
        @pl.when(i >= 1)
        def _():
            wait_pieces(i - 1, 1 - slot)
        wait_pieces(i, slot)


def _dispatch_call(qtab, btab, ttab, pstart, pend, counts, post, h2):
    grid_spec = pltpu.PrefetchScalarGridSpec(
        num_scalar_prefetch=6,
        grid=(NTOK // TD,),
        in_specs=[
            pl.BlockSpec((8, TD), lambda i, *_: (i, 0)),
            pl.BlockSpec((TD, D_MODEL), lambda i, *_: (i, 0)),
        ],
        out_specs=pl.BlockSpec(memory_space=pl.ANY),
        scratch_shapes=[pltpu.VMEM((2, LP, D_MODEL), F32), pltpu.VMEM((TE, D_MODEL), F32),
                        pltpu.SemaphoreType.DMA((2,)), pltpu.SemaphoreType.DMA(())],
    )
    return pl.pallas_call(
        _dispatch_kernel,
        grid_spec=grid_spec,
        out_shape=jax.ShapeDtypeStruct((NROWS, D_MODEL), F32),
        compiler_params=pltpu.CompilerParams(
            dimension_semantics=("arbitrary",), vmem_limit_bytes=VMEM_LIMIT),
        name="moe_dispatch",
    )(qtab, btab, ttab, pstart, pend, counts, post, h2)


def _combine_kernel(qtab_ref, btab_ref, ttab_ref, pstart_ref, pos_ref, prob_ref, ys_ref, o_ref, ybuf, sems):
    i = pl.program_id(0)
    n = pl.num_programs(0)
    slot = i % 2

    def piece(s):
        def copy(loc, src):
            return pltpu.make_async_copy(ys_ref.at[pl.ds(src, PIECE)], ybuf.at[s, pl.ds(loc, PIECE)],
                                         sems.at[s])
        return copy

    def start_pieces(tile, s):
        _for_each_piece(tile, qtab_ref, btab_ref, pstart_ref, lambda loc, src: piece(s)(loc, src).start())

    @pl.when(i == 0)
    def _():
        ybuf[...] = jnp.zeros_like(ybuf)
        start_pieces(0, 0)

    @pl.when(i + 1 < n)
    def _():
        start_pieces(i + 1, 1 - slot)

    def wait_piece(j, carry):
        piece(slot)(0, 0).wait()
        return carry

    lax.fori_loop(0, ttab_ref[i], wait_piece, 0)

    pos = pos_ref[...]
    prob = prob_ref[...]
    pcol = lax.broadcasted_iota(jnp.int32, (TD, LP), 1)
    w = jnp.zeros((TD, LP), F32)
    for k in range(TOP_K):
        w = w + jnp.where(pcol == pos[:, k:k + 1], prob[:, k:k + 1], 0.0)
    o_ref[...] = _dot(w.astype(BF16), ybuf[slot].astype(BF16))


def _combine_call(qtab, btab, ttab, pstart, pos, rprob, ys):
    grid_spec = pltpu.PrefetchScalarGridSpec(
        num_scalar_prefetch=4,
        grid=(NTOK // TD,),
        in_specs=[
            pl.BlockSpec((TD, LANES), lambda i, *_: (i, 0)),
            pl.BlockSpec((TD, LANES), lambda i, *_: (i, 0)),
            pl.BlockSpec(memory_space=pl.ANY),
        ],
        out_specs=pl.BlockSpec((TD, D_MODEL), lambda i, *_: (i, 0)),
        scratch_shapes=[pltpu.VMEM((2, LP, D_MODEL), F32), pltpu.SemaphoreType.DMA((2,))],
    )
    return pl.pallas_call(
        _combine_kernel,
        grid_spec=grid_spec,
        out_shape=jax.ShapeDtypeStruct((NTOK, D_MODEL), F32),
        compiler_params=pltpu.CompilerParams(
            dimension_semantics=("arbitrary",), vmem_limit_bytes=VMEM_LIMIT),
        name="moe_combine",
    )(qtab, btab, ttab, pstart, pos, rprob, ys)


def _ffn_kernel(cnt_ref, pend_ref, xs_ref, wg_ref, bg_ref, wu_ref, bu_ref, wd_ref, bd_ref,
                ys_ref, wg_s, wu_s, wd_s, xbuf, ybuf, sem_in, sem_out):
    e = pl.program_id(0)
    nrows = ((cnt_ref[e] + (TE - 1)) // TE) * TE
    row0 = pend_ref[e] - nrows
    nbig = nrows // TB
    nsmall = (nrows - nbig * TB) // TE

    wg_s[...] = wg_ref[...].astype(BF16)
    wu_s[...] = wu_ref[...].astype(BF16)
    wd_s[...] = wd_ref[...].astype(BF16)

    def run_tiles(nt, base, size):
        def rows(t):
            return pl.ds(pl.multiple_of(base + t * size, TE), size)

        def x_copy(t, slot):
            return pltpu.make_async_copy(xs_ref.at[rows(t)], xbuf.at[slot, pl.ds(0, size)], sem_in.at[slot])

        def y_copy(t, slot):
            return pltpu.make_async_copy(ybuf.at[slot, pl.ds(0, size)], ys_ref.at[rows(t)], sem_out.at[slot])

        @pl.when(nt > 0)
        def _():
            x_copy(0, 0).start()

        def tile(t, carry):
            slot = t % 2

            @pl.when(t + 1 < nt)
            def _():
                x_copy(t + 1, 1 - slot).start()

            x_copy(t, slot).wait()

            @pl.when(t >= 2)
            def _():
                y_copy(t - 2, slot).wait()

            x = xbuf[slot, 0:size].astype(BF16)
            y = bd_ref[...]
            for c in range(D_MODEL // FFC):
                cs = slice(c * FFC, (c + 1) * FFC)
                gate = jnp.minimum(_dot(x, wg_s[:, cs]) + bg_ref[:, cs], SW_LIMIT)
                up = jnp.clip(_dot(x, wu_s[:, cs]) + bu_ref[:, cs], -SW_LIMIT, SW_LIMIT)
                act = (up + 1.0) * gate * _sigmoid(SW_ALPHA * gate)
                y = y + _dot(act.astype(BF16), wd_s[cs, :])
            ybuf[slot, 0:size] = y
            y_copy(t, slot).start()
            return carry

        lax.fori_loop(0, nt, tile, 0)

        @pl.when(nt >= 2)
        def _():
            y_copy(nt - 2, nt % 2).wait()

        @pl.when(nt >= 1)
        def _():
            y_copy(nt - 1, (nt - 1) % 2).wait()

    run_tiles(nbig, row0, TB)
    run_tiles(nsmall, row0 + nbig * TB, TE)

    @pl.when(e == N_EXP - 1)
    def _():
        ybuf[0, 0:TE] = jnp.zeros((TE, D_MODEL), F32)

        def tail_fill(t, carry):
            cp = pltpu.make_async_copy(
                ybuf.at[0, pl.ds(0, TE)], ys_ref.at[pl.ds(pl.multiple_of(t * TE, TE), TE)], sem_out.at[0])
            cp.start()
            cp.wait()
            return carry

        lax.fori_loop(pend_ref[N_EXP - 1] // TE, NTE, tail_fill, 0)


def _ffn_call(layer, counts, pend, xs, wg, bg, wu, bu, wd, bd):
    wspec = pl.BlockSpec((None, None, D_MODEL, D_MODEL), lambda e, c, p: (layer, e, 0, 0))
    bspec = pl.BlockSpec((None, None, 1, D_MODEL), lambda e, c, p: (layer, e, 0, 0))
    grid_spec = pltpu.PrefetchScalarGridSpec(
        num_scalar_prefetch=2,
        grid=(N_EXP,),
        in_specs=[pl.BlockSpec(memory_space=pl.ANY), wspec, bspec, wspec, bspec, wspec, bspec],
        out_specs=pl.BlockSpec(memory_space=pl.ANY),
        scratch_shapes=[pltpu.VMEM((D_MODEL, D_MODEL), BF16)] * 3 + [
            pltpu.VMEM((2, TB, D_MODEL), F32), pltpu.VMEM((2, TB, D_MODEL), F32),
            pltpu.SemaphoreType.DMA((2,)), pltpu.SemaphoreType.DMA((2,))],
    )
    b4 = lambda b: b.reshape(DEPTH, N_EXP, 1, D_MODEL)
    return pl.pallas_call(
        _ffn_kernel,
        grid_spec=grid_spec,
        out_shape=jax.ShapeDtypeStruct((NROWS, D_MODEL), F32),
        compiler_params=pltpu.CompilerParams(
            dimension_semantics=("arbitrary",), vmem_limit_bytes=VMEM_LIMIT),
        name="expert_ffn",
    )(counts, pend, xs, wg, b4(bg), wu, b4(bu), wd, b4(bd))


def _moe(layer, h2, ridx, rprob, tcnt, wg, bg, wu, bu, wd, bd):
    pos, post, tab, meta = _route_call(ridx, tcnt)
    counts = meta[0, :N_EXP].astype(jnp.int32)
    pstart = meta[1, :N_EXP].astype(jnp.int32)
    pend = meta[2, :N_EXP].astype(jnp.int32)
    tab = tab.reshape(NTOK // TD, 8, LANES)
    qtab = tab[:, 0, :N_EXP].reshape(-1)
    btab = tab[:, 1, :N_EXP].reshape(-1)
    ttab = tab[:, 2, 0]
    xs = _dispatch_call(qtab, btab, ttab, pstart, pend, counts, post, h2)
    ys = _ffn_call(layer, counts, pend, xs, wg, bg, wu, bu, wd, bd)
    return _combine_call(qtab, btab, ttab, pstart, pos, rprob, ys)


def _final_kernel(x1_ref, moe_ref, g2_ref, ng_ref, yp_ref, ys_ref, y_s):
    step = pl.program_id(0)
    x = _gate_res(x1_ref[...], g2_ref[...], moe_ref[...])
    y = _rms_rows(x, ng_ref[...])

    @pl.when(step < NPT)
    def _():
        nlb = D_MODEL // LANES
        for c in range(nlb):
            y_s[c] = y[:, c * LANES:(c + 1) * LANES]
        for b in range(BATCH):
            yp_ref[b] = jnp.concatenate(
                [y_s[c, pl.ds(b, TM // BATCH, stride=BATCH), :] for c in range(nlb)], axis=1)

    @pl.when(step == NPT)
    def _():
        ys_ref[...] = y


def _final_call(x1, moe, mod_l, norm_g):
    tt = TM // BATCH
    return pl.pallas_call(
        _final_kernel,
        grid=(NTILES,),
        in_specs=[_row_spec(D_MODEL), _row_spec(D_MODEL), _mod_spec(5), _full_spec((1, D_MODEL))],
        out_specs=[pl.BlockSpec((BATCH, tt, D_MODEL), lambda i: (0, jnp.minimum(i, NPT - 1), 0)),
                   _full_spec((NS, D_MODEL))],
        out_shape=[jax.ShapeDtypeStruct((BATCH, SEQ, D_MODEL), F32),
                   jax.ShapeDtypeStruct((NS, D_MODEL), F32)],
        scratch_shapes=[pltpu.VMEM((D_MODEL // LANES, TM, LANES), F32)],
        compiler_params=pltpu.CompilerParams(
            dimension_semantics=("arbitrary",), vmem_limit_bytes=VMEM_LIMIT),
        name="final_norm",
    )(x1, moe, mod_l, norm_g)


def _block_diag(w):
    n, d, e = w.shape
    eye = jnp.eye(n, dtype=w.dtype)
    return (eye[:, None, :, None] * w[:, :, None, :]).reshape(n * d, n * e)


def _pad_lanes(v, offset):
    out = jnp.zeros((1, LANES), F32)
    return out.at[0, offset:offset + v.shape[0]].set(v)


def _mixer_params(l, w_in, lru_conv_w, lru_conv_b, lru_wa, lru_ba, lru_wx, lru_bx, lru_lambda,
                  gdn_conv_w, gdn_a_log, gdn_dt_bias, gdn_norm_g, gla_wg2, gla_bg, gla_norm_g):
    w = w_in[l]
    w_in_r = jnp.concatenate(
        [w[:, 0:2048], w[:, 2056:2824], w[:, 2048:2056], w[:, 2824:2840],
         jnp.zeros((D_MODEL, D_INP - 2840), F32)], axis=1).astype(BF16)
    row = lambda v: v.reshape(1, -1)
    mp = dict(
        lcw=lru_conv_w[l], lcb=row(lru_conv_b[l]),
        wa=_block_diag(lru_wa[l]).astype(BF16), ba=row(lru_ba[l]),
        wx=_block_diag(lru_wx[l]).astype(BF16), bx=row(lru_bx[l]),
        lam=row(lru_lambda[l]),
        gcw=gdn_conv_w[l],
        gpar=jnp.concatenate([_pad_lanes(gdn_a_log[l], GDN_H), _pad_lanes(gdn_dt_bias[l], GDN_H)], axis=0),
        gng=row(jnp.tile(gdn_norm_g[l], GDN_H)),
        wg2=jnp.zeros((LANES, LANES), F32).at[2 * GDN_H:2 * GDN_H + GLA_RANK].set(gla_wg2[l]),
        gbg=row(gla_bg[l]),
        lng=row(jnp.tile(gla_norm_g[l], GLA_H)),
    )
    return w_in_r, mp


def kernel(x_prompt, x_sample, state_lru_conv, state_lru_h, state_gdn_conv, state_gdn_S, state_gla_S, c_prompt, c_sample, ada_w, ada_b, norm1_g, norm2_g, w_in, lru_conv_w, lru_conv_b, lru_wa, lru_ba, lru_wx, lru_bx, lru_lambda, gdn_conv_w, gdn_a_log, gdn_dt_bias, gdn_norm_g, gla_wg2, gla_bg, gla_norm_g, w_out, router_w, router_b, exp_w_gate, exp_b_gate, exp_w_up, exp_b_up, exp_w_down, exp_b_down, final_norm_g):
    xs_t = x_sample.transpose(1, 0, 2).reshape(NS, D_MODEL)
    mod = _mod_call(jnp.concatenate([c_prompt, c_sample], axis=0), ada_w, ada_b)
    mod = jnp.stack([jnp.tile(mod[:, :BATCH], (1, MODB // BATCH, 1)), mod[:, BATCH:]], axis=1)

    p_states, s_states = [], []
    x = moe = None
    for l in range(DEPTH):
        w_in_r, mp = _mixer_params(l, w_in, lru_conv_w, lru_conv_b, lru_wa, lru_ba, lru_wx, lru_bx,
                                   lru_lambda, gdn_conv_w, gdn_a_log, gdn_dt_bias, gdn_norm_g,
                                   gla_wg2, gla_bg, gla_norm_g)
        stream_ins = (x_prompt, xs_t) if l == 0 else (x, moe, mod[l - 1])
        x, proj = _in_call(l == 0, stream_ins, mod[l], norm1_g[l].reshape(1, D_MODEL), w_in_r)
        mix_p, p_lconv, p_lh, p_gconv, p_gs, p_lst = _mix_prompt_call(proj, mp)
        states = (state_lru_conv[l].transpose(1, 0, 2), state_lru_h[l],
                  state_gdn_conv[l].transpose(1, 0, 2),
                  state_gdn_S[l].reshape(DEC_BATCH, -1), state_gla_S[l].reshape(DEC_BATCH, -1))
        mix_s, s_lconv, s_lh, s_gconv, s_gs, s_ls = _mix_sample_call(proj, states, mp)
        rw = jnp.zeros((D_MODEL, LANES), F32).at[:, :N_EXP].set(router_w[l])
        rb = jnp.zeros((1, LANES), F32).at[0, :N_EXP].set(router_b[l])
        x, h2, ridx, rprob, tcnt = _out_call(mix_p, mix_s.reshape(NS, D_MODEL), x, mod[l],
                                             w_out[l].astype(BF16), norm2_g[l].reshape(1, D_MODEL), rw, rb)
        moe = _moe(l, h2, ridx, rprob, tcnt, exp_w_gate, exp_b_gate, exp_w_up, exp_b_up,
                   exp_w_down, exp_b_down)
        p_gs = jnp.stack([p_gs[:, h * GDN_DK:(h + 1) * GDN_DK, h * GDN_DV:(h + 1) * GDN_DV]
                          for h in range(GDN_H)], axis=1)
        p_lst = jnp.stack([p_lst[:, h * GLA_DV:(h + 1) * GLA_DV, h * GLA_DK:(h + 1) * GLA_DK]
                           for h in range(GLA_H)], axis=1)
        p_states.append((p_lconv.reshape(3, BATCH, LRU_W).transpose(1, 0, 2), p_lh,
                         p_gconv.reshape(3, BATCH, GDN_CONV).transpose(1, 0, 2), p_gs,
                         p_lst.transpose(0, 1, 3, 2)))
        s_states.append((s_lconv.transpose(1, 0, 2), s_lh, s_gconv.transpose(1, 0, 2),
                         s_gs.reshape(DEC_BATCH, GDN_H, GDN_DK, GDN_DV),
                         s_ls.reshape(DEC_BATCH, GLA_H, GLA_DK, GLA_DV)))
    y_prompt, y_s = _final_call(x, moe, mod[DEPTH - 1], final_norm_g.reshape(1, D_MODEL))
    y_sample = y_s.reshape(DEC_SEQ, DEC_BATCH, D_MODEL).transpose(1, 0, 2)
    ps = [jnp.stack([s[j] for s in p_states]) for j in range(5)]
    ss = [jnp.stack([s[j] for s in s_states]) for j in range(5)]
    return (y_prompt, y_sample, *ps, *ss)
```

```python
import functools

import numpy as np
import jax
import jax.numpy as jnp
from jax import lax
from jax.experimental import pallas as pl
from jax.experimental.pallas import tpu as pltpu

F32 = jnp.float32
BF16 = jnp.bfloat16

D_MODEL = 1024
BATCH = 8
SEQ = 2048
DEPTH = 2
DEC_BATCH = 128
DEC_SEQ = 4
CONV_W = 4
LRU_W = 512
LRU_BLOCKS = 8
LRU_C = 8.0
GDN_H = 4
GDN_DK = 64
GDN_DV = 64
GDN_CONV = GDN_H * (2 * GDN_DK + GDN_DV)
GLA_H = 4
GLA_DK = 32
GLA_DV = 64
GLA_RANK = 16
GLA_TAU = 16.0
GLA_CHUNK = 16
N_EXP = 32
TOP_K = 4
SW_LIMIT = 7.0
SW_ALPHA = 1.702
EPS = 1e-6

NP = BATCH * SEQ
NS = DEC_BATCH * DEC_SEQ
NTOK = NP + NS
TM = 512
NPT = NP // TM
NTILES = NTOK // TM
LANES = 128
MODB = 128

C_AX, C_AG, C_QKV, C_BZ = 0, 512, 1024, 1792
C_CQ, C_CK, C_CV, C_CR, C_SM = 2048, 2176, 2304, 2560, 2816
D_INP = 2944

TT = 64
BB = 32
TE = 256
TB = 768
FFC = 512
TD = TM
PIECE = 8
LP = 2304
NA = NTOK * TOP_K
NROWS = -(-(NA + (NTOK // TD) * N_EXP * (PIECE - 1) + N_EXP * (TE - 1)) // TE) * TE
NTE = NROWS // TE

VMEM_LIMIT = 50 * 1024 * 1024


def _dot(a, b):
    return jnp.dot(a, b, preferred_element_type=F32)


def _dot_nt(a, b):
    return lax.dot_general(a, b, (((1,), (1,)), ((), ())), preferred_element_type=F32)


def _dot_tn(a, b):
    return lax.dot_general(a, b, (((0,), (0,)), ((), ())), preferred_element_type=F32)


def _mm_tn(a, b):
    return _dot_tn(a.astype(BF16), b.astype(BF16))


def _split3(x):
    x1 = x.astype(BF16)
    r = x - x1.astype(F32)
    x2 = r.astype(BF16)
    x3 = (r - x2.astype(F32)).astype(BF16)
    return x1, x2, x3


def _dot3(a, b):
    a1 = a.astype(BF16)
    a2 = (a - a1.astype(F32)).astype(BF16)
    b1 = b.astype(BF16)
    b2 = (b - b1.astype(F32)).astype(BF16)
    return _dot(a1, b1) + (_dot(a2, b1) + _dot(a1, b2))


def _dot_mask_l(mask, x):
    x1, x2, x3 = _split3(x)
    return _dot(mask, x1) + (_dot(mask, x2) + _dot(mask, x3))


def _dot_mask_r(x, mask):
    x1 = x.astype(BF16)
    x2 = (x - x1.astype(F32)).astype(BF16)
    return _dot(x1, mask) + _dot(x2, mask)


def _dot_mask_r3(x, mask):
    x1, x2, x3 = _split3(x)
    return _dot(x1, mask) + (_dot(x2, mask) + _dot(x3, mask))


def _sigmoid(x):
    return jax.nn.sigmoid(x)


def _silu(x):
    return x * jax.nn.sigmoid(x)


def _softplus(x):
    return jnp.maximum(x, 0.0) + jnp.log1p(jnp.exp(-jnp.abs(x)))


def _rms_rows(x, g):
    return x * lax.rsqrt(jnp.mean(x * x, axis=-1, keepdims=True) + EPS) * g


def _modulate(y, scale, shift):
    rows = y.shape[0]
    y3 = y.reshape(rows // MODB, MODB, y.shape[1])
    return (y3 * (1.0 + scale) + shift).reshape(y.shape)


def _gate_res(x, gate, y):
    rows = y.shape[0]
    y3 = y.reshape(rows // MODB, MODB, y.shape[1])
    return x + (gate * y3).reshape(y.shape)


def _mod_kernel(c_ref, w_ref, b_ref, o_ref):
    o_ref[...] = _dot3(_silu(c_ref[...]), w_ref[...]) + b_ref[...]


def _mod_call(c_all, ada_w, ada_b):
    tn = 768
    rows = c_all.shape[0]
    return pl.pallas_call(
        _mod_kernel,
        grid=(DEPTH, 6 * D_MODEL // tn),
        in_specs=[
            pl.BlockSpec((rows, D_MODEL), lambda l, j: (0, 0)),
            pl.BlockSpec((None, D_MODEL, tn), lambda l, j: (l, 0, j)),
            pl.BlockSpec((None, 1, tn), lambda l, j: (l, 0, j)),
        ],
        out_specs=pl.BlockSpec((None, rows, tn), lambda l, j: (l, 0, j)),
        out_shape=jax.ShapeDtypeStruct((DEPTH, rows, 6 * D_MODEL), F32),
        compiler_params=pltpu.CompilerParams(
            dimension_semantics=("arbitrary", "arbitrary"), vmem_limit_bytes=VMEM_LIMIT),
        name="adaln_mod",
    )(c_all, ada_w, ada_b.reshape(DEPTH, 1, 6 * D_MODEL))


def _mod_spec(chunk):
    return pl.BlockSpec((None, MODB, D_MODEL), lambda i: (i // NPT, 0, chunk))


def _row_spec(width):
    return pl.BlockSpec((TM, width), lambda i: (i, 0))


def _full_spec(shape):
    nd = len(shape)
    return pl.BlockSpec(shape, lambda i: (0,) * nd)


def _in_kernel(first, *refs):
    if first:
        (xp_ref, xs_ref, n1_ref, sc_ref, sh_ref, w_ref, x_ref, proj_ref, x_s) = refs
        step = pl.program_id(0)
        nlb = D_MODEL // LANES

        @pl.when(step < NPT)
        def _():
            for b in range(BATCH):
                for c in range(nlb):
                    x_s[c, pl.ds(b, TM // BATCH, stride=BATCH), :] = xp_ref[b, :, c * LANES:(c + 1) * LANES]

        @pl.when(step == NPT)
        def _():
            for c in range(nlb):
                x_s[c] = xs_ref[:, c * LANES:(c + 1) * LANES]

        x = jnp.concatenate([x_s[c] for c in range(nlb)], axis=1)
    else:
        (x1_ref, moe_ref, g2_ref, n1_ref, sc_ref, sh_ref, w_ref, x_ref, proj_ref) = refs
        x = _gate_res(x1_ref[...], g2_ref[...], moe_ref[...])
    x_ref[...] = x
    h = _modulate(_rms_rows(x, n1_ref[...]), sc_ref[...], sh_ref[...])
    proj_ref[...] = _dot(h.astype(BF16), w_ref[...])


def _in_call(first, stream_ins, mod_l, norm_g, w_in_r):
    if first:
        specs = [pl.BlockSpec((BATCH, TM // BATCH, D_MODEL), lambda i: (0, jnp.minimum(i, NPT - 1), 0)),
                 _full_spec((NS, D_MODEL))]
        scratch = [pltpu.VMEM((D_MODEL // LANES, TM, LANES), F32)]
    else:
        specs = [_row_spec(D_MODEL), _row_spec(D_MODEL), _mod_spec(5)]
        scratch = []
    specs += [_full_spec((1, D_MODEL)), _mod_spec(1), _mod_spec(0), _full_spec((D_MODEL, D_INP))]
    return pl.pallas_call(
        functools.partial(_in_kernel, first),
        grid=(NTILES,),
        in_specs=specs,
        out_specs=[_row_spec(D_MODEL), _row_spec(D_INP)],
        out_shape=[jax.ShapeDtypeStruct((NTOK, D_MODEL), F32), jax.ShapeDtypeStruct((NTOK, D_INP), F32)],
        scratch_shapes=scratch,
        compiler_params=pltpu.CompilerParams(
            dimension_semantics=("arbitrary",), vmem_limit_bytes=VMEM_LIMIT),
        name="in_proj",
    )(*stream_ins, norm_g, mod_l, mod_l, w_in_r)


def _lru_gates(xc, wa_ref, ba_ref, wx_ref, bx_ref, lam_ref):
    xb = xc.astype(BF16)
    r = _sigmoid(_dot(xb, wa_ref[...]) + ba_ref[...])
    i = _sigmoid(_dot(xb, wx_ref[...]) + bx_ref[...])
    log_a = -LRU_C * r * _softplus(-lam_ref[...])
    a = jnp.exp(log_a)
    mult = jnp.sqrt(1.0 - jnp.exp(2.0 * log_a))
    return a, mult, i * xc


def _gelu(x):
    return jax.nn.gelu(x, approximate=True)


def _gdn_prep(qkv, small, gpar_ref, ones_blk):
    qkv = _silu(qkv)
    q = qkv[:, 0:256]
    k = qkv[:, 256:512]
    v = qkv[:, 512:768]
    q = q * lax.rsqrt(_dot_mask_r(q * q, ones_blk) + EPS) * (GDN_DK ** -0.5)
    k = k * lax.rsqrt(_dot_mask_r(k * k, ones_blk) + EPS)
    beta = _sigmoid(small)
    g = -jnp.exp(gpar_ref[0:1, :]) * _softplus(small + gpar_ref[1:2, :])
    lane = lax.broadcasted_iota(jnp.int32, small.shape, 1)
    bg = jnp.where(lane < GDN_H, beta, g)
    return q, k, v, bg


def _gla_prep(proj_q, proj_k, small, wg2_ref, gbg_ref):
    pre = _dot3(small, wg2_ref[...]) + gbg_ref[...]
    gk = -_softplus(-pre) / GLA_TAU
    return proj_q * (GLA_DK ** -0.5), proj_k, gk


def _head_norm_gate(o, norm_g, gate_in, ones_blk):
    ms = _dot_mask_r(o * o, ones_blk) * (1.0 / GDN_DV)
    return o * lax.rsqrt(ms + EPS) * norm_g * _silu(gate_in)


def _mix_prompt_kernel(proj_ref, lcw_ref, lcb_ref, wa_ref, ba_ref, wx_ref, bx_ref, lam_ref,
                       gcw_ref, gpar_ref, gng_ref, wg2_ref, gbg_ref, lng_ref,
                       ones_ref, tri_ref, e2_ref, btblk_ref,
                       mix_ref, lconv_ref, lh_ref, gconv_ref, gs_ref, lst_ref,
                       exta, extb, a_s, u_s, hs_s, q_s, k_s, v_s, bg_s, og_s,
                       q2_s, k2_s, v2_s, gk_s, ol_s):
    nb = BATCH
    rows = TT * nb
    step = pl.program_id(0)

    @pl.when(step == 0)
    def _():
        exta[0:3 * nb, :] = jnp.zeros((3 * nb, LRU_W), F32)
        extb[0:3 * nb, :] = jnp.zeros((3 * nb, GDN_CONV), F32)
        lh_ref[...] = jnp.zeros_like(lh_ref)
        gs_ref[...] = jnp.zeros_like(gs_ref)
        lst_ref[...] = jnp.zeros_like(lst_ref)

    def conv(ext, x, w_ref):
        ext[pl.ds(3 * nb, rows), :] = x
        y = ext[pl.ds(0, rows), :] * w_ref[0:1, :]
        for j in range(1, CONV_W):
            y = y + ext[pl.ds(j * nb, rows), :] * w_ref[j:j + 1, :]
        tail = ext[pl.ds(rows, 3 * nb), :]
        ext[pl.ds(0, 3 * nb), :] = tail
        return y, tail

    ones_blk = ones_ref[...]

    xa, tail = conv(exta, proj_ref[:, C_AX:C_AX + LRU_W], lcw_ref)
    lconv_ref[...] = tail
    xc = xa + lcb_ref[...]
    a, mult, ix = _lru_gates(xc, wa_ref, ba_ref, wx_ref, bx_ref, lam_ref)
    rid = lax.broadcasted_iota(jnp.int32, (rows, LRU_W), 0)
    mult = jnp.where((rid < nb) & (step == 0), 1.0, mult)
    a_s[...] = a
    u_s[...] = mult * ix

    def scan_body(t, h):
        off = pl.multiple_of(t * nb, nb)
        h = a_s[pl.ds(off, nb), :] * h + u_s[pl.ds(off, nb), :]
        hs_s[pl.ds(off, nb), :] = h
        return h

    lh_ref[...] = lax.fori_loop(0, TT, scan_body, lh_ref[...], unroll=8)
    mix_ref[:, 0:LRU_W] = (hs_s[...] * _gelu(proj_ref[:, C_AG:C_AG + LRU_W])).astype(BF16)

    small = proj_ref[:, C_SM:C_SM + LANES]
    qkv, tail = conv(extb, proj_ref[:, C_QKV:C_QKV + GDN_CONV], gcw_ref)
    gconv_ref[...] = tail
    q, k, v, bg = _gdn_prep(qkv, small, gpar_ref, ones_blk)
    for j in range(2):
        ls = slice(j * LANES, (j + 1) * LANES)
        q_s[j] = q[:, ls]
        k_s[j] = k[:, ls]
        v_s[j] = v[:, ls]
    bg_s[...] = bg

    q2, k2, gk = _gla_prep(proj_ref[:, C_CQ:C_CQ + 128], proj_ref[:, C_CK:C_CK + 128],
                           small, wg2_ref, gbg_ref)
    q2_s[...] = q2
    k2_s[...] = k2
    gk_s[...] = gk
    for j in range(2):
        v2_s[j] = proj_ref[:, C_CV + j * LANES:C_CV + (j + 1) * LANES]

    ri = lax.broadcasted_iota(jnp.int32, (TT, 256), 0)
    lane = lax.broadcasted_iota(jnp.int32, (TT, 256), 1)
    cj = lane % TT
    incl = ri >= cj
    strict = ri > cj
    eye = ri == cj
    blk16m = (ri // 16) == (cj // 16)
    blk32m = (ri // 32) == (cj // 32)
    blk_causal = incl & blk16m
    hmask = [(lane // GDN_DV == h).astype(BF16) for h in range(GDN_H)]
    lane2 = lax.broadcasted_iota(jnp.int32, (TT, LANES), 1)
    hmask2 = [(lane2 // GLA_DK == h).astype(BF16) for h in range(GLA_H)]
    bdm = (lax.broadcasted_iota(jnp.int32, (256, 256), 0) // GDN_DK
           == lax.broadcasted_iota(jnp.int32, (256, 256), 1) // GDN_DV)
    bdm2 = (lax.broadcasted_iota(jnp.int32, (256, LANES), 0) // GLA_DV
            == lax.broadcasted_iota(jnp.int32, (256, LANES), 1) // GLA_DK)
    tri = tri_ref[...]
    e2 = e2_ref[...]
    btblk = btblk_ref[...]
    ones8 = jnp.ones((8, TT), BF16)
    seqs = range(nb)
    rsel = [pl.ds(b, TT, stride=nb) for b in seqs]

    def bd(x):
        xb = x.astype(BF16)
        return jnp.concatenate([xb * m for m in hmask], axis=0)

    def bd2(x):
        xb = x.astype(BF16)
        return jnp.concatenate([xb * m for m in hmask2], axis=0)

    def cat2(ref, b):
        return jnp.concatenate([ref[0, rsel[b], :], ref[1, rsel[b], :]], axis=1)

    def each(f, *lists):
        return [f(*xs) for xs in zip(*lists)]

    q = [cat2(q_s, b) for b in seqs]
    k = [cat2(k_s, b) for b in seqs]
    v = [cat2(v_s, b) for b in seqs]
    bgb = [bg_s[rsel[b], :] for b in seqs]
    gc_all = each(lambda x: _dot_mask_l(tri, x), bgb)
    both = each(lambda x, c: _dot_mask_r3(jnp.where(lane2 < GDN_H, x, c), e2), bgb, gc_all)
    beta = [x[:, 0:256] for x in both]
    gc = [x[:, 256:512] for x in both]
    gc_row = each(lambda g: _dot_mask_l(ones8, jnp.where(eye, g, 0.0))[0:1, :], gc)
    decay = each(lambda g, gr: jnp.exp(jnp.where(incl, g - gr, -jnp.inf)), gc, gc_row)
    egc = each(jnp.exp, gc)
    kb = each(lambda a, b_: a * b_, k, beta)
    bdk = each(bd, k)
    kq = each(lambda kb_, q_, m: _dot_nt(jnp.concatenate([kb_, q_], axis=0).astype(BF16), m), kb, q, bdk)
    amat = each(lambda x, d: jnp.where(strict, x[0:TT] * d, 0.0), kq, decay)
    qk = each(lambda x, d: jnp.where(incl, x[TT:] * d, 0.0), kq, decay)
    y = each(lambda a: -jnp.where(blk16m, a, 0.0), amat)
    n = y
    bdy = each(bd, y)
    for _ in range(3):
        y = each(lambda y_, m: _dot(y_.astype(BF16), m), y, bdy)
        bdy = each(bd, y)
        n = each(lambda n_, y_, m: n_ + y_ + _dot(n_.astype(BF16), m), n, y, bdy)
    t = each(lambda n_: jnp.where(eye, 1.0, 0.0) + n_, n)
    for lower in (each(lambda a: jnp.where(blk32m & ~blk16m, a, 0.0), amat),
                  each(lambda a: jnp.where(blk32m, 0.0, a), amat)):
        tl = each(lambda t_, l_: _dot(t_.astype(BF16), bd(l_)), t, lower)
        t = each(lambda t_, x: t_ - _dot(x.astype(BF16), bd(t_)), t, tl)
    uw = each(lambda t_, v_, b_, kb_, e: _dot(
        t_.astype(BF16), jnp.concatenate([bd(v_ * b_), bd(kb_ * e)], axis=1)), t, v, beta, kb, egc)
    s = [gs_ref[b] for b in seqs]
    ws = each(lambda x, q_, e, s_: _dot(
        jnp.concatenate([x[:, 256:512], q_ * e], axis=0).astype(BF16), s_.astype(BF16)), uw, q, egc, s)
    vn = each(lambda x, w_: x[:, 0:256] - w_[0:TT], uw, ws)
    o = each(lambda w_, a, v_: w_[TT:] + _dot(a.astype(BF16), bd(v_)), ws, qk, vn)
    for b in seqs:
        g_last = gc[b][TT - 1:TT, :]
        kd = k[b] * jnp.exp(g_last - gc[b])
        gs_ref[b] = s[b] * jnp.exp(g_last) + jnp.where(bdm, _mm_tn(kd, vn[b]), 0.0)
        for j in range(2):
            og_s[j, rsel[b], :] = o[b][:, j * LANES:(j + 1) * LANES]

    cum = [_dot_mask_l(btblk, gk_s[rsel[b], :]) for b in seqs]
    bc = [x[0:TT] for x in cum]
    bl = [x[TT:] for x in cum]
    qb = [q2_s[rsel[b], :] for b in seqs]
    kb2 = [k2_s[rsel[b], :] for b in seqs]
    vb = [cat2(v2_s, b) for b in seqs]
    qi = each(lambda a, c: (a * jnp.exp(c)).astype(BF16), qb, bc)
    ki = each(lambda a, c: a * jnp.exp(-c), kb2, bc)
    kst = each(lambda a, l_, c: (a * jnp.exp(l_ - c)).astype(BF16), kb2, bl, bc)
    ebl = each(jnp.exp, bl)
    sc = each(lambda a, c: jnp.where(blk_causal, _dot_nt(a, bd2(c)), 0.0), qi, ki)
    oh = each(lambda a, v_: _dot(a.astype(BF16), bd(v_)), sc, vb)
    st = [lst_ref[b] for b in seqs]
    vbb = each(lambda v_: v_.astype(BF16), vb)
    outs = [[] for _ in seqs]
    for c in range(TT // GLA_CHUNK):
        rs = slice(c * GLA_CHUNK, (c + 1) * GLA_CHUNK)
        for b in seqs:
            outs[b].append(oh[b][rs] + _dot_nt(qi[b][rs], st[b].astype(BF16)))
        upd = [jnp.where(bdm2, _dot_tn(vbb[b][rs], kst[b][rs]), 0.0) for b in seqs]
        st = [st[b] * ebl[b][c * GLA_CHUNK:c * GLA_CHUNK + 1, :] + upd[b] for b in seqs]
    for b in seqs:
        lst_ref[b] = st[b]
        ol = jnp.concatenate(outs[b], axis=0)
        for j in range(2):
            ol_s[j, rsel[b], :] = ol[:, j * LANES:(j + 1) * LANES]

    og = jnp.concatenate([og_s[0], og_s[1]], axis=1)
    out_b = _head_norm_gate(og, gng_ref[...], proj_ref[:, C_BZ:C_BZ + 256], ones_blk)
    mix_ref[:, 512:768] = out_b.astype(BF16)
    ol = jnp.concatenate([ol_s[0], ol_s[1]], axis=1)
    out_c = _head_norm_gate(ol, lng_ref[...], proj_ref[:, C_CR:C_CR + 256], ones_blk)
    mix_ref[:, 768:1024] = out_c.astype(BF16)


def _mix_consts():
    r = np.arange(TT)
    tri = (r[:, None] >= r[None, :]).astype(np.float32)
    same = (r[:, None] // GLA_CHUNK) == (r[None, :] // GLA_CHUNK)
    bt16 = (tri.astype(bool) & same).astype(np.float32)
    blk16 = same.astype(np.float32)
    c = np.arange(256)
    ones_blk = ((c[:, None] // GDN_DV) == (c[None, :] // GDN_DV)).astype(np.float32)
    e2 = np.zeros((LANES, 512), np.float32)
    for h in range(GDN_H):
        e2[h, h * GDN_DV:(h + 1) * GDN_DV] = 1.0
        e2[GDN_H + h, 256 + h * GDN_DV:256 + (h + 1) * GDN_DV] = 1.0
    btblk = np.concatenate([bt16, blk16], axis=0)
    return (jnp.asarray(ones_blk, BF16), jnp.asarray(tri, BF16),
            jnp.asarray(e2, BF16), jnp.asarray(btblk, BF16))


def _mix_prompt_call(proj, mp):
    rows = TT * BATCH
    consts = _mix_consts()
    params = (mp["lcw"], mp["lcb"], mp["wa"], mp["ba"], mp["wx"], mp["bx"], mp["lam"],
              mp["gcw"], mp["gpar"], mp["gng"], mp["wg2"], mp["gbg"], mp["lng"]) + consts
    out_shape = [
        jax.ShapeDtypeStruct((NP, D_MODEL), BF16),
        jax.ShapeDtypeStruct((3 * BATCH, LRU_W), F32),
        jax.ShapeDtypeStruct((BATCH, LRU_W), F32),
        jax.ShapeDtypeStruct((3 * BATCH, GDN_CONV), F32),
        jax.ShapeDtypeStruct((BATCH, GDN_H * GDN_DK, GDN_H * GDN_DV), F32),
        jax.ShapeDtypeStruct((BATCH, GLA_H * GLA_DV, GLA_H * GLA_DK), F32),
    ]
    out_specs = [pl.BlockSpec((rows, D_MODEL), lambda i: (i, 0))] + [
        _full_spec(s.shape) for s in out_shape[1:]]
    scratch = [
        pltpu.VMEM((rows + 3 * BATCH, LRU_W), F32),
        pltpu.VMEM((rows + 3 * BATCH, GDN_CONV), F32),
        pltpu.VMEM((rows, LRU_W), F32), pltpu.VMEM((rows, LRU_W), F32), pltpu.VMEM((rows, LRU_W), F32),
        pltpu.VMEM((2, rows, LANES), F32), pltpu.VMEM((2, rows, LANES), F32), pltpu.VMEM((2, rows, LANES), F32),
        pltpu.VMEM((rows, LANES), F32), pltpu.VMEM((2, rows, LANES), F32),
        pltpu.VMEM((rows, LANES), F32), pltpu.VMEM((rows, LANES), F32), pltpu.VMEM((2, rows, LANES), F32),
        pltpu.VMEM((rows, LANES), F32), pltpu.VMEM((2, rows, LANES), F32),
    ]
    return pl.pallas_call(
        _mix_prompt_kernel,
        grid=(SEQ // TT,),
        in_specs=[pl.BlockSpec((rows, D_INP), lambda i: (i, 0))] + [_full_spec(p.shape) for p in params],
        out_specs=out_specs,
        out_shape=out_shape,
        scratch_shapes=scratch,
        compiler_params=pltpu.CompilerParams(
            dimension_semantics=("arbitrary",), vmem_limit_bytes=VMEM_LIMIT),
        name="mix_prompt",
    )(proj, *params)


def _pair_bcast(x, p, lo_mask):
    return jnp.where(lo_mask, x[:, 2 * p:2 * p + 1], x[:, 2 * p + 1:2 * p + 2])


def _fold_pairs(acc):
    return acc[:, 0:64] + acc[:, 64:128]


def _mix_sample_kernel(p0_ref, p1_ref, p2_ref, p3_ref, lconv_in, lh_in, gconv_in, gs_in, ls_in,
                       lcw_ref, lcb_ref, wa_ref, ba_ref, wx_ref, bx_ref, lam_ref,
                       gcw_ref, gpar_ref, gng_ref, wg2_ref, gbg_ref, lng_ref, ones_ref,
                       mix_ref, lconv_ref, lh_ref, gconv_ref, gs_ref, ls_ref):
    rows = DEC_SEQ * BB
    prefs = (p0_ref, p1_ref, p2_ref, p3_ref)
    ones_blk = ones_ref[...]

    def cols(c0, width):
        return [p[:, c0:c0 + width] for p in prefs]

    def conv(prev, xs, w_ref):
        ext = [prev[j] for j in range(CONV_W - 1)] + xs
        ys = []
        for t in range(DEC_SEQ):
            y = ext[t] * w_ref[0:1, :]
            for j in range(1, CONV_W):
                y = y + ext[t + j] * w_ref[j:j + 1, :]
            ys.append(y)
        return jnp.concatenate(ys, axis=0), ext[DEC_SEQ:]

    def rows_of(x, t):
        return x[t * BB:(t + 1) * BB]

    xa, tail = conv(lconv_in, cols(C_AX, LRU_W), lcw_ref)
    for j in range(CONV_W - 1):
        lconv_ref[j] = tail[j]
    xc = xa + lcb_ref[...]
    a, mult, ix = _lru_gates(xc, wa_ref, ba_ref, wx_ref, bx_ref, lam_ref)
    u = mult * ix
    h = lh_in[...]
    hs = []
    for t in range(DEC_SEQ):
        h = rows_of(a, t) * h + rows_of(u, t)
        hs.append(h)
    lh_ref[...] = h
    ag = jnp.concatenate(cols(C_AG, LRU_W), axis=0)
    mix_a = jnp.concatenate(hs, axis=0) * _gelu(ag)

    small = jnp.concatenate(cols(C_SM, LANES), axis=0)
    qkv, tail = conv(gconv_in, cols(C_QKV, GDN_CONV), gcw_ref)
    for j in range(CONV_W - 1):
        gconv_ref[j] = tail[j]
    q, k, v, bg = _gdn_prep(qkv, small, gpar_ref, ones_blk)
    lo_mask = lax.broadcasted_iota(jnp.int32, (BB, LANES), 1) < 64
    gs_ref[...] = gs_in[...]
    hd = GDN_DK * GDN_DV
    o_heads = []
    for h in range(GDN_H):
        o_t = []
        for t in range(DEC_SEQ):
            rs = slice(t * BB, (t + 1) * BB)
            cs = slice(h * GDN_DK, (h + 1) * GDN_DK)
            eg = jnp.exp(bg[rs, GDN_H + h:GDN_H + h + 1])
            beta = bg[rs, h:h + 1]
            kt, qt, vt = k[rs, cs], q[rs, cs], v[rs, cs]
            kks = [_pair_bcast(kt, p, lo_mask) for p in range(GDN_DK // 2)]
            acc = jnp.zeros((BB, LANES), F32)
            for p in range(GDN_DK // 2):
                acc = acc + gs_ref[:, h * hd + p * LANES:h * hd + (p + 1) * LANES] * kks[p]
            vn = beta * (vt - eg * _fold_pairs(acc))
            vn2 = jnp.concatenate([vn, vn], axis=1)
            oacc = jnp.zeros((BB, LANES), F32)
            for p in range(GDN_DK // 2):
                sl = slice(h * hd + p * LANES, h * hd + (p + 1) * LANES)
                s = eg * gs_ref[:, sl] + kks[p] * vn2
                gs_ref[:, sl] = s
                oacc = oacc + s * _pair_bcast(qt, p, lo_mask)
            o_t.append(_fold_pairs(oacc))
        o_heads.append(jnp.concatenate(o_t, axis=0))
    o_b = jnp.concatenate(o_heads, axis=1)
    bz = jnp.concatenate(cols(C_BZ, 256), axis=0)
    mix_b = _head_norm_gate(o_b, gng_ref[...], bz, ones_blk)

    q2, k2, gk = _gla_prep(jnp.concatenate(cols(C_CQ, 128), axis=0),
                           jnp.concatenate(cols(C_CK, 128), axis=0), small, wg2_ref, gbg_ref)
    v2 = jnp.concatenate(cols(C_CV, 256), axis=0)
    ls_ref[...] = ls_in[...]
    hd2 = GLA_DK * GLA_DV
    o_heads = []
    for h in range(GLA_H):
        o_t = []
        for t in range(DEC_SEQ):
            rs = slice(t * BB, (t + 1) * BB)
            cs = slice(h * GLA_DK, (h + 1) * GLA_DK)
            al = jnp.exp(gk[rs, cs])
            kt, qt = k2[rs, cs], q2[rs, cs]
            vt = v2[rs, h * GLA_DV:(h + 1) * GLA_DV]
            v2x = jnp.concatenate([vt, vt], axis=1)
            oacc = jnp.zeros((BB, LANES), F32)
            for p in range(GLA_DK // 2):
                sl = slice(h * hd2 + p * LANES, h * hd2 + (p + 1) * LANES)
                s = _pair_bcast(al, p, lo_mask) * ls_ref[:, sl] + _pair_bcast(kt, p, lo_mask) * v2x
                ls_ref[:, sl] = s
                oacc = oacc + s * _pair_bcast(qt, p, lo_mask)
            o_t.append(_fold_pairs(oacc))
        o_heads.append(jnp.concatenate(o_t, axis=0))
    o_c = jnp.concatenate(o_heads, axis=1)
    cr = jnp.concatenate(cols(C_CR, 256), axis=0)
    mix_c = _head_norm_gate(o_c, lng_ref[...], cr, ones_blk)

    mix = jnp.concatenate([mix_a, mix_b, mix_c], axis=1).astype(BF16)
    for t in range(DEC_SEQ):
        mix_ref[t] = mix[t * BB:(t + 1) * BB]


def _mix_sample_call(proj, states, mp):
    lconv, lh, gconv, gs, ls = states
    ones_blk = _mix_consts()[0]
    params = (mp["lcw"], mp["lcb"], mp["wa"], mp["ba"], mp["wx"], mp["bx"], mp["lam"],
              mp["gcw"], mp["gpar"], mp["gng"], mp["wg2"], mp["gbg"], mp["lng"], ones_blk)
    nblk = DEC_BATCH // BB
    base = NP // BB

    def proj_spec(t):
        return pl.BlockSpec((BB, D_INP), lambda j: (base + t * nblk + j, 0))

    def bspec3(n, width):
        return pl.BlockSpec((n, BB, width), lambda j: (0, j, 0))

    def bspec2(width):
        return pl.BlockSpec((BB, width), lambda j: (j, 0))

    gdn_flat = GDN_H * GDN_DK * GDN_DV
    gla_flat = GLA_H * GLA_DK * GLA_DV
    state_specs = [bspec3(3, LRU_W), bspec2(LRU_W), bspec3(3, GDN_CONV), bspec2(gdn_flat), bspec2(gla_flat)]
    out_shape = [
        jax.ShapeDtypeStruct((DEC_SEQ, DEC_BATCH, D_MODEL), BF16),
        jax.ShapeDtypeStruct((3, DEC_BATCH, LRU_W), F32),
        jax.ShapeDtypeStruct((DEC_BATCH, LRU_W), F32),
        jax.ShapeDtypeStruct((3, DEC_BATCH, GDN_CONV), F32),
        jax.ShapeDtypeStruct((DEC_BATCH, gdn_flat), F32),
        jax.ShapeDtypeStruct((DEC_BATCH, gla_flat), F32),
    ]
    return pl.pallas_call(
        _mix_sample_kernel,
        grid=(nblk,),
        in_specs=[proj_spec(t) for t in range(DEC_SEQ)] + state_specs + [_full_spec(p.shape) for p in params],
        out_specs=[bspec3(DEC_SEQ, D_MODEL)] + state_specs,
        out_shape=out_shape,
        compiler_params=pltpu.CompilerParams(
            dimension_semantics=("arbitrary",), vmem_limit_bytes=VMEM_LIMIT),
        name="mix_sample",
    )(proj, proj, proj, proj, lconv, lh, gconv, gs, ls, *params)


def _out_kernel(mp_ref, ms_ref, x_ref, w_ref, g1_ref, n2_ref, sc_ref, sh_ref, rw_ref, rb_ref,
                x1_ref, h2_ref, ridx_ref, rprob_ref, tcnt_ref):
    step = pl.program_id(0)
    mix = jnp.where(step < NPT, mp_ref[...], ms_ref[...])
    x1 = _gate_res(x_ref[...], g1_ref[...], _dot(mix, w_ref[...]))
    x1_ref[...] = x1
    h2 = _modulate(_rms_rows(x1, n2_ref[...]), sc_ref[...], sh_ref[...])
    h2_ref[...] = h2
    logits = _dot3(h2, rw_ref[...]) + rb_ref[...]
    lane = lax.broadcasted_iota(jnp.int32, logits.shape, 1)
    cur = jnp.where(lane < N_EXP, logits, -jnp.inf)
    vals, idxs = [], []
    for _ in range(TOP_K):
        m = jnp.max(cur, axis=-1, keepdims=True)
        idx = jnp.min(jnp.where(cur == m, lane, LANES), axis=-1, keepdims=True)
        vals.append(m)
        idxs.append(idx)
        cur = jnp.where(lane == idx, -jnp.inf, cur)
    es = [jnp.exp(v - vals[0]) for v in vals]
    den = es[0] + es[1] + es[2] + es[3]
    ridx = jnp.zeros(logits.shape, jnp.int32)
    rprob = jnp.zeros(logits.shape, F32)
    for j in range(TOP_K):
        ridx = jnp.where(lane == j, idxs[j], ridx)
        rprob = jnp.where(lane == j, es[j] / den, rprob)
    ridx_ref[...] = ridx
    rprob_ref[...] = rprob
    onehot = jnp.zeros(logits.shape, F32)
    for idx in idxs:
        onehot = onehot + jnp.where(lane == idx, 1.0, 0.0)
    colsum = jnp.sum(onehot, axis=0, keepdims=True)
    aligned = jnp.floor((colsum + (PIECE - 1)) * (1.0 / PIECE)) * PIECE
    tcnt_ref[...] = jnp.concatenate([aligned, jnp.zeros((7, LANES), F32)], axis=0)


def _out_call(mix_p, mix_s, x, mod_l, w_out, norm_g, rw, rb):
    return pl.pallas_call(
        _out_kernel,
        grid=(NTILES,),
        in_specs=[
            pl.BlockSpec((TM, D_MODEL), lambda i: (jnp.minimum(i, NPT - 1), 0)),
            _full_spec((NS, D_MODEL)),
            _row_spec(D_MODEL),
            _full_spec((D_MODEL, D_MODEL)),
            _mod_spec(2),
            _full_spec((1, D_MODEL)),
            _mod_spec(4),
            _mod_spec(3),
            _full_spec((D_MODEL, LANES)),
            _full_spec((1, LANES)),
        ],
        out_specs=[_row_spec(D_MODEL), _row_spec(D_MODEL), _row_spec(LANES), _row_spec(LANES),
                   pl.BlockSpec((8, LANES), lambda i: (i, 0))],
        out_shape=[
            jax.ShapeDtypeStruct((NTOK, D_MODEL), F32),
            jax.ShapeDtypeStruct((NTOK, D_MODEL), F32),
            jax.ShapeDtypeStruct((NTOK, LANES), jnp.int32),
            jax.ShapeDtypeStruct((NTOK, LANES), F32),
            jax.ShapeDtypeStruct((NTILES * 8, LANES), F32),
        ],
        compiler_params=pltpu.CompilerParams(
            dimension_semantics=("arbitrary",), vmem_limit_bytes=VMEM_LIMIT),
        name="out_proj_router",
    )(mix_p, mix_s, x, w_out, mod_l, norm_g, mod_l, mod_l, rw, rb)


def _excl_lane_cumsum(row):
    r = lax.broadcasted_iota(jnp.int32, (LANES, LANES), 0)
    c = lax.broadcasted_iota(jnp.int32, (LANES, LANES), 1)
    before = jnp.where(r < c, 1.0, 0.0).astype(BF16)
    return _dot_mask_r3(jnp.broadcast_to(row, (8, LANES)), before)[0:1]


def _route_kernel(ridx_ref, tcnt_ref, total_ref, pos_ref, post_ref, tab_ref, meta_ref, cnt_s):
    i = pl.program_id(0)
    lane = lax.broadcasted_iota(jnp.int32, (TD, LANES), 1)
    ridx = ridx_ref[...]
    hits = [lane == ridx[:, k:k + 1] for k in range(TOP_K)]
    onehot = jnp.zeros((TD, LANES), F32)
    for hit in hits:
        onehot = onehot + jnp.where(hit, 1.0, 0.0)
    aligned = tcnt_ref[0:1, :]

    @pl.when(i == 0)
    def _():
        total = total_ref[...]
        padded = jnp.floor((total + (TE - 1)) * (1.0 / TE)) * TE
        pstart = _excl_lane_cumsum(padded)
        meta_ref[...] = jnp.concatenate(
            [total, pstart, pstart + padded, jnp.zeros((5, LANES), F32)], axis=0)
        cnt_s[...] = jnp.zeros_like(cnt_s)

    r = lax.broadcasted_iota(jnp.int32, (TD, TD), 0)
    c = lax.broadcasted_iota(jnp.int32, (TD, TD), 1)
    earlier = jnp.where(c < r, 1.0, 0.0).astype(BF16)
    pos = _excl_lane_cumsum(aligned) + _dot(earlier, onehot.astype(BF16))
    posk = jnp.zeros((TD, LANES), F32)
    for k, hit in enumerate(hits):
        d = jnp.sum(jnp.where(hit, pos, 0.0), axis=1, keepdims=True)
        posk = jnp.where(lane == k, d, posk)
    pos_ref[...] = posk.astype(jnp.int32)
    post_ref[...] = posk.T[0:8].astype(jnp.int32)
    pieces = aligned * (1.0 / PIECE)
    npieces = jnp.broadcast_to(jnp.sum(pieces, axis=1, keepdims=True), (1, LANES))
    tab_ref[...] = jnp.concatenate(
        [pieces, cnt_s[...], npieces, jnp.zeros((5, LANES), F32)], axis=0).astype(jnp.int32)
    cnt_s[...] += aligned


def _route_call(ridx, tcnt):
    nblk = NTOK // TD
    total = jnp.sum(tcnt.reshape(nblk, 8, LANES)[:, 0], axis=0, keepdims=True)
    return pl.pallas_call(
        _route_kernel,
        grid=(nblk,),
        in_specs=[pl.BlockSpec((TD, LANES), lambda i: (i, 0)),
                  pl.BlockSpec((8, LANES), lambda i: (i, 0)),
                  _full_spec((1, LANES))],
        out_specs=[pl.BlockSpec((TD, LANES), lambda i: (i, 0)),
                   pl.BlockSpec((8, TD), lambda i: (i, 0)),
                   pl.BlockSpec((8, LANES), lambda i: (i, 0)),
                   pl.BlockSpec((8, LANES), lambda i: (0, 0))],
        out_shape=[jax.ShapeDtypeStruct((NTOK, LANES), jnp.int32),
                   jax.ShapeDtypeStruct((nblk * 8, TD), jnp.int32),
                   jax.ShapeDtypeStruct((nblk * 8, LANES), jnp.int32),
                   jax.ShapeDtypeStruct((8, LANES), F32)],
        scratch_shapes=[pltpu.VMEM((1, LANES), F32)],
        compiler_params=pltpu.CompilerParams(
            dimension_semantics=("arbitrary",), vmem_limit_bytes=VMEM_LIMIT),
        name="route",
    )(ridx, tcnt, total)


def _for_each_piece(i, ptab_ref, btab_ref, pstart_ref, fn):
    off = jnp.int32(0)
    for e in range(N_EXP):
        npieces = ptab_ref[i * N_EXP + e]
        dst0 = pstart_ref[e] + btab_ref[i * N_EXP + e]

        def body(j, carry, off=off, dst0=dst0):
            fn(pl.multiple_of(off + j * PIECE, PIECE), pl.multiple_of(dst0 + j * PIECE, PIECE))
            return carry

        lax.fori_loop(0, npieces, body, 0)
        off = off + npieces * PIECE


def _dispatch_kernel(qtab_ref, btab_ref, ttab_ref, pstart_ref, pend_ref, cnt_ref, post_ref, h_ref, xs_ref,
                     sbuf, zero_s, sems, semz):
    i = pl.program_id(0)
    n = pl.num_programs(0)
    slot = i % 2

    def piece(s):
        def copy(src, dst):
            return pltpu.make_async_copy(sbuf.at[s, pl.ds(src, PIECE)], xs_ref.at[pl.ds(dst, PIECE)],
                                         sems.at[s])
        return copy

    def start_pieces(tile, s):
        _for_each_piece(tile, qtab_ref, btab_ref, pstart_ref, lambda src, dst: piece(s)(src, dst).start())

    def wait_pieces(tile, s):
        def body(j, carry):
            piece(s)(0, 0).wait()
            return carry
        lax.fori_loop(0, ttab_ref[tile], body, 0)

    @pl.when(i == 0)
    def _():
        zero_s[...] = jnp.zeros_like(zero_s)

        def fill(start):
            return pltpu.make_async_copy(zero_s, xs_ref.at[pl.ds(pl.multiple_of(start, TE), TE)], semz)

        for e in range(N_EXP):
            @pl.when(cnt_ref[e] > 0)
            def _():
                fill(pend_ref[e] - TE).start()
        for e in range(N_EXP):
            @pl.when(cnt_ref[e] > 0)
            def _():
                fill(pend_ref[e] - TE).wait()

        def tail_start(t, carry):
            fill(t * TE).start()
            return carry

        def tail_wait(t, carry):
            fill(t * TE).wait()
            return carry

        first_unused = pend_ref[N_EXP - 1] // TE
        lax.fori_loop(first_unused, NTE, tail_start, 0)
        lax.fori_loop(first_unused, NTE, tail_wait, 0)

    @pl.when(i >= 2)
    def _():
        wait_pieces(i - 2, slot)

    post = post_ref[...]
    prow = lax.broadcasted_iota(jnp.int32, (LP, TD), 0)
    sel = prow == post[0:1, :]
    for k in range(1, TOP_K):
        sel = sel | (prow == post[k:k + 1, :])
    onehot = jnp.where(sel, 1.0, 0.0).astype(BF16)
    sbuf[slot] = _dot(onehot, h_ref[...].astype(BF16))
    start_pieces(i, slot)

    @pl.when(i == n - 1)
    def _():
        @pl.when(i >= 1)
        def _():
            wait_pieces(i - 1, 1 - slot)
        wait_pieces(i, slot)


def _dispatch_call(qtab, btab, ttab, pstart, pend, counts, post, h2):
    grid_spec = pltpu.PrefetchScalarGridSpec(
        num_scalar_prefetch=6,
        grid=(NTOK // TD,),
        in_specs=[
            pl.BlockSpec((8, TD), lambda i, *_: (i, 0)),
            pl.BlockSpec((TD, D_MODEL), lambda i, *_: (i, 0)),
        ],
        out_specs=pl.BlockSpec(memory_space=pl.ANY),
        scratch_shapes=[pltpu.VMEM((2, LP, D_MODEL), F32), pltpu.VMEM((TE, D_MODEL), F32),
                        pltpu.SemaphoreType.DMA((2,)), pltpu.SemaphoreType.DMA(())],
    )
    return pl.pallas_call(
        _dispatch_kernel,
        grid_spec=grid_spec,
        out_shape=jax.ShapeDtypeStruct((NROWS, D_MODEL), F32),
        compiler_params=pltpu.CompilerParams(
            dimension_semantics=("arbitrary",), vmem_limit_bytes=VMEM_LIMIT),
        name="moe_dispatch",
    )(qtab, btab, ttab, pstart, pend, counts, post, h2)


def _combine_kernel(qtab_ref, btab_ref, ttab_ref, pstart_ref, pos_ref, prob_ref, ys_ref, o_ref, ybuf, sems):
    i = pl.program_id(0)
    n = pl.num_programs(0)
    slot = i % 2

    def piece(s):
        def copy(loc, src):
            return pltpu.make_async_copy(ys_ref.at[pl.ds(src, PIECE)], ybuf.at[s, pl.ds(loc, PIECE)],
                                         sems.at[s])
        return copy

    def start_pieces(tile, s):
        _for_each_piece(tile, qtab_ref, btab_ref, pstart_ref, lambda loc, src: piece(s)(loc, src).start())

    @pl.when(i == 0)
    def _():
        ybuf[...] = jnp.zeros_like(ybuf)
        start_pieces(0, 0)

    @pl.when(i + 1 < n)
    def _():
        start_pieces(i + 1, 1 - slot)

    def wait_piece(j, carry):
        piece(slot)(0, 0).wait()
        return carry

    lax.fori_loop(0, ttab_ref[i], wait_piece, 0)

    pos = pos_ref[...]
    prob = prob_ref[...]
    pcol = lax.broadcasted_iota(jnp.int32, (TD, LP), 1)
    w = jnp.zeros((TD, LP), F32)
    for k in range(TOP_K):
        w = w + jnp.where(pcol == pos[:, k:k + 1], prob[:, k:k + 1], 0.0)
    o_ref[...] = _dot(w.astype(BF16), ybuf[slot].astype(BF16))


def _combine_call(qtab, btab, ttab, pstart, pos, rprob, ys):
    grid_spec = pltpu.PrefetchScalarGridSpec(
        num_scalar_prefetch=4,
        grid=(NTOK // TD,),
        in_specs=[
            pl.BlockSpec((TD, LANES), lambda i, *_: (i, 0)),
            pl.BlockSpec((TD, LANES), lambda i, *_: (i, 0)),
            pl.BlockSpec(memory_space=pl.ANY),
        ],
        out_specs=pl.BlockSpec((TD, D_MODEL), lambda i, *_: (i, 0)),
        scratch_shapes=[pltpu.VMEM((2, LP, D_MODEL), F32), pltpu.SemaphoreType.DMA((2,))],
    )
    return pl.pallas_call(
        _combine_kernel,
        grid_spec=grid_spec,
        out_shape=jax.ShapeDtypeStruct((NTOK, D_MODEL), F32),
        compiler_params=pltpu.CompilerParams(
            dimension_semantics=("arbitrary",), vmem_limit_bytes=VMEM_LIMIT),
        name="moe_combine",
    )(qtab, btab, ttab, pstart, pos, rprob, ys)


def _ffn_kernel(cnt_ref, pend_ref, xs_ref, wg_ref, bg_ref, wu_ref, bu_ref, wd_ref, bd_ref,
                ys_ref, wg_s, wu_s, wd_s, xbuf, ybuf, sem_in, sem_out):
    e = pl.program_id(0)
    nrows = ((cnt_ref[e] + (TE - 1)) // TE) * TE
    row0 = pend_ref[e] - nrows
    nbig = nrows // TB
    nsmall = (nrows - nbig * TB) // TE

    wg_s[...] = wg_ref[...].astype(BF16)
    wu_s[...] = wu_ref[...].astype(BF16)
    wd_s[...] = wd_ref[...].astype(BF16)

    def run_tiles(nt, base, size):
        def rows(t):
            return pl.ds(pl.multiple_of(base + t * size, TE), size)

        def x_copy(t, slot):
            return pltpu.make_async_copy(xs_ref.at[rows(t)], xbuf.at[slot, pl.ds(0, size)], sem_in.at[slot])

        def y_copy(t, slot):
            return pltpu.make_async_copy(ybuf.at[slot, pl.ds(0, size)], ys_ref.at[rows(t)], sem_out.at[slot])

        @pl.when(nt > 0)
        def _():
            x_copy(0, 0).start()

        def tile(t, carry):
            slot = t % 2

            @pl.when(t + 1 < nt)
            def _():
                x_copy(t + 1, 1 - slot).start()

            x_copy(t, slot).wait()

            @pl.when(t >= 2)
            def _():
                y_copy(t - 2, slot).wait()

            x = xbuf[slot, 0:size].astype(BF16)
            y = bd_ref[...]
            for c in range(D_MODEL // FFC):
                cs = slice(c * FFC, (c + 1) * FFC)
                gate = jnp.minimum(_dot(x, wg_s[:, cs]) + bg_ref[:, cs], SW_LIMIT)
                up = jnp.clip(_dot(x, wu_s[:, cs]) + bu_ref[:, cs], -SW_LIMIT, SW_LIMIT)
                act = (up + 1.0) * gate * _sigmoid(SW_ALPHA * gate)
                y = y + _dot(act.astype(BF16), wd_s[cs, :])
            ybuf[slot, 0:size] = y
            y_copy(t, slot).start()
            return carry

        lax.fori_loop(0, nt, tile, 0)

        @pl.when(nt >= 2)
        def _():
            y_copy(nt - 2, nt % 2).wait()

        @pl.when(nt >= 1)
        def _():
            y_copy(nt - 1, (nt - 1) % 2).wait()

    run_tiles(nbig, row0, TB)
    run_tiles(nsmall, row0 + nbig * TB, TE)

    @pl.when(e == N_EXP - 1)
    def _():
        ybuf[0, 0:TE] = jnp.zeros((TE, D_MODEL), F32)

        def fill(t):
            return pltpu.make_async_copy(
                ybuf.at[0, pl.ds(0, TE)], ys_ref.at[pl.ds(pl.multiple_of(t * TE, TE), TE)], sem_out.at[0])

        def tail_start(t, carry):
            fill(t).start()
            return carry

        def tail_wait(t, carry):
            fill(t).wait()
            return carry

        first_unused = pend_ref[N_EXP - 1] // TE
        lax.fori_loop(first_unused, NTE, tail_start, 0)
        lax.fori_loop(first_unused, NTE, tail_wait, 0)


def _ffn_call(layer, counts, pend, xs, wg, bg, wu, bu, wd, bd):
    wspec = pl.BlockSpec((None, None, D_MODEL, D_MODEL), lambda e, c, p: (layer, e, 0, 0))
    bspec = pl.BlockSpec((None, None, 1, D_MODEL), lambda e, c, p: (layer, e, 0, 0))
    grid_spec = pltpu.PrefetchScalarGridSpec(
        num_scalar_prefetch=2,
        grid=(N_EXP,),
        in_specs=[pl.BlockSpec(memory_space=pl.ANY), wspec, bspec, wspec, bspec, wspec, bspec],
        out_specs=pl.BlockSpec(memory_space=pl.ANY),
        scratch_shapes=[pltpu.VMEM((D_MODEL, D_MODEL), BF16)] * 3 + [
            pltpu.VMEM((2, TB, D_MODEL), F32), pltpu.VMEM((2, TB, D_MODEL), F32),
            pltpu.SemaphoreType.DMA((2,)), pltpu.SemaphoreType.DMA((2,))],
    )
    b4 = lambda b: b.reshape(DEPTH, N_EXP, 1, D_MODEL)
    return pl.pallas_call(
        _ffn_kernel,
        grid_spec=grid_spec,
        out_shape=jax.ShapeDtypeStruct((NROWS, D_MODEL), F32),
        compiler_params=pltpu.CompilerParams(
            dimension_semantics=("arbitrary",), vmem_limit_bytes=VMEM_LIMIT),
        name="expert_ffn",
    )(counts, pend, xs, wg, b4(bg), wu, b4(bu), wd, b4(bd))


def _moe(layer, h2, ridx, rprob, tcnt, wg, bg, wu, bu, wd, bd):
    pos, post, tab, meta = _route_call(ridx, tcnt)
    counts = meta[0, :N_EXP].astype(jnp.int32)
    pstart = meta[1, :N_EXP].astype(jnp.int32)
    pend = meta[2, :N_EXP].astype(jnp.int32)
    tab = tab.reshape(NTOK // TD, 8, LANES)
    qtab = tab[:, 0, :N_EXP].reshape(-1)
    btab = tab[:, 1, :N_EXP].reshape(-1)
    ttab = tab[:, 2, 0]
    xs = _dispatch_call(qtab, btab, ttab, pstart, pend, counts, post, h2)
    ys = _ffn_call(layer, counts, pend, xs, wg, bg, wu, bu, wd, bd)
    return _combine_call(qtab, btab, ttab, pstart, pos, rprob, ys)


def _final_kernel(x1_ref, moe_ref, g2_ref, ng_ref, yp_ref, ys_ref, y_s):
    step = pl.program_id(0)
    x = _gate_res(x1_ref[...], g2_ref[...], moe_ref[...])
    y = _rms_rows(x, ng_ref[...])

    @pl.when(step < NPT)
    def _():
        nlb = D_MODEL // LANES
        for c in range(nlb):
            y_s[c] = y[:, c * LANES:(c + 1) * LANES]
        for b in range(BATCH):
            yp_ref[b] = jnp.concatenate(
                [y_s[c, pl.ds(b, TM // BATCH, stride=BATCH), :] for c in range(nlb)], axis=1)

    @pl.when(step == NPT)
    def _():
        ys_ref[...] = y


def _final_call(x1, moe, mod_l, norm_g):
    tt = TM // BATCH
    return pl.pallas_call(
        _final_kernel,
        grid=(NTILES,),
        in_specs=[_row_spec(D_MODEL), _row_spec(D_MODEL), _mod_spec(5), _full_spec((1, D_MODEL))],
        out_specs=[pl.BlockSpec((BATCH, tt, D_MODEL), lambda i: (0, jnp.minimum(i, NPT - 1), 0)),
                   _full_spec((NS, D_MODEL))],
        out_shape=[jax.ShapeDtypeStruct((BATCH, SEQ, D_MODEL), F32),
                   jax.ShapeDtypeStruct((NS, D_MODEL), F32)],
        scratch_shapes=[pltpu.VMEM((D_MODEL // LANES, TM, LANES), F32)],
        compiler_params=pltpu.CompilerParams(
            dimension_semantics=("arbitrary",), vmem_limit_bytes=VMEM_LIMIT),
        name="final_norm",
    )(x1, moe, mod_l, norm_g)


def _block_diag(w):
    n, d, e = w.shape
    eye = jnp.eye(n, dtype=w.dtype)
    return (eye[:, None, :, None] * w[:, :, None, :]).reshape(n * d, n * e)


def _pad_lanes(v, offset):
    out = jnp.zeros((1, LANES), F32)
    return out.at[0, offset:offset + v.shape[0]].set(v)


def _mixer_params(l, w_in, lru_conv_w, lru_conv_b, lru_wa, lru_ba, lru_wx, lru_bx, lru_lambda,
                  gdn_conv_w, gdn_a_log, gdn_dt_bias, gdn_norm_g, gla_wg2, gla_bg, gla_norm_g):
    w = w_in[l]
    w_in_r = jnp.concatenate(
        [w[:, 0:2048], w[:, 2056:2824], w[:, 2048:2056], w[:, 2824:2840],
         jnp.zeros((D_MODEL, D_INP - 2840), F32)], axis=1).astype(BF16)
    row = lambda v: v.reshape(1, -1)
    mp = dict(
        lcw=lru_conv_w[l], lcb=row(lru_conv_b[l]),
        wa=_block_diag(lru_wa[l]).astype(BF16), ba=row(lru_ba[l]),
        wx=_block_diag(lru_wx[l]).astype(BF16), bx=row(lru_bx[l]),
        lam=row(lru_lambda[l]),
        gcw=gdn_conv_w[l],
        gpar=jnp.concatenate([_pad_lanes(gdn_a_log[l], GDN_H), _pad_lanes(gdn_dt_bias[l], GDN_H)], axis=0),
        gng=row(jnp.tile(gdn_norm_g[l], GDN_H)),
        wg2=jnp.zeros((LANES, LANES), F32).at[2 * GDN_H:2 * GDN_H + GLA_RANK].set(gla_wg2[l]),
        gbg=row(gla_bg[l]),
        lng=row(jnp.tile(gla_norm_g[l], GLA_H)),
    )
    return w_in_r, mp


def kernel(x_prompt, x_sample, state_lru_conv, state_lru_h, state_gdn_conv, state_gdn_S, state_gla_S, c_prompt, c_sample, ada_w, ada_b, norm1_g, norm2_g, w_in, lru_conv_w, lru_conv_b, lru_wa, lru_ba, lru_wx, lru_bx, lru_lambda, gdn_conv_w, gdn_a_log, gdn_dt_bias, gdn_norm_g, gla_wg2, gla_bg, gla_norm_g, w_out, router_w, router_b, exp_w_gate, exp_b_gate, exp_w_up, exp_b_up, exp_w_down, exp_b_down, final_norm_g):
    xs_t = x_sample.transpose(1, 0, 2).reshape(NS, D_MODEL)
    mod = _mod_call(jnp.concatenate([c_prompt, c_sample], axis=0), ada_w, ada_b)
    mod = jnp.stack([jnp.tile(mod[:, :BATCH], (1, MODB // BATCH, 1)), mod[:, BATCH:]], axis=1)

    p_states, s_states = [], []
    x = moe = None
    for l in range(DEPTH):
        w_in_r, mp = _mixer_params(l, w_in, lru_conv_w, lru_conv_b, lru_wa, lru_ba, lru_wx, lru_bx,
                                   lru_lambda, gdn_conv_w, gdn_a_log, gdn_dt_bias, gdn_norm_g,
                                   gla_wg2, gla_bg, gla_norm_g)
        stream_ins = (x_prompt, xs_t) if l == 0 else (x, moe, mod[l - 1])
        x, proj = _in_call(l == 0, stream_ins, mod[l], norm1_g[l].reshape(1, D_MODEL), w_in_r)
        mix_p, p_lconv, p_lh, p_gconv, p_gs, p_lst = _mix_prompt_call(proj, mp)
        states = (state_lru_conv[l].transpose(1, 0, 2), state_lru_h[l],
                  state_gdn_conv[l].transpose(1, 0, 2),
                  state_gdn_S[l].reshape(DEC_BATCH, -1), state_gla_S[l].reshape(DEC_BATCH, -1))
        mix_s, s_lconv, s_lh, s_gconv, s_gs, s_ls = _mix_sample_call(proj, states, mp)
        rw = jnp.zeros((D_MODEL, LANES), F32).at[:, :N_EXP].set(router_w[l])
        rb = jnp.zeros((1, LANES), F32).at[0, :N_EXP].set(router_b[l])
        x, h2, ridx, rprob, tcnt = _out_call(mix_p, mix_s.reshape(NS, D_MODEL), x, mod[l],
                                             w_out[l].astype(BF16), norm2_g[l].reshape(1, D_MODEL), rw, rb)
        moe = _moe(l, h2, ridx, rprob, tcnt, exp_w_gate, exp_b_gate, exp_w_up, exp_b_up,
                   exp_w_down, exp_b_down)
        p_gs = jnp.stack([p_gs[:, h * GDN_DK:(h + 1) * GDN_DK, h * GDN_DV:(h + 1) * GDN_DV]
                          for h in range(GDN_H)], axis=1)
        p_lst = jnp.stack([p_lst[:, h * GLA_DV:(h + 1) * GLA_DV, h * GLA_DK:(h + 1) * GLA_DK]
                           for h in range(GLA_H)], axis=1)
        p_states.append((p_lconv.reshape(3, BATCH, LRU_W).transpose(1, 0, 2), p_lh,
                         p_gconv.reshape(3, BATCH, GDN_CONV).transpose(1, 0, 2), p_gs,
                         p_lst.transpose(0, 1, 3, 2)))
        s_states.append((s_lconv.transpose(1, 0, 2), s_lh, s_gconv.transpose(1, 0, 2),
                         s_gs.reshape(DEC_BATCH, GDN_H, GDN_DK, GDN_DV),
                         s_ls.reshape(DEC_BATCH, GLA_H, GLA_DK, GLA_DV)))
    y_prompt, y_s = _final_call(x, moe, mod[DEPTH - 1], final_norm_g.reshape(1, D_MODEL))
    y_sample = y_s.reshape(DEC_SEQ, DEC_BATCH, D_MODEL).transpose(1, 0, 2)
    ps = [jnp.stack([s[j] for s in p_states]) for j in range(5)]
    ss = [jnp.stack([s[j] for s in s_states]) for j in range(5)]
    return (y_prompt, y_sample, *ps, *ss)
```

```python
import functools

import numpy as np
import jax
import jax.numpy as jnp
from jax import lax
from jax.experimental import pallas as pl
from jax.experimental.pallas import tpu as pltpu

F32 = jnp.float32
BF16 = jnp.bfloat16

D_MODEL = 1024
BATCH = 8
SEQ = 2048
DEPTH = 2
DEC_BATCH = 128
DEC_SEQ = 4
CONV_W = 4
LRU_W = 512
LRU_BLOCKS = 8
LRU_C = 8.0
GDN_H = 4
GDN_DK = 64
GDN_DV = 64
GDN_CONV = GDN_H * (2 * GDN_DK + GDN_DV)
GLA_H = 4
GLA_DK = 32
GLA_DV = 64
GLA_RANK = 16
GLA_TAU = 16.0
GLA_CHUNK = 16
N_EXP = 32
TOP_K = 4
SW_LIMIT = 7.0
SW_ALPHA = 1.702
EPS = 1e-6

NP = BATCH * SEQ
NS = DEC_BATCH * DEC_SEQ
NTOK = NP + NS
TM = 512
NPT = NP // TM
NTILES = NTOK // TM
LANES = 128
MODB = 128

C_AX, C_AG, C_QKV, C_BZ = 0, 512, 1024, 1792
C_CQ, C_CK, C_CV, C_CR, C_SM = 2048, 2176, 2304, 2560, 2816
D_INP = 2944

TT = 64
BB = 32
TE = 256
TB = 768
FFC = 512
TD = TM
PIECE = 8
LP = 2304
NA = NTOK * TOP_K
NROWS = -(-(NA + (NTOK // TD) * N_EXP * (PIECE - 1) + N_EXP * (TE - 1)) // TE) * TE
NTE = NROWS // TE

VMEM_LIMIT = 50 * 1024 * 1024


def _dot(a, b):
    return jnp.dot(a, b, preferred_element_type=F32)


def _dot_nt(a, b):
    return lax.dot_general(a, b, (((1,), (1,)), ((), ())), preferred_element_type=F32)


def _dot_tn(a, b):
    return lax.dot_general(a, b, (((0,), (0,)), ((), ())), preferred_element_type=F32)


def _mm_tn(a, b):
    return _dot_tn(a.astype(BF16), b.astype(BF16))


def _split3(x):
    x1 = x.astype(BF16)
    r = x - x1.astype(F32)
    x2 = r.astype(BF16)
    x3 = (r - x2.astype(F32)).astype(BF16)
    return x1, x2, x3


def _dot3(a, b):
    a1 = a.astype(BF16)
    a2 = (a - a1.astype(F32)).astype(BF16)
    b1 = b.astype(BF16)
    b2 = (b - b1.astype(F32)).astype(BF16)
    return _dot(a1, b1) + (_dot(a2, b1) + _dot(a1, b2))


def _dot_mask_l(mask, x):
    x1, x2, x3 = _split3(x)
    return _dot(mask, x1) + (_dot(mask, x2) + _dot(mask, x3))


def _dot_mask_r(x, mask):
    x1 = x.astype(BF16)
    x2 = (x - x1.astype(F32)).astype(BF16)
    return _dot(x1, mask) + _dot(x2, mask)


def _dot_mask_r3(x, mask):
    x1, x2, x3 = _split3(x)
    return _dot(x1, mask) + (_dot(x2, mask) + _dot(x3, mask))


def _sigmoid(x):
    return jax.nn.sigmoid(x)


def _silu(x):
    return x * jax.nn.sigmoid(x)


def _softplus(x):
    return jnp.maximum(x, 0.0) + jnp.log1p(jnp.exp(-jnp.abs(x)))


def _rms_rows(x, g):
    return x * lax.rsqrt(jnp.mean(x * x, axis=-1, keepdims=True) + EPS) * g


def _modulate(y, scale, shift):
    rows = y.shape[0]
    y3 = y.reshape(rows // MODB, MODB, y.shape[1])
    return (y3 * (1.0 + scale) + shift).reshape(y.shape)


def _gate_res(x, gate, y):
    rows = y.shape[0]
    y3 = y.reshape(rows // MODB, MODB, y.shape[1])
    return x + (gate * y3).reshape(y.shape)


def _mod_kernel(c_ref, w_ref, b_ref, o_ref):
    o_ref[...] = _dot3(_silu(c_ref[...]), w_ref[...]) + b_ref[...]


def _mod_call(c_all, ada_w, ada_b):
    tn = 768
    rows = c_all.shape[0]
    return pl.pallas_call(
        _mod_kernel,
        grid=(DEPTH, 6 * D_MODEL // tn),
        in_specs=[
            pl.BlockSpec((rows, D_MODEL), lambda l, j: (0, 0)),
            pl.BlockSpec((None, D_MODEL, tn), lambda l, j: (l, 0, j)),
            pl.BlockSpec((None, 1, tn), lambda l, j: (l, 0, j)),
        ],
        out_specs=pl.BlockSpec((None, rows, tn), lambda l, j: (l, 0, j)),
        out_shape=jax.ShapeDtypeStruct((DEPTH, rows, 6 * D_MODEL), F32),
        compiler_params=pltpu.CompilerParams(
            dimension_semantics=("arbitrary", "arbitrary"), vmem_limit_bytes=VMEM_LIMIT),
        name="adaln_mod",
    )(c_all, ada_w, ada_b.reshape(DEPTH, 1, 6 * D_MODEL))


def _mod_spec(chunk):
    return pl.BlockSpec((None, MODB, D_MODEL), lambda i: (i // NPT, 0, chunk))


def _row_spec(width):
    return pl.BlockSpec((TM, width), lambda i: (i, 0))


def _full_spec(shape):
    nd = len(shape)
    return pl.BlockSpec(shape, lambda i: (0,) * nd)


def _in_kernel(first, *refs):
    if first:
        (xp_ref, xs_ref, n1_ref, sc_ref, sh_ref, w_ref, x_ref, proj_ref, x_s) = refs
        step = pl.program_id(0)
        nlb = D_MODEL // LANES

        @pl.when(step < NPT)
        def _():
            for b in range(BATCH):
                for c in range(nlb):
                    x_s[c, pl.ds(b, TM // BATCH, stride=BATCH), :] = xp_ref[b, :, c * LANES:(c + 1) * LANES]

        @pl.when(step == NPT)
        def _():
            for c in range(nlb):
                x_s[c] = xs_ref[:, c * LANES:(c + 1) * LANES]

        x = jnp.concatenate([x_s[c] for c in range(nlb)], axis=1)
    else:
        (x1_ref, moe_ref, g2_ref, n1_ref, sc_ref, sh_ref, w_ref, x_ref, proj_ref) = refs
        x = _gate_res(x1_ref[...], g2_ref[...], moe_ref[...])
    x_ref[...] = x
    h = _modulate(_rms_rows(x, n1_ref[...]), sc_ref[...], sh_ref[...])
    proj_ref[...] = _dot(h.astype(BF16), w_ref[...])


def _in_call(first, stream_ins, mod_l, norm_g, w_in_r):
    if first:
        specs = [pl.BlockSpec((BATCH, TM // BATCH, D_MODEL), lambda i: (0, jnp.minimum(i, NPT - 1), 0)),
                 _full_spec((NS, D_MODEL))]
        scratch = [pltpu.VMEM((D_MODEL // LANES, TM, LANES), F32)]
    else:
        specs = [_row_spec(D_MODEL), _row_spec(D_MODEL), _mod_spec(5)]
        scratch = []
    specs += [_full_spec((1, D_MODEL)), _mod_spec(1), _mod_spec(0), _full_spec((D_MODEL, D_INP))]
    return pl.pallas_call(
        functools.partial(_in_kernel, first),
        grid=(NTILES,),
        in_specs=specs,
        out_specs=[_row_spec(D_MODEL), _row_spec(D_INP)],
        out_shape=[jax.ShapeDtypeStruct((NTOK, D_MODEL), F32), jax.ShapeDtypeStruct((NTOK, D_INP), F32)],
        scratch_shapes=scratch,
        compiler_params=pltpu.CompilerParams(
            dimension_semantics=("arbitrary",), vmem_limit_bytes=VMEM_LIMIT),
        name="in_proj",
    )(*stream_ins, norm_g, mod_l, mod_l, w_in_r)


def _lru_gates(xc, wa_ref, ba_ref, wx_ref, bx_ref, lam_ref):
    xb = xc.astype(BF16)
    r = _sigmoid(_dot(xb, wa_ref[...]) + ba_ref[...])
    i = _sigmoid(_dot(xb, wx_ref[...]) + bx_ref[...])
    log_a = -LRU_C * r * _softplus(-lam_ref[...])
    a = jnp.exp(log_a)
    mult = jnp.sqrt(1.0 - jnp.exp(2.0 * log_a))
    return a, mult, i * xc


def _gelu(x):
    return jax.nn.gelu(x, approximate=True)


def _gdn_prep(qkv, small, gpar_ref, ones_blk):
    qkv = _silu(qkv)
    q = qkv[:, 0:256]
    k = qkv[:, 256:512]
    v = qkv[:, 512:768]
    q = q * lax.rsqrt(_dot_mask_r(q * q, ones_blk) + EPS) * (GDN_DK ** -0.5)
    k = k * lax.rsqrt(_dot_mask_r(k * k, ones_blk) + EPS)
    beta = _sigmoid(small)
    g = -jnp.exp(gpar_ref[0:1, :]) * _softplus(small + gpar_ref[1:2, :])
    lane = lax.broadcasted_iota(jnp.int32, small.shape, 1)
    bg = jnp.where(lane < GDN_H, beta, g)
    return q, k, v, bg


def _gla_prep(proj_q, proj_k, small, wg2_ref, gbg_ref):
    pre = _dot3(small, wg2_ref[...]) + gbg_ref[...]
    gk = -_softplus(-pre) / GLA_TAU
    return proj_q * (GLA_DK ** -0.5), proj_k, gk


def _head_norm_gate(o, norm_g, gate_in, ones_blk):
    ms = _dot_mask_r(o * o, ones_blk) * (1.0 / GDN_DV)
    return o * lax.rsqrt(ms + EPS) * norm_g * _silu(gate_in)


def _mix_prompt_kernel(proj_ref, lcw_ref, lcb_ref, wa_ref, ba_ref, wx_ref, bx_ref, lam_ref,
                       gcw_ref, gpar_ref, gng_ref, wg2_ref, gbg_ref, lng_ref,
                       ones_ref, tri_ref, e2_ref, btblk_ref,
                       mix_ref, lconv_ref, lh_ref, gconv_ref, gs_ref, lst_ref,
                       exta, extb, a_s, u_s, hs_s, q_s, k_s, v_s, bg_s, og_s,
                       q2_s, k2_s, v2_s, gk_s, ol_s):
    nb = BATCH
    rows = TT * nb
    step = pl.program_id(0)

    @pl.when(step == 0)
    def _():
        exta[0:3 * nb, :] = jnp.zeros((3 * nb, LRU_W), F32)
        extb[0:3 * nb, :] = jnp.zeros((3 * nb, GDN_CONV), F32)
        lh_ref[...] = jnp.zeros_like(lh_ref)
        gs_ref[...] = jnp.zeros_like(gs_ref)
        lst_ref[...] = jnp.zeros_like(lst_ref)

    def conv(ext, x, w_ref):
        ext[pl.ds(3 * nb, rows), :] = x
        y = ext[pl.ds(0, rows), :] * w_ref[0:1, :]
        for j in range(1, CONV_W):
            y = y + ext[pl.ds(j * nb, rows), :] * w_ref[j:j + 1, :]
        tail = ext[pl.ds(rows, 3 * nb), :]
        ext[pl.ds(0, 3 * nb), :] = tail
        return y, tail

    ones_blk = ones_ref[...]

    xa, tail = conv(exta, proj_ref[:, C_AX:C_AX + LRU_W], lcw_ref)
    lconv_ref[...] = tail
    xc = xa + lcb_ref[...]
    a, mult, ix = _lru_gates(xc, wa_ref, ba_ref, wx_ref, bx_ref, lam_ref)
    rid = lax.broadcasted_iota(jnp.int32, (rows, LRU_W), 0)
    mult = jnp.where((rid < nb) & (step == 0), 1.0, mult)
    a_s[...] = a
    u_s[...] = mult * ix

    def scan_body(t, h):
        off = pl.multiple_of(t * nb, nb)
        h = a_s[pl.ds(off, nb), :] * h + u_s[pl.ds(off, nb), :]
        hs_s[pl.ds(off, nb), :] = h
        return h

    lh_ref[...] = lax.fori_loop(0, TT, scan_body, lh_ref[...], unroll=8)
    mix_ref[:, 0:LRU_W] = (hs_s[...] * _gelu(proj_ref[:, C_AG:C_AG + LRU_W])).astype(BF16)

    small = proj_ref[:, C_SM:C_SM + LANES]
    qkv, tail = conv(extb, proj_ref[:, C_QKV:C_QKV + GDN_CONV], gcw_ref)
    gconv_ref[...] = tail
    q, k, v, bg = _gdn_prep(qkv, small, gpar_ref, ones_blk)
    for j in range(2):
        ls = slice(j * LANES, (j + 1) * LANES)
        q_s[j] = q[:, ls]
        k_s[j] = k[:, ls]
        v_s[j] = v[:, ls]
    bg_s[...] = bg

    q2, k2, gk = _gla_prep(proj_ref[:, C_CQ:C_CQ + 128], proj_ref[:, C_CK:C_CK + 128],
                           small, wg2_ref, gbg_ref)
    q2_s[...] = q2
    k2_s[...] = k2
    gk_s[...] = gk
    for j in range(2):
        v2_s[j] = proj_ref[:, C_CV + j * LANES:C_CV + (j + 1) * LANES]

    ri = lax.broadcasted_iota(jnp.int32, (TT, 256), 0)
    lane = lax.broadcasted_iota(jnp.int32, (TT, 256), 1)
    cj = lane % TT
    incl = ri >= cj
    strict = ri > cj
    eye = ri == cj
    blk16m = (ri // 16) == (cj // 16)
    blk32m = (ri // 32) == (cj // 32)
    blk_causal = incl & blk16m
    hmask = [(lane // GDN_DV == h).astype(BF16) for h in range(GDN_H)]
    lane2 = lax.broadcasted_iota(jnp.int32, (TT, LANES), 1)
    hmask2 = [(lane2 // GLA_DK == h).astype(BF16) for h in range(GLA_H)]
    bdm = (lax.broadcasted_iota(jnp.int32, (256, 256), 0) // GDN_DK
           == lax.broadcasted_iota(jnp.int32, (256, 256), 1) // GDN_DV)
    bdm2 = (lax.broadcasted_iota(jnp.int32, (256, LANES), 0) // GLA_DV
            == lax.broadcasted_iota(jnp.int32, (256, LANES), 1) // GLA_DK)
    tri = tri_ref[...]
    e2 = e2_ref[...]
    btblk = btblk_ref[...]
    ones8 = jnp.ones((8, TT), BF16)
    seqs = range(nb)
    rsel = [pl.ds(b, TT, stride=nb) for b in seqs]

    def bd(x):
        xb = x.astype(BF16)
        return jnp.concatenate([xb * m for m in hmask], axis=0)

    def bd2(x):
        xb = x.astype(BF16)
        return jnp.concatenate([xb * m for m in hmask2], axis=0)

    def cat2(ref, b):
        return jnp.concatenate([ref[0, rsel[b], :], ref[1, rsel[b], :]], axis=1)

    def each(f, *lists):
        return [f(*xs) for xs in zip(*lists)]

    q = [cat2(q_s, b) for b in seqs]
    k = [cat2(k_s, b) for b in seqs]
    v = [cat2(v_s, b) for b in seqs]
    bgb = [bg_s[rsel[b], :] for b in seqs]
    gc_all = each(lambda x: _dot_mask_l(tri, x), bgb)
    both = each(lambda x, c: _dot_mask_r3(jnp.where(lane2 < GDN_H, x, c), e2), bgb, gc_all)
    beta = [x[:, 0:256] for x in both]
    gc = [x[:, 256:512] for x in both]
    gc_row = each(lambda g: _dot_mask_l(ones8, jnp.where(eye, g, 0.0))[0:1, :], gc)
    decay = each(lambda g, gr: jnp.exp(jnp.where(incl, g - gr, -jnp.inf)), gc, gc_row)
    egc = each(jnp.exp, gc)
    kb = each(lambda a, b_: a * b_, k, beta)
    bdk = each(bd, k)
    kq = each(lambda kb_, q_, m: _dot_nt(jnp.concatenate([kb_, q_], axis=0).astype(BF16), m), kb, q, bdk)
    amat = each(lambda x, d: jnp.where(strict, x[0:TT] * d, 0.0), kq, decay)
    qk = each(lambda x, d: jnp.where(incl, x[TT:] * d, 0.0), kq, decay)
    y = each(lambda a: -jnp.where(blk16m, a, 0.0), amat)
    n = y
    bdy = each(bd, y)
    for _ in range(3):
        y = each(lambda y_, m: _dot(y_.astype(BF16), m), y, bdy)
        bdy = each(bd, y)
        n = each(lambda n_, y_, m: n_ + y_ + _dot(n_.astype(BF16), m), n, y, bdy)
    t = each(lambda n_: jnp.where(eye, 1.0, 0.0) + n_, n)
    for lower in (each(lambda a: jnp.where(blk32m & ~blk16m, a, 0.0), amat),
                  each(lambda a: jnp.where(blk32m, 0.0, a), amat)):
        tl = each(lambda t_, l_: _dot(t_.astype(BF16), bd(l_)), t, lower)
        t = each(lambda t_, x: t_ - _dot(x.astype(BF16), bd(t_)), t, tl)
    uw = each(lambda t_, v_, b_, kb_, e: _dot(
        t_.astype(BF16), jnp.concatenate([bd(v_ * b_), bd(kb_ * e)], axis=1)), t, v, beta, kb, egc)
    s = [gs_ref[b] for b in seqs]
    ws = each(lambda x, q_, e, s_: _dot(
        jnp.concatenate([x[:, 256:512], q_ * e], axis=0).astype(BF16), s_.astype(BF16)), uw, q, egc, s)
    vn = each(lambda x, w_: x[:, 0:256] - w_[0:TT], uw, ws)
    o = each(lambda w_, a, v_: w_[TT:] + _dot(a.astype(BF16), bd(v_)), ws, qk, vn)
    for b in seqs:
        g_last = gc[b][TT - 1:TT, :]
        kd = k[b] * jnp.exp(g_last - gc[b])
        gs_ref[b] = s[b] * jnp.exp(g_last) + jnp.where(bdm, _mm_tn(kd, vn[b]), 0.0)
        for j in range(2):
            og_s[j, rsel[b], :] = o[b][:, j * LANES:(j + 1) * LANES]

    cum = [_dot_mask_l(btblk, gk_s[rsel[b], :]) for b in seqs]
    bc = [x[0:TT] for x in cum]
    bl = [x[TT:] for x in cum]
    qb = [q2_s[rsel[b], :] for b in seqs]
    kb2 = [k2_s[rsel[b], :] for b in seqs]
    vb = [cat2(v2_s, b) for b in seqs]
    qi = each(lambda a, c: (a * jnp.exp(c)).astype(BF16), qb, bc)
    ki = each(lambda a, c: a * jnp.exp(-c), kb2, bc)
    kst = each(lambda a, l_, c: (a * jnp.exp(l_ - c)).astype(BF16), kb2, bl, bc)
    ebl = each(jnp.exp, bl)
    sc = each(lambda a, c: jnp.where(blk_causal, _dot_nt(a, bd2(c)), 0.0), qi, ki)
    oh = each(lambda a, v_: _dot(a.astype(BF16), bd(v_)), sc, vb)
    st = [lst_ref[b] for b in seqs]
    vbb = each(lambda v_: v_.astype(BF16), vb)
    outs = [[] for _ in seqs]
    for c in range(TT // GLA_CHUNK):
        rs = slice(c * GLA_CHUNK, (c + 1) * GLA_CHUNK)
        for b in seqs:
            outs[b].append(oh[b][rs] + _dot_nt(qi[b][rs], st[b].astype(BF16)))
        upd = [jnp.where(bdm2, _dot_tn(vbb[b][rs], kst[b][rs]), 0.0) for b in seqs]
        st = [st[b] * ebl[b][c * GLA_CHUNK:c * GLA_CHUNK + 1, :] + upd[b] for b in seqs]
    for b in seqs:
        lst_ref[b] = st[b]
        ol = jnp.concatenate(outs[b], axis=0)
        for j in range(2):
            ol_s[j, rsel[b], :] = ol[:, j * LANES:(j + 1) * LANES]

    og = jnp.concatenate([og_s[0], og_s[1]], axis=1)
    out_b = _head_norm_gate(og, gng_ref[...], proj_ref[:, C_BZ:C_BZ + 256], ones_blk)
    mix_ref[:, 512:768] = out_b.astype(BF16)
    ol = jnp.concatenate([ol_s[0], ol_s[1]], axis=1)
    out_c = _head_norm_gate(ol, lng_ref[...], proj_ref[:, C_CR:C_CR + 256], ones_blk)
    mix_ref[:, 768:1024] = out_c.astype(BF16)


def _mix_consts():
    r = np.arange(TT)
    tri = (r[:, None] >= r[None, :]).astype(np.float32)
    same = (r[:, None] // GLA_CHUNK) == (r[None, :] // GLA_CHUNK)
    bt16 = (tri.astype(bool) & same).astype(np.float32)
    blk16 = same.astype(np.float32)
    c = np.arange(256)
    ones_blk = ((c[:, None] // GDN_DV) == (c[None, :] // GDN_DV)).astype(np.float32)
    e2 = np.zeros((LANES, 512), np.float32)
    for h in range(GDN_H):
        e2[h, h * GDN_DV:(h + 1) * GDN_DV] = 1.0
        e2[GDN_H + h, 256 + h * GDN_DV:256 + (h + 1) * GDN_DV] = 1.0
    btblk = np.concatenate([bt16, blk16], axis=0)
    return (jnp.asarray(ones_blk, BF16), jnp.asarray(tri, BF16),
            jnp.asarray(e2, BF16), jnp.asarray(btblk, BF16))


def _mix_prompt_call(proj, mp):
    rows = TT * BATCH
    consts = _mix_consts()
    params = (mp["lcw"], mp["lcb"], mp["wa"], mp["ba"], mp["wx"], mp["bx"], mp["lam"],
              mp["gcw"], mp["gpar"], mp["gng"], mp["wg2"], mp["gbg"], mp["lng"]) + consts
    out_shape = [
        jax.ShapeDtypeStruct((NP, D_MODEL), BF16),
        jax.ShapeDtypeStruct((3 * BATCH, LRU_W), F32),
        jax.ShapeDtypeStruct((BATCH, LRU_W), F32),
        jax.ShapeDtypeStruct((3 * BATCH, GDN_CONV), F32),
        jax.ShapeDtypeStruct((BATCH, GDN_H * GDN_DK, GDN_H * GDN_DV), F32),
        jax.ShapeDtypeStruct((BATCH, GLA_H * GLA_DV, GLA_H * GLA_DK), F32),
    ]
    out_specs = [pl.BlockSpec((rows, D_MODEL), lambda i: (i, 0))] + [
        _full_spec(s.shape) for s in out_shape[1:]]
    scratch = [
        pltpu.VMEM((rows + 3 * BATCH, LRU_W), F32),
        pltpu.VMEM((rows + 3 * BATCH, GDN_CONV), F32),
        pltpu.VMEM((rows, LRU_W), F32), pltpu.VMEM((rows, LRU_W), F32), pltpu.VMEM((rows, LRU_W), F32),
        pltpu.VMEM((2, rows, LANES), F32), pltpu.VMEM((2, rows, LANES), F32), pltpu.VMEM((2, rows, LANES), F32),
        pltpu.VMEM((rows, LANES), F32), pltpu.VMEM((2, rows, LANES), F32),
        pltpu.VMEM((rows, LANES), F32), pltpu.VMEM((rows, LANES), F32), pltpu.VMEM((2, rows, LANES), F32),
        pltpu.VMEM((rows, LANES), F32), pltpu.VMEM((2, rows, LANES), F32),
    ]
    return pl.pallas_call(
        _mix_prompt_kernel,
        grid=(SEQ // TT,),
        in_specs=[pl.BlockSpec((rows, D_INP), lambda i: (i, 0))] + [_full_spec(p.shape) for p in params],
        out_specs=out_specs,
        out_shape=out_shape,
        scratch_shapes=scratch,
        compiler_params=pltpu.CompilerParams(
            dimension_semantics=("arbitrary",), vmem_limit_bytes=VMEM_LIMIT),
        name="mix_prompt",
    )(proj, *params)


def _pair_bcast(x, p, lo_mask):
    return jnp.where(lo_mask, x[:, 2 * p:2 * p + 1], x[:, 2 * p + 1:2 * p + 2])


def _fold_pairs(acc):
    return acc[:, 0:64] + acc[:, 64:128]


def _mix_sample_kernel(p0_ref, p1_ref, p2_ref, p3_ref, lconv_in, lh_in, gconv_in, gs_in, ls_in,
                       lcw_ref, lcb_ref, wa_ref, ba_ref, wx_ref, bx_ref, lam_ref,
                       gcw_ref, gpar_ref, gng_ref, wg2_ref, gbg_ref, lng_ref, ones_ref,
                       mix_ref, lconv_ref, lh_ref, gconv_ref, gs_ref, ls_ref):
    rows = DEC_SEQ * BB
    prefs = (p0_ref, p1_ref, p2_ref, p3_ref)
    ones_blk = ones_ref[...]

    def cols(c0, width):
        return [p[:, c0:c0 + width] for p in prefs]

    def conv(prev, xs, w_ref):
        ext = [prev[j] for j in range(CONV_W - 1)] + xs
        ys = []
        for t in range(DEC_SEQ):
            y = ext[t] * w_ref[0:1, :]
            for j in range(1, CONV_W):
                y = y + ext[t + j] * w_ref[j:j + 1, :]
            ys.append(y)
        return jnp.concatenate(ys, axis=0), ext[DEC_SEQ:]

    def rows_of(x, t):
        return x[t * BB:(t + 1) * BB]

    xa, tail = conv(lconv_in, cols(C_AX, LRU_W), lcw_ref)
    for j in range(CONV_W - 1):
        lconv_ref[j] = tail[j]
    xc = xa + lcb_ref[...]
    a, mult, ix = _lru_gates(xc, wa_ref, ba_ref, wx_ref, bx_ref, lam_ref)
    u = mult * ix
    h = lh_in[...]
    hs = []
    for t in range(DEC_SEQ):
        h = rows_of(a, t) * h + rows_of(u, t)
        hs.append(h)
    lh_ref[...] = h
    ag = jnp.concatenate(cols(C_AG, LRU_W), axis=0)
    mix_a = jnp.concatenate(hs, axis=0) * _gelu(ag)

    small = jnp.concatenate(cols(C_SM, LANES), axis=0)
    qkv, tail = conv(gconv_in, cols(C_QKV, GDN_CONV), gcw_ref)
    for j in range(CONV_W - 1):
        gconv_ref[j] = tail[j]
    q, k, v, bg = _gdn_prep(qkv, small, gpar_ref, ones_blk)
    lo_mask = lax.broadcasted_iota(jnp.int32, (BB, LANES), 1) < 64
    gs_ref[...] = gs_in[...]
    hd = GDN_DK * GDN_DV
    o_heads = []
    for h in range(GDN_H):
        o_t = []
        for t in range(DEC_SEQ):
            rs = slice(t * BB, (t + 1) * BB)
            cs = slice(h * GDN_DK, (h + 1) * GDN_DK)
            eg = jnp.exp(bg[rs, GDN_H + h:GDN_H + h + 1])
            beta = bg[rs, h:h + 1]
            kt, qt, vt = k[rs, cs], q[rs, cs], v[rs, cs]
            kks = [_pair_bcast(kt, p, lo_mask) for p in range(GDN_DK // 2)]
            acc = jnp.zeros((BB, LANES), F32)
            for p in range(GDN_DK // 2):
                acc = acc + gs_ref[:, h * hd + p * LANES:h * hd + (p + 1) * LANES] * kks[p]
            vn = beta * (vt - eg * _fold_pairs(acc))
            vn2 = jnp.concatenate([vn, vn], axis=1)
            oacc = jnp.zeros((BB, LANES), F32)
            for p in range(GDN_DK // 2):
                sl = slice(h * hd + p * LANES, h * hd + (p + 1) * LANES)
                s = eg * gs_ref[:, sl] + kks[p] * vn2
                gs_ref[:, sl] = s
                oacc = oacc + s * _pair_bcast(qt, p, lo_mask)
            o_t.append(_fold_pairs(oacc))
        o_heads.append(jnp.concatenate(o_t, axis=0))
    o_b = jnp.concatenate(o_heads, axis=1)
    bz = jnp.concatenate(cols(C_BZ, 256), axis=0)
    mix_b = _head_norm_gate(o_b, gng_ref[...], bz, ones_blk)

    q2, k2, gk = _gla_prep(jnp.concatenate(cols(C_CQ, 128), axis=0),
                           jnp.concatenate(cols(C_CK, 128), axis=0), small, wg2_ref, gbg_ref)
    v2 = jnp.concatenate(cols(C_CV, 256), axis=0)
    ls_ref[...] = ls_in[...]
    hd2 = GLA_DK * GLA_DV
    o_heads = []
    for h in range(GLA_H):
        o_t = []
        for t in range(DEC_SEQ):
            rs = slice(t * BB, (t + 1) * BB)
            cs = slice(h * GLA_DK, (h + 1) * GLA_DK)
            al = jnp.exp(gk[rs, cs])
            kt, qt = k2[rs, cs], q2[rs, cs]
            vt = v2[rs, h * GLA_DV:(h + 1) * GLA_DV]
            v2x = jnp.concatenate([vt, vt], axis=1)
            oacc = jnp.zeros((BB, LANES), F32)
            for p in range(GLA_DK // 2):
                sl = slice(h * hd2 + p * LANES, h * hd2 + (p + 1) * LANES)
                s = _pair_bcast(al, p, lo_mask) * ls_ref[:, sl] + _pair_bcast(kt, p, lo_mask) * v2x
                ls_ref[:, sl] = s
                oacc = oacc + s * _pair_bcast(qt, p, lo_mask)
            o_t.append(_fold_pairs(oacc))
        o_heads.append(jnp.concatenate(o_t, axis=0))
    o_c = jnp.concatenate(o_heads, axis=1)
    cr = jnp.concatenate(cols(C_CR, 256), axis=0)
    mix_c = _head_norm_gate(o_c, lng_ref[...], cr, ones_blk)

    mix = jnp.concatenate([mix_a, mix_b, mix_c], axis=1).astype(BF16)
    for t in range(DEC_SEQ):
        mix_ref[t] = mix[t * BB:(t + 1) * BB]


def _mix_sample_call(proj, states, mp):
    lconv, lh, gconv, gs, ls = states
    ones_blk = _mix_consts()[0]
    params = (mp["lcw"], mp["lcb"], mp["wa"], mp["ba"], mp["wx"], mp["bx"], mp["lam"],
              mp["gcw"], mp["gpar"], mp["gng"], mp["wg2"], mp["gbg"], mp["lng"], ones_blk)
    nblk = DEC_BATCH // BB
    base = NP // BB

    def proj_spec(t):
        return pl.BlockSpec((BB, D_INP), lambda j: (base + t * nblk + j, 0))

    def bspec3(n, width):
        return pl.BlockSpec((n, BB, width), lambda j: (0, j, 0))

    def bspec2(width):
        return pl.BlockSpec((BB, width), lambda j: (j, 0))

    gdn_flat = GDN_H * GDN_DK * GDN_DV
    gla_flat = GLA_H * GLA_DK * GLA_DV
    state_specs = [bspec3(3, LRU_W), bspec2(LRU_W), bspec3(3, GDN_CONV), bspec2(gdn_flat), bspec2(gla_flat)]
    out_shape = [
        jax.ShapeDtypeStruct((DEC_SEQ, DEC_BATCH, D_MODEL), BF16),
        jax.ShapeDtypeStruct((3, DEC_BATCH, LRU_W), F32),
        jax.ShapeDtypeStruct((DEC_BATCH, LRU_W), F32),
        jax.ShapeDtypeStruct((3, DEC_BATCH, GDN_CONV), F32),
        jax.ShapeDtypeStruct((DEC_BATCH, gdn_flat), F32),
        jax.ShapeDtypeStruct((DEC_BATCH, gla_flat), F32),
    ]
    return pl.pallas_call(
        _mix_sample_kernel,
        grid=(nblk,),
        in_specs=[proj_spec(t) for t in range(DEC_SEQ)] + state_specs + [_full_spec(p.shape) for p in params],
        out_specs=[bspec3(DEC_SEQ, D_MODEL)] + state_specs,
        out_shape=out_shape,
        compiler_params=pltpu.CompilerParams(
            dimension_semantics=("arbitrary",), vmem_limit_bytes=VMEM_LIMIT),
        name="mix_sample",
    )(proj, proj, proj, proj, lconv, lh, gconv, gs, ls, *params)


def _out_kernel(mp_ref, ms_ref, x_ref, w_ref, g1_ref, n2_ref, sc_ref, sh_ref, rw_ref, rb_ref,
                x1_ref, h2_ref, ridx_ref, rprob_ref, tcnt_ref):
    step = pl.program_id(0)
    mix = jnp.where(step < NPT, mp_ref[...], ms_ref[...])
    x1 = _gate_res(x_ref[...], g1_ref[...], _dot(mix, w_ref[...]))
    x1_ref[...] = x1
    h2 = _modulate(_rms_rows(x1, n2_ref[...]), sc_ref[...], sh_ref[...])
    h2_ref[...] = h2
    logits = _dot3(h2, rw_ref[...]) + rb_ref[...]
    lane = lax.broadcasted_iota(jnp.int32, logits.shape, 1)
    cur = jnp.where(lane < N_EXP, logits, -jnp.inf)
    vals, idxs = [], []
    for _ in range(TOP_K):
        m = jnp.max(cur, axis=-1, keepdims=True)
        idx = jnp.min(jnp.where(cur == m, lane, LANES), axis=-1, keepdims=True)
        vals.append(m)
        idxs.append(idx)
        cur = jnp.where(lane == idx, -jnp.inf, cur)
    es = [jnp.exp(v - vals[0]) for v in vals]
    den = es[0] + es[1] + es[2] + es[3]
    ridx = jnp.zeros(logits.shape, jnp.int32)
    rprob = jnp.zeros(logits.shape, F32)
    for j in range(TOP_K):
        ridx = jnp.where(lane == j, idxs[j], ridx)
        rprob = jnp.where(lane == j, es[j] / den, rprob)
    ridx_ref[...] = ridx
    rprob_ref[...] = rprob
    onehot = jnp.zeros(logits.shape, F32)
    for idx in idxs:
        onehot = onehot + jnp.where(lane == idx, 1.0, 0.0)
    colsum = jnp.sum(onehot, axis=0, keepdims=True)
    aligned = jnp.floor((colsum + (PIECE - 1)) * (1.0 / PIECE)) * PIECE
    tcnt_ref[...] = jnp.concatenate([aligned, jnp.zeros((7, LANES), F32)], axis=0)


def _out_call(mix_p, mix_s, x, mod_l, w_out, norm_g, rw, rb):
    return pl.pallas_call(
        _out_kernel,
        grid=(NTILES,),
        in_specs=[
            pl.BlockSpec((TM, D_MODEL), lambda i: (jnp.minimum(i, NPT - 1), 0)),
            _full_spec((NS, D_MODEL)),
            _row_spec(D_MODEL),
            _full_spec((D_MODEL, D_MODEL)),
            _mod_spec(2),
            _full_spec((1, D_MODEL)),
            _mod_spec(4),
            _mod_spec(3),
            _full_spec((D_MODEL, LANES)),
            _full_spec((1, LANES)),
        ],
        out_specs=[_row_spec(D_MODEL), _row_spec(D_MODEL), _row_spec(LANES), _row_spec(LANES),
                   pl.BlockSpec((8, LANES), lambda i: (i, 0))],
        out_shape=[
            jax.ShapeDtypeStruct((NTOK, D_MODEL), F32),
            jax.ShapeDtypeStruct((NTOK, D_MODEL), F32),
            jax.ShapeDtypeStruct((NTOK, LANES), jnp.int32),
            jax.ShapeDtypeStruct((NTOK, LANES), F32),
            jax.ShapeDtypeStruct((NTILES * 8, LANES), F32),
        ],
        compiler_params=pltpu.CompilerParams(
            dimension_semantics=("arbitrary",), vmem_limit_bytes=VMEM_LIMIT),
        name="out_proj_router",
    )(mix_p, mix_s, x, w_out, mod_l, norm_g, mod_l, mod_l, rw, rb)


def _excl_lane_cumsum(row):
    r = lax.broadcasted_iota(jnp.int32, (LANES, LANES), 0)
    c = lax.broadcasted_iota(jnp.int32, (LANES, LANES), 1)
    before = jnp.where(r < c, 1.0, 0.0).astype(BF16)
    return _dot_mask_r3(jnp.broadcast_to(row, (8, LANES)), before)[0:1]


def _route_kernel(ridx_ref, tcnt_ref, total_ref, pos_ref, post_ref, tab_ref, meta_ref, cnt_s):
    i = pl.program_id(0)
    lane = lax.broadcasted_iota(jnp.int32, (TD, LANES), 1)
    ridx = ridx_ref[...]
    hits = [lane == ridx[:, k:k + 1] for k in range(TOP_K)]
    onehot = jnp.zeros((TD, LANES), F32)
    for hit in hits:
        onehot = onehot + jnp.where(hit, 1.0, 0.0)
    aligned = tcnt_ref[0:1, :]

    @pl.when(i == 0)
    def _():
        total = total_ref[...]
        padded = jnp.floor((total + (TE - 1)) * (1.0 / TE)) * TE
        pstart = _excl_lane_cumsum(padded)
        meta_ref[...] = jnp.concatenate(
            [total, pstart, pstart + padded, jnp.zeros((5, LANES), F32)], axis=0)
        cnt_s[...] = jnp.zeros_like(cnt_s)

    r = lax.broadcasted_iota(jnp.int32, (TD, TD), 0)
    c = lax.broadcasted_iota(jnp.int32, (TD, TD), 1)
    earlier = jnp.where(c < r, 1.0, 0.0).astype(BF16)
    pos = _excl_lane_cumsum(aligned) + _dot(earlier, onehot.astype(BF16))
    posk = jnp.zeros((TD, LANES), F32)
    for k, hit in enumerate(hits):
        d = jnp.sum(jnp.where(hit, pos, 0.0), axis=1, keepdims=True)
        posk = jnp.where(lane == k, d, posk)
    pos_ref[...] = posk.astype(jnp.int32)
    post_ref[...] = posk.T[0:8].astype(jnp.int32)
    pieces = aligned * (1.0 / PIECE)
    npieces = jnp.broadcast_to(jnp.sum(pieces, axis=1, keepdims=True), (1, LANES))
    tab_ref[...] = jnp.concatenate(
        [pieces, cnt_s[...], npieces, jnp.zeros((5, LANES), F32)], axis=0).astype(jnp.int32)
    cnt_s[...] += aligned


def _route_call(ridx, tcnt):
    nblk = NTOK // TD
    total = jnp.sum(tcnt.reshape(nblk, 8, LANES)[:, 0], axis=0, keepdims=True)
    return pl.pallas_call(
        _route_kernel,
        grid=(nblk,),
        in_specs=[pl.BlockSpec((TD, LANES), lambda i: (i, 0)),
                  pl.BlockSpec((8, LANES), lambda i: (i, 0)),
                  _full_spec((1, LANES))],
        out_specs=[pl.BlockSpec((TD, LANES), lambda i: (i, 0)),
                   pl.BlockSpec((8, TD), lambda i: (i, 0)),
                   pl.BlockSpec((8, LANES), lambda i: (i, 0)),
                   pl.BlockSpec((8, LANES), lambda i: (0, 0))],
        out_shape=[jax.ShapeDtypeStruct((NTOK, LANES), jnp.int32),
                   jax.ShapeDtypeStruct((nblk * 8, TD), jnp.int32),
                   jax.ShapeDtypeStruct((nblk * 8, LANES), jnp.int32),
                   jax.ShapeDtypeStruct((8, LANES), F32)],
        scratch_shapes=[pltpu.VMEM((1, LANES), F32)],
        compiler_params=pltpu.CompilerParams(
            dimension_semantics=("arbitrary",), vmem_limit_bytes=VMEM_LIMIT),
        name="route",
    )(ridx, tcnt, total)


def _for_each_piece(i, ptab_ref, btab_ref, pstart_ref, fn):
    off = jnp.int32(0)
    for e in range(N_EXP):
        npieces = ptab_ref[i * N_EXP + e]
        dst0 = pstart_ref[e] + btab_ref[i * N_EXP + e]

        def body(j, carry, off=off, dst0=dst0):
            fn(pl.multiple_of(off + j * PIECE, PIECE), pl.multiple_of(dst0 + j * PIECE, PIECE))
            return carry

        lax.fori_loop(0, npieces, body, 0)
        off = off + npieces * PIECE


def _dispatch_kernel(qtab_ref, btab_ref, ttab_ref, pstart_ref, pend_ref, cnt_ref, post_ref, h_ref, xs_ref,
                     sbuf, zero_s, sems, semz):
    i = pl.program_id(0)
    n = pl.num_programs(0)
    slot = i % 2

    def piece(s):
        def copy(src, dst):
            return pltpu.make_async_copy(sbuf.at[s, pl.ds(src, PIECE)], xs_ref.at[pl.ds(dst, PIECE)],
                                         sems.at[s])
        return copy

    def start_pieces(tile, s):
        _for_each_piece(tile, qtab_ref, btab_ref, pstart_ref, lambda src, dst: piece(s)(src, dst).start())

    def wait_pieces(tile, s):
        def body(j, carry):
            piece(s)(0, 0).wait()
            return carry
        lax.fori_loop(0, ttab_ref[tile], body, 0)

    @pl.when(i == 0)
    def _():
        zero_s[...] = jnp.zeros_like(zero_s)

        def fill(start):
            return pltpu.make_async_copy(zero_s, xs_ref.at[pl.ds(pl.multiple_of(start, TE), TE)], semz)

        for e in range(N_EXP):
            @pl.when(cnt_ref[e] > 0)
            def _():
                fill(pend_ref[e] - TE).start()
        for e in range(N_EXP):
            @pl.when(cnt_ref[e] > 0)
            def _():
                fill(pend_ref[e] - TE).wait()

        def tail_start(t, carry):
            fill(t * TE).start()
            return carry

        def tail_wait(t, carry):
            fill(t * TE).wait()
            return carry

        first_unused = pend_ref[N_EXP - 1] // TE
        lax.fori_loop(first_unused, NTE, tail_start, 0)
        lax.fori_loop(first_unused, NTE, tail_wait, 0)

    @pl.when(i >= 2)
    def _():
        wait_pieces(i - 2, slot)

    post = post_ref[...]
    prow = lax.broadcasted_iota(jnp.int32, (LP, TD), 0)
    sel = prow == post[0:1, :]
    for k in range(1, TOP_K):
        sel = sel | (prow == post[k:k + 1, :])
    onehot = jnp.where(sel, 1.0, 0.0).astype(BF16)
    sbuf[slot] = _dot(onehot, h_ref[...].astype(BF16))
    start_pieces(i, slot)

    @pl.when(i == n - 1)
    def _():
        @pl.when(i >= 1)
        def _():
            wait_pieces(i - 1, 1 - slot)
        wait_pieces(i, slot)


def _dispatch_call(qtab, btab, ttab, pstart, pend, counts, post, h2):
    grid_spec = pltpu.PrefetchScalarGridSpec(
        num_scalar_prefetch=6,
        grid=(NTOK // TD,),
        in_specs=[
            pl.BlockSpec((8, TD), lambda i, *_: (i, 0)),
            pl.BlockSpec((TD, D_MODEL), lambda i, *_: (i, 0)),
        ],
        out_specs=pl.BlockSpec(memory_space=pl.ANY),
        scratch_shapes=[pltpu.VMEM((2, LP, D_MODEL), F32), pltpu.VMEM((TE, D_MODEL), F32),
                        pltpu.SemaphoreType.DMA((2,)), pltpu.SemaphoreType.DMA(())],
    )
    return pl.pallas_call(
        _dispatch_kernel,
        grid_spec=grid_spec,
        out_shape=jax.ShapeDtypeStruct((NROWS, D_MODEL), F32),
        compiler_params=pltpu.CompilerParams(
            dimension_semantics=("arbitrary",), vmem_limit_bytes=VMEM_LIMIT),
        name="moe_dispatch",
    )(qtab, btab, ttab, pstart, pend, counts, post, h2)


def _combine_kernel(qtab_ref, btab_ref, ttab_ref, pstart_ref, pos_ref, prob_ref, ys_ref, o_ref, ybuf, sems):
    i = pl.program_id(0)
    n = pl.num_programs(0)
    slot = i % 2

    def piece(s):
        def copy(loc, src):
            return pltpu.make_async_copy(ys_ref.at[pl.ds(src, PIECE)], ybuf.at[s, pl.ds(loc, PIECE)],
                                         sems.at[s])
        return copy

    def start_pieces(tile, s):
        _for_each_piece(tile, qtab_ref, btab_ref, pstart_ref, lambda loc, src: piece(s)(loc, src).start())

    @pl.when(i == 0)
    def _():
        ybuf[...] = jnp.zeros_like(ybuf)
        start_pieces(0, 0)

    @pl.when(i + 1 < n)
    def _():
        start_pieces(i + 1, 1 - slot)

    def wait_piece(j, carry):
        piece(slot)(0, 0).wait()
        return carry

    lax.fori_loop(0, ttab_ref[i], wait_piece, 0)

    pos = pos_ref[...]
    prob = prob_ref[...]
    pcol = lax.broadcasted_iota(jnp.int32, (TD, LP), 1)
    w = jnp.zeros((TD, LP), F32)
    for k in range(TOP_K):
        w = w + jnp.where(pcol == pos[:, k:k + 1], prob[:, k:k + 1], 0.0)
    o_ref[...] = _dot(w.astype(BF16), ybuf[slot].astype(BF16))


def _combine_call(qtab, btab, ttab, pstart, pos, rprob, ys):
    grid_spec = pltpu.PrefetchScalarGridSpec(
        num_scalar_prefetch=4,
        grid=(NTOK // TD,),
        in_specs=[
            pl.BlockSpec((TD, LANES), lambda i, *_: (i, 0)),
            pl.BlockSpec((TD, LANES), lambda i, *_: (i, 0)),
            pl.BlockSpec(memory_space=pl.ANY),
        ],
        out_specs=pl.BlockSpec((TD, D_MODEL), lambda i, *_: (i, 0)),
        scratch_shapes=[pltpu.VMEM((2, LP, D_MODEL), F32), pltpu.SemaphoreType.DMA((2,))],
    )
    return pl.pallas_call(
        _combine_kernel,
        grid_spec=grid_spec,
        out_shape=jax.ShapeDtypeStruct((NTOK, D_MODEL), F32),
        compiler_params=pltpu.CompilerParams(
            dimension_semantics=("arbitrary",), vmem_limit_bytes=VMEM_LIMIT),
        name="moe_combine",
    )(qtab, btab, ttab, pstart, pos, rprob, ys)


def _ffn_kernel(cnt_ref, pend_ref, xs_ref, wg_ref, bg_ref, wu_ref, bu_ref, wd_ref, bd_ref,
                ys_ref, wg_s, wu_s, wd_s, xbuf, ybuf, sem_in, sem_out):
    e = pl.program_id(0)
    nrows = ((cnt_ref[e] + (TE - 1)) // TE) * TE
    row0 = pend_ref[e] - nrows
    nbig = nrows // TB
    nsmall = (nrows - nbig * TB) // TE

    def cast_weights():
        wg_s[...] = wg_ref[...].astype(BF16)
        wu_s[...] = wu_ref[...].astype(BF16)
        wd_s[...] = wd_ref[...].astype(BF16)

    def run_tiles(nt, base, size, after_first_start=None):
        def rows(t):
            return pl.ds(pl.multiple_of(base + t * size, TE), size)

        def x_copy(t, slot):
            return pltpu.make_async_copy(xs_ref.at[rows(t)], xbuf.at[slot, pl.ds(0, size)], sem_in.at[slot])

        def y_copy(t, slot):
            return pltpu.make_async_copy(ybuf.at[slot, pl.ds(0, size)], ys_ref.at[rows(t)], sem_out.at[slot])

        @pl.when(nt > 0)
        def _():
            x_copy(0, 0).start()

        if after_first_start is not None:
            after_first_start()

        def tile(t, carry):
            slot = t % 2

            @pl.when(t + 1 < nt)
            def _():
                x_copy(t + 1, 1 - slot).start()

            x_copy(t, slot).wait()

            @pl.when(t >= 2)
            def _():
                y_copy(t - 2, slot).wait()

            x = xbuf[slot, 0:size].astype(BF16)
            y = bd_ref[...]
            for c in range(D_MODEL // FFC):
                cs = slice(c * FFC, (c + 1) * FFC)
                gate = jnp.minimum(_dot(x, wg_s[:, cs]) + bg_ref[:, cs], SW_LIMIT)
                up = jnp.clip(_dot(x, wu_s[:, cs]) + bu_ref[:, cs], -SW_LIMIT, SW_LIMIT)
                act = (up + 1.0) * gate * _sigmoid(SW_ALPHA * gate)
                y = y + _dot(act.astype(BF16), wd_s[cs, :])
            ybuf[slot, 0:size] = y
            y_copy(t, slot).start()
            return carry

        lax.fori_loop(0, nt, tile, 0)

        @pl.when(nt >= 2)
        def _():
            y_copy(nt - 2, nt % 2).wait()

        @pl.when(nt >= 1)
        def _():
            y_copy(nt - 1, (nt - 1) % 2).wait()

    run_tiles(nbig, row0, TB, after_first_start=cast_weights)
    run_tiles(nsmall, row0 + nbig * TB, TE)

    @pl.when(e == N_EXP - 1)
    def _():
        ybuf[0, 0:TE] = jnp.zeros((TE, D_MODEL), F32)

        def fill(t):
            return pltpu.make_async_copy(
                ybuf.at[0, pl.ds(0, TE)], ys_ref.at[pl.ds(pl.multiple_of(t * TE, TE), TE)], sem_out.at[0])

        def tail_start(t, carry):
            fill(t).start()
            return carry

        def tail_wait(t, carry):
            fill(t).wait()
            return carry

        first_unused = pend_ref[N_EXP - 1] // TE
        lax.fori_loop(first_unused, NTE, tail_start, 0)
        lax.fori_loop(first_unused, NTE, tail_wait, 0)


def _ffn_call(layer, counts, pend, xs, wg, bg, wu, bu, wd, bd):
    wspec = pl.BlockSpec((None, None, D_MODEL, D_MODEL), lambda e, c, p: (layer, e, 0, 0))
    bspec = pl.BlockSpec((None, None, 1, D_MODEL), lambda e, c, p: (layer, e, 0, 0))
    grid_spec = pltpu.PrefetchScalarGridSpec(
        num_scalar_prefetch=2,
        grid=(N_EXP,),
        in_specs=[pl.BlockSpec(memory_space=pl.ANY), wspec, bspec, wspec, bspec, wspec, bspec],
        out_specs=pl.BlockSpec(memory_space=pl.ANY),
        scratch_shapes=[pltpu.VMEM((D_MODEL, D_MODEL), BF16)] * 3 + [
            pltpu.VMEM((2, TB, D_MODEL), F32), pltpu.VMEM((2, TB, D_MODEL), F32),
            pltpu.SemaphoreType.DMA((2,)), pltpu.SemaphoreType.DMA((2,))],
    )
    b4 = lambda b: b.reshape(DEPTH, N_EXP, 1, D_MODEL)
    return pl.pallas_call(
        _ffn_kernel,
        grid_spec=grid_spec,
        out_shape=jax.ShapeDtypeStruct((NROWS, D_MODEL), F32),
        compiler_params=pltpu.CompilerParams(
            dimension_semantics=("arbitrary",), vmem_limit_bytes=VMEM_LIMIT),
        name="expert_ffn",
    )(counts, pend, xs, wg, b4(bg), wu, b4(bu), wd, b4(bd))


def _moe(layer, h2, ridx, rprob, tcnt, wg, bg, wu, bu, wd, bd):
    pos, post, tab, meta = _route_call(ridx, tcnt)
    counts = meta[0, :N_EXP].astype(jnp.int32)
    pstart = meta[1, :N_EXP].astype(jnp.int32)
    pend = meta[2, :N_EXP].astype(jnp.int32)
    tab = tab.reshape(NTOK // TD, 8, LANES)
    qtab = tab[:, 0, :N_EXP].reshape(-1)
    btab = tab[:, 1, :N_EXP].reshape(-1)
    ttab = tab[:, 2, 0]
    xs = _dispatch_call(qtab, btab, ttab, pstart, pend, counts, post, h2)
    ys = _ffn_call(layer, counts, pend, xs, wg, bg, wu, bu, wd, bd)
    return _combine_call(qtab, btab, ttab, pstart, pos, rprob, ys)


def _final_kernel(x1_ref, moe_ref, g2_ref, ng_ref, yp_ref, ys_ref, y_s):
    step = pl.program_id(0)
    x = _gate_res(x1_ref[...], g2_ref[...], moe_ref[...])
    y = _rms_rows(x, ng_ref[...])

    @pl.when(step < NPT)
    def _():
        nlb = D_MODEL // LANES
        for c in range(nlb):
            y_s[c] = y[:, c * LANES:(c + 1) * LANES]
        for b in range(BATCH):
            yp_ref[b] = jnp.concatenate(
                [y_s[c, pl.ds(b, TM // BATCH, stride=BATCH), :] for c in range(nlb)], axis=1)

    @pl.when(step == NPT)
    def _():
        ys_ref[...] = y


def _final_call(x1, moe, mod_l, norm_g):
    tt = TM // BATCH
    return pl.pallas_call(
        _final_kernel,
        grid=(NTILES,),
        in_specs=[_row_spec(D_MODEL), _row_spec(D_MODEL), _mod_spec(5), _full_spec((1, D_MODEL))],
        out_specs=[pl.BlockSpec((BATCH, tt, D_MODEL), lambda i: (0, jnp.minimum(i, NPT - 1), 0)),
                   _full_spec((NS, D_MODEL))],
        out_shape=[jax.ShapeDtypeStruct((BATCH, SEQ, D_MODEL), F32),
                   jax.ShapeDtypeStruct((NS, D_MODEL), F32)],
        scratch_shapes=[pltpu.VMEM((D_MODEL // LANES, TM, LANES), F32)],
        compiler_params=pltpu.CompilerParams(
            dimension_semantics=("arbitrary",), vmem_limit_bytes=VMEM_LIMIT),
        name="final_norm",
    )(x1, moe, mod_l, norm_g)


def _block_diag(w):
    n, d, e = w.shape
    eye = jnp.eye(n, dtype=w.dtype)
    return (eye[:, None, :, None] * w[:, :, None, :]).reshape(n * d, n * e)


def _pad_lanes(v, offset):
    out = jnp.zeros((1, LANES), F32)
    return out.at[0, offset:offset + v.shape[0]].set(v)


def _mixer_params(l, w_in, lru_conv_w, lru_conv_b, lru_wa, lru_ba, lru_wx, lru_bx, lru_lambda,
                  gdn_conv_w, gdn_a_log, gdn_dt_bias, gdn_norm_g, gla_wg2, gla_bg, gla_norm_g):
    w = w_in[l]
    w_in_r = jnp.concatenate(
        [w[:, 0:2048], w[:, 2056:2824], w[:, 2048:2056], w[:, 2824:2840],
         jnp.zeros((D_MODEL, D_INP - 2840), F32)], axis=1).astype(BF16)
    row = lambda v: v.reshape(1, -1)
    mp = dict(
        lcw=lru_conv_w[l], lcb=row(lru_conv_b[l]),
        wa=_block_diag(lru_wa[l]).astype(BF16), ba=row(lru_ba[l]),
        wx=_block_diag(lru_wx[l]).astype(BF16), bx=row(lru_bx[l]),
        lam=row(lru_lambda[l]),
        gcw=gdn_conv_w[l],
        gpar=jnp.concatenate([_pad_lanes(gdn_a_log[l], GDN_H), _pad_lanes(gdn_dt_bias[l], GDN_H)], axis=0),
        gng=row(jnp.tile(gdn_norm_g[l], GDN_H)),
        wg2=jnp.zeros((LANES, LANES), F32).at[2 * GDN_H:2 * GDN_H + GLA_RANK].set(gla_wg2[l]),
        gbg=row(gla_bg[l]),
        lng=row(jnp.tile(gla_norm_g[l], GLA_H)),
    )
    return w_in_r, mp


def kernel(x_prompt, x_sample, state_lru_conv, state_lru_h, state_gdn_conv, state_gdn_S, state_gla_S, c_prompt, c_sample, ada_w, ada_b, norm1_g, norm2_g, w_in, lru_conv_w, lru_conv_b, lru_wa, lru_ba, lru_wx, lru_bx, lru_lambda, gdn_conv_w, gdn_a_log, gdn_dt_bias, gdn_norm_g, gla_wg2, gla_bg, gla_norm_g, w_out, router_w, router_b, exp_w_gate, exp_b_gate, exp_w_up, exp_b_up, exp_w_down, exp_b_down, final_norm_g):
    xs_t = x_sample.transpose(1, 0, 2).reshape(NS, D_MODEL)
    mod = _mod_call(jnp.concatenate([c_prompt, c_sample], axis=0), ada_w, ada_b)
    mod = jnp.stack([jnp.tile(mod[:, :BATCH], (1, MODB // BATCH, 1)), mod[:, BATCH:]], axis=1)

    p_states, s_states = [], []
    x = moe = None
    for l in range(DEPTH):
        w_in_r, mp = _mixer_params(l, w_in, lru_conv_w, lru_conv_b, lru_wa, lru_ba, lru_wx, lru_bx,
                                   lru_lambda, gdn_conv_w, gdn_a_log, gdn_dt_bias, gdn_norm_g,
                                   gla_wg2, gla_bg, gla_norm_g)
        stream_ins = (x_prompt, xs_t) if l == 0 else (x, moe, mod[l - 1])
        x, proj = _in_call(l == 0, stream_ins, mod[l], norm1_g[l].reshape(1, D_MODEL), w_in_r)
        mix_p, p_lconv, p_lh, p_gconv, p_gs, p_lst = _mix_prompt_call(proj, mp)
        states = (state_lru_conv[l].transpose(1, 0, 2), state_lru_h[l],
                  state_gdn_conv[l].transpose(1, 0, 2),
                  state_gdn_S[l].reshape(DEC_BATCH, -1), state_gla_S[l].reshape(DEC_BATCH, -1))
        mix_s, s_lconv, s_lh, s_gconv, s_gs, s_ls = _mix_sample_call(proj, states, mp)
        rw = jnp.zeros((D_MODEL, LANES), F32).at[:, :N_EXP].set(router_w[l])
        rb = jnp.zeros((1, LANES), F32).at[0, :N_EXP].set(router_b[l])
        x, h2, ridx, rprob, tcnt = _out_call(mix_p, mix_s.reshape(NS, D_MODEL), x, mod[l],
                                             w_out[l].astype(BF16), norm2_g[l].reshape(1, D_MODEL), rw, rb)
        moe = _moe(l, h2, ridx, rprob, tcnt, exp_w_gate, exp_b_gate, exp_w_up, exp_b_up,
                   exp_w_down, exp_b_down)
        p_gs = jnp.stack([p_gs[:, h * GDN_DK:(h + 1) * GDN_DK, h * GDN_DV:(h + 1) * GDN_DV]
                          for h in range(GDN_H)], axis=1)
        p_lst = jnp.stack([p_lst[:, h * GLA_DV:(h + 1) * GLA_DV, h * GLA_DK:(h + 1) * GLA_DK]
                           for h in range(GLA_H)], axis=1)
        p_states.append((p_lconv.reshape(3, BATCH, LRU_W).transpose(1, 0, 2), p_lh,
                         p_gconv.reshape(3, BATCH, GDN_CONV).transpose(1, 0, 2), p_gs,
                         p_lst.transpose(0, 1, 3, 2)))
        s_states.append((s_lconv.transpose(1, 0, 2), s_lh, s_gconv.transpose(1, 0, 2),
                         s_gs.reshape(DEC_BATCH, GDN_H, GDN_DK, GDN_DV),
                         s_ls.reshape(DEC_BATCH, GLA_H, GLA_DK, GLA_DV)))
    y_prompt, y_s = _final_call(x, moe, mod[DEPTH - 1], final_norm_g.reshape(1, D_MODEL))
    y_sample = y_s.reshape(DEC_SEQ, DEC_BATCH, D_MODEL).transpose(1, 0, 2)
    ps = [jnp.stack([s[j] for s in p_states]) for j in range(5)]
    ss = [jnp.stack([s[j] for s in s_states]) for j in range(5)]
    return (y_prompt, y_sample, *ps, *ss)
```
